```python
import math
import jax, jax.numpy as jnp
from jax import lax
import numpy as np

D_MODEL = 1024
BATCH = 8
SEQ = 4096
DEPTH = 1

HEAD_DIM = 64
HEADS_PER_GROUP = 4
DILATION_PATTERNS = ((128, 1), (512, 4), (2048, 16))
N_ATTN_GROUPS = len(DILATION_PATTERNS)
N_ATTN_HEADS = N_ATTN_GROUPS * HEADS_PER_GROUP
ATTN_WIDTH = N_ATTN_HEADS * HEAD_DIM
ATTN_OUT_WIDTH = HEADS_PER_GROUP * HEAD_DIM
N_BUCKETS = 32
MAX_DISTANCE = 2048
SSM_WIDTH = D_MODEL // 2
SSM_GROUP = 16
SSM_GROUPS = SSM_WIDTH // SSM_GROUP
SSM_STATE = 64
DT_MIN = 1e-3
DT_MAX = 1e-1
D_FF = 4 * D_MODEL
ALPHA = (2.0 * DEPTH) ** 0.25
BETA = (8.0 * DEPTH) ** -0.25
LN_EPS = 1e-5
NEG_INF = -1e30
IN_WIDTH = 3 * ATTN_WIDTH + SSM_WIDTH + 2 * D_MODEL
SPLITS = (ATTN_WIDTH, 2 * ATTN_WIDTH, 3 * ATTN_WIDTH, 3 * ATTN_WIDTH + SSM_WIDTH)

kernel_name = "hybrid_s5_dilated_attn_gated_deepnorm"


def layer_norm(x, g, b):
    xf = x.astype(jnp.float32)
    mu = xf.mean(-1, keepdims=True)
    xc = xf - mu
    var = (xc * xc).mean(-1, keepdims=True)
    return (xc * lax.rsqrt(var + LN_EPS)).astype(x.dtype) * g + b


def t5_bucket(dist):
    max_exact = N_BUCKETS // 2
    d = jnp.maximum(dist, 1).astype(jnp.float32)
    large = max_exact + (jnp.log(d / max_exact) / math.log(MAX_DISTANCE / max_exact)
                         * (N_BUCKETS - max_exact)).astype(jnp.int32)
    large = jnp.minimum(large, N_BUCKETS - 1)
    return jnp.where(dist < max_exact, dist, large)


def dilated_window_attention(q, k, v, rel_bias, window, dilation):
    bsz, seqlen, nh, hd = q.shape
    span = window // dilation
    n = seqlen // dilation
    blk = min(span, n)
    nb = -(-n // blk)
    pad = nb * blk - n

    def to_blocks(t):
        t = t.reshape(bsz, n, dilation, nh, hd).transpose(0, 2, 3, 1, 4)
        t = jnp.pad(t, ((0, 0), (0, 0), (0, 0), (0, pad), (0, 0)))
        return t.reshape(bsz, dilation, nh, nb, blk, hd)

    def with_prev(t):
        prev = jnp.pad(t[:, :, :, :-1], ((0, 0), (0, 0), (0, 0), (1, 0), (0, 0), (0, 0)))
        return jnp.concatenate([prev, t], axis=4)

    qb, kb, vb = to_blocks(q), to_blocks(k), to_blocks(v)
    kw, vw = with_prev(kb), with_prev(vb)

    qi = jnp.arange(blk)[:, None]
    kj = jnp.arange(2 * blk)[None, :]
    rel = qi + blk - kj
    key_pos = jnp.arange(nb)[:, None, None] * blk - blk + kj[None]
    valid = (rel >= 0) & (rel <= span) & (key_pos >= 0)
    bias = rel_bias[t5_bucket(jnp.maximum(rel, 0) * dilation)]
    bias = bias.transpose(2, 0, 1)[:, None].astype(jnp.float32)

    logits = jnp.einsum('brhnqe,brhnke->brhnqk', qb, kw).astype(jnp.float32) * (hd ** -0.5) + bias
    logits = jnp.where(valid, logits, NEG_INF)
    m = logits.max(-1, keepdims=True)
    e = jnp.exp(logits - m)
    s = e.sum(-1, keepdims=True)
    out = jnp.einsum('brhnqk,brhnke->brhnqe', (e / s).astype(v.dtype), vw)
    lse = (m + jnp.log(s))[..., 0]

    out = out.reshape(bsz, dilation, nh, nb * blk, hd)[:, :, :, :n]
    out = out.transpose(0, 3, 1, 2, 4).reshape(bsz, seqlen, nh, hd)
    lse = lse.reshape(bsz, dilation, nh, nb * blk)[..., :n]
    lse = lse.transpose(0, 3, 1, 2).reshape(bsz, seqlen, nh)
    return out, lse


def s5_ssm(u, lambda_re, lambda_im, log_dt, b_re, b_im, c_re, c_im, d_skip):
    f32 = jnp.float32
    bsz, seqlen, _ = u.shape
    ug = u.reshape(bsz, seqlen, SSM_GROUPS, SSM_GROUP).astype(f32)
    dt = jnp.exp(log_dt.astype(f32))[:, None]
    lr, li = lambda_re.astype(f32), lambda_im.astype(f32)
    mag = jnp.exp(lr * dt)
    ab_re, ab_im = mag * jnp.cos(li * dt), mag * jnp.sin(li * dt)
    den = lr * lr + li * li
    nr = ab_re - 1.0
    k_re = (nr * lr + ab_im * li) / den
    k_im = (ab_im * lr - nr * li) / den
    bu_re = jnp.einsum('blgh,gph->blgp', ug, b_re.astype(f32))
    bu_im = jnp.einsum('blgh,gph->blgp', ug, b_im.astype(f32))
    x_re = k_re * bu_re - k_im * bu_im
    x_im = k_re * bu_im + k_im * bu_re
    a_re = jnp.broadcast_to(ab_re, x_re.shape)
    a_im = jnp.broadcast_to(ab_im, x_im.shape)

    def combine(left, right):
        a1r, a1i, b1r, b1i = left
        a2r, a2i, b2r, b2i = right
        return (a1r * a2r - a1i * a2i,
                a1r * a2i + a1i * a2r,
                a2r * b1r - a2i * b1i + b2r,
                a2r * b1i + a2i * b1r + b2i)

    _, _, h_re, h_im = lax.associative_scan(combine, (a_re, a_im, x_re, x_im), axis=1)
    y = (jnp.einsum('blgp,ghp->blgh', h_re, c_re.astype(f32))
         - jnp.einsum('blgp,ghp->blgh', h_im, c_im.astype(f32))
         + d_skip.astype(f32) * ug)
    return y.reshape(bsz, seqlen, SSM_WIDTH).astype(u.dtype)


def setup_inputs(seed: int = 0) -> dict:
    key = jax.random.key(seed)
    ks = jax.random.split(key, 24)
    nrm = jax.random.normal
    sd = D_MODEL ** -0.5
    x = nrm(ks[0], (BATCH, SEQ, D_MODEL), jnp.float32)
    w_in = jnp.concatenate([
        nrm(ks[1], (DEPTH, D_MODEL, 2 * ATTN_WIDTH)) * sd,
        nrm(ks[2], (DEPTH, D_MODEL, ATTN_WIDTH)) * (BETA * sd),
        nrm(ks[3], (DEPTH, D_MODEL, SSM_WIDTH)) * sd,
        nrm(ks[4], (DEPTH, D_MODEL, 2 * D_MODEL)) * sd,
    ], axis=-1)
    b_gate = 0.02 * nrm(ks[5], (DEPTH, 2 * D_MODEL))
    lambda_re = -0.5 + 0.01 * nrm(ks[6], (DEPTH, SSM_GROUPS, SSM_STATE))
    lambda_im = jnp.broadcast_to(math.pi * jnp.arange(SSM_STATE, dtype=jnp.float32),
                                 (DEPTH, SSM_GROUPS, SSM_STATE)) + 0.0
    log_dt = jax.random.uniform(ks[7], (DEPTH, SSM_GROUPS), jnp.float32,
                                math.log(DT_MIN), math.log(DT_MAX))
    bs = (2.0 * SSM_GROUP) ** -0.5
    ssm_b_re = nrm(ks[8], (DEPTH, SSM_GROUPS, SSM_STATE, SSM_GROUP)) * bs
    ssm_b_im = nrm(ks[9], (DEPTH, SSM_GROUPS, SSM_STATE, SSM_GROUP)) * bs
    ssm_c_re = nrm(ks[10], (DEPTH, SSM_GROUPS, SSM_GROUP, SSM_STATE)) * 0.5
    ssm_c_im = nrm(ks[11], (DEPTH, SSM_GROUPS, SSM_GROUP, SSM_STATE)) * 0.5
    ssm_d = nrm(ks[12], (DEPTH, SSM_GROUPS, SSM_GROUP))
    w_glu = nrm(ks[13], (DEPTH, SSM_WIDTH, 2 * SSM_WIDTH)) * SSM_WIDTH ** -0.5
    w_ssm_proj = nrm(ks[14], (DEPTH, SSM_WIDTH, D_MODEL)) * SSM_WIDTH ** -0.5
    rel_bias = 0.1 * nrm(ks[15], (N_BUCKETS, N_ATTN_HEADS))
    w_attn_proj = nrm(ks[16], (DEPTH, ATTN_OUT_WIDTH, D_MODEL)) * ATTN_OUT_WIDTH ** -0.5
    w_out = nrm(ks[17], (DEPTH, D_MODEL, D_MODEL)) * (BETA * sd)
    ln1_g = 1.0 + 0.02 * nrm(ks[18], (DEPTH, D_MODEL))
    ln1_b = 0.02 * nrm(ks[19], (DEPTH, D_MODEL))
    w_up = nrm(ks[20], (DEPTH, D_MODEL, D_FF)) * (BETA * sd)
    w_down = nrm(ks[21], (DEPTH, D_FF, D_MODEL)) * (BETA * D_FF ** -0.5)
    ln2_g = 1.0 + 0.02 * nrm(ks[22], (DEPTH, D_MODEL))
    ln2_b = 0.02 * nrm(ks[23], (DEPTH, D_MODEL))
    return {"x": x, "w_in": w_in, "b_gate": b_gate, "lambda_re": lambda_re,
            "lambda_im": lambda_im, "log_dt": log_dt, "ssm_b_re": ssm_b_re,
            "ssm_b_im": ssm_b_im, "ssm_c_re": ssm_c_re, "ssm_c_im": ssm_c_im,
            "ssm_d": ssm_d, "w_glu": w_glu, "w_ssm_proj": w_ssm_proj,
            "rel_bias": rel_bias, "w_attn_proj": w_attn_proj, "w_out": w_out,
            "ln1_g": ln1_g, "ln1_b": ln1_b, "w_up": w_up, "w_down": w_down,
            "ln2_g": ln2_g, "ln2_b": ln2_b}


def reference(x, w_in, b_gate, lambda_re, lambda_im, log_dt, ssm_b_re, ssm_b_im,
              ssm_c_re, ssm_c_im, ssm_d, w_glu, w_ssm_proj, rel_bias, w_attn_proj,
              w_out, ln1_g, ln1_b, w_up, w_down, ln2_g, ln2_b):
    bsz, seqlen, _ = x.shape
    h = x
    for l in range(DEPTH):
        z = h @ w_in[l]
        q, k, v, u, g = jnp.split(z, SPLITS, axis=-1)
        qh = q.reshape(bsz, seqlen, N_ATTN_GROUPS, HEADS_PER_GROUP, HEAD_DIM)
        kh = k.reshape(bsz, seqlen, N_ATTN_GROUPS, HEADS_PER_GROUP, HEAD_DIM)
        vh = v.reshape(bsz, seqlen, N_ATTN_GROUPS, HEADS_PER_GROUP, HEAD_DIM)

        outs, lses = [], []
        for gi, (window, dilation) in enumerate(DILATION_PATTERNS):
            o, s = dilated_window_attention(
                qh[:, :, gi], kh[:, :, gi], vh[:, :, gi],
                rel_bias[:, gi * HEADS_PER_GROUP:(gi + 1) * HEADS_PER_GROUP],
                window, dilation)
            outs.append(o)
            lses.append(s)
        wts = jax.nn.softmax(jnp.stack(lses), axis=0)
        y_attn = (wts[..., None] * jnp.stack(outs).astype(jnp.float32)).sum(0)
        y_attn = y_attn.astype(h.dtype).reshape(bsz, seqlen, ATTN_OUT_WIDTH)

        y_ssm = jax.nn.gelu(s5_ssm(u, lambda_re[l], lambda_im[l], log_dt[l], ssm_b_re[l],
                                   ssm_b_im[l], ssm_c_re[l], ssm_c_im[l], ssm_d[l]))
        glu_a, glu_b = jnp.split(y_ssm @ w_glu[l], 2, axis=-1)
        y_ssm = glu_a * jax.nn.sigmoid(glu_b)

        gate_ssm, gate_attn = jnp.split(jax.nn.sigmoid(g + b_gate[l]), 2, axis=-1)
        mix = (gate_ssm * (y_ssm @ w_ssm_proj[l]) + gate_attn * (y_attn @ w_attn_proj[l])) @ w_out[l]
        h = layer_norm(ALPHA * h + mix, ln1_g[l], ln1_b[l])

        ff = jnp.square(jax.nn.relu(h @ w_up[l])) @ w_down[l]
        h = layer_norm(ALPHA * h + ff, ln2_g[l], ln2_b[l])
    return h
```

```python
import functools
import math

import jax
import jax.numpy as jnp
from jax import lax
from jax.experimental import pallas as pl
from jax.experimental.pallas import tpu as pltpu

F32 = jnp.float32
BF16 = jnp.bfloat16

D_MODEL = 1024
HEAD_DIM = 64
HEADS_PER_GROUP = 4
GROUP_WIDTH = HEADS_PER_GROUP * HEAD_DIM
DILATION_PATTERNS = ((128, 1), (512, 4), (2048, 16))
N_GROUPS = len(DILATION_PATTERNS)
ATTN_WIDTH = N_GROUPS * GROUP_WIDTH
N_BUCKETS = 32
MAX_DISTANCE = 2048
SSM_WIDTH = 512
SSM_GROUP = 16
SSM_GROUPS = 32
SSM_STATE = 64
D_FF = 4 * D_MODEL
DEPTH = 1
ALPHA = (2.0 * DEPTH) ** 0.25
LN_EPS = 1e-5
NEG_INF = -1e30

ATTN_BLOCK = 128
SSM_CHUNK = 16
SSM_CW = SSM_CHUNK * SSM_GROUP
VMEM_LIMIT_BYTES = 56 * 1024 * 1024


def _params(n_axes):
    return pltpu.CompilerParams(dimension_semantics=("arbitrary",) * n_axes,
                                vmem_limit_bytes=VMEM_LIMIT_BYTES)


def _const_spec(shape):
    nd = len(shape)
    return pl.BlockSpec(shape, lambda *_: (0,) * nd)


def _in_proj_kernel(x_ref, w_ref, bg_ref, qkv0_ref, qkv1_ref, qkv2_ref, u_ref, g_ref):
    xb = x_ref[...].astype(BF16)

    def mm(lo, hi):
        return jnp.dot(xb, w_ref[:, lo:hi], preferred_element_type=F32)

    gw3 = 3 * GROUP_WIDTH
    for gi, ref in enumerate((qkv0_ref, qkv1_ref, qkv2_ref)):
        for c in range(3):
            lo = gi * gw3 + c * GROUP_WIDTH
            ref[:, c * GROUP_WIDTH:(c + 1) * GROUP_WIDTH] = mm(lo, lo + GROUP_WIDTH).astype(BF16)
    base = N_GROUPS * gw3
    u_ref[...] = mm(base, base + SSM_WIDTH).astype(BF16)
    base += SSM_WIDTH
    for c in range(4):
        lo, hi = c * 512, (c + 1) * 512
        z = mm(base + lo, base + hi) + bg_ref[:, lo:hi]
        g_ref[:, lo:hi] = jax.nn.sigmoid(z).astype(BF16)


def _in_proj(x2, w_perm, b_gate, tm):
    t = x2.shape[0]
    n_in = w_perm.shape[1]
    row = lambda w: pl.BlockSpec((tm, w), lambda i: (i, 0))
    return pl.pallas_call(
        _in_proj_kernel,
        grid=(t // tm,),
        in_specs=[row(D_MODEL), _const_spec((D_MODEL, n_in)), _const_spec((1, 2 * D_MODEL))],
        out_specs=[row(3 * GROUP_WIDTH)] * 3 + [row(SSM_WIDTH), row(2 * D_MODEL)],
        out_shape=[jax.ShapeDtypeStruct((t, 3 * GROUP_WIDTH), BF16)] * 3
        + [jax.ShapeDtypeStruct((t, SSM_WIDTH), BF16), jax.ShapeDtypeStruct((t, 2 * D_MODEL), BF16)],
        compiler_params=_params(1),
        name="in_proj",
    )(x2, w_perm, b_gate)


def _attn_kernel(bucket_ref, relb_ref, cur_ref, prev_ref, out_ref, lse_ref, bias_scr, kv_scr, *, nq):
    blk = ATTN_BLOCK
    first = (pl.program_id(0) == 0) & (pl.program_id(1) == 0) & (pl.program_id(2) == 0)

    @pl.when(first)
    def _build_bias():
        bucket = bucket_ref[...]
        for h in range(HEADS_PER_GROUP):
            acc = jnp.full(bucket.shape, NEG_INF, F32)
            for bkt in range(N_BUCKETS):
                acc = jnp.where(bucket == bkt, relb_ref[bkt, h], acc)
            bias_scr[h * blk:(h + 1) * blk, :] = acc

    kv_scr[0:blk, :] = prev_ref[:, GROUP_WIDTH:3 * GROUP_WIDTH]
    kv_scr[blk:, :] = cur_ref[:, GROUP_WIDTH:3 * GROUP_WIDTH]

    lane_head = lax.broadcasted_iota(jnp.int32, (blk, GROUP_WIDTH), 1) // HEAD_DIM
    has_prev = pl.program_id(2) > 0
    for j in range(nq):
        q = cur_ref[j * blk:(j + 1) * blk, 0:GROUP_WIDTH]
        qs = jnp.concatenate(
            [jnp.where(lane_head == h, q, jnp.zeros_like(q)) for h in range(HEADS_PER_GROUP)], axis=0)
        k2 = kv_scr[j * blk:(j + 2) * blk, 0:GROUP_WIDTH]
        v2 = kv_scr[j * blk:(j + 2) * blk, GROUP_WIDTH:2 * GROUP_WIDTH]
        s = lax.dot_general(qs, k2, (((1,), (1,)), ((), ())), preferred_element_type=F32)
        s = s + bias_scr[...]
        if j == 0:
            col = lax.broadcasted_iota(jnp.int32, s.shape, 1)
            s = jnp.where((col >= blk) | has_prev, s, NEG_INF)
        m = jnp.max(s, axis=-1, keepdims=True)
        p = jnp.exp(s - m)
        l = jnp.sum(p, axis=-1, keepdims=True)
        pv = jnp.dot(p.astype(BF16), v2, preferred_element_type=F32)
        pv = pv * (1.0 / l)
        lse = m + jnp.log(l)
        o = pv[0:blk]
        e = jnp.broadcast_to(lse[0:blk], (blk, GROUP_WIDTH))
        for h in range(1, HEADS_PER_GROUP):
            sel = lane_head == h
            o = jnp.where(sel, pv[h * blk:(h + 1) * blk], o)
            e = jnp.where(sel, lse[h * blk:(h + 1) * blk], e)
        out_ref[j * blk:(j + 1) * blk, :] = o.astype(BF16)
        lse_ref[j * blk:(j + 1) * blk, :] = e


def _t5_bucket(dist):
    max_exact = N_BUCKETS // 2
    d = jnp.maximum(dist, 1).astype(F32)
    large = max_exact + (jnp.log(d / max_exact) / math.log(MAX_DISTANCE / max_exact)
                         * (N_BUCKETS - max_exact)).astype(jnp.int32)
    large = jnp.minimum(large, N_BUCKETS - 1)
    return jnp.where(dist < max_exact, dist, large)


def _bucket_table(window, dilation):
    blk = ATTN_BLOCK
    span = window // dilation
    rel = jnp.arange(blk)[:, None] + blk - jnp.arange(2 * blk)[None, :]
    valid = (rel >= 0) & (rel <= span)
    return jnp.where(valid, _t5_bucket(jnp.maximum(rel, 0) * dilation), -1).astype(jnp.int32)


def _attention_group(qkv, rel_bias_g, window, dilation, bsz, seqlen, mt):
    assert window // dilation == ATTN_BLOCK
    n = seqlen // dilation
    mt = min(mt, n)
    nq = mt // ATTN_BLOCK
    w3 = 3 * GROUP_WIDTH
    view = qkv.reshape(bsz, n, dilation * w3)
    cur = pl.BlockSpec((None, mt, w3), lambda b, r, i: (b, i, r))
    prev = pl.BlockSpec((None, ATTN_BLOCK, w3), lambda b, r, i: (b, jnp.maximum(i * nq - 1, 0), r))
    ospec = pl.BlockSpec((None, mt, GROUP_WIDTH), lambda b, r, i: (b, i, r))
    out, lse = pl.pallas_call(
        functools.partial(_attn_kernel, nq=nq),
        grid=(bsz, dilation, n // mt),
        in_specs=[_const_spec((ATTN_BLOCK, 2 * ATTN_BLOCK)),
                  pl.BlockSpec(memory_space=pltpu.SMEM), cur, prev],
        out_specs=[ospec, ospec],
        out_shape=[jax.ShapeDtypeStruct((bsz, n, dilation * GROUP_WIDTH), BF16),
                   jax.ShapeDtypeStruct((bsz, n, dilation * GROUP_WIDTH), F32)],
        scratch_shapes=[pltpu.VMEM((HEADS_PER_GROUP * ATTN_BLOCK, 2 * ATTN_BLOCK), F32),
                        pltpu.VMEM((mt + ATTN_BLOCK, 2 * GROUP_WIDTH), BF16)],
        compiler_params=_params(3),
        name=f"attn_d{dilation}",
    )(_bucket_table(window, dilation), rel_bias_g, view, view)
    t = bsz * seqlen
    return out.reshape(t, GROUP_WIDTH), lse.reshape(t, GROUP_WIDTH)


def _ssm_weights(lambda_re, lambda_im, log_dt, b_re, b_im, c_re, c_im, d_skip):
    f32 = F32
    cs = SSM_CHUNK
    dt = jnp.exp(log_dt.astype(f32))[:, None]
    lr, li = lambda_re.astype(f32), lambda_im.astype(f32)
    mag = jnp.exp(lr * dt)
    ab_re, ab_im = mag * jnp.cos(li * dt), mag * jnp.sin(li * dt)
    den = lr * lr + li * li
    nr = ab_re - 1.0
    k_re = (nr * lr + ab_im * li) / den
    k_im = (ab_im * lr - nr * li) / den
    br, bi = b_re.astype(f32), b_im.astype(f32)
    bb_re = k_re[..., None] * br - k_im[..., None] * bi
    bb_im = k_re[..., None] * bi + k_im[..., None] * br
    j = jnp.arange(cs + 1, dtype=f32)[None, :, None]
    pmag = jnp.exp(lr[:, None, :] * dt[:, :, None] * j)
    ang = li[:, None, :] * dt[:, :, None] * j
    p_re, p_im = pmag * jnp.cos(ang), pmag * jnp.sin(ang)
    cr, ci = c_re.astype(f32), c_im.astype(f32)
    hi = lax.Precision.HIGHEST
    cp_re = cr[:, None] * p_re[:, :, None, :] - ci[:, None] * p_im[:, :, None, :]
    cp_im = cr[:, None] * p_im[:, :, None, :] + ci[:, None] * p_re[:, :, None, :]
    kern = (jnp.einsum('gjhp,gpk->gjkh', cp_re, bb_re, precision=hi)
            - jnp.einsum('gjhp,gpk->gjkh', cp_im, bb_im, precision=hi))
    s_idx = jnp.arange(cs)[:, None]
    t_idx = jnp.arange(cs)[None, :]
    lag = t_idx - s_idx
    toep = jnp.where((lag >= 0)[None, :, :, None, None],
                     kern[:, jnp.clip(lag, 0, cs)], 0.0)
    eye_h = jnp.eye(SSM_GROUP, dtype=f32)
    toep = toep + ((lag == 0)[None, :, :, None, None]
                   * (eye_h[None] * d_skip.astype(f32)[:, None, :])[:, None, None])
    toep = toep.transpose(0, 1, 3, 2, 4).reshape(SSM_GROUPS, SSM_CW, SSM_CW)
    pe_re, pe_im = p_re[:, cs - 1 - jnp.arange(cs)], p_im[:, cs - 1 - jnp.arange(cs)]
    st_re = pe_re[:, :, None, :] * bb_re.transpose(0, 2, 1)[:, None] - pe_im[:, :, None, :] * bb_im.transpose(0, 2, 1)[:, None]
    st_im = pe_re[:, :, None, :] * bb_im.transpose(0, 2, 1)[:, None] + pe_im[:, :, None, :] * bb_re.transpose(0, 2, 1)[:, None]
    st_re = st_re.reshape(SSM_GROUPS, SSM_CW, SSM_STATE)
    st_im = st_im.reshape(SSM_GROUPS, SSM_CW, SSM_STATE)
    bst = jnp.concatenate([st_re, st_im, st_im, st_re], axis=-1)
    ro_re = cp_re[:, 1:].transpose(0, 3, 1, 2).reshape(SSM_GROUPS, SSM_STATE, SSM_CW)
    ro_im = cp_im[:, 1:].transpose(0, 3, 1, 2).reshape(SSM_GROUPS, SSM_STATE, SSM_CW)
    cst = jnp.concatenate([ro_re, -ro_im], axis=1)
    a_re, a_im = p_re[:, cs], p_im[:, cs]
    avec = jnp.stack([jnp.concatenate([a_re, a_re], -1),
                      jnp.concatenate([-a_im, a_im], -1)], axis=1)
    return toep.astype(BF16), bst.astype(BF16), cst.astype(BF16), avec


def _ssm_kernel(u_ref, toep_ref, bst_ref, cst_ref, avec_ref, y_ref, zz_scr, hp_scr, *, n_chunks, bsz):
    u = u_ref[...]
    zz_scr[...] = jnp.dot(u, bst_ref[...], preferred_element_type=F32)
    a_r = jnp.broadcast_to(avec_ref[0:1, :], (bsz, 2 * SSM_STATE))
    a_s = jnp.broadcast_to(avec_ref[1:2, :], (bsz, 2 * SSM_STATE))

    def step(c, carry):
        h, hs = carry
        row = pl.multiple_of(c * bsz, bsz)
        hp_scr[pl.ds(row, bsz), :] = h
        z = zz_scr[pl.ds(row, bsz), :]
        h_new = a_r * h + a_s * hs + z[:, 0:2 * SSM_STATE]
        hs_new = a_r * hs - a_s * h + z[:, 2 * SSM_STATE:]
        return h_new, hs_new

    zero = jnp.zeros((bsz, 2 * SSM_STATE), F32)
    lax.fori_loop(0, n_chunks, step, (zero, zero), unroll=8)
    y = jnp.dot(u, toep_ref[...], preferred_element_type=F32)
    y = y + jnp.dot(hp_scr[...].astype(BF16), cst_ref[...], preferred_element_type=F32)
    y_ref[...] = jax.nn.gelu(y).astype(BF16)


def _ssm(u2, weights, bsz, seqlen):
    toep, bst, cst, avec = weights
    nc = seqlen // SSM_CHUNK
    rows = nc * bsz
    ug = u2.reshape(bsz, nc, SSM_CHUNK, SSM_GROUPS, SSM_GROUP).transpose(3, 1, 0, 2, 4)
    ug = ug.reshape(SSM_GROUPS, rows, SSM_CW)
    grp = lambda r, c: pl.BlockSpec((None, r, c), lambda g: (g, 0, 0))
    yg = pl.pallas_call(
        functools.partial(_ssm_kernel, n_chunks=nc, bsz=bsz),
        grid=(SSM_GROUPS,),
        in_specs=[grp(rows, SSM_CW), grp(SSM_CW, SSM_CW), grp(SSM_CW, 4 * SSM_STATE),
                  grp(2 * SSM_STATE, SSM_CW), grp(2, 2 * SSM_STATE)],
        out_specs=grp(rows, SSM_CW),
        out_shape=jax.ShapeDtypeStruct((SSM_GROUPS, rows, SSM_CW), BF16),
        scratch_shapes=[pltpu.VMEM((rows, 4 * SSM_STATE), F32), pltpu.VMEM((rows, 2 * SSM_STATE), F32)],
        compiler_params=_params(1),
        name="ssm",
    )(ug, toep, bst, cst, avec)
    y = yg.reshape(SSM_GROUPS, nc, bsz, SSM_CHUNK, SSM_GROUP).transpose(2, 1, 3, 0, 4)
    return y.reshape(bsz * seqlen, SSM_WIDTH)


def _layer_norm(v, g, b):
    mu = jnp.mean(v, axis=-1, keepdims=True)
    vc = v - mu
    var = jnp.mean(vc * vc, axis=-1, keepdims=True)
    return vc * lax.rsqrt(var + LN_EPS) * g + b


def _merge_kernel(o0_ref, o1_ref, o2_ref, l0_ref, l1_ref, l2_ref, ys_ref, g_ref, x_ref,
                  wglu_ref, wsp_ref, wap_ref, wout_ref, lng_ref, lnb_ref, h_ref):
    l0, l1, l2 = l0_ref[...], l1_ref[...], l2_ref[...]
    mx = jnp.maximum(jnp.maximum(l0, l1), l2)
    e0, e1, e2 = jnp.exp(l0 - mx), jnp.exp(l1 - mx), jnp.exp(l2 - mx)
    num = e0 * o0_ref[...].astype(F32) + e1 * o1_ref[...].astype(F32) + e2 * o2_ref[...].astype(F32)
    y_attn = (num / (e0 + e1 + e2)).astype(BF16)
    glu = jnp.dot(ys_ref[...], wglu_ref[...], preferred_element_type=F32)
    y_ssm = (glu[:, :SSM_WIDTH] * jax.nn.sigmoid(glu[:, SSM_WIDTH:])).astype(BF16)
    pa = jnp.dot(y_ssm, wsp_ref[...], preferred_element_type=F32)
    pb = jnp.dot(y_attn, wap_ref[...], preferred_element_type=F32)
    gated = (g_ref[:, :D_MODEL].astype(F32) * pa + g_ref[:, D_MODEL:].astype(F32) * pb).astype(BF16)
    mix = jnp.dot(gated, wout_ref[...], preferred_element_type=F32)
    h_ref[...] = _layer_norm(ALPHA * x_ref[...] + mix, lng_ref[...], lnb_ref[...])


def _merge(outs, lses, ys, gates, x2, w_glu, w_sp, w_ap, w_out, ln_g, ln_b, tm):
    t = x2.shape[0]
    row = lambda w: pl.BlockSpec((tm, w), lambda i: (i, 0))
    return pl.pallas_call(
        _merge_kernel,
        grid=(t // tm,),
        in_specs=[row(GROUP_WIDTH)] * 6 + [row(SSM_WIDTH), row(2 * D_MODEL), row(D_MODEL),
                  _const_spec(w_glu.shape), _const_spec(w_sp.shape), _const_spec(w_ap.shape),
                  _const_spec(w_out.shape), _const_spec((1, D_MODEL)), _const_spec((1, D_MODEL))],
        out_specs=row(D_MODEL),
        out_shape=jax.ShapeDtypeStruct((t, D_MODEL), F32),
        compiler_params=_params(1),
        name="merge_ln1",
    )(*outs, *lses, ys, gates, x2, w_glu, w_sp, w_ap, w_out, ln_g, ln_b)


FF_CHUNK = 1024


def _ffn_kernel(h_ref, wup_ref, wdn_ref, lng_ref, lnb_ref, o_ref):
    h = h_ref[...]
    hb = h.astype(BF16)
    acc = ALPHA * h
    for c in range(D_FF // FF_CHUNK):
        lo, hi = c * FF_CHUNK, (c + 1) * FF_CHUNK
        up = jnp.dot(hb, wup_ref[:, lo:hi], preferred_element_type=F32)
        act = jnp.square(jnp.maximum(up, 0.0)).astype(BF16)
        acc = acc + jnp.dot(act, wdn_ref[lo:hi, :], preferred_element_type=F32)
    o_ref[...] = _layer_norm(acc, lng_ref[...], lnb_ref[...])


def _ffn(h1, w_up, w_down, ln_g, ln_b, tm):
    t = h1.shape[0]
    row = pl.BlockSpec((tm, D_MODEL), lambda i: (i, 0))
    return pl.pallas_call(
        _ffn_kernel,
        grid=(t // tm,),
        in_specs=[row, _const_spec(w_up.shape), _const_spec(w_down.shape),
                  _const_spec((1, D_MODEL)), _const_spec((1, D_MODEL))],
        out_specs=row,
        out_shape=jax.ShapeDtypeStruct((t, D_MODEL), F32),
        compiler_params=_params(1),
        name="ffn_ln2",
    )(h1, w_up, w_down, ln_g, ln_b)


def _permute_w_in(w):
    aw = ATTN_WIDTH
    cols = []
    for gi in range(N_GROUPS):
        lo, hi = gi * GROUP_WIDTH, (gi + 1) * GROUP_WIDTH
        cols += [w[:, lo:hi] * (HEAD_DIM ** -0.5), w[:, aw + lo:aw + hi], w[:, 2 * aw + lo:2 * aw + hi]]
    cols.append(w[:, 3 * aw:])
    return jnp.concatenate(cols, axis=1).astype(BF16)


def _layer(h2, bsz, seqlen, l, w_in, b_gate, lambda_re, lambda_im, log_dt, ssm_b_re, ssm_b_im,
           ssm_c_re, ssm_c_im, ssm_d, w_glu, w_ssm_proj, rel_bias, w_attn_proj, w_out,
           ln1_g, ln1_b, w_up, w_down, ln2_g, ln2_b, tm=512, attn_mt=512):
    qkv0, qkv1, qkv2, u, gates = _in_proj(h2, _permute_w_in(w_in[l]), b_gate[l][None, :], tm)
    outs, lses = [], []
    for gi, ((window, dilation), qkv) in enumerate(zip(DILATION_PATTERNS, (qkv0, qkv1, qkv2))):
        rb = rel_bias[:, gi * HEADS_PER_GROUP:(gi + 1) * HEADS_PER_GROUP].astype(F32)
        o, s = _attention_group(qkv, rb, window, dilation, bsz, seqlen, attn_mt)
        outs.append(o)
        lses.append(s)
    ssm_w = _ssm_weights(lambda_re[l], lambda_im[l], log_dt[l], ssm_b_re[l], ssm_b_im[l],
                         ssm_c_re[l], ssm_c_im[l], ssm_d[l])
    ys = _ssm(u, ssm_w, bsz, seqlen)
    h1 = _merge(outs, lses, ys, gates, h2, w_glu[l].astype(BF16), w_ssm_proj[l].astype(BF16),
                w_attn_proj[l].astype(BF16), w_out[l].astype(BF16),
                ln1_g[l][None, :], ln1_b[l][None, :], tm)
    return _ffn(h1, w_up[l].astype(BF16), w_down[l].astype(BF16), ln2_g[l][None, :], ln2_b[l][None, :], tm)


def kernel(x, w_in, b_gate, lambda_re, lambda_im, log_dt, ssm_b_re, ssm_b_im, ssm_c_re, ssm_c_im,
           ssm_d, w_glu, w_ssm_proj, rel_bias, w_attn_proj, w_out, ln1_g, ln1_b, w_up, w_down,
           ln2_g, ln2_b):
    bsz, seqlen, d = x.shape
    h = x.reshape(bsz * seqlen, d)
    for l in range(w_in.shape[0]):
        h = _layer(h, bsz, seqlen, l, w_in, b_gate, lambda_re, lambda_im, log_dt, ssm_b_re, ssm_b_im,
                   ssm_c_re, ssm_c_im, ssm_d, w_glu, w_ssm_proj, rel_bias, w_attn_proj, w_out,
                   ln1_g, ln1_b, w_up, w_down, ln2_g, ln2_b)
    return h.reshape(bsz, seqlen, d)
```

```python
import functools
import math

import jax
import jax.numpy as jnp
from jax import lax
from jax.experimental import pallas as pl
from jax.experimental.pallas import tpu as pltpu

F32 = jnp.float32
BF16 = jnp.bfloat16

D_MODEL = 1024
HEAD_DIM = 64
HEADS_PER_GROUP = 4
GROUP_WIDTH = HEADS_PER_GROUP * HEAD_DIM
DILATION_PATTERNS = ((128, 1), (512, 4), (2048, 16))
N_GROUPS = len(DILATION_PATTERNS)
ATTN_WIDTH = N_GROUPS * GROUP_WIDTH
N_BUCKETS = 32
MAX_DISTANCE = 2048
SSM_WIDTH = 512
SSM_GROUP = 16
SSM_GROUPS = 32
SSM_STATE = 64
D_FF = 4 * D_MODEL
DEPTH = 1
ALPHA = (2.0 * DEPTH) ** 0.25
LN_EPS = 1e-5
NEG_INF = -1e30

ATTN_BLOCK = 128
SSM_CHUNK = 16
SSM_CW = SSM_CHUNK * SSM_GROUP
LANES = 128
VMEM_LIMIT_BYTES =56 * 1024 * 1024


def _params(n_axes):
    return pltpu.CompilerParams(dimension_semantics=("arbitrary",) * n_axes,
                                vmem_limit_bytes=VMEM_LIMIT_BYTES)


def _const_spec(shape):
    nd = len(shape)
    return pl.BlockSpec(shape, lambda *_: (0,) * nd)


def _in_proj_kernel(x_ref, w_ref, bg_ref, qkv0_ref, qkv1_ref, qkv2_ref, u_ref, g_ref, scr):
    xb = x_ref[...].astype(BF16)
    tm = xb.shape[0]

    def mm(lo, hi):
        return jnp.dot(xb, w_ref[:, lo:hi], preferred_element_type=F32)

    gw3 = 3 * GROUP_WIDTH
    for gi, ref in enumerate((qkv0_ref, qkv1_ref, qkv2_ref)):
        d = DILATION_PATTERNS[gi][1]
        for c in range(3):
            lo = gi * gw3 + c * GROUP_WIDTH
            res = mm(lo, lo + GROUP_WIDTH)
            if d == 1:
                ref[0, :, c * GROUP_WIDTH:(c + 1) * GROUP_WIDTH] = res.astype(BF16)
                continue
            for k in range(GROUP_WIDTH // LANES):
                scr[k] = res[:, k * LANES:(k + 1) * LANES]
            for r in range(d):
                for k in range(GROUP_WIDTH // LANES):
                    col = c * GROUP_WIDTH + k * LANES
                    ref[r, :, col:col + LANES] = scr[k, pl.ds(r, tm // d, stride=d), :].astype(BF16)
    base = N_GROUPS * gw3
    u_ref[...] = mm(base, base + SSM_WIDTH).astype(BF16)
    base += SSM_WIDTH
    for c in range(4):
        lo, hi = c * 512, (c + 1) * 512
        z = mm(base + lo, base + hi) + bg_ref[:, lo:hi]
        g_ref[:, lo:hi] = jax.nn.sigmoid(z).astype(BF16)


def _in_proj(x2, w_perm, b_gate, bsz, seqlen, tm):
    t = x2.shape[0]
    n_in = w_perm.shape[1]
    tiles = seqlen // tm
    row = lambda w: pl.BlockSpec((tm, w), lambda i: (i, 0))
    w3 = 3 * GROUP_WIDTH
    dils = [d for _, d in DILATION_PATTERNS]
    res_spec = lambda d: pl.BlockSpec((None, d, tm // d, w3), lambda i: (i // tiles, 0, i % tiles, 0))
    return pl.pallas_call(
        _in_proj_kernel,
        grid=(t // tm,),
        in_specs=[row(D_MODEL), _const_spec((D_MODEL, n_in)), _const_spec((1, 2 * D_MODEL))],
        out_specs=[res_spec(d) for d in dils] + [row(SSM_WIDTH), row(2 * D_MODEL)],
        out_shape=[jax.ShapeDtypeStruct((bsz, d, seqlen // d, w3), BF16) for d in dils]
        + [jax.ShapeDtypeStruct((t, SSM_WIDTH), BF16), jax.ShapeDtypeStruct((t, 2 * D_MODEL), BF16)],
        scratch_shapes=[pltpu.VMEM((GROUP_WIDTH // LANES, tm, LANES), F32)],
        compiler_params=_params(1),
        name="in_proj",
    )(x2, w_perm, b_gate)


def _attn_kernel(bucket_ref, relb_ref, cur_ref, prev_ref, out_ref, lse_ref, bias_scr, kv_scr, *, nq):
    blk = ATTN_BLOCK
    first = (pl.program_id(0) == 0) & (pl.program_id(1) == 0) & (pl.program_id(2) == 0)

    @pl.when(first)
    def _build_bias():
        bucket = bucket_ref[...]
        for h in range(HEADS_PER_GROUP):
            acc = jnp.full(bucket.shape, NEG_INF, F32)
            for bkt in range(N_BUCKETS):
                acc = jnp.where(bucket == bkt, relb_ref[bkt, h], acc)
            bias_scr[h * blk:(h + 1) * blk, :] = acc

    kv_scr[0:blk, :] = prev_ref[:, GROUP_WIDTH:3 * GROUP_WIDTH]
    kv_scr[blk:, :] = cur_ref[:, GROUP_WIDTH:3 * GROUP_WIDTH]

    lane_head = lax.broadcasted_iota(jnp.int32, (blk, GROUP_WIDTH), 1) // HEAD_DIM
    has_prev = pl.program_id(2) > 0
    for j in range(nq):
        q = cur_ref[j * blk:(j + 1) * blk, 0:GROUP_WIDTH]
        qs = jnp.concatenate(
            [jnp.where(lane_head == h, q, jnp.zeros_like(q)) for h in range(HEADS_PER_GROUP)], axis=0)
        k2 = kv_scr[j * blk:(j + 2) * blk, 0:GROUP_WIDTH]
        v2 = kv_scr[j * blk:(j + 2) * blk, GROUP_WIDTH:2 * GROUP_WIDTH]
        s = lax.dot_general(qs, k2, (((1,), (1,)), ((), ())), preferred_element_type=F32)
        s = s + bias_scr[...]
        if j == 0:
            col = lax.broadcasted_iota(jnp.int32, s.shape, 1)
            s = jnp.where((col >= blk) | has_prev, s, NEG_INF)
        m = jnp.max(s, axis=-1, keepdims=True)
        p = jnp.exp(s - m)
        l = jnp.sum(p, axis=-1, keepdims=True)
        pv = jnp.dot(p.astype(BF16), v2, preferred_element_type=F32)
        pv = pv * (1.0 / l)
        lse = m + jnp.log(l)
        o = pv[0:blk]
        e = jnp.broadcast_to(lse[0:blk], (blk, GROUP_WIDTH))
        for h in range(1, HEADS_PER_GROUP):
            sel = lane_head == h
            o = jnp.where(sel, pv[h * blk:(h + 1) * blk], o)
            e = jnp.where(sel, lse[h * blk:(h + 1) * blk], e)
        out_ref[j * blk:(j + 1) * blk, :] = o.astype(BF16)
        lse_ref[j * blk:(j + 1) * blk, :] = e


def _t5_bucket(dist):
    max_exact = N_BUCKETS // 2
    d = jnp.maximum(dist, 1).astype(F32)
    large = max_exact + (jnp.log(d / max_exact) / math.log(MAX_DISTANCE / max_exact)
                         * (N_BUCKETS - max_exact)).astype(jnp.int32)
    large = jnp.minimum(large, N_BUCKETS - 1)
    return jnp.where(dist < max_exact, dist, large)


def _bucket_table(window, dilation):
    blk = ATTN_BLOCK
    span = window // dilation
    rel = jnp.arange(blk)[:, None] + blk - jnp.arange(2 * blk)[None, :]
    valid = (rel >= 0) & (rel <= span)
    return jnp.where(valid, _t5_bucket(jnp.maximum(rel, 0) * dilation), -1).astype(jnp.int32)


def _attention_group(qkv, rel_bias_g, window, dilation, mt):
    assert window // dilation == ATTN_BLOCK
    bsz, _, n, w3 = qkv.shape
    mt = min(mt, n)
    nq = mt // ATTN_BLOCK
    cur = pl.BlockSpec((None, None, mt, w3), lambda b, r, i: (b, r, i, 0))
    prev = pl.BlockSpec((None, None, ATTN_BLOCK, w3), lambda b, r, i: (b, r, jnp.maximum(i * nq - 1, 0), 0))
    ospec = pl.BlockSpec((None, None, mt, GROUP_WIDTH), lambda b, r, i: (b, r, i, 0))
    return pl.pallas_call(
        functools.partial(_attn_kernel, nq=nq),
        grid=(bsz, dilation, n // mt),
        in_specs=[_const_spec((ATTN_BLOCK, 2 * ATTN_BLOCK)),
                  pl.BlockSpec(memory_space=pltpu.SMEM), cur, prev],
        out_specs=[ospec, ospec],
        out_shape=[jax.ShapeDtypeStruct((bsz, dilation, n, GROUP_WIDTH), BF16),
                   jax.ShapeDtypeStruct((bsz, dilation, n, GROUP_WIDTH), F32)],
        scratch_shapes=[pltpu.VMEM((HEADS_PER_GROUP * ATTN_BLOCK, 2 * ATTN_BLOCK), F32),
                        pltpu.VMEM((mt + ATTN_BLOCK, 2 * GROUP_WIDTH), BF16)],
        compiler_params=_params(3),
        name=f"attn_d{dilation}",
    )(_bucket_table(window, dilation), rel_bias_g, qkv, qkv)


def _ssm_weights(lambda_re, lambda_im, log_dt, b_re, b_im, c_re, c_im, d_skip):
    f32 = F32
    cs = SSM_CHUNK
    dt = jnp.exp(log_dt.astype(f32))[:, None]
    lr, li = lambda_re.astype(f32), lambda_im.astype(f32)
    mag = jnp.exp(lr * dt)
    ab_re, ab_im = mag * jnp.cos(li * dt), mag * jnp.sin(li * dt)
    den = lr * lr + li * li
    nr = ab_re - 1.0
    k_re = (nr * lr + ab_im * li) / den
    k_im = (ab_im * lr - nr * li) / den
    br, bi = b_re.astype(f32), b_im.astype(f32)
    bb_re = k_re[..., None] * br - k_im[..., None] * bi
    bb_im = k_re[..., None] * bi + k_im[..., None] * br
    j = jnp.arange(cs + 1, dtype=f32)[None, :, None]
    pmag = jnp.exp(lr[:, None, :] * dt[:, :, None] * j)
    ang = li[:, None, :] * dt[:, :, None] * j
    p_re, p_im = pmag * jnp.cos(ang), pmag * jnp.sin(ang)
    cr, ci = c_re.astype(f32), c_im.astype(f32)
    hi = lax.Precision.HIGHEST
    cp_re = cr[:, None] * p_re[:, :, None, :] - ci[:, None] * p_im[:, :, None, :]
    cp_im = cr[:, None] * p_im[:, :, None, :] + ci[:, None] * p_re[:, :, None, :]
    kern = (jnp.einsum('gjhp,gpk->gjkh', cp_re, bb_re, precision=hi)
            - jnp.einsum('gjhp,gpk->gjkh', cp_im, bb_im, precision=hi))
    s_idx = jnp.arange(cs)[:, None]
    t_idx = jnp.arange(cs)[None, :]
    lag = t_idx - s_idx
    toep = jnp.where((lag >= 0)[None, :, :, None, None],
                     kern[:, jnp.clip(lag, 0, cs)], 0.0)
    eye_h = jnp.eye(SSM_GROUP, dtype=f32)
    toep = toep + ((lag == 0)[None, :, :, None, None]
                   * (eye_h[None] * d_skip.astype(f32)[:, None, :])[:, None, None])
    toep = toep.transpose(0, 1, 3, 2, 4).reshape(SSM_GROUPS, SSM_CW, SSM_CW)
    pe_re, pe_im = p_re[:, cs - 1 - jnp.arange(cs)], p_im[:, cs - 1 - jnp.arange(cs)]
    st_re = pe_re[:, :, None, :] * bb_re.transpose(0, 2, 1)[:, None] - pe_im[:, :, None, :] * bb_im.transpose(0, 2, 1)[:, None]
    st_im = pe_re[:, :, None, :] * bb_im.transpose(0, 2, 1)[:, None] + pe_im[:, :, None, :] * bb_re.transpose(0, 2, 1)[:, None]
    st_re = st_re.reshape(SSM_GROUPS, SSM_CW, SSM_STATE)
    st_im = st_im.reshape(SSM_GROUPS, SSM_CW, SSM_STATE)
    bst = jnp.concatenate([st_re, st_im, st_im, st_re], axis=-1)
    ro_re = cp_re[:, 1:].transpose(0, 3, 1, 2).reshape(SSM_GROUPS, SSM_STATE, SSM_CW)
    ro_im = cp_im[:, 1:].transpose(0, 3, 1, 2).reshape(SSM_GROUPS, SSM_STATE, SSM_CW)
    cst = jnp.concatenate([ro_re, -ro_im], axis=1)
    a_re, a_im = p_re[:, cs], p_im[:, cs]
    avec = jnp.stack([jnp.concatenate([a_re, a_re], -1),
                      jnp.concatenate([-a_im, a_im], -1)], axis=1)
    return toep.astype(BF16), bst.astype(BF16), cst.astype(BF16), avec


def _ssm_kernel(u_ref, toep_ref, bst_ref, cst_ref, avec_ref, y_ref, zz_scr, hp_scr, *, n_chunks, bsz):
    u = u_ref[...]
    zz_scr[...] = jnp.dot(u, bst_ref[...], preferred_element_type=F32)
    a_r = jnp.broadcast_to(avec_ref[0:1, :], (bsz, 2 * SSM_STATE))
    a_s = jnp.broadcast_to(avec_ref[1:2, :], (bsz, 2 * SSM_STATE))

    def step(c, carry):
        h, hs = carry
        row = pl.multiple_of(c * bsz, bsz)
        hp_scr[pl.ds(row, bsz), :] = h
        z = zz_scr[pl.ds(row, bsz), :]
        h_new = a_r * h + a_s * hs + z[:, 0:2 * SSM_STATE]
        hs_new = a_r * hs - a_s * h + z[:, 2 * SSM_STATE:]
        return h_new, hs_new

    zero = jnp.zeros((bsz, 2 * SSM_STATE), F32)
    lax.fori_loop(0, n_chunks, step, (zero, zero), unroll=8)
    y = jnp.dot(u, toep_ref[...], preferred_element_type=F32)
    y = y + jnp.dot(hp_scr[...].astype(BF16), cst_ref[...], preferred_element_type=F32)
    y_ref[...] = jax.nn.gelu(y).astype(BF16)


def _ssm(u2, weights, bsz, seqlen):
    toep, bst, cst, avec = weights
    nc = seqlen // SSM_CHUNK
    rows = nc * bsz
    ug = u2.reshape(bsz, nc, SSM_CHUNK, SSM_GROUPS, SSM_GROUP).transpose(3, 1, 0, 2, 4)
    ug = ug.reshape(SSM_GROUPS, rows, SSM_CW)
    grp = lambda r, c: pl.BlockSpec((None, r, c), lambda g: (g, 0, 0))
    yg = pl.pallas_call(
        functools.partial(_ssm_kernel, n_chunks=nc, bsz=bsz),
        grid=(SSM_GROUPS,),
        in_specs=[grp(rows, SSM_CW), grp(SSM_CW, SSM_CW), grp(SSM_CW, 4 * SSM_STATE),
                  grp(2 * SSM_STATE, SSM_CW), grp(2, 2 * SSM_STATE)],
        out_specs=grp(rows, SSM_CW),
        out_shape=jax.ShapeDtypeStruct((SSM_GROUPS, rows, SSM_CW), BF16),
        scratch_shapes=[pltpu.VMEM((rows, 4 * SSM_STATE), F32), pltpu.VMEM((rows, 2 * SSM_STATE), F32)],
        compiler_params=_params(1),
        name="ssm",
    )(ug, toep, bst, cst, avec)
    y = yg.reshape(SSM_GROUPS, nc, bsz, SSM_CHUNK, SSM_GROUP).transpose(2, 1, 3, 0, 4)
    return y.reshape(bsz * seqlen, SSM_WIDTH)


def _layer_norm(v, g, b):
    mu = jnp.mean(v, axis=-1, keepdims=True)
    vc = v - mu
    var = jnp.mean(vc * vc, axis=-1, keepdims=True)
    return vc * lax.rsqrt(var + LN_EPS) * g + b


def _merge_kernel(o0_ref, o1_ref, o2_ref, l0_ref, l1_ref, l2_ref, ys_ref, g_ref, x_ref,
                  wglu_ref, wsp_ref, wap_ref, wout_ref, lng_ref, lnb_ref, h_ref, o_scr, l_scr):
    tm = x_ref.shape[0]

    def token_order(ref, scr, d):
        if d == 1:
            return ref[0].astype(F32)
        nk = ref.shape[-1] // LANES
        for r in range(d):
            v = ref[r].astype(F32)
            for k in range(nk):
                scr[k, pl.ds(r, tm // d, stride=d), :] = v[:, k * LANES:(k + 1) * LANES]
        return jnp.concatenate([scr[k] for k in range(nk)], axis=-1)

    ls, outs = [], []
    for gi, (o_ref, l_ref) in enumerate(((o0_ref, l0_ref), (o1_ref, l1_ref), (o2_ref, l2_ref))):
        d = DILATION_PATTERNS[gi][1]
        outs.append(token_order(o_ref, o_scr.at[gi], d))
        ls.append(token_order(l_ref, l_scr.at[gi], d))
    mx = jnp.maximum(jnp.maximum(ls[0], ls[1]), ls[2])
    es = [jnp.exp(l - mx) for l in ls]
    num = es[0] * outs[0] + es[1] * outs[1] + es[2] * outs[2]
    y_attn = (num / (es[0] + es[1] + es[2])).astype(BF16)
    glu = jnp.dot(ys_ref[...], wglu_ref[...], preferred_element_type=F32)
    y_ssm = (glu[:, :SSM_WIDTH] * jax.nn.sigmoid(glu[:, SSM_WIDTH:])).astype(BF16)
    pa = jnp.dot(y_ssm, wsp_ref[...], preferred_element_type=F32)
    pb = jnp.dot(y_attn, wap_ref[...], preferred_element_type=F32)
    gated = (g_ref[:, :D_MODEL].astype(F32) * pa + g_ref[:, D_MODEL:].astype(F32) * pb).astype(BF16)
    mix = jnp.dot(gated, wout_ref[...], preferred_element_type=F32)
    h_ref[...] = _layer_norm(ALPHA * x_ref[...] + mix, lng_ref[...], lnb_ref[...])


def _merge(outs, lses, ys, gates, x2, w_glu, w_sp, w_ap, w_out, ln_g, ln_b, seqlen, tm):
    t = x2.shape[0]
    tiles = seqlen // tm
    row = lambda w: pl.BlockSpec((tm, w), lambda i: (i, 0))
    res_spec = lambda d: pl.BlockSpec((None, d, tm // d, GROUP_WIDTH), lambda i: (i // tiles, 0, i % tiles, 0))
    res_specs = [res_spec(d) for _, d in DILATION_PATTERNS]
    return pl.pallas_call(
        _merge_kernel,
        grid=(t // tm,),
        in_specs=res_specs + res_specs + [row(SSM_WIDTH), row(2 * D_MODEL), row(D_MODEL),
                  _const_spec(w_glu.shape), _const_spec(w_sp.shape), _const_spec(w_ap.shape),
                  _const_spec(w_out.shape), _const_spec((1, D_MODEL)), _const_spec((1, D_MODEL))],
        out_specs=row(D_MODEL),
        out_shape=jax.ShapeDtypeStruct((t, D_MODEL), F32),
        scratch_shapes=[pltpu.VMEM((N_GROUPS, GROUP_WIDTH // LANES, tm, LANES), F32),
                        pltpu.VMEM((N_GROUPS, GROUP_WIDTH // LANES, tm, LANES), F32)],
        compiler_params=_params(1),
        name="merge_ln1",
    )(*outs, *lses, ys, gates, x2, w_glu, w_sp, w_ap, w_out, ln_g, ln_b)


FF_CHUNK = 1024


def _ffn_kernel(h_ref, wup_ref, wdn_ref, lng_ref, lnb_ref, o_ref):
    h = h_ref[...]
    hb = h.astype(BF16)
    acc = ALPHA * h
    for c in range(D_FF // FF_CHUNK):
        lo, hi = c * FF_CHUNK, (c + 1) * FF_CHUNK
        up = jnp.dot(hb, wup_ref[:, lo:hi], preferred_element_type=F32)
        act = jnp.square(jnp.maximum(up, 0.0)).astype(BF16)
        acc = acc + jnp.dot(act, wdn_ref[lo:hi, :], preferred_element_type=F32)
    o_ref[...] = _layer_norm(acc, lng_ref[...], lnb_ref[...])


def _ffn(h1, w_up, w_down, ln_g, ln_b, tm):
    t = h1.shape[0]
    row = pl.BlockSpec((tm, D_MODEL), lambda i: (i, 0))
    return pl.pallas_call(
        _ffn_kernel,
        grid=(t // tm,),
        in_specs=[row, _const_spec(w_up.shape), _const_spec(w_down.shape),
                  _const_spec((1, D_MODEL)), _const_spec((1, D_MODEL))],
        out_specs=row,
        out_shape=jax.ShapeDtypeStruct((t, D_MODEL), F32),
        compiler_params=_params(1),
        name="ffn_ln2",
    )(h1, w_up, w_down, ln_g, ln_b)


def _permute_w_in(w):
    aw = ATTN_WIDTH
    cols = []
    for gi in range(N_GROUPS):
        lo, hi = gi * GROUP_WIDTH, (gi + 1) * GROUP_WIDTH
        cols += [w[:, lo:hi] * (HEAD_DIM ** -0.5), w[:, aw + lo:aw + hi], w[:, 2 * aw + lo:2 * aw + hi]]
    cols.append(w[:, 3 * aw:])
    return jnp.concatenate(cols, axis=1).astype(BF16)


def _layer(h2, bsz, seqlen, l, w_in, b_gate, lambda_re, lambda_im, log_dt, ssm_b_re, ssm_b_im,
           ssm_c_re, ssm_c_im, ssm_d, w_glu, w_ssm_proj, rel_bias, w_attn_proj, w_out,
           ln1_g, ln1_b, w_up, w_down, ln2_g, ln2_b, tm=512, attn_mt=512):
    qkv0, qkv1, qkv2, u, gates = _in_proj(h2, _permute_w_in(w_in[l]), b_gate[l][None, :], bsz, seqlen, tm)
    outs, lses = [], []
    for gi, ((window, dilation), qkv) in enumerate(zip(DILATION_PATTERNS, (qkv0, qkv1, qkv2))):
        rb = rel_bias[:, gi * HEADS_PER_GROUP:(gi + 1) * HEADS_PER_GROUP].astype(F32)
        o, s = _attention_group(qkv, rb, window, dilation, attn_mt)
        outs.append(o)
        lses.append(s)
    ssm_w = _ssm_weights(lambda_re[l], lambda_im[l], log_dt[l], ssm_b_re[l], ssm_b_im[l],
                         ssm_c_re[l], ssm_c_im[l], ssm_d[l])
    ys = _ssm(u, ssm_w, bsz, seqlen)
    h1 = _merge(outs, lses, ys, gates, h2, w_glu[l].astype(BF16), w_ssm_proj[l].astype(BF16),
                w_attn_proj[l].astype(BF16), w_out[l].astype(BF16),
                ln1_g[l][None, :], ln1_b[l][None, :], seqlen, tm)
    return _ffn(h1, w_up[l].astype(BF16), w_down[l].astype(BF16), ln2_g[l][None, :], ln2_b[l][None, :], tm)


def kernel(x, w_in, b_gate, lambda_re, lambda_im, log_dt, ssm_b_re, ssm_b_im, ssm_c_re, ssm_c_im,
           ssm_d, w_glu, w_ssm_proj, rel_bias, w_attn_proj, w_out, ln1_g, ln1_b, w_up, w_down,
           ln2_g, ln2_b):
    bsz, seqlen, d = x.shape
    h = x.reshape(bsz * seqlen, d)
    for l in range(w_in.shape[0]):
        h = _layer(h, bsz, seqlen, l, w_in, b_gate, lambda_re, lambda_im, log_dt, ssm_b_re, ssm_b_im,
                   ssm_c_re, ssm_c_im, ssm_d, w_glu, w_ssm_proj, rel_bias, w_attn_proj, w_out,
                   ln1_g, ln1_b, w_up, w_down, ln2_g, ln2_b)
    return h.reshape(bsz, seqlen, d)
```

```python
import functools
import math

import jax
import jax.numpy as jnp
from jax import lax
from jax.experimental import pallas as pl
from jax.experimental.pallas import tpu as pltpu

F32 = jnp.float32
BF16 = jnp.bfloat16

D_MODEL = 1024
HEAD_DIM = 64
HEADS_PER_GROUP = 4
GROUP_WIDTH = HEADS_PER_GROUP * HEAD_DIM
DILATION_PATTERNS = ((128, 1), (512, 4), (2048, 16))
N_GROUPS = len(DILATION_PATTERNS)
ATTN_WIDTH = N_GROUPS * GROUP_WIDTH
N_BUCKETS = 32
MAX_DISTANCE = 2048
SSM_WIDTH = 512
SSM_GROUP = 16
SSM_GROUPS = 32
SSM_STATE = 64
D_FF = 4 * D_MODEL
DEPTH = 1
ALPHA = (2.0 * DEPTH) ** 0.25
LN_EPS = 1e-5
NEG_INF = -1e30

ATTN_BLOCK = 128
SSM_CHUNK = 16
SSM_OCT = 8
LANES = 128
VMEM_LIMIT_BYTES =56 * 1024 * 1024


def _params(n_axes):
    return pltpu.CompilerParams(dimension_semantics=("arbitrary",) * n_axes,
                                vmem_limit_bytes=VMEM_LIMIT_BYTES)


def _const_spec(shape):
    nd = len(shape)
    return pl.BlockSpec(shape, lambda *_: (0,) * nd)


def _in_proj_kernel(x_ref, w_ref, bg_ref, qkv0_ref, qkv1_ref, qkv2_ref, u_ref, g_ref, scr):
    xb = x_ref[...].astype(BF16)
    tm = xb.shape[0]

    def mm(lo, hi):
        return jnp.dot(xb, w_ref[:, lo:hi], preferred_element_type=F32)

    def emit(ref, res, d, col0):
        width = res.shape[1]
        if d == 1:
            ref[0, :, col0:col0 + width] = res.astype(BF16)
            return
        for k in range(width // LANES):
            scr[k] = res[:, k * LANES:(k + 1) * LANES]
        for r in range(d):
            for k in range(width // LANES):
                col = col0 + k * LANES
                ref[r, :, col:col + LANES] = scr[k, pl.ds(r, tm // d, stride=d), :].astype(BF16)

    gw3 = 3 * GROUP_WIDTH
    for gi, ref in enumerate((qkv0_ref, qkv1_ref, qkv2_ref)):
        for c in range(3):
            lo = gi * gw3 + c * GROUP_WIDTH
            emit(ref, mm(lo, lo + GROUP_WIDTH), DILATION_PATTERNS[gi][1], c * GROUP_WIDTH)
    base = N_GROUPS * gw3
    emit(u_ref, mm(base, base + SSM_WIDTH), SSM_CHUNK, 0)
    base += SSM_WIDTH
    for c in range(4):
        lo, hi = c * 512, (c + 1) * 512
        z = mm(base + lo, base + hi) + bg_ref[:, lo:hi]
        g_ref[:, lo:hi] = jax.nn.sigmoid(z).astype(BF16)


def _in_proj(x2, w_perm, b_gate, bsz, seqlen, tm):
    t = x2.shape[0]
    n_in = w_perm.shape[1]
    tiles = seqlen // tm
    row = lambda w: pl.BlockSpec((tm, w), lambda i: (i, 0))
    w3 = 3 * GROUP_WIDTH
    dils = [d for _, d in DILATION_PATTERNS]
    res_spec = lambda d, w: pl.BlockSpec((None, d, tm // d, w), lambda i: (i // tiles, 0, i % tiles, 0))
    return pl.pallas_call(
        _in_proj_kernel,
        grid=(t // tm,),
        in_specs=[row(D_MODEL), _const_spec((D_MODEL, n_in)), _const_spec((1, 2 * D_MODEL))],
        out_specs=[res_spec(d, w3) for d in dils] + [res_spec(SSM_CHUNK, SSM_WIDTH), row(2 * D_MODEL)],
        out_shape=[jax.ShapeDtypeStruct((bsz, d, seqlen // d, w3), BF16) for d in dils]
        + [jax.ShapeDtypeStruct((bsz, SSM_CHUNK, seqlen // SSM_CHUNK, SSM_WIDTH), BF16),
           jax.ShapeDtypeStruct((t, 2 * D_MODEL), BF16)],
        scratch_shapes=[pltpu.VMEM((SSM_WIDTH // LANES, tm, LANES), F32)],
        compiler_params=_params(1),
        name="in_proj",
    )(x2, w_perm, b_gate)


def _attn_kernel(bucket_ref, relb_ref, cur_ref, prev_ref, out_ref, lse_ref, bias_scr, kv_scr, *, nq):
    blk = ATTN_BLOCK
    first = (pl.program_id(0) == 0) & (pl.program_id(1) == 0) & (pl.program_id(2) == 0)

    @pl.when(first)
    def _build_bias():
        bucket = bucket_ref[...]
        for h in range(HEADS_PER_GROUP):
            acc = jnp.full(bucket.shape, NEG_INF, F32)
            for bkt in range(N_BUCKETS):
                acc = jnp.where(bucket == bkt, relb_ref[bkt, h], acc)
            bias_scr[h * blk:(h + 1) * blk, :] = acc

    kv_scr[0:blk, :] = prev_ref[:, GROUP_WIDTH:3 * GROUP_WIDTH]
    kv_scr[blk:, :] = cur_ref[:, GROUP_WIDTH:3 * GROUP_WIDTH]

    lane_head = lax.broadcasted_iota(jnp.int32, (blk, GROUP_WIDTH), 1) // HEAD_DIM
    has_prev = pl.program_id(2) > 0
    for j in range(nq):
        q = cur_ref[j * blk:(j + 1) * blk, 0:GROUP_WIDTH]
        qs = jnp.concatenate(
            [jnp.where(lane_head == h, q, jnp.zeros_like(q)) for h in range(HEADS_PER_GROUP)], axis=0)
        k2 = kv_scr[j * blk:(j + 2) * blk, 0:GROUP_WIDTH]
        v2 = kv_scr[j * blk:(j + 2) * blk, GROUP_WIDTH:2 * GROUP_WIDTH]
        s = lax.dot_general(qs, k2, (((1,), (1,)), ((), ())), preferred_element_type=F32)
        s = s + bias_scr[...]
        if j == 0:
            col = lax.broadcasted_iota(jnp.int32, s.shape, 1)
            s = jnp.where((col >= blk) | has_prev, s, NEG_INF)
        m = jnp.max(s, axis=-1, keepdims=True)
        p = jnp.exp(s - m)
        l = jnp.sum(p, axis=-1, keepdims=True)
        pv = jnp.dot(p.astype(BF16), v2, preferred_element_type=F32)
        pv = pv * (1.0 / l)
        lse = m + jnp.log(l)
        o = pv[0:blk]
        e = jnp.broadcast_to(lse[0:blk], (blk, GROUP_WIDTH))
        for h in range(1, HEADS_PER_GROUP):
            sel = lane_head == h
            o = jnp.where(sel, pv[h * blk:(h + 1) * blk], o)
            e = jnp.where(sel, lse[h * blk:(h + 1) * blk], e)
        out_ref[j * blk:(j + 1) * blk, :] = o.astype(BF16)
        lse_ref[j * blk:(j + 1) * blk, :] = e


def _t5_bucket(dist):
    max_exact = N_BUCKETS // 2
    d = jnp.maximum(dist, 1).astype(F32)
    large = max_exact + (jnp.log(d / max_exact) / math.log(MAX_DISTANCE / max_exact)
                         * (N_BUCKETS - max_exact)).astype(jnp.int32)
    large = jnp.minimum(large, N_BUCKETS - 1)
    return jnp.where(dist < max_exact, dist, large)


def _bucket_table(window, dilation):
    blk = ATTN_BLOCK
    span = window // dilation
    rel = jnp.arange(blk)[:, None] + blk - jnp.arange(2 * blk)[None, :]
    valid = (rel >= 0) & (rel <= span)
    return jnp.where(valid, _t5_bucket(jnp.maximum(rel, 0) * dilation), -1).astype(jnp.int32)


def _attention_group(qkv, rel_bias_g, window, dilation, mt):
    assert window // dilation == ATTN_BLOCK
    bsz, _, n, w3 = qkv.shape
    mt = min(mt, n)
    nq = mt // ATTN_BLOCK
    cur = pl.BlockSpec((None, None, mt, w3), lambda b, r, i: (b, r, i, 0))
    prev = pl.BlockSpec((None, None, ATTN_BLOCK, w3), lambda b, r, i: (b, r, jnp.maximum(i * nq - 1, 0), 0))
    ospec = pl.BlockSpec((None, None, mt, GROUP_WIDTH), lambda b, r, i: (b, r, i, 0))
    return pl.pallas_call(
        functools.partial(_attn_kernel, nq=nq),
        grid=(bsz, dilation, n // mt),
        in_specs=[_const_spec((ATTN_BLOCK, 2 * ATTN_BLOCK)),
                  pl.BlockSpec(memory_space=pltpu.SMEM), cur, prev],
        out_specs=[ospec, ospec],
        out_shape=[jax.ShapeDtypeStruct((bsz, dilation, n, GROUP_WIDTH), BF16),
                   jax.ShapeDtypeStruct((bsz, dilation, n, GROUP_WIDTH), F32)],
        scratch_shapes=[pltpu.VMEM((HEADS_PER_GROUP * ATTN_BLOCK, 2 * ATTN_BLOCK), F32),
                        pltpu.VMEM((mt + ATTN_BLOCK, 2 * GROUP_WIDTH), BF16)],
        compiler_params=_params(3),
        name=f"attn_d{dilation}",
    )(_bucket_table(window, dilation), rel_bias_g, qkv, qkv)


def _ssm_weights(lambda_re, lambda_im, log_dt, b_re, b_im, c_re, c_im, d_skip):
    f32 = F32
    cs = SSM_CHUNK
    dt = jnp.exp(log_dt.astype(f32))[:, None]
    lr, li = lambda_re.astype(f32), lambda_im.astype(f32)
    mag = jnp.exp(lr * dt)
    ab_re, ab_im = mag * jnp.cos(li * dt), mag * jnp.sin(li * dt)
    den = lr * lr + li * li
    nr = ab_re - 1.0
    k_re = (nr * lr + ab_im * li) / den
    k_im = (ab_im * lr - nr * li) / den
    br, bi = b_re.astype(f32), b_im.astype(f32)
    bb_re = k_re[..., None] * br - k_im[..., None] * bi
    bb_im = k_re[..., None] * bi + k_im[..., None] * br
    j = jnp.arange(cs + 1, dtype=f32)[None, :, None]
    pmag = jnp.exp(lr[:, None, :] * dt[:, :, None] * j)
    ang = li[:, None, :] * dt[:, :, None] * j
    p_re, p_im = pmag * jnp.cos(ang), pmag * jnp.sin(ang)
    cr, ci = c_re.astype(f32), c_im.astype(f32)
    hi = lax.Precision.HIGHEST
    cp_re = cr[:, None] * p_re[:, :, None, :] - ci[:, None] * p_im[:, :, None, :]
    cp_im = cr[:, None] * p_im[:, :, None, :] + ci[:, None] * p_re[:, :, None, :]
    kern = (jnp.einsum('gjhp,gpk->gjkh', cp_re, bb_re, precision=hi)
            - jnp.einsum('gjhp,gpk->gjkh', cp_im, bb_im, precision=hi))
    s_idx = jnp.arange(cs)[:, None]
    t_idx = jnp.arange(cs)[None, :]
    lag = t_idx - s_idx
    toep = jnp.where((lag >= 0)[None, :, :, None, None],
                     kern[:, jnp.clip(lag, 0, cs)], 0.0)
    eye_h = jnp.eye(SSM_GROUP, dtype=f32)
    toep = toep + ((lag == 0)[None, :, :, None, None]
                   * (eye_h[None] * d_skip.astype(f32)[:, None, :])[:, None, None])
    no = SSM_GROUPS // SSM_OCT
    eye_o = jnp.eye(SSM_OCT, dtype=f32)
    npair = cs // 2
    toep8 = toep.reshape(no, SSM_OCT, npair, 2, npair, 2, SSM_GROUP, SSM_GROUP)
    pairs = [(sp, tp) for tp in range(npair) for sp in range(tp + 1)]
    sp_idx = jnp.array([p[0] for p in pairs])
    tp_idx = jnp.array([p[1] for p in pairs])
    sel = toep8[:, :, sp_idx, :, tp_idx]
    sel = sel.transpose(1, 0, 3, 2, 5, 4, 6)
    wt = sel[:, :, :, :, :, :, None, :] * eye_o[None, None, None, :, None, None, :, None]
    wt = wt.reshape(no, len(pairs), 2 * LANES, 2 * LANES)
    pe_re, pe_im = p_re[:, cs - 1 - jnp.arange(cs)], p_im[:, cs - 1 - jnp.arange(cs)]
    bbt_re, bbt_im = bb_re.transpose(0, 2, 1)[:, None], bb_im.transpose(0, 2, 1)[:, None]
    st_re = pe_re[:, :, None, :] * bbt_re - pe_im[:, :, None, :] * bbt_im
    st_im = pe_re[:, :, None, :] * bbt_im + pe_im[:, :, None, :] * bbt_re
    st = jnp.stack([st_re, st_im], axis=3).reshape(no, SSM_OCT, cs, SSM_GROUP, 2, SSM_STATE)
    st = st.transpose(0, 2, 1, 3, 4, 5)
    bst = st[:, :, :, :, :, None, :] * eye_o[None, None, :, None, None, :, None]
    bst = bst.reshape(no, cs * LANES, 2 * SSM_OCT * SSM_STATE)
    ro = jnp.stack([cp_re[:, 1:], -cp_im[:, 1:]], axis=1)
    ro = ro.reshape(no, SSM_OCT, 2, cs, SSM_GROUP, SSM_STATE).transpose(0, 2, 1, 5, 3, 4)
    cst = ro[:, :, :, :, :, None, :] * eye_o[None, None, :, None, None, :, None]
    cst = cst.reshape(no, 2 * SSM_OCT * SSM_STATE, cs * LANES)
    avec = jnp.stack([p_re[:, cs].reshape(no, SSM_OCT * SSM_STATE),
                      p_im[:, cs].reshape(no, SSM_OCT * SSM_STATE)], axis=1)
    return wt.astype(BF16), bst.astype(BF16), cst.astype(BF16), avec


def _ssm_kernel(u_ref, wt_ref, bst_ref, cst_ref, a_ref, y_ref, zz_scr, hp_scr, *, n_chunks, nb):
    rows = nb * n_chunks
    nk = SSM_OCT * SSM_STATE // LANES
    u_t = [u_ref[:, s].reshape(rows, LANES) for s in range(SSM_CHUNK)]
    z = jnp.dot(jnp.concatenate(u_t, axis=-1), bst_ref[...], preferred_element_type=F32)
    for k in range(2 * nk):
        zz_scr[k] = z[:, k * LANES:(k + 1) * LANES]
    a_re = [jnp.broadcast_to(a_ref[0:1, k * LANES:(k + 1) * LANES], (nb, LANES)) for k in range(nk)]
    a_im = [jnp.broadcast_to(a_ref[1:2, k * LANES:(k + 1) * LANES], (nb, LANES)) for k in range(nk)]

    def step(c, carry):
        h_re, h_im = carry
        rows_c = pl.ds(c, nb, stride=n_chunks)
        new_re, new_im = [], []
        for k in range(nk):
            hp_scr[k, rows_c, :] = h_re[k]
            hp_scr[nk + k, rows_c, :] = h_im[k]
            new_re.append(a_re[k] * h_re[k] - a_im[k] * h_im[k] + zz_scr[k, rows_c, :])
            new_im.append(a_re[k] * h_im[k] + a_im[k] * h_re[k] + zz_scr[nk + k, rows_c, :])
        return tuple(new_re), tuple(new_im)

    zero = tuple(jnp.zeros((nb, LANES), F32) for _ in range(nk))
    lax.fori_loop(0, n_chunks, step, (zero, zero), unroll=4)
    hp = jnp.concatenate([hp_scr[k] for k in range(2 * nk)], axis=-1).astype(BF16)
    tw = 2 * LANES
    for tp in range(SSM_CHUNK // 2):
        base = tp * (tp + 1) // 2
        y = jnp.dot(hp, cst_ref[:, tp * tw:(tp + 1) * tw], preferred_element_type=F32)
        y = y + jnp.dot(jnp.concatenate(u_t[:2 * (tp + 1)], axis=-1),
                        wt_ref[base:base + tp + 1].reshape((tp + 1) * tw, tw), preferred_element_type=F32)
        y = jax.nn.gelu(y).astype(BF16)
        y_ref[:, 2 * tp] = y[:, :LANES].reshape(nb, n_chunks, LANES)
        y_ref[:, 2 * tp + 1] = y[:, LANES:].reshape(nb, n_chunks, LANES)


def _ssm(u16, weights, nb):
    wt, bst, cst, avec = weights
    bsz, cs, nc, width = u16.shape
    oct_spec = lambda a: pl.BlockSpec((None,) + a.shape[1:], lambda o, b: (o,) + (0,) * (a.ndim - 1))
    io_spec = pl.BlockSpec((nb, cs, nc, LANES), lambda o, b: (b, 0, 0, o))
    n_tiles = 2 * SSM_OCT * SSM_STATE // LANES
    return pl.pallas_call(
        functools.partial(_ssm_kernel, n_chunks=nc, nb=nb),
        grid=(width // LANES, bsz // nb),
        in_specs=[io_spec, oct_spec(wt), oct_spec(bst), oct_spec(cst), oct_spec(avec)],
        out_specs=io_spec,
        out_shape=jax.ShapeDtypeStruct(u16.shape, BF16),
        scratch_shapes=[pltpu.VMEM((n_tiles, nb * nc, LANES), F32), pltpu.VMEM((n_tiles, nb * nc, LANES), F32)],
        compiler_params=_params(2),
        name="ssm",
    )(u16, wt, bst, cst, avec)


def _layer_norm(v, g, b):
    mu = jnp.mean(v, axis=-1, keepdims=True)
    vc = v - mu
    var = jnp.mean(vc * vc, axis=-1, keepdims=True)
    return vc * lax.rsqrt(var + LN_EPS) * g + b


def _merge_kernel(o0_ref, o1_ref, o2_ref, l0_ref, l1_ref, l2_ref, ys_ref, g_ref, x_ref,
                  wglu_ref, wsp_ref, wap_ref, wout_ref, lng_ref, lnb_ref, h_ref, o_scr, l_scr, y_scr):
    tm = x_ref.shape[0]

    def token_order(ref, scr, d):
        if d == 1:
            return ref[0].astype(F32)
        nk = ref.shape[-1] // LANES
        for r in range(d):
            v = ref[r].astype(F32)
            for k in range(nk):
                scr[k, pl.ds(r, tm // d, stride=d), :] = v[:, k * LANES:(k + 1) * LANES]
        return jnp.concatenate([scr[k] for k in range(nk)], axis=-1)

    ls, outs = [], []
    for gi, (o_ref, l_ref) in enumerate(((o0_ref, l0_ref), (o1_ref, l1_ref), (o2_ref, l2_ref))):
        d = DILATION_PATTERNS[gi][1]
        outs.append(token_order(o_ref, o_scr.at[gi], d))
        ls.append(token_order(l_ref, l_scr.at[gi], d))
    mx = jnp.maximum(jnp.maximum(ls[0], ls[1]), ls[2])
    es = [jnp.exp(l - mx) for l in ls]
    num = es[0] * outs[0] + es[1] * outs[1] + es[2] * outs[2]
    y_attn = (num / (es[0] + es[1] + es[2])).astype(BF16)
    ys = token_order(ys_ref, y_scr, SSM_CHUNK).astype(BF16)
    glu = jnp.dot(ys, wglu_ref[...], preferred_element_type=F32)
    y_ssm = (glu[:, :SSM_WIDTH] * jax.nn.sigmoid(glu[:, SSM_WIDTH:])).astype(BF16)
    pa = jnp.dot(y_ssm, wsp_ref[...], preferred_element_type=F32)
    pb = jnp.dot(y_attn, wap_ref[...], preferred_element_type=F32)
    gated = (g_ref[:, :D_MODEL].astype(F32) * pa + g_ref[:, D_MODEL:].astype(F32) * pb).astype(BF16)
    mix = jnp.dot(gated, wout_ref[...], preferred_element_type=F32)
    h_ref[...] = _layer_norm(ALPHA * x_ref[...] + mix, lng_ref[...], lnb_ref[...])


def _merge(outs, lses, ys, gates, x2, w_glu, w_sp, w_ap, w_out, ln_g, ln_b, seqlen, tm):
    t = x2.shape[0]
    tiles = seqlen // tm
    row = lambda w: pl.BlockSpec((tm, w), lambda i: (i, 0))
    res_spec = lambda d, w: pl.BlockSpec((None, d, tm // d, w), lambda i: (i // tiles, 0, i % tiles, 0))
    res_specs = [res_spec(d, GROUP_WIDTH) for _, d in DILATION_PATTERNS]
    return pl.pallas_call(
        _merge_kernel,
        grid=(t // tm,),
        in_specs=res_specs + res_specs + [res_spec(SSM_CHUNK, SSM_WIDTH), row(2 * D_MODEL), row(D_MODEL),
                  _const_spec(w_glu.shape), _const_spec(w_sp.shape), _const_spec(w_ap.shape),
                  _const_spec(w_out.shape), _const_spec((1, D_MODEL)), _const_spec((1, D_MODEL))],
        out_specs=row(D_MODEL),
        out_shape=jax.ShapeDtypeStruct((t, D_MODEL), F32),
        scratch_shapes=[pltpu.VMEM((N_GROUPS, GROUP_WIDTH // LANES, tm, LANES), F32),
                        pltpu.VMEM((N_GROUPS, GROUP_WIDTH // LANES, tm, LANES), F32),
                        pltpu.VMEM((SSM_WIDTH // LANES, tm, LANES), F32)],
        compiler_params=_params(1),
        name="merge_ln1",
    )(*outs, *lses, ys, gates, x2, w_glu, w_sp, w_ap, w_out, ln_g, ln_b)


FF_CHUNK = 1024


def _ffn_kernel(h_ref, wup_ref, wdn_ref, lng_ref, lnb_ref, o_ref):
    h = h_ref[...]
    hb = h.astype(BF16)
    acc = ALPHA * h
    for c in range(D_FF // FF_CHUNK):
        lo, hi = c * FF_CHUNK, (c + 1) * FF_CHUNK
        up = jnp.dot(hb, wup_ref[:, lo:hi], preferred_element_type=F32)
        act = jnp.square(jnp.maximum(up, 0.0)).astype(BF16)
        acc = acc + jnp.dot(act, wdn_ref[lo:hi, :], preferred_element_type=F32)
    o_ref[...] = _layer_norm(acc, lng_ref[...], lnb_ref[...])


def _ffn(h1, w_up, w_down, ln_g, ln_b, tm):
    t = h1.shape[0]
    row = pl.BlockSpec((tm, D_MODEL), lambda i: (i, 0))
    return pl.pallas_call(
        _ffn_kernel,
        grid=(t // tm,),
        in_specs=[row, _const_spec(w_up.shape), _const_spec(w_down.shape),
                  _const_spec((1, D_MODEL)), _const_spec((1, D_MODEL))],
        out_specs=row,
        out_shape=jax.ShapeDtypeStruct((t, D_MODEL), F32),
        compiler_params=_params(1),
        name="ffn_ln2",
    )(h1, w_up, w_down, ln_g, ln_b)


def _permute_w_in(w):
    aw = ATTN_WIDTH
    cols = []
    for gi in range(N_GROUPS):
        lo, hi = gi * GROUP_WIDTH, (gi + 1) * GROUP_WIDTH
        cols += [w[:, lo:hi] * (HEAD_DIM ** -0.5), w[:, aw + lo:aw + hi], w[:, 2 * aw + lo:2 * aw + hi]]
    cols.append(w[:, 3 * aw:])
    return jnp.concatenate(cols, axis=1).astype(BF16)


def _layer(h2, bsz, seqlen, l, w_in, b_gate, lambda_re, lambda_im, log_dt, ssm_b_re, ssm_b_im,
           ssm_c_re, ssm_c_im, ssm_d, w_glu, w_ssm_proj, rel_bias, w_attn_proj, w_out,
           ln1_g, ln1_b, w_up, w_down, ln2_g, ln2_b, tm=512, attn_mt=512):
    qkv0, qkv1, qkv2, u, gates = _in_proj(h2, _permute_w_in(w_in[l]), b_gate[l][None, :], bsz, seqlen, tm)
    outs, lses = [], []
    for gi, ((window, dilation), qkv) in enumerate(zip(DILATION_PATTERNS, (qkv0, qkv1, qkv2))):
        rb = rel_bias[:, gi * HEADS_PER_GROUP:(gi + 1) * HEADS_PER_GROUP].astype(F32)
        o, s = _attention_group(qkv, rb, window, dilation, attn_mt)
        outs.append(o)
        lses.append(s)
    ssm_w = _ssm_weights(lambda_re[l], lambda_im[l], log_dt[l], ssm_b_re[l], ssm_b_im[l],
                         ssm_c_re[l], ssm_c_im[l], ssm_d[l])
    ys = _ssm(u, ssm_w, nb=2)
    h1 = _merge(outs, lses, ys, gates, h2, w_glu[l].astype(BF16), w_ssm_proj[l].astype(BF16),
                w_attn_proj[l].astype(BF16), w_out[l].astype(BF16),
                ln1_g[l][None, :], ln1_b[l][None, :], seqlen, tm)
    return _ffn(h1, w_up[l].astype(BF16), w_down[l].astype(BF16), ln2_g[l][None, :], ln2_b[l][None, :], tm)


def kernel(x, w_in, b_gate, lambda_re, lambda_im, log_dt, ssm_b_re, ssm_b_im, ssm_c_re, ssm_c_im,
           ssm_d, w_glu, w_ssm_proj, rel_bias, w_attn_proj, w_out, ln1_g, ln1_b, w_up, w_down,
           ln2_g, ln2_b):
    bsz, seqlen, d = x.shape
    h = x.reshape(bsz * seqlen, d)
    for l in range(w_in.shape[0]):
        h = _layer(h, bsz, seqlen, l, w_in, b_gate, lambda_re, lambda_im, log_dt, ssm_b_re, ssm_b_im,
                   ssm_c_re, ssm_c_im, ssm_d, w_glu, w_ssm_proj, rel_bias, w_attn_proj, w_out,
                   ln1_g, ln1_b, w_up, w_down, ln2_g, ln2_b)
    return h.reshape(bsz, seqlen, d)
```

```python
import functools
import math

import jax
import jax.numpy as jnp
from jax import lax
from jax.experimental import pallas as pl
from jax.experimental.pallas import tpu as pltpu

F32 = jnp.float32
BF16 = jnp.bfloat16

D_MODEL = 1024
HEAD_DIM = 64
HEADS_PER_GROUP = 4
GROUP_WIDTH = HEADS_PER_GROUP * HEAD_DIM
DILATION_PATTERNS = ((128, 1), (512, 4), (2048, 16))
N_GROUPS = len(DILATION_PATTERNS)
ATTN_WIDTH = N_GROUPS * GROUP_WIDTH
N_BUCKETS = 32
MAX_DISTANCE = 2048
SSM_WIDTH = 512
SSM_GROUP = 16
SSM_GROUPS = 32
SSM_STATE = 64
D_FF = 4 * D_MODEL
DEPTH = 1
ALPHA = (2.0 * DEPTH) ** 0.25
LN_EPS = 1e-5
NEG_INF = -1e30

ATTN_BLOCK = 128
SSM_CHUNK = 16
SSM_OCT = 8
LANES = 128
VMEM_LIMIT_BYTES =56 * 1024 * 1024


def _params(n_axes):
    return pltpu.CompilerParams(dimension_semantics=("arbitrary",) * n_axes,
                                vmem_limit_bytes=VMEM_LIMIT_BYTES)


def _const_spec(shape):
    nd = len(shape)
    return pl.BlockSpec(shape, lambda *_: (0,) * nd)


def _in_proj_kernel(x_ref, w_ref, bg_ref, qkv0_ref, qkv1_ref, qkv2_ref, u_ref, g_ref, scr):
    xb = x_ref[...].astype(BF16)
    tm = xb.shape[0]

    def mm(lo, hi):
        return jnp.dot(xb, w_ref[:, lo:hi], preferred_element_type=F32)

    def emit(ref, res, d, col0):
        width = res.shape[1]
        if d == 1:
            ref[0, :, col0:col0 + width] = res.astype(BF16)
            return
        for k in range(width // LANES):
            scr[k] = res[:, k * LANES:(k + 1) * LANES]
        for r in range(d):
            for k in range(width // LANES):
                col = col0 + k * LANES
                ref[r, :, col:col + LANES] = scr[k, pl.ds(r, tm // d, stride=d), :].astype(BF16)

    gw3 = 3 * GROUP_WIDTH
    for gi, ref in enumerate((qkv0_ref, qkv1_ref, qkv2_ref)):
        for c in range(3):
            lo = gi * gw3 + c * GROUP_WIDTH
            emit(ref, mm(lo, lo + GROUP_WIDTH), DILATION_PATTERNS[gi][1], c * GROUP_WIDTH)
    base = N_GROUPS * gw3
    emit(u_ref, mm(base, base + SSM_WIDTH), SSM_CHUNK, 0)
    base += SSM_WIDTH
    for c in range(4):
        lo, hi = c * 512, (c + 1) * 512
        z = mm(base + lo, base + hi) + bg_ref[:, lo:hi]
        g_ref[:, lo:hi] = jax.nn.sigmoid(z).astype(BF16)


def _in_proj(x2, w_perm, b_gate, bsz, seqlen, tm):
    t = x2.shape[0]
    n_in = w_perm.shape[1]
    tiles = seqlen // tm
    row = lambda w: pl.BlockSpec((tm, w), lambda i: (i, 0))
    w3 = 3 * GROUP_WIDTH
    dils = [d for _, d in DILATION_PATTERNS]
    res_spec = lambda d, w: pl.BlockSpec((None, d, tm // d, w), lambda i: (i // tiles, 0, i % tiles, 0))
    return pl.pallas_call(
        _in_proj_kernel,
        grid=(t // tm,),
        in_specs=[row(D_MODEL), _const_spec((D_MODEL, n_in)), _const_spec((1, 2 * D_MODEL))],
        out_specs=[res_spec(d, w3) for d in dils] + [res_spec(SSM_CHUNK, SSM_WIDTH), row(2 * D_MODEL)],
        out_shape=[jax.ShapeDtypeStruct((bsz, d, seqlen // d, w3), BF16) for d in dils]
        + [jax.ShapeDtypeStruct((bsz, SSM_CHUNK, seqlen // SSM_CHUNK, SSM_WIDTH), BF16),
           jax.ShapeDtypeStruct((t, 2 * D_MODEL), BF16)],
        scratch_shapes=[pltpu.VMEM((SSM_WIDTH // LANES, tm, LANES), F32)],
        compiler_params=_params(1),
        name="in_proj",
    )(x2, w_perm, b_gate)


def _attn_kernel(bucket_ref, relb_ref, cur_ref, prev_ref, out_ref, lse_ref, bias_scr, kv_scr, *, nq):
    blk = ATTN_BLOCK
    first = (pl.program_id(0) == 0) & (pl.program_id(1) == 0) & (pl.program_id(2) == 0)

    @pl.when(first)
    def _build_bias():
        bucket = bucket_ref[...]
        for h in range(HEADS_PER_GROUP):
            acc = jnp.full(bucket.shape, NEG_INF, F32)
            for bkt in range(N_BUCKETS):
                acc = jnp.where(bucket == bkt, relb_ref[bkt, h], acc)
            bias_scr[h * blk:(h + 1) * blk, :] = acc

    kv_scr[0:blk, :] = prev_ref[:, GROUP_WIDTH:3 * GROUP_WIDTH]
    kv_scr[blk:, :] = cur_ref[:, GROUP_WIDTH:3 * GROUP_WIDTH]

    lane_head = lax.broadcasted_iota(jnp.int32, (blk, GROUP_WIDTH), 1) // HEAD_DIM
    has_prev = pl.program_id(2) > 0
    for j in range(nq):
        q = cur_ref[j * blk:(j + 1) * blk, 0:GROUP_WIDTH]
        qs = jnp.concatenate(
            [jnp.where(lane_head == h, q, jnp.zeros_like(q)) for h in range(HEADS_PER_GROUP)], axis=0)
        k2 = kv_scr[j * blk:(j + 2) * blk, 0:GROUP_WIDTH]
        v2 = kv_scr[j * blk:(j + 2) * blk, GROUP_WIDTH:2 * GROUP_WIDTH]
        s = lax.dot_general(qs, k2, (((1,), (1,)), ((), ())), preferred_element_type=F32)
        s = s + bias_scr[...]
        if j == 0:
            col = lax.broadcasted_iota(jnp.int32, s.shape, 1)
            s = jnp.where((col >= blk) | has_prev, s, NEG_INF)
        m = jnp.max(s, axis=-1, keepdims=True)
        p = jnp.exp(s - m)
        l = jnp.sum(p, axis=-1, keepdims=True)
        pv = jnp.dot(p.astype(BF16), v2, preferred_element_type=F32)
        pv = pv * (1.0 / l)
        lse = m + jnp.log(l)
        o = pv[0:blk]
        e = jnp.broadcast_to(lse[0:blk], (blk, GROUP_WIDTH))
        for h in range(1, HEADS_PER_GROUP):
            sel = lane_head == h
            o = jnp.where(sel, pv[h * blk:(h + 1) * blk], o)
            e = jnp.where(sel, lse[h * blk:(h + 1) * blk], e)
        out_ref[j * blk:(j + 1) * blk, :] = o.astype(BF16)
        lse_ref[j * blk:(j + 1) * blk, :] = e


def _t5_bucket(dist):
    max_exact = N_BUCKETS // 2
    d = jnp.maximum(dist, 1).astype(F32)
    large = max_exact + (jnp.log(d / max_exact) / math.log(MAX_DISTANCE / max_exact)
                         * (N_BUCKETS - max_exact)).astype(jnp.int32)
    large = jnp.minimum(large, N_BUCKETS - 1)
    return jnp.where(dist < max_exact, dist, large)


def _bucket_table(window, dilation):
    blk = ATTN_BLOCK
    span = window // dilation
    rel = jnp.arange(blk)[:, None] + blk - jnp.arange(2 * blk)[None, :]
    valid = (rel >= 0) & (rel <= span)
    return jnp.where(valid, _t5_bucket(jnp.maximum(rel, 0) * dilation), -1).astype(jnp.int32)


def _attention_group(qkv, rel_bias_g, window, dilation, mt):
    assert window // dilation == ATTN_BLOCK
    bsz, _, n, w3 = qkv.shape
    mt = min(mt, n)
    nq = mt // ATTN_BLOCK
    cur = pl.BlockSpec((None, None, mt, w3), lambda b, r, i: (b, r, i, 0))
    prev = pl.BlockSpec((None, None, ATTN_BLOCK, w3), lambda b, r, i: (b, r, jnp.maximum(i * nq - 1, 0), 0))
    ospec = pl.BlockSpec((None, None, mt, GROUP_WIDTH), lambda b, r, i: (b, r, i, 0))
    return pl.pallas_call(
        functools.partial(_attn_kernel, nq=nq),
        grid=(bsz, dilation, n // mt),
        in_specs=[_const_spec((ATTN_BLOCK, 2 * ATTN_BLOCK)),
                  pl.BlockSpec(memory_space=pltpu.SMEM), cur, prev],
        out_specs=[ospec, ospec],
        out_shape=[jax.ShapeDtypeStruct((bsz, dilation, n, GROUP_WIDTH), BF16),
                   jax.ShapeDtypeStruct((bsz, dilation, n, GROUP_WIDTH), F32)],
        scratch_shapes=[pltpu.VMEM((HEADS_PER_GROUP * ATTN_BLOCK, 2 * ATTN_BLOCK), F32),
                        pltpu.VMEM((mt + ATTN_BLOCK, 2 * GROUP_WIDTH), BF16)],
        compiler_params=_params(3),
        name=f"attn_d{dilation}",
    )(_bucket_table(window, dilation), rel_bias_g, qkv, qkv)


def _ssm_weights(lambda_re, lambda_im, log_dt, b_re, b_im, c_re, c_im, d_skip):
    f32 = F32
    cs = SSM_CHUNK
    dt = jnp.exp(log_dt.astype(f32))[:, None]
    lr, li = lambda_re.astype(f32), lambda_im.astype(f32)
    mag = jnp.exp(lr * dt)
    ab_re, ab_im = mag * jnp.cos(li * dt), mag * jnp.sin(li * dt)
    den = lr * lr + li * li
    nr = ab_re - 1.0
    k_re = (nr * lr + ab_im * li) / den
    k_im = (ab_im * lr - nr * li) / den
    br, bi = b_re.astype(f32), b_im.astype(f32)
    bb_re = k_re[..., None] * br - k_im[..., None] * bi
    bb_im = k_re[..., None] * bi + k_im[..., None] * br
    j = jnp.arange(cs + 1, dtype=f32)[None, :, None]
    pmag = jnp.exp(lr[:, None, :] * dt[:, :, None] * j)
    ang = li[:, None, :] * dt[:, :, None] * j
    p_re, p_im = pmag * jnp.cos(ang), pmag * jnp.sin(ang)
    cr, ci = c_re.astype(f32), c_im.astype(f32)
    hi = lax.Precision.HIGHEST
    cp_re = cr[:, None] * p_re[:, :, None, :] - ci[:, None] * p_im[:, :, None, :]
    cp_im = cr[:, None] * p_im[:, :, None, :] + ci[:, None] * p_re[:, :, None, :]
    kern = (jnp.einsum('gjhp,gpk->gjkh', cp_re, bb_re, precision=hi)
            - jnp.einsum('gjhp,gpk->gjkh', cp_im, bb_im, precision=hi))
    s_idx = jnp.arange(cs)[:, None]
    t_idx = jnp.arange(cs)[None, :]
    lag = t_idx - s_idx
    toep = jnp.where((lag >= 0)[None, :, :, None, None],
                     kern[:, jnp.clip(lag, 0, cs)], 0.0)
    eye_h = jnp.eye(SSM_GROUP, dtype=f32)
    toep = toep + ((lag == 0)[None, :, :, None, None]
                   * (eye_h[None] * d_skip.astype(f32)[:, None, :])[:, None, None])
    no = SSM_GROUPS // SSM_OCT
    toep_c = toep.reshape(no, SSM_OCT, cs, cs, SSM_GROUP, SSM_GROUP).transpose(0, 2, 1, 4, 3, 5)
    toep_c = toep_c.reshape(no, cs * LANES, cs * SSM_GROUP)
    pe_re, pe_im = p_re[:, cs - 1 - jnp.arange(cs)], p_im[:, cs - 1 - jnp.arange(cs)]
    bbt_re, bbt_im = bb_re.transpose(0, 2, 1)[:, None], bb_im.transpose(0, 2, 1)[:, None]
    st_re = pe_re[:, :, None, :] * bbt_re - pe_im[:, :, None, :] * bbt_im
    st_im = pe_re[:, :, None, :] * bbt_im + pe_im[:, :, None, :] * bbt_re
    st = jnp.stack([st_re, st_im], axis=3).reshape(no, SSM_OCT, cs, SSM_GROUP, 2, SSM_STATE)
    bst_c = st.transpose(0, 2, 1, 3, 4, 5).reshape(no, cs * LANES, 2 * SSM_STATE)
    ro = jnp.stack([cp_re[:, 1:], -cp_im[:, 1:]], axis=1)
    ro = ro.reshape(no, SSM_OCT, 2, cs, SSM_GROUP, SSM_STATE).transpose(0, 2, 1, 5, 3, 4)
    cst_c = ro.reshape(no, 2 * SSM_OCT * SSM_STATE, cs * SSM_GROUP)
    avec = jnp.stack([p_re[:, cs].reshape(no, SSM_OCT * SSM_STATE),
                      p_im[:, cs].reshape(no, SSM_OCT * SSM_STATE)], axis=1)
    return toep_c.astype(BF16), bst_c.astype(BF16), cst_c.astype(BF16), avec


def _expand_ssm_weights(toep_ref, bstc_ref, cstc_ref, wt_scr, bst_scr, cst_scr):
    tw = 2 * LANES
    sh_g, sh_n = SSM_GROUP.bit_length() - 1, SSM_STATE.bit_length() - 1
    r = lax.broadcasted_iota(jnp.int32, (tw, tw), 0)
    c = lax.broadcasted_iota(jnp.int32, (tw, tw), 1)
    col_grp = (c >> sh_g) & (SSM_OCT - 1)
    row_grp = (r >> sh_g) & (SSM_OCT - 1)
    for tp in range(SSM_CHUNK // 2):
        src_col = (2 * tp + (c >> (sh_g + 3))) * SSM_GROUP + (c & (SSM_GROUP - 1))
        e = jnp.where(r == src_col, 1.0, 0.0).astype(BF16)
        base = tp * (tp + 1) // 2
        for sp in range(tp + 1):
            x = jnp.dot(toep_ref[sp * tw:(sp + 1) * tw, :], e, preferred_element_type=F32)
            wt_scr[base + sp] = jnp.where(row_grp == col_grp, x, 0.0).astype(BF16)
        for j in range(2 * SSM_OCT * SSM_STATE // tw):
            x = jnp.dot(cstc_ref[j * tw:(j + 1) * tw, :], e, preferred_element_type=F32)
            row_grp_n = ((j * tw + r) >> sh_n) & (SSM_OCT - 1)
            cst_scr[j * tw:(j + 1) * tw, tp * tw:(tp + 1) * tw] = jnp.where(row_grp_n == col_grp, x, 0.0).astype(BF16)
    ns = 2 * SSM_OCT * SSM_STATE
    rb = lax.broadcasted_iota(jnp.int32, (2 * SSM_STATE, ns), 0)
    cb = lax.broadcasted_iota(jnp.int32, (2 * SSM_STATE, ns), 1)
    src_col_b = (cb >> (sh_n + 3)) * SSM_STATE + (cb & (SSM_STATE - 1))
    eb = jnp.where(rb == src_col_b, 1.0, 0.0).astype(BF16)
    rr = lax.broadcasted_iota(jnp.int32, (tw, ns), 0)
    cc = lax.broadcasted_iota(jnp.int32, (tw, ns), 1)
    keep = ((rr >> sh_g) & (SSM_OCT - 1)) == ((cc >> sh_n) & (SSM_OCT - 1))
    for j in range(SSM_CHUNK * LANES // tw):
        x = jnp.dot(bstc_ref[j * tw:(j + 1) * tw, :], eb, preferred_element_type=F32)
        bst_scr[j * tw:(j + 1) * tw, :] = jnp.where(keep, x, 0.0).astype(BF16)


def _ssm_kernel(u_ref, toep_ref, bstc_ref, cstc_ref, a_ref, y_ref, wt_ref, bst_ref, cst_ref, zz_scr, hp_scr,
                *, n_chunks, nb):
    @pl.when(pl.program_id(1) == 0)
    def _new_octet():
        _expand_ssm_weights(toep_ref, bstc_ref, cstc_ref, wt_ref, bst_ref, cst_ref)

    rows = nb * n_chunks
    nk = SSM_OCT * SSM_STATE // LANES
    u_t = [u_ref[:, s].reshape(rows, LANES) for s in range(SSM_CHUNK)]
    z = jnp.dot(jnp.concatenate(u_t, axis=-1), bst_ref[...], preferred_element_type=F32)
    for k in range(2 * nk):
        zz_scr[k] = z[:, k * LANES:(k + 1) * LANES]
    a_re = [jnp.broadcast_to(a_ref[0:1, k * LANES:(k + 1) * LANES], (nb, LANES)) for k in range(nk)]
    a_im = [jnp.broadcast_to(a_ref[1:2, k * LANES:(k + 1) * LANES], (nb, LANES)) for k in range(nk)]

    def step(c, carry):
        h_re, h_im = carry
        rows_c = pl.ds(c, nb, stride=n_chunks)
        new_re, new_im = [], []
        for k in range(nk):
            hp_scr[k, rows_c, :] = h_re[k]
            hp_scr[nk + k, rows_c, :] = h_im[k]
            new_re.append(a_re[k] * h_re[k] - a_im[k] * h_im[k] + zz_scr[k, rows_c, :])
            new_im.append(a_re[k] * h_im[k] + a_im[k] * h_re[k] + zz_scr[nk + k, rows_c, :])
        return tuple(new_re), tuple(new_im)

    zero = tuple(jnp.zeros((nb, LANES), F32) for _ in range(nk))
    lax.fori_loop(0, n_chunks, step, (zero, zero), unroll=4)
    hp = jnp.concatenate([hp_scr[k] for k in range(2 * nk)], axis=-1).astype(BF16)
    tw = 2 * LANES
    for tp in range(SSM_CHUNK // 2):
        base = tp * (tp + 1) // 2
        y = jnp.dot(hp, cst_ref[:, tp * tw:(tp + 1) * tw], preferred_element_type=F32)
        y = y + jnp.dot(jnp.concatenate(u_t[:2 * (tp + 1)], axis=-1),
                        wt_ref[base:base + tp + 1].reshape((tp + 1) * tw, tw), preferred_element_type=F32)
        y = jax.nn.gelu(y).astype(BF16)
        y_ref[:, 2 * tp] = y[:, :LANES].reshape(nb, n_chunks, LANES)
        y_ref[:, 2 * tp + 1] = y[:, LANES:].reshape(nb, n_chunks, LANES)


def _ssm(u16, weights, nb):
    toep_c, bst_c, cst_c, avec = weights
    bsz, cs, nc, width = u16.shape
    oct_spec = lambda a: pl.BlockSpec((None,) + a.shape[1:], lambda o, b: (o,) + (0,) * (a.ndim - 1))
    io_spec = pl.BlockSpec((nb, cs, nc, LANES), lambda o, b: (b, 0, 0, o))
    n_state = 2 * SSM_OCT * SSM_STATE
    n_pairs = (cs // 2) * (cs // 2 + 1) // 2
    return pl.pallas_call(
        functools.partial(_ssm_kernel, n_chunks=nc, nb=nb),
        grid=(width // LANES, bsz // nb),
        in_specs=[io_spec, oct_spec(toep_c), oct_spec(bst_c), oct_spec(cst_c), oct_spec(avec)],
        out_specs=io_spec,
        out_shape=jax.ShapeDtypeStruct(u16.shape, BF16),
        scratch_shapes=[pltpu.VMEM((n_pairs, 2 * LANES, 2 * LANES), BF16),
                        pltpu.VMEM((cs * LANES, n_state), BF16),
                        pltpu.VMEM((n_state, cs * LANES), BF16),
                        pltpu.VMEM((n_state // LANES, nb * nc, LANES), F32),
                        pltpu.VMEM((n_state // LANES, nb * nc, LANES), F32)],
        compiler_params=_params(2),
        name="ssm",
    )(u16, toep_c, bst_c, cst_c, avec)


def _layer_norm(v, g, b):
    mu = jnp.mean(v, axis=-1, keepdims=True)
    vc = v - mu
    var = jnp.mean(vc * vc, axis=-1, keepdims=True)
    return vc * lax.rsqrt(var + LN_EPS) * g + b


def _merge_kernel(o0_ref, o1_ref, o2_ref, l0_ref, l1_ref, l2_ref, ys_ref, g_ref, x_ref,
                  wglu_ref, wsp_ref, wap_ref, wout_ref, lng_ref, lnb_ref, h_ref, o_scr, l_scr, y_scr):
    tm = x_ref.shape[0]

    def token_order(ref, scr, d):
        if d == 1:
            return ref[0].astype(F32)
        nk = ref.shape[-1] // LANES
        for r in range(d):
            v = ref[r].astype(F32)
            for k in range(nk):
                scr[k, pl.ds(r, tm // d, stride=d), :] = v[:, k * LANES:(k + 1) * LANES]
        return jnp.concatenate([scr[k] for k in range(nk)], axis=-1)

    ls, outs = [], []
    for gi, (o_ref, l_ref) in enumerate(((o0_ref, l0_ref), (o1_ref, l1_ref), (o2_ref, l2_ref))):
        d = DILATION_PATTERNS[gi][1]
        outs.append(token_order(o_ref, o_scr.at[gi], d))
        ls.append(token_order(l_ref, l_scr.at[gi], d))
    mx = jnp.maximum(jnp.maximum(ls[0], ls[1]), ls[2])
    es = [jnp.exp(l - mx) for l in ls]
    num = es[0] * outs[0] + es[1] * outs[1] + es[2] * outs[2]
    y_attn = (num / (es[0] + es[1] + es[2])).astype(BF16)
    ys = token_order(ys_ref, y_scr, SSM_CHUNK).astype(BF16)
    glu = jnp.dot(ys, wglu_ref[...], preferred_element_type=F32)
    y_ssm = (glu[:, :SSM_WIDTH] * jax.nn.sigmoid(glu[:, SSM_WIDTH:])).astype(BF16)
    pa = jnp.dot(y_ssm, wsp_ref[...], preferred_element_type=F32)
    pb = jnp.dot(y_attn, wap_ref[...], preferred_element_type=F32)
    gated = (g_ref[:, :D_MODEL].astype(F32) * pa + g_ref[:, D_MODEL:].astype(F32) * pb).astype(BF16)
    mix = jnp.dot(gated, wout_ref[...], preferred_element_type=F32)
    h_ref[...] = _layer_norm(ALPHA * x_ref[...] + mix, lng_ref[...], lnb_ref[...])


def _merge(outs, lses, ys, gates, x2, w_glu, w_sp, w_ap, w_out, ln_g, ln_b, seqlen, tm):
    t = x2.shape[0]
    tiles = seqlen // tm
    row = lambda w: pl.BlockSpec((tm, w), lambda i: (i, 0))
    res_spec = lambda d, w: pl.BlockSpec((None, d, tm // d, w), lambda i: (i // tiles, 0, i % tiles, 0))
    res_specs = [res_spec(d, GROUP_WIDTH) for _, d in DILATION_PATTERNS]
    return pl.pallas_call(
        _merge_kernel,
        grid=(t // tm,),
        in_specs=res_specs + res_specs + [res_spec(SSM_CHUNK, SSM_WIDTH), row(2 * D_MODEL), row(D_MODEL),
                  _const_spec(w_glu.shape), _const_spec(w_sp.shape), _const_spec(w_ap.shape),
                  _const_spec(w_out.shape), _const_spec((1, D_MODEL)), _const_spec((1, D_MODEL))],
        out_specs=row(D_MODEL),
        out_shape=jax.ShapeDtypeStruct((t, D_MODEL), F32),
        scratch_shapes=[pltpu.VMEM((N_GROUPS, GROUP_WIDTH // LANES, tm, LANES), F32),
                        pltpu.VMEM((N_GROUPS, GROUP_WIDTH // LANES, tm, LANES), F32),
                        pltpu.VMEM((SSM_WIDTH // LANES, tm, LANES), F32)],
        compiler_params=_params(1),
        name="merge_ln1",
    )(*outs, *lses, ys, gates, x2, w_glu, w_sp, w_ap, w_out, ln_g, ln_b)


FF_CHUNK = 1024


def _ffn_kernel(h_ref, wup_ref, wdn_ref, lng_ref, lnb_ref, o_ref):
    h = h_ref[...]
    hb = h.astype(BF16)
    acc = ALPHA * h
    for c in range(D_FF // FF_CHUNK):
        lo, hi = c * FF_CHUNK, (c + 1) * FF_CHUNK
        up = jnp.dot(hb, wup_ref[:, lo:hi], preferred_element_type=F32)
        act = jnp.square(jnp.maximum(up, 0.0)).astype(BF16)
        acc = acc + jnp.dot(act, wdn_ref[lo:hi, :], preferred_element_type=F32)
    o_ref[...] = _layer_norm(acc, lng_ref[...], lnb_ref[...])


def _ffn(h1, w_up, w_down, ln_g, ln_b, tm):
    t = h1.shape[0]
    row = pl.BlockSpec((tm, D_MODEL), lambda i: (i, 0))
    return pl.pallas_call(
        _ffn_kernel,
        grid=(t // tm,),
        in_specs=[row, _const_spec(w_up.shape), _const_spec(w_down.shape),
                  _const_spec((1, D_MODEL)), _const_spec((1, D_MODEL))],
        out_specs=row,
        out_shape=jax.ShapeDtypeStruct((t, D_MODEL), F32),
        compiler_params=_params(1),
        name="ffn_ln2",
    )(h1, w_up, w_down, ln_g, ln_b)


def _permute_w_in(w):
    aw = ATTN_WIDTH
    cols = []
    for gi in range(N_GROUPS):
        lo, hi = gi * GROUP_WIDTH, (gi + 1) * GROUP_WIDTH
        cols += [w[:, lo:hi] * (HEAD_DIM ** -0.5), w[:, aw + lo:aw + hi], w[:, 2 * aw + lo:2 * aw + hi]]
    cols.append(w[:, 3 * aw:])
    return jnp.concatenate(cols, axis=1).astype(BF16)


def _layer(h2, bsz, seqlen, l, w_in, b_gate, lambda_re, lambda_im, log_dt, ssm_b_re, ssm_b_im,
           ssm_c_re, ssm_c_im, ssm_d, w_glu, w_ssm_proj, rel_bias, w_attn_proj, w_out,
           ln1_g, ln1_b, w_up, w_down, ln2_g, ln2_b, tm=512, attn_mt=512):
    qkv0, qkv1, qkv2, u, gates = _in_proj(h2, _permute_w_in(w_in[l]), b_gate[l][None, :], bsz, seqlen, tm)
    outs, lses = [], []
    for gi, ((window, dilation), qkv) in enumerate(zip(DILATION_PATTERNS, (qkv0, qkv1, qkv2))):
        rb = rel_bias[:, gi * HEADS_PER_GROUP:(gi + 1) * HEADS_PER_GROUP].astype(F32)
        o, s = _attention_group(qkv, rb, window, dilation, attn_mt)
        outs.append(o)
        lses.append(s)
    ssm_w = _ssm_weights(lambda_re[l], lambda_im[l], log_dt[l], ssm_b_re[l], ssm_b_im[l],
                         ssm_c_re[l], ssm_c_im[l], ssm_d[l])
    ys = _ssm(u, ssm_w, nb=4)
    h1 = _merge(outs, lses, ys, gates, h2, w_glu[l].astype(BF16), w_ssm_proj[l].astype(BF16),
                w_attn_proj[l].astype(BF16), w_out[l].astype(BF16),
                ln1_g[l][None, :], ln1_b[l][None, :], seqlen, tm)
    return _ffn(h1, w_up[l].astype(BF16), w_down[l].astype(BF16), ln2_g[l][None, :], ln2_b[l][None, :], tm)


def kernel(x, w_in, b_gate, lambda_re, lambda_im, log_dt, ssm_b_re, ssm_b_im, ssm_c_re, ssm_c_im,
           ssm_d, w_glu, w_ssm_proj, rel_bias, w_attn_proj, w_out, ln1_g, ln1_b, w_up, w_down,
           ln2_g, ln2_b):
    bsz, seqlen, d = x.shape
    h = x.reshape(bsz * seqlen, d)
    for l in range(w_in.shape[0]):
        h = _layer(h, bsz, seqlen, l, w_in, b_gate, lambda_re, lambda_im, log_dt, ssm_b_re, ssm_b_im,
                   ssm_c_re, ssm_c_im, ssm_d, w_glu, w_ssm_proj, rel_bias, w_attn_proj, w_out,
                   ln1_g, ln1_b, w_up, w_down, ln2_g, ln2_b)
    return h.reshape(bsz, seqlen, d)
```

```python
import functools
import math

import jax
import jax.numpy as jnp
from jax import lax
from jax.experimental import pallas as pl
from jax.experimental.pallas import tpu as pltpu

F32 = jnp.float32
BF16 = jnp.bfloat16

D_MODEL = 1024
HEAD_DIM = 64
HEADS_PER_GROUP = 4
GROUP_WIDTH = HEADS_PER_GROUP * HEAD_DIM
DILATION_PATTERNS = ((128, 1), (512, 4), (2048, 16))
N_GROUPS = len(DILATION_PATTERNS)
ATTN_WIDTH = N_GROUPS * GROUP_WIDTH
N_BUCKETS = 32
MAX_DISTANCE = 2048
SSM_WIDTH = 512
SSM_GROUP = 16
SSM_GROUPS = 32
SSM_STATE = 64
D_FF = 4 * D_MODEL
DEPTH = 1
ALPHA = (2.0 * DEPTH) ** 0.25
LN_EPS = 1e-5
NEG_INF = -1e30

ATTN_BLOCK = 128
SSM_CHUNK = 16
SSM_OCT = 8
LANES = 128
VMEM_LIMIT_BYTES =56 * 1024 * 1024


def _params(n_axes):
    return pltpu.CompilerParams(dimension_semantics=("arbitrary",) * n_axes,
                                vmem_limit_bytes=VMEM_LIMIT_BYTES)


def _const_spec(shape):
    nd = len(shape)
    return pl.BlockSpec(shape, lambda *_: (0,) * nd)


def _in_proj_kernel(x_ref, w_ref, bg_ref, qkv0_ref, qkv1_ref, qkv2_ref, u_ref, g_ref, scr):
    xb = x_ref[...].astype(BF16)
    tm = xb.shape[0]

    def mm(lo, hi):
        return jnp.dot(xb, w_ref[:, lo:hi], preferred_element_type=F32)

    def emit(ref, res, d, col0):
        width = res.shape[1]
        if d == 1:
            ref[0, :, col0:col0 + width] = res.astype(BF16)
            return
        for k in range(width // LANES):
            scr[k] = res[:, k * LANES:(k + 1) * LANES]
        for r in range(d):
            for k in range(width // LANES):
                col = col0 + k * LANES
                ref[r, :, col:col + LANES] = scr[k, pl.ds(r, tm // d, stride=d), :].astype(BF16)

    gw3 = 3 * GROUP_WIDTH
    for gi, ref in enumerate((qkv0_ref, qkv1_ref, qkv2_ref)):
        for c in range(3):
            lo = gi * gw3 + c * GROUP_WIDTH
            emit(ref, mm(lo, lo + GROUP_WIDTH), DILATION_PATTERNS[gi][1], c * GROUP_WIDTH)
    base = N_GROUPS * gw3
    emit(u_ref, mm(base, base + SSM_WIDTH), SSM_CHUNK, 0)
    base += SSM_WIDTH
    for c in range(4):
        lo, hi = c * 512, (c + 1) * 512
        z = mm(base + lo, base + hi) + bg_ref[:, lo:hi]
        g_ref[:, lo:hi] = jax.nn.sigmoid(z).astype(BF16)


def _in_proj(x2, w_perm, b_gate, bsz, seqlen, tm):
    t = x2.shape[0]
    n_in = w_perm.shape[1]
    tiles = seqlen // tm
    row = lambda w: pl.BlockSpec((tm, w), lambda i: (i, 0))
    w3 = 3 * GROUP_WIDTH
    dils = [d for _, d in DILATION_PATTERNS]
    res_spec = lambda d, w: pl.BlockSpec((None, d, tm // d, w), lambda i: (i // tiles, 0, i % tiles, 0))
    return pl.pallas_call(
        _in_proj_kernel,
        grid=(t // tm,),
        in_specs=[row(D_MODEL), _const_spec((D_MODEL, n_in)), _const_spec((1, 2 * D_MODEL))],
        out_specs=[res_spec(d, w3) for d in dils] + [res_spec(SSM_CHUNK, SSM_WIDTH), row(2 * D_MODEL)],
        out_shape=[jax.ShapeDtypeStruct((bsz, d, seqlen // d, w3), BF16) for d in dils]
        + [jax.ShapeDtypeStruct((bsz, SSM_CHUNK, seqlen // SSM_CHUNK, SSM_WIDTH), BF16),
           jax.ShapeDtypeStruct((t, 2 * D_MODEL), BF16)],
        scratch_shapes=[pltpu.VMEM((SSM_WIDTH // LANES, tm, LANES), F32)],
        compiler_params=_params(1),
        name="in_proj",
    )(x2, w_perm, b_gate)


def _attn_kernel(bucket_ref, relb_ref, cur_ref, prev_ref, out_ref, lse_ref, bias_scr, kv_scr, *, nq):
    blk = ATTN_BLOCK
    first = (pl.program_id(0) == 0) & (pl.program_id(1) == 0) & (pl.program_id(2) == 0)

    @pl.when(first)
    def _build_bias():
        bucket = bucket_ref[...]
        for h in range(HEADS_PER_GROUP):
            acc = jnp.full(bucket.shape, NEG_INF, F32)
            for bkt in range(N_BUCKETS):
                acc = jnp.where(bucket == bkt, relb_ref[bkt, h], acc)
            bias_scr[h * blk:(h + 1) * blk, :] = acc

    kv_scr[0:blk, :] = prev_ref[:, GROUP_WIDTH:3 * GROUP_WIDTH]
    kv_scr[blk:, :] = cur_ref[:, GROUP_WIDTH:3 * GROUP_WIDTH]

    lane_head = lax.broadcasted_iota(jnp.int32, (blk, GROUP_WIDTH), 1) // HEAD_DIM
    has_prev = pl.program_id(2) > 0
    for j in range(nq):
        q = cur_ref[j * blk:(j + 1) * blk, 0:GROUP_WIDTH]
        qs = jnp.concatenate(
            [jnp.where(lane_head == h, q, jnp.zeros_like(q)) for h in range(HEADS_PER_GROUP)], axis=0)
        k2 = kv_scr[j * blk:(j + 2) * blk, 0:GROUP_WIDTH]
        v2 = kv_scr[j * blk:(j + 2) * blk, GROUP_WIDTH:2 * GROUP_WIDTH]
        s = lax.dot_general(qs, k2, (((1,), (1,)), ((), ())), preferred_element_type=F32)
        s = s + bias_scr[...]
        if j == 0:
            col = lax.broadcasted_iota(jnp.int32, s.shape, 1)
            s = jnp.where((col >= blk) | has_prev, s, NEG_INF)
        m = jnp.max(s, axis=-1, keepdims=True)
        p = jnp.exp(s - m)
        l = jnp.sum(p, axis=-1, keepdims=True)
        pv = jnp.dot(p.astype(BF16), v2, preferred_element_type=F32)
        pv = pv * (1.0 / l)
        lse = m + jnp.log(l)
        o = pv[0:blk]
        e = jnp.broadcast_to(lse[0:blk], (blk, GROUP_WIDTH))
        for h in range(1, HEADS_PER_GROUP):
            sel = lane_head == h
            o = jnp.where(sel, pv[h * blk:(h + 1) * blk], o)
            e = jnp.where(sel, lse[h * blk:(h + 1) * blk], e)
        out_ref[j * blk:(j + 1) * blk, :] = o.astype(BF16)
        lse_ref[j * blk:(j + 1) * blk, :] = e


def _t5_bucket(dist):
    max_exact = N_BUCKETS // 2
    d = jnp.maximum(dist, 1).astype(F32)
    large = max_exact + (jnp.log(d / max_exact) / math.log(MAX_DISTANCE / max_exact)
                         * (N_BUCKETS - max_exact)).astype(jnp.int32)
    large = jnp.minimum(large, N_BUCKETS - 1)
    return jnp.where(dist < max_exact, dist, large)


def _bucket_table(window, dilation):
    blk = ATTN_BLOCK
    span = window // dilation
    rel = jnp.arange(blk)[:, None] + blk - jnp.arange(2 * blk)[None, :]
    valid = (rel >= 0) & (rel <= span)
    return jnp.where(valid, _t5_bucket(jnp.maximum(rel, 0) * dilation), -1).astype(jnp.int32)


def _attention_group(qkv, rel_bias_g, window, dilation, mt):
    assert window // dilation == ATTN_BLOCK
    bsz, _, n, w3 = qkv.shape
    mt = min(mt, n)
    nq = mt // ATTN_BLOCK
    cur = pl.BlockSpec((None, None, mt, w3), lambda b, r, i: (b, r, i, 0))
    prev = pl.BlockSpec((None, None, ATTN_BLOCK, w3), lambda b, r, i: (b, r, jnp.maximum(i * nq - 1, 0), 0))
    ospec = pl.BlockSpec((None, None, mt, GROUP_WIDTH), lambda b, r, i: (b, r, i, 0))
    return pl.pallas_call(
        functools.partial(_attn_kernel, nq=nq),
        grid=(bsz, dilation, n // mt),
        in_specs=[_const_spec((ATTN_BLOCK, 2 * ATTN_BLOCK)),
                  pl.BlockSpec(memory_space=pltpu.SMEM), cur, prev],
        out_specs=[ospec, ospec],
        out_shape=[jax.ShapeDtypeStruct((bsz, dilation, n, GROUP_WIDTH), BF16),
                   jax.ShapeDtypeStruct((bsz, dilation, n, GROUP_WIDTH), F32)],
        scratch_shapes=[pltpu.VMEM((HEADS_PER_GROUP * ATTN_BLOCK, 2 * ATTN_BLOCK), F32),
                        pltpu.VMEM((mt + ATTN_BLOCK, 2 * GROUP_WIDTH), BF16)],
        compiler_params=_params(3),
        name=f"attn_d{dilation}",
    )(_bucket_table(window, dilation), rel_bias_g, qkv, qkv)


def _ssm_prep_kernel(logdt_ref, lr_ref, li_ref, bt_re_ref, bt_im_ref, c_re_ref, c_im_ref, dl_ref,
                     toep_ref, bst_ref, cst_ref, a_ref):
    cs = SSM_CHUNK
    dt = jnp.exp(jnp.full((1, SSM_STATE), logdt_ref[pl.program_id(0)], F32))
    lr, li = lr_ref[...], li_ref[...]
    mag = jnp.exp(lr * dt)
    ab_re, ab_im = mag * jnp.cos(li * dt), mag * jnp.sin(li * dt)
    den = lr * lr + li * li
    nr = ab_re - 1.0
    k_re = (nr * lr + ab_im * li) / den
    k_im = (ab_im * lr - nr * li) / den
    bt_re, bt_im = bt_re_ref[...], bt_im_ref[...]
    bb_re = k_re * bt_re - k_im * bt_im
    bb_im = k_re * bt_im + k_im * bt_re
    j = lax.broadcasted_iota(jnp.int32, (cs + 8, SSM_STATE), 0).astype(F32)
    pmag = jnp.exp(lr * dt * j)
    ang = li * dt * j
    p_re, p_im = pmag * jnp.cos(ang), pmag * jnp.sin(ang)
    c_re, c_im = c_re_ref[...], c_im_ref[...]
    cp_re = [c_re * p_re[i:i + 1] - c_im * p_im[i:i + 1] for i in range(cs + 1)]
    cp_im = [c_re * p_im[i:i + 1] + c_im * p_re[i:i + 1] for i in range(cs + 1)]
    nt = (((1,), (1,)), ((), ()))
    hi = lax.Precision.HIGHEST
    kcat = (lax.dot_general(bb_re, jnp.concatenate(cp_re[:cs], axis=0), nt, precision=hi,
                            preferred_element_type=F32)
            - lax.dot_general(bb_im, jnp.concatenate(cp_im[:cs], axis=0), nt, precision=hi,
                              preferred_element_type=F32))
    lane = lax.broadcasted_iota(jnp.int32, kcat.shape, 1)
    row = lax.broadcasted_iota(jnp.int32, kcat.shape, 0)
    dl = dl_ref[...]
    for s in range(cs):
        off = s * SSM_GROUP
        t_s = kcat if s == 0 else jnp.where(lane >= off, pltpu.roll(kcat, off, 1), 0.0)
        toep_ref[s] = (t_s + jnp.where(lane == off + row, dl, 0.0)).astype(BF16)
        pe_re, pe_im = p_re[cs - 1 - s:cs - s], p_im[cs - 1 - s:cs - s]
        st_re = pe_re * bb_re - pe_im * bb_im
        st_im = pe_re * bb_im + pe_im * bb_re
        bst_ref[s] = jnp.concatenate([st_re, st_im], axis=-1).astype(BF16)
    ro = jnp.concatenate([jnp.concatenate(cp_re[1:], axis=0),
                          -jnp.concatenate(cp_im[1:], axis=0)], axis=-1)
    ro_t = ro.T
    cst_ref[0] = ro_t[:SSM_STATE].astype(BF16)
    cst_ref[1] = ro_t[SSM_STATE:].astype(BF16)
    a_ref[...] = jnp.concatenate([p_re[cs:cs + 1], p_im[cs:cs + 1]], axis=0)


def _ssm_weights(lambda_re, lambda_im, log_dt, b_re, b_im, c_re, c_im, d_skip):
    f32 = F32
    cs, ng, no = SSM_CHUNK, SSM_GROUPS, SSM_GROUPS // SSM_OCT
    grp = lambda *shape: pl.BlockSpec((None,) + shape, lambda g: (g,) + (0,) * len(shape))
    in_oct = lambda *shape: pl.BlockSpec((None, shape[0], None) + shape[1:],
                                         lambda g: (g // SSM_OCT, 0, g % SSM_OCT) + (0,) * (len(shape) - 1))
    toep_c, bst_c, cst_c, a32 = pl.pallas_call(
        _ssm_prep_kernel,
        grid=(ng,),
        in_specs=[pl.BlockSpec(memory_space=pltpu.SMEM), grp(1, SSM_STATE), grp(1, SSM_STATE),
                  grp(SSM_GROUP, SSM_STATE), grp(SSM_GROUP, SSM_STATE),
                  grp(SSM_GROUP, SSM_STATE), grp(SSM_GROUP, SSM_STATE), grp(1, cs * SSM_GROUP)],
        out_specs=[in_oct(cs, SSM_GROUP, cs * SSM_GROUP), in_oct(cs, SSM_GROUP, 2 * SSM_STATE),
                   in_oct(2, SSM_STATE, cs * SSM_GROUP), grp(2, SSM_STATE)],
        out_shape=[jax.ShapeDtypeStruct((no, cs, SSM_OCT, SSM_GROUP, cs * SSM_GROUP), BF16),
                   jax.ShapeDtypeStruct((no, cs, SSM_OCT, SSM_GROUP, 2 * SSM_STATE), BF16),
                   jax.ShapeDtypeStruct((no, 2, SSM_OCT, SSM_STATE, cs * SSM_GROUP), BF16),
                   jax.ShapeDtypeStruct((ng, 2, SSM_STATE), f32)],
        compiler_params=_params(1),
        name="ssm_prep",
    )(log_dt.astype(f32), lambda_re.astype(f32)[:, None, :], lambda_im.astype(f32)[:, None, :],
      b_re.astype(f32).transpose(0, 2, 1), b_im.astype(f32).transpose(0, 2, 1),
      c_re.astype(f32), c_im.astype(f32), jnp.tile(d_skip.astype(f32), (1, cs))[:, None, :])
    avec = a32.reshape(no, SSM_OCT, 2, SSM_STATE).transpose(0, 2, 1, 3).reshape(no, 2, SSM_OCT * SSM_STATE)
    return (toep_c.reshape(no, cs * LANES, cs * SSM_GROUP),
            bst_c.reshape(no, cs * LANES, 2 * SSM_STATE),
            cst_c.reshape(no, 2 * SSM_OCT * SSM_STATE, cs * SSM_GROUP),
            avec)


def _expand_ssm_weights(toep_ref, bstc_ref, cstc_ref, wt_scr, bst_scr, cst_scr):
    tw = 2 * LANES
    sh_g, sh_n = SSM_GROUP.bit_length() - 1, SSM_STATE.bit_length() - 1
    r = lax.broadcasted_iota(jnp.int32, (tw, tw), 0)
    c = lax.broadcasted_iota(jnp.int32, (tw, tw), 1)
    col_grp = (c >> sh_g) & (SSM_OCT - 1)
    row_grp = (r >> sh_g) & (SSM_OCT - 1)
    for tp in range(SSM_CHUNK // 2):
        src_col = (2 * tp + (c >> (sh_g + 3))) * SSM_GROUP + (c & (SSM_GROUP - 1))
        e = jnp.where(r == src_col, 1.0, 0.0).astype(BF16)
        base = tp * (tp + 1) // 2
        for sp in range(tp + 1):
            x = jnp.dot(toep_ref[sp * tw:(sp + 1) * tw, :], e, preferred_element_type=F32)
            wt_scr[base + sp] = jnp.where(row_grp == col_grp, x, 0.0).astype(BF16)
        for j in range(2 * SSM_OCT * SSM_STATE // tw):
            x = jnp.dot(cstc_ref[j * tw:(j + 1) * tw, :], e, preferred_element_type=F32)
            row_grp_n = ((j * tw + r) >> sh_n) & (SSM_OCT - 1)
            cst_scr[j * tw:(j + 1) * tw, tp * tw:(tp + 1) * tw] = jnp.where(row_grp_n == col_grp, x, 0.0).astype(BF16)
    ns = 2 * SSM_OCT * SSM_STATE
    rb = lax.broadcasted_iota(jnp.int32, (2 * SSM_STATE, ns), 0)
    cb = lax.broadcasted_iota(jnp.int32, (2 * SSM_STATE, ns), 1)
    src_col_b = (cb >> (sh_n + 3)) * SSM_STATE + (cb & (SSM_STATE - 1))
    eb = jnp.where(rb == src_col_b, 1.0, 0.0).astype(BF16)
    rr = lax.broadcasted_iota(jnp.int32, (tw, ns), 0)
    cc = lax.broadcasted_iota(jnp.int32, (tw, ns), 1)
    keep = ((rr >> sh_g) & (SSM_OCT - 1)) == ((cc >> sh_n) & (SSM_OCT - 1))
    for j in range(SSM_CHUNK * LANES // tw):
        x = jnp.dot(bstc_ref[j * tw:(j + 1) * tw, :], eb, preferred_element_type=F32)
        bst_scr[j * tw:(j + 1) * tw, :] = jnp.where(keep, x, 0.0).astype(BF16)


def _ssm_kernel(u_ref, toep_ref, bstc_ref, cstc_ref, a_ref, y_ref, wt_ref, bst_ref, cst_ref, zz_scr, hp_scr,
                *, n_chunks, nb):
    @pl.when(pl.program_id(1) == 0)
    def _new_octet():
        _expand_ssm_weights(toep_ref, bstc_ref, cstc_ref, wt_ref, bst_ref, cst_ref)

    rows = nb * n_chunks
    nk = SSM_OCT * SSM_STATE // LANES
    u_t = [u_ref[:, s].reshape(rows, LANES) for s in range(SSM_CHUNK)]
    z = jnp.dot(jnp.concatenate(u_t, axis=-1), bst_ref[...], preferred_element_type=F32)
    for k in range(2 * nk):
        zz_scr[k] = z[:, k * LANES:(k + 1) * LANES]
    a_re = [jnp.broadcast_to(a_ref[0:1, k * LANES:(k + 1) * LANES], (nb, LANES)) for k in range(nk)]
    a_im = [jnp.broadcast_to(a_ref[1:2, k * LANES:(k + 1) * LANES], (nb, LANES)) for k in range(nk)]

    def step(c, carry):
        h_re, h_im = carry
        rows_c = pl.ds(c, nb, stride=n_chunks)
        new_re, new_im = [], []
        for k in range(nk):
            hp_scr[k, rows_c, :] = h_re[k]
            hp_scr[nk + k, rows_c, :] = h_im[k]
            new_re.append(a_re[k] * h_re[k] - a_im[k] * h_im[k] + zz_scr[k, rows_c, :])
            new_im.append(a_re[k] * h_im[k] + a_im[k] * h_re[k] + zz_scr[nk + k, rows_c, :])
        return tuple(new_re), tuple(new_im)

    zero = tuple(jnp.zeros((nb, LANES), F32) for _ in range(nk))
    lax.fori_loop(0, n_chunks, step, (zero, zero), unroll=4)
    hp = jnp.concatenate([hp_scr[k] for k in range(2 * nk)], axis=-1).astype(BF16)
    tw = 2 * LANES
    for tp in range(SSM_CHUNK // 2):
        base = tp * (tp + 1) // 2
        y = jnp.dot(hp, cst_ref[:, tp * tw:(tp + 1) * tw], preferred_element_type=F32)
        y = y + jnp.dot(jnp.concatenate(u_t[:2 * (tp + 1)], axis=-1),
                        wt_ref[base:base + tp + 1].reshape((tp + 1) * tw, tw), preferred_element_type=F32)
        y = jax.nn.gelu(y).astype(BF16)
        y_ref[:, 2 * tp] = y[:, :LANES].reshape(nb, n_chunks, LANES)
        y_ref[:, 2 * tp + 1] = y[:, LANES:].reshape(nb, n_chunks, LANES)


def _ssm(u16, weights, nb):
    toep_c, bst_c, cst_c, avec = weights
    bsz, cs, nc, width = u16.shape
    oct_spec = lambda a: pl.BlockSpec((None,) + a.shape[1:], lambda o, b: (o,) + (0,) * (a.ndim - 1))
    io_spec = pl.BlockSpec((nb, cs, nc, LANES), lambda o, b: (b, 0, 0, o))
    n_state = 2 * SSM_OCT * SSM_STATE
    n_pairs = (cs // 2) * (cs // 2 + 1) // 2
    return pl.pallas_call(
        functools.partial(_ssm_kernel, n_chunks=nc, nb=nb),
        grid=(width // LANES, bsz // nb),
        in_specs=[io_spec, oct_spec(toep_c), oct_spec(bst_c), oct_spec(cst_c), oct_spec(avec)],
        out_specs=io_spec,
        out_shape=jax.ShapeDtypeStruct(u16.shape, BF16),
        scratch_shapes=[pltpu.VMEM((n_pairs, 2 * LANES, 2 * LANES), BF16),
                        pltpu.VMEM((cs * LANES, n_state), BF16),
                        pltpu.VMEM((n_state, cs * LANES), BF16),
                        pltpu.VMEM((n_state // LANES, nb * nc, LANES), F32),
                        pltpu.VMEM((n_state // LANES, nb * nc, LANES), F32)],
        compiler_params=_params(2),
        name="ssm",
    )(u16, toep_c, bst_c, cst_c, avec)


def _layer_norm(v, g, b):
    mu = jnp.mean(v, axis=-1, keepdims=True)
    vc = v - mu
    var = jnp.mean(vc * vc, axis=-1, keepdims=True)
    return vc * lax.rsqrt(var + LN_EPS) * g + b


def _merge_kernel(o0_ref, o1_ref, o2_ref, l0_ref, l1_ref, l2_ref, ys_ref, g_ref, x_ref,
                  wglu_ref, wsp_ref, wap_ref, wout_ref, lng_ref, lnb_ref, h_ref, o_scr, l_scr, y_scr):
    tm = x_ref.shape[0]

    def token_order(ref, scr, d):
        if d == 1:
            return ref[0].astype(F32)
        nk = ref.shape[-1] // LANES
        for r in range(d):
            v = ref[r].astype(F32)
            for k in range(nk):
                scr[k, pl.ds(r, tm // d, stride=d), :] = v[:, k * LANES:(k + 1) * LANES]
        return jnp.concatenate([scr[k] for k in range(nk)], axis=-1)

    ls, outs = [], []
    for gi, (o_ref, l_ref) in enumerate(((o0_ref, l0_ref), (o1_ref, l1_ref), (o2_ref, l2_ref))):
        d = DILATION_PATTERNS[gi][1]
        outs.append(token_order(o_ref, o_scr.at[gi], d))
        ls.append(token_order(l_ref, l_scr.at[gi], d))
    mx = jnp.maximum(jnp.maximum(ls[0], ls[1]), ls[2])
    es = [jnp.exp(l - mx) for l in ls]
    num = es[0] * outs[0] + es[1] * outs[1] + es[2] * outs[2]
    y_attn = (num / (es[0] + es[1] + es[2])).astype(BF16)
    ys = token_order(ys_ref, y_scr, SSM_CHUNK).astype(BF16)
    glu = jnp.dot(ys, wglu_ref[...], preferred_element_type=F32)
    y_ssm = (glu[:, :SSM_WIDTH] * jax.nn.sigmoid(glu[:, SSM_WIDTH:])).astype(BF16)
    pa = jnp.dot(y_ssm, wsp_ref[...], preferred_element_type=F32)
    pb = jnp.dot(y_attn, wap_ref[...], preferred_element_type=F32)
    gated = (g_ref[:, :D_MODEL].astype(F32) * pa + g_ref[:, D_MODEL:].astype(F32) * pb).astype(BF16)
    mix = jnp.dot(gated, wout_ref[...], preferred_element_type=F32)
    h_ref[...] = _layer_norm(ALPHA * x_ref[...] + mix, lng_ref[...], lnb_ref[...])


def _merge(outs, lses, ys, gates, x2, w_glu, w_sp, w_ap, w_out, ln_g, ln_b, seqlen, tm):
    t = x2.shape[0]
    tiles = seqlen // tm
    row = lambda w: pl.BlockSpec((tm, w), lambda i: (i, 0))
    res_spec = lambda d, w: pl.BlockSpec((None, d, tm // d, w), lambda i: (i // tiles, 0, i % tiles, 0))
    res_specs = [res_spec(d, GROUP_WIDTH) for _, d in DILATION_PATTERNS]
    return pl.pallas_call(
        _merge_kernel,
        grid=(t // tm,),
        in_specs=res_specs + res_specs + [res_spec(SSM_CHUNK, SSM_WIDTH), row(2 * D_MODEL), row(D_MODEL),
                  _const_spec(w_glu.shape), _const_spec(w_sp.shape), _const_spec(w_ap.shape),
                  _const_spec(w_out.shape), _const_spec((1, D_MODEL)), _const_spec((1, D_MODEL))],
        out_specs=row(D_MODEL),
        out_shape=jax.ShapeDtypeStruct((t, D_MODEL), F32),
        scratch_shapes=[pltpu.VMEM((N_GROUPS, GROUP_WIDTH // LANES, tm, LANES), F32),
                        pltpu.VMEM((N_GROUPS, GROUP_WIDTH // LANES, tm, LANES), F32),
                        pltpu.VMEM((SSM_WIDTH // LANES, tm, LANES), F32)],
        compiler_params=_params(1),
        name="merge_ln1",
    )(*outs, *lses, ys, gates, x2, w_glu, w_sp, w_ap, w_out, ln_g, ln_b)


FF_CHUNK = 1024


def _ffn_kernel(h_ref, wup_ref, wdn_ref, lng_ref, lnb_ref, o_ref):
    h = h_ref[...]
    hb = h.astype(BF16)
    acc = ALPHA * h
    for c in range(D_FF // FF_CHUNK):
        lo, hi = c * FF_CHUNK, (c + 1) * FF_CHUNK
        up = jnp.dot(hb, wup_ref[:, lo:hi], preferred_element_type=F32)
        act = jnp.square(jnp.maximum(up, 0.0)).astype(BF16)
        acc = acc + jnp.dot(act, wdn_ref[lo:hi, :], preferred_element_type=F32)
    o_ref[...] = _layer_norm(acc, lng_ref[...], lnb_ref[...])


def _ffn(h1, w_up, w_down, ln_g, ln_b, tm):
    t = h1.shape[0]
    row = pl.BlockSpec((tm, D_MODEL), lambda i: (i, 0))
    return pl.pallas_call(
        _ffn_kernel,
        grid=(t // tm,),
        in_specs=[row, _const_spec(w_up.shape), _const_spec(w_down.shape),
                  _const_spec((1, D_MODEL)), _const_spec((1, D_MODEL))],
        out_specs=row,
        out_shape=jax.ShapeDtypeStruct((t, D_MODEL), F32),
        compiler_params=_params(1),
        name="ffn_ln2",
    )(h1, w_up, w_down, ln_g, ln_b)


def _permute_w_in(w):
    aw = ATTN_WIDTH
    cols = []
    for gi in range(N_GROUPS):
        lo, hi = gi * GROUP_WIDTH, (gi + 1) * GROUP_WIDTH
        cols += [w[:, lo:hi] * (HEAD_DIM ** -0.5), w[:, aw + lo:aw + hi], w[:, 2 * aw + lo:2 * aw + hi]]
    cols.append(w[:, 3 * aw:])
    return jnp.concatenate(cols, axis=1).astype(BF16)


def _layer(h2, bsz, seqlen, l, w_in, b_gate, lambda_re, lambda_im, log_dt, ssm_b_re, ssm_b_im,
           ssm_c_re, ssm_c_im, ssm_d, w_glu, w_ssm_proj, rel_bias, w_attn_proj, w_out,
           ln1_g, ln1_b, w_up, w_down, ln2_g, ln2_b, tm=512, attn_mt=512):
    qkv0, qkv1, qkv2, u, gates = _in_proj(h2, _permute_w_in(w_in[l]), b_gate[l][None, :], bsz, seqlen, tm)
    outs, lses = [], []
    for gi, ((window, dilation), qkv) in enumerate(zip(DILATION_PATTERNS, (qkv0, qkv1, qkv2))):
        rb = rel_bias[:, gi * HEADS_PER_GROUP:(gi + 1) * HEADS_PER_GROUP].astype(F32)
        o, s = _attention_group(qkv, rb, window, dilation, attn_mt)
        outs.append(o)
        lses.append(s)
    ssm_w = _ssm_weights(lambda_re[l], lambda_im[l], log_dt[l], ssm_b_re[l], ssm_b_im[l],
                         ssm_c_re[l], ssm_c_im[l], ssm_d[l])
    ys = _ssm(u, ssm_w, nb=4)
    h1 = _merge(outs, lses, ys, gates, h2, w_glu[l].astype(BF16), w_ssm_proj[l].astype(BF16),
                w_attn_proj[l].astype(BF16), w_out[l].astype(BF16),
                ln1_g[l][None, :], ln1_b[l][None, :], seqlen, tm)
    return _ffn(h1, w_up[l].astype(BF16), w_down[l].astype(BF16), ln2_g[l][None, :], ln2_b[l][None, :], tm)


def kernel(x, w_in, b_gate, lambda_re, lambda_im, log_dt, ssm_b_re, ssm_b_im, ssm_c_re, ssm_c_im,
           ssm_d, w_glu, w_ssm_proj, rel_bias, w_attn_proj, w_out, ln1_g, ln1_b, w_up, w_down,
           ln2_g, ln2_b):
    bsz, seqlen, d = x.shape
    h = x.reshape(bsz * seqlen, d)
    for l in range(w_in.shape[0]):
        h = _layer(h, bsz, seqlen, l, w_in, b_gate, lambda_re, lambda_im, log_dt, ssm_b_re, ssm_b_im,
                   ssm_c_re, ssm_c_im, ssm_d, w_glu, w_ssm_proj, rel_bias, w_attn_proj, w_out,
                   ln1_g, ln1_b, w_up, w_down, ln2_g, ln2_b)
    return h.reshape(bsz, seqlen, d)
```

```python
import functools
import math

import jax
import jax.numpy as jnp
from jax import lax
from jax.experimental import pallas as pl
from jax.experimental.pallas import tpu as pltpu

F32 = jnp.float32
BF16 = jnp.bfloat16

D_MODEL = 1024
HEAD_DIM = 64
HEADS_PER_GROUP = 4
GROUP_WIDTH = HEADS_PER_GROUP * HEAD_DIM
DILATION_PATTERNS = ((128, 1), (512, 4), (2048, 16))
N_GROUPS = len(DILATION_PATTERNS)
ATTN_WIDTH = N_GROUPS * GROUP_WIDTH
N_BUCKETS = 32
MAX_DISTANCE = 2048
SSM_WIDTH = 512
SSM_GROUP = 16
SSM_GROUPS = 32
SSM_STATE = 64
D_FF = 4 * D_MODEL
DEPTH = 1
ALPHA = (2.0 * DEPTH) ** 0.25
LN_EPS = 1e-5
NEG_INF = -1e30

ATTN_BLOCK = 128
SSM_CHUNK = 16
SSM_OCT = 8
LANES = 128
VMEM_LIMIT_BYTES =56 * 1024 * 1024


def _params(n_axes):
    return pltpu.CompilerParams(dimension_semantics=("arbitrary",) * n_axes,
                                vmem_limit_bytes=VMEM_LIMIT_BYTES)


def _const_spec(shape):
    nd = len(shape)
    return pl.BlockSpec(shape, lambda *_: (0,) * nd)


def _in_proj_kernel(x_ref, w_ref, bg_ref, qkv0_ref, qkv1_ref, qkv2_ref, u_ref, g_ref, scr):
    xb = x_ref[...].astype(BF16)
    tm = xb.shape[0]

    def mm(lo, hi):
        return jnp.dot(xb, w_ref[:, lo:hi], preferred_element_type=F32)

    def emit(ref, res, d, col0):
        width = res.shape[1]
        if d == 1:
            ref[0, :, col0:col0 + width] = res.astype(BF16)
            return
        slot0 = next_slot[0]
        next_slot[0] += width // LANES
        for k in range(width // LANES):
            scr[slot0 + k] = res[:, k * LANES:(k + 1) * LANES]
        for r in range(d):
            for k in range(width // LANES):
                col = col0 + k * LANES
                ref[r, :, col:col + LANES] = scr[slot0 + k, pl.ds(r, tm // d, stride=d), :].astype(BF16)

    next_slot = [0]

    gw3 = 3 * GROUP_WIDTH
    for gi, ref in enumerate((qkv0_ref, qkv1_ref, qkv2_ref)):
        for c in range(3):
            lo = gi * gw3 + c * GROUP_WIDTH
            emit(ref, mm(lo, lo + GROUP_WIDTH), DILATION_PATTERNS[gi][1], c * GROUP_WIDTH)
    base = N_GROUPS * gw3
    emit(u_ref, mm(base, base + SSM_WIDTH), SSM_CHUNK, 0)
    base += SSM_WIDTH
    for c in range(4):
        lo, hi = c * 512, (c + 1) * 512
        z = mm(base + lo, base + hi) + bg_ref[:, lo:hi]
        g_ref[:, lo:hi] = jax.nn.sigmoid(z).astype(BF16)


def _in_proj(x2, w_perm, b_gate, bsz, seqlen, tm):
    t = x2.shape[0]
    n_in = w_perm.shape[1]
    tiles = seqlen // tm
    row = lambda w: pl.BlockSpec((tm, w), lambda i: (i, 0))
    w3 = 3 * GROUP_WIDTH
    dils = [d for _, d in DILATION_PATTERNS]
    n_stage = (sum(d > 1 for d in dils) * w3 + SSM_WIDTH) // LANES
    res_spec = lambda d, w:pl.BlockSpec((None, d, tm // d, w), lambda i: (i // tiles, 0, i % tiles, 0))
    return pl.pallas_call(
        _in_proj_kernel,
        grid=(t // tm,),
        in_specs=[row(D_MODEL), _const_spec((D_MODEL, n_in)), _const_spec((1, 2 * D_MODEL))],
        out_specs=[res_spec(d, w3) for d in dils] + [res_spec(SSM_CHUNK, SSM_WIDTH), row(2 * D_MODEL)],
        out_shape=[jax.ShapeDtypeStruct((bsz, d, seqlen // d, w3), BF16) for d in dils]
        + [jax.ShapeDtypeStruct((bsz, SSM_CHUNK, seqlen // SSM_CHUNK, SSM_WIDTH), BF16),
           jax.ShapeDtypeStruct((t, 2 * D_MODEL), BF16)],
        scratch_shapes=[pltpu.VMEM((n_stage, tm, LANES), F32)],
        compiler_params=_params(1),
        name="in_proj",
    )(x2, w_perm, b_gate)


def _attn_kernel(bucket_ref, relb_ref, cur_ref, prev_ref, out_ref, lse_ref, bias_scr, kv_scr, *, nq, nres):
    blk = ATTN_BLOCK
    first = (pl.program_id(0) == 0) & (pl.program_id(1) == 0) & (pl.program_id(2) == 0)

    @pl.when(first)
    def _build_bias():
        bucket = bucket_ref[...]
        for h in range(HEADS_PER_GROUP):
            acc = jnp.full(bucket.shape, NEG_INF, F32)
            for bkt in range(N_BUCKETS):
                acc = jnp.where(bucket == bkt, relb_ref[bkt, h], acc)
            bias_scr[h * blk:(h + 1) * blk, :] = acc

    lane_head = lax.broadcasted_iota(jnp.int32, (blk, GROUP_WIDTH), 1) // HEAD_DIM
    col = lax.broadcasted_iota(jnp.int32, (HEADS_PER_GROUP * blk, 2 * blk), 1)
    keep_first = (col >= blk) | (pl.program_id(2) > 0)
    nt = (((1,), (1,)), ((), ()))
    for r in range(nres):
        kv_scr[r, 0:blk, :] = prev_ref[r, :, GROUP_WIDTH:3 * GROUP_WIDTH]
        kv_scr[r, blk:, :] = cur_ref[r, :, GROUP_WIDTH:3 * GROUP_WIDTH]
        for j in range(nq):
            q = cur_ref[r, j * blk:(j + 1) * blk, 0:GROUP_WIDTH]
            k2 = kv_scr[r, j * blk:(j + 2) * blk, 0:GROUP_WIDTH]
            v2 = kv_scr[r, j * blk:(j + 2) * blk, GROUP_WIDTH:2 * GROUP_WIDTH]
            qs = jnp.concatenate(
                [jnp.where(lane_head == h, q, jnp.zeros_like(q)) for h in range(HEADS_PER_GROUP)], axis=0)
            s = lax.dot_general(qs, k2, nt, preferred_element_type=F32) + bias_scr[...]
            if j == 0:
                s = jnp.where(keep_first, s, NEG_INF)
            m = jnp.max(s, axis=-1, keepdims=True)
            p = jnp.exp(s - m)
            l = jnp.sum(p, axis=-1, keepdims=True)
            pv = jnp.dot(p.astype(BF16), v2, preferred_element_type=F32) * (1.0 / l)
            lse = m + jnp.log(l)
            o = pv[0:blk]
            e = jnp.broadcast_to(lse[0:blk], (blk, GROUP_WIDTH))
            for h in range(1, HEADS_PER_GROUP):
                sel = lane_head == h
                o = jnp.where(sel, pv[h * blk:(h + 1) * blk], o)
                e = jnp.where(sel, lse[h * blk:(h + 1) * blk], e)
            out_ref[r, j * blk:(j + 1) * blk, :] = o.astype(BF16)
            lse_ref[r, j * blk:(j + 1) * blk, :] = e


def _t5_bucket(dist):
    max_exact = N_BUCKETS // 2
    d = jnp.maximum(dist, 1).astype(F32)
    large = max_exact + (jnp.log(d / max_exact) / math.log(MAX_DISTANCE / max_exact)
                         * (N_BUCKETS - max_exact)).astype(jnp.int32)
    large = jnp.minimum(large, N_BUCKETS - 1)
    return jnp.where(dist < max_exact, dist, large)


def _bucket_table(window, dilation):
    blk = ATTN_BLOCK
    span = window // dilation
    rel = jnp.arange(blk)[:, None] + blk - jnp.arange(2 * blk)[None, :]
    valid = (rel >= 0) & (rel <= span)
    return jnp.where(valid, _t5_bucket(jnp.maximum(rel, 0) * dilation), -1).astype(jnp.int32)


def _attention_group(qkv, rel_bias_g, window, dilation, mt):
    assert window // dilation == ATTN_BLOCK
    bsz, _, n, w3 = qkv.shape
    rows = mt
    mt = min(rows, n)
    nres = rows // mt
    nq = mt // ATTN_BLOCK
    cur = pl.BlockSpec((None, nres, mt, w3), lambda b, r, i: (b, r, i, 0))
    prev = pl.BlockSpec((None, nres, ATTN_BLOCK, w3), lambda b, r, i: (b, r, jnp.maximum(i * nq - 1, 0), 0))
    ospec = pl.BlockSpec((None, nres, mt, GROUP_WIDTH), lambda b, r, i: (b, r, i, 0))
    return pl.pallas_call(
        functools.partial(_attn_kernel, nq=nq, nres=nres),
        grid=(bsz, dilation // nres, n // mt),
        in_specs=[_const_spec((ATTN_BLOCK, 2 * ATTN_BLOCK)),
                  pl.BlockSpec(memory_space=pltpu.SMEM), cur, prev],
        out_specs=[ospec, ospec],
        out_shape=[jax.ShapeDtypeStruct((bsz, dilation, n, GROUP_WIDTH), BF16),
                   jax.ShapeDtypeStruct((bsz, dilation, n, GROUP_WIDTH), F32)],
        scratch_shapes=[pltpu.VMEM((HEADS_PER_GROUP * ATTN_BLOCK, 2 * ATTN_BLOCK), F32),
                        pltpu.VMEM((nres, mt + ATTN_BLOCK, 2 * GROUP_WIDTH), BF16)],
        compiler_params=_params(3),
        name=f"attn_d{dilation}",
    )(_bucket_table(window, dilation), rel_bias_g, qkv, qkv)


def _ssm_prep_kernel(logdt_ref, lr_ref, li_ref, bt_re_ref, bt_im_ref, c_re_ref, c_im_ref, dl_ref,
                     toep_ref, bst_ref, cst_ref, a_ref):
    cs = SSM_CHUNK
    dt = jnp.exp(jnp.full((1, SSM_STATE), logdt_ref[pl.program_id(0)], F32))
    lr, li = lr_ref[...], li_ref[...]
    mag = jnp.exp(lr * dt)
    ab_re, ab_im = mag * jnp.cos(li * dt), mag * jnp.sin(li * dt)
    den = lr * lr + li * li
    nr = ab_re - 1.0
    k_re = (nr * lr + ab_im * li) / den
    k_im = (ab_im * lr - nr * li) / den
    bt_re, bt_im = bt_re_ref[...], bt_im_ref[...]
    bb_re = k_re * bt_re - k_im * bt_im
    bb_im = k_re * bt_im + k_im * bt_re
    j = lax.broadcasted_iota(jnp.int32, (cs + 8, SSM_STATE), 0).astype(F32)
    pmag = jnp.exp(lr * dt * j)
    ang = li * dt * j
    p_re, p_im = pmag * jnp.cos(ang), pmag * jnp.sin(ang)
    c_re, c_im = c_re_ref[...], c_im_ref[...]
    cp_re = [c_re * p_re[i:i + 1] - c_im * p_im[i:i + 1] for i in range(cs + 1)]
    cp_im = [c_re * p_im[i:i + 1] + c_im * p_re[i:i + 1] for i in range(cs + 1)]
    nt = (((1,), (1,)), ((), ()))
    hi = lax.Precision.HIGHEST
    kcat = (lax.dot_general(bb_re, jnp.concatenate(cp_re[:cs], axis=0), nt, precision=hi,
                            preferred_element_type=F32)
            - lax.dot_general(bb_im, jnp.concatenate(cp_im[:cs], axis=0), nt, precision=hi,
                              preferred_element_type=F32))
    lane = lax.broadcasted_iota(jnp.int32, kcat.shape, 1)
    row = lax.broadcasted_iota(jnp.int32, kcat.shape, 0)
    dl = dl_ref[...]
    for s in range(cs):
        off = s * SSM_GROUP
        t_s = kcat if s == 0 else jnp.where(lane >= off, pltpu.roll(kcat, off, 1), 0.0)
        toep_ref[s] = (t_s + jnp.where(lane == off + row, dl, 0.0)).astype(BF16)
        pe_re, pe_im = p_re[cs - 1 - s:cs - s], p_im[cs - 1 - s:cs - s]
        st_re = pe_re * bb_re - pe_im * bb_im
        st_im = pe_re * bb_im + pe_im * bb_re
        bst_ref[s] = jnp.concatenate([st_re, st_im], axis=-1).astype(BF16)
    ro = jnp.concatenate([jnp.concatenate(cp_re[1:], axis=0),
                          -jnp.concatenate(cp_im[1:], axis=0)], axis=-1)
    ro_t = ro.T
    cst_ref[0] = ro_t[:SSM_STATE].astype(BF16)
    cst_ref[1] = ro_t[SSM_STATE:].astype(BF16)
    a_ref[...] = jnp.concatenate([p_re[cs:cs + 1], p_im[cs:cs + 1]], axis=0)


def _ssm_weights(lambda_re, lambda_im, log_dt, b_re, b_im, c_re, c_im, d_skip):
    f32 = F32
    cs, ng, no = SSM_CHUNK, SSM_GROUPS, SSM_GROUPS // SSM_OCT
    grp = lambda *shape: pl.BlockSpec((None,) + shape, lambda g: (g,) + (0,) * len(shape))
    in_oct = lambda *shape: pl.BlockSpec((None, shape[0], None) + shape[1:],
                                         lambda g: (g // SSM_OCT, 0, g % SSM_OCT) + (0,) * (len(shape) - 1))
    toep_c, bst_c, cst_c, a32 = pl.pallas_call(
        _ssm_prep_kernel,
        grid=(ng,),
        in_specs=[pl.BlockSpec(memory_space=pltpu.SMEM), grp(1, SSM_STATE), grp(1, SSM_STATE),
                  grp(SSM_GROUP, SSM_STATE), grp(SSM_GROUP, SSM_STATE),
                  grp(SSM_GROUP, SSM_STATE), grp(SSM_GROUP, SSM_STATE), grp(1, cs * SSM_GROUP)],
        out_specs=[in_oct(cs, SSM_GROUP, cs * SSM_GROUP), in_oct(cs, SSM_GROUP, 2 * SSM_STATE),
                   in_oct(2, SSM_STATE, cs * SSM_GROUP), grp(2, SSM_STATE)],
        out_shape=[jax.ShapeDtypeStruct((no, cs, SSM_OCT, SSM_GROUP, cs * SSM_GROUP), BF16),
                   jax.ShapeDtypeStruct((no, cs, SSM_OCT, SSM_GROUP, 2 * SSM_STATE), BF16),
                   jax.ShapeDtypeStruct((no, 2, SSM_OCT, SSM_STATE, cs * SSM_GROUP), BF16),
                   jax.ShapeDtypeStruct((ng, 2, SSM_STATE), f32)],
        compiler_params=_params(1),
        name="ssm_prep",
    )(log_dt.astype(f32), lambda_re.astype(f32)[:, None, :], lambda_im.astype(f32)[:, None, :],
      b_re.astype(f32).transpose(0, 2, 1), b_im.astype(f32).transpose(0, 2, 1),
      c_re.astype(f32), c_im.astype(f32), jnp.tile(d_skip.astype(f32), (1, cs))[:, None, :])
    avec = a32.reshape(no, SSM_OCT, 2, SSM_STATE).transpose(0, 2, 1, 3).reshape(no, 2, SSM_OCT * SSM_STATE)
    return (toep_c.reshape(no, cs * LANES, cs * SSM_GROUP),
            bst_c.reshape(no, cs * LANES, 2 * SSM_STATE),
            cst_c.reshape(no, 2 * SSM_OCT * SSM_STATE, cs * SSM_GROUP),
            avec)


def _expand_ssm_weights(toep_ref, bstc_ref, cstc_ref, wt_scr, bst_scr, cst_scr):
    tw = 2 * LANES
    sh_g, sh_n = SSM_GROUP.bit_length() - 1, SSM_STATE.bit_length() - 1
    r = lax.broadcasted_iota(jnp.int32, (tw, tw), 0)
    c = lax.broadcasted_iota(jnp.int32, (tw, tw), 1)
    col_grp = (c >> sh_g) & (SSM_OCT - 1)
    row_grp = (r >> sh_g) & (SSM_OCT - 1)
    for tp in range(SSM_CHUNK // 2):
        src_col = (2 * tp + (c >> (sh_g + 3))) * SSM_GROUP + (c & (SSM_GROUP - 1))
        e = jnp.where(r == src_col, 1.0, 0.0).astype(BF16)
        base = tp * (tp + 1) // 2
        for sp in range(tp + 1):
            x = jnp.dot(toep_ref[sp * tw:(sp + 1) * tw, :], e, preferred_element_type=F32)
            wt_scr[base + sp] = jnp.where(row_grp == col_grp, x, 0.0).astype(BF16)
        for j in range(2 * SSM_OCT * SSM_STATE // tw):
            x = jnp.dot(cstc_ref[j * tw:(j + 1) * tw, :], e, preferred_element_type=F32)
            row_grp_n = ((j * tw + r) >> sh_n) & (SSM_OCT - 1)
            cst_scr[j * tw:(j + 1) * tw, tp * tw:(tp + 1) * tw] = jnp.where(row_grp_n == col_grp, x, 0.0).astype(BF16)
    ns = 2 * SSM_OCT * SSM_STATE
    rb = lax.broadcasted_iota(jnp.int32, (2 * SSM_STATE, ns), 0)
    cb = lax.broadcasted_iota(jnp.int32, (2 * SSM_STATE, ns), 1)
    src_col_b = (cb >> (sh_n + 3)) * SSM_STATE + (cb & (SSM_STATE - 1))
    eb = jnp.where(rb == src_col_b, 1.0, 0.0).astype(BF16)
    rr = lax.broadcasted_iota(jnp.int32, (tw, ns), 0)
    cc = lax.broadcasted_iota(jnp.int32, (tw, ns), 1)
    keep = ((rr >> sh_g) & (SSM_OCT - 1)) == ((cc >> sh_n) & (SSM_OCT - 1))
    for j in range(SSM_CHUNK * LANES // tw):
        x = jnp.dot(bstc_ref[j * tw:(j + 1) * tw, :], eb, preferred_element_type=F32)
        bst_scr[j * tw:(j + 1) * tw, :] = jnp.where(keep, x, 0.0).astype(BF16)


def _ssm_kernel(u_ref, toep_ref, bstc_ref, cstc_ref, a_ref, y_ref, wt_ref, bst_ref, cst_ref, zz_scr, hp_scr,
                *, n_chunks, nb):
    @pl.when(pl.program_id(1) == 0)
    def _new_octet():
        _expand_ssm_weights(toep_ref, bstc_ref, cstc_ref, wt_ref, bst_ref, cst_ref)

    rows = nb * n_chunks
    nk = SSM_OCT * SSM_STATE // LANES
    u_t = [u_ref[:, s].reshape(rows, LANES) for s in range(SSM_CHUNK)]
    z = jnp.dot(jnp.concatenate(u_t, axis=-1), bst_ref[...], preferred_element_type=F32)
    for k in range(2 * nk):
        zz_scr[k] = z[:, k * LANES:(k + 1) * LANES]
    a_re = [jnp.broadcast_to(a_ref[0:1, k * LANES:(k + 1) * LANES], (nb, LANES)) for k in range(nk)]
    a_im = [jnp.broadcast_to(a_ref[1:2, k * LANES:(k + 1) * LANES], (nb, LANES)) for k in range(nk)]

    def step(c, carry):
        h_re, h_im = carry
        rows_c = pl.ds(c, nb, stride=n_chunks)
        new_re, new_im = [], []
        for k in range(nk):
            hp_scr[k, rows_c, :] = h_re[k]
            hp_scr[nk + k, rows_c, :] = h_im[k]
            new_re.append(a_re[k] * h_re[k] - a_im[k] * h_im[k] + zz_scr[k, rows_c, :])
            new_im.append(a_re[k] * h_im[k] + a_im[k] * h_re[k] + zz_scr[nk + k, rows_c, :])
        return tuple(new_re), tuple(new_im)

    zero = tuple(jnp.zeros((nb, LANES), F32) for _ in range(nk))
    lax.fori_loop(0, n_chunks, step, (zero, zero), unroll=4)
    hp = jnp.concatenate([hp_scr[k] for k in range(2 * nk)], axis=-1).astype(BF16)
    tw = 2 * LANES
    for tp in range(SSM_CHUNK // 2):
        base = tp * (tp + 1) // 2
        y = jnp.dot(hp, cst_ref[:, tp * tw:(tp + 1) * tw], preferred_element_type=F32)
        y = y + jnp.dot(jnp.concatenate(u_t[:2 * (tp + 1)], axis=-1),
                        wt_ref[base:base + tp + 1].reshape((tp + 1) * tw, tw), preferred_element_type=F32)
        y = jax.nn.gelu(y).astype(BF16)
        y_ref[:, 2 * tp] = y[:, :LANES].reshape(nb, n_chunks, LANES)
        y_ref[:, 2 * tp + 1] = y[:, LANES:].reshape(nb, n_chunks, LANES)


def _ssm(u16, weights, nb):
    toep_c, bst_c, cst_c, avec = weights
    bsz, cs, nc, width = u16.shape
    oct_spec = lambda a: pl.BlockSpec((None,) + a.shape[1:], lambda o, b: (o,) + (0,) * (a.ndim - 1))
    io_spec = pl.BlockSpec((nb, cs, nc, LANES), lambda o, b: (b, 0, 0, o))
    n_state = 2 * SSM_OCT * SSM_STATE
    n_pairs = (cs // 2) * (cs // 2 + 1) // 2
    return pl.pallas_call(
        functools.partial(_ssm_kernel, n_chunks=nc, nb=nb),
        grid=(width // LANES, bsz // nb),
        in_specs=[io_spec, oct_spec(toep_c), oct_spec(bst_c), oct_spec(cst_c), oct_spec(avec)],
        out_specs=io_spec,
        out_shape=jax.ShapeDtypeStruct(u16.shape, BF16),
        scratch_shapes=[pltpu.VMEM((n_pairs, 2 * LANES, 2 * LANES), BF16),
                        pltpu.VMEM((cs * LANES, n_state), BF16),
                        pltpu.VMEM((n_state, cs * LANES), BF16),
                        pltpu.VMEM((n_state // LANES, nb * nc, LANES), F32),
                        pltpu.VMEM((n_state // LANES, nb * nc, LANES), F32)],
        compiler_params=_params(2),
        name="ssm",
    )(u16, toep_c, bst_c, cst_c, avec)


def _layer_norm(v, g, b):
    mu = jnp.mean(v, axis=-1, keepdims=True)
    vc = v - mu
    var = jnp.mean(vc * vc, axis=-1, keepdims=True)
    return vc * lax.rsqrt(var + LN_EPS) * g + b


def _merge_kernel(o0_ref, o1_ref, o2_ref, l0_ref, l1_ref, l2_ref, ys_ref, g_ref, x_ref,
                  wglu_ref, wsp_ref, wap_ref, wout_ref, lng_ref, lnb_ref, h_ref, o_scr, l_scr, y_scr):
    tm = x_ref.shape[0]

    def token_order(ref, scr, d):
        if d == 1:
            return ref[0].astype(F32)
        nk = ref.shape[-1] // LANES
        for r in range(d):
            v = ref[r].astype(F32)
            for k in range(nk):
                scr[k, pl.ds(r, tm // d, stride=d), :] = v[:, k * LANES:(k + 1) * LANES]
        return jnp.concatenate([scr[k] for k in range(nk)], axis=-1)

    ls, outs = [], []
    for gi, (o_ref, l_ref) in enumerate(((o0_ref, l0_ref), (o1_ref, l1_ref), (o2_ref, l2_ref))):
        d = DILATION_PATTERNS[gi][1]
        outs.append(token_order(o_ref, o_scr.at[gi], d))
        ls.append(token_order(l_ref, l_scr.at[gi], d))
    mx = jnp.maximum(jnp.maximum(ls[0], ls[1]), ls[2])
    es = [jnp.exp(l - mx) for l in ls]
    num = es[0] * outs[0] + es[1] * outs[1] + es[2] * outs[2]
    y_attn = (num / (es[0] + es[1] + es[2])).astype(BF16)
    ys = token_order(ys_ref, y_scr, SSM_CHUNK).astype(BF16)
    glu = jnp.dot(ys, wglu_ref[...], preferred_element_type=F32)
    y_ssm = (glu[:, :SSM_WIDTH] * jax.nn.sigmoid(glu[:, SSM_WIDTH:])).astype(BF16)
    pa = jnp.dot(y_ssm, wsp_ref[...], preferred_element_type=F32)
    pb = jnp.dot(y_attn, wap_ref[...], preferred_element_type=F32)
    gated = (g_ref[:, :D_MODEL].astype(F32) * pa + g_ref[:, D_MODEL:].astype(F32) * pb).astype(BF16)
    mix = jnp.dot(gated, wout_ref[...], preferred_element_type=F32)
    h_ref[...] = _layer_norm(ALPHA * x_ref[...] + mix, lng_ref[...], lnb_ref[...])


def _merge(outs, lses, ys, gates, x2, w_glu, w_sp, w_ap, w_out, ln_g, ln_b, seqlen, tm):
    t = x2.shape[0]
    tiles = seqlen // tm
    row = lambda w: pl.BlockSpec((tm, w), lambda i: (i, 0))
    res_spec = lambda d, w: pl.BlockSpec((None, d, tm // d, w), lambda i: (i // tiles, 0, i % tiles, 0))
    res_specs = [res_spec(d, GROUP_WIDTH) for _, d in DILATION_PATTERNS]
    return pl.pallas_call(
        _merge_kernel,
        grid=(t // tm,),
        in_specs=res_specs + res_specs + [res_spec(SSM_CHUNK, SSM_WIDTH), row(2 * D_MODEL), row(D_MODEL),
                  _const_spec(w_glu.shape), _const_spec(w_sp.shape), _const_spec(w_ap.shape),
                  _const_spec(w_out.shape), _const_spec((1, D_MODEL)), _const_spec((1, D_MODEL))],
        out_specs=row(D_MODEL),
        out_shape=jax.ShapeDtypeStruct((t, D_MODEL), F32),
        scratch_shapes=[pltpu.VMEM((N_GROUPS, GROUP_WIDTH // LANES, tm, LANES), F32),
                        pltpu.VMEM((N_GROUPS, GROUP_WIDTH // LANES, tm, LANES), F32),
                        pltpu.VMEM((SSM_WIDTH // LANES, tm, LANES), F32)],
        compiler_params=_params(1),
        name="merge_ln1",
    )(*outs, *lses, ys, gates, x2, w_glu, w_sp, w_ap, w_out, ln_g, ln_b)


FF_CHUNK = 1024


def _ffn_kernel(h_ref, wup_ref, wdn_ref, lng_ref, lnb_ref, o_ref):
    h = h_ref[...]
    hb = h.astype(BF16)
    acc = ALPHA * h
    for c in range(D_FF // FF_CHUNK):
        lo, hi = c * FF_CHUNK, (c + 1) * FF_CHUNK
        up = jnp.dot(hb, wup_ref[:, lo:hi], preferred_element_type=F32)
        act = jnp.square(jnp.maximum(up, 0.0)).astype(BF16)
        acc = acc + jnp.dot(act, wdn_ref[lo:hi, :], preferred_element_type=F32)
    o_ref[...] = _layer_norm(acc, lng_ref[...], lnb_ref[...])


def _ffn(h1, w_up, w_down, ln_g, ln_b, tm):
    t = h1.shape[0]
    row = pl.BlockSpec((tm, D_MODEL), lambda i: (i, 0))
    return pl.pallas_call(
        _ffn_kernel,
        grid=(t // tm,),
        in_specs=[row, _const_spec(w_up.shape), _const_spec(w_down.shape),
                  _const_spec((1, D_MODEL)), _const_spec((1, D_MODEL))],
        out_specs=row,
        out_shape=jax.ShapeDtypeStruct((t, D_MODEL), F32),
        compiler_params=_params(1),
        name="ffn_ln2",
    )(h1, w_up, w_down, ln_g, ln_b)


def _permute_w_in(w):
    aw = ATTN_WIDTH
    cols = []
    for gi in range(N_GROUPS):
        lo, hi = gi * GROUP_WIDTH, (gi + 1) * GROUP_WIDTH
        cols += [w[:, lo:hi] * (HEAD_DIM ** -0.5), w[:, aw + lo:aw + hi], w[:, 2 * aw + lo:2 * aw + hi]]
    cols.append(w[:, 3 * aw:])
    return jnp.concatenate(cols, axis=1).astype(BF16)


def _layer(h2, bsz, seqlen, l, w_in, b_gate, lambda_re, lambda_im, log_dt, ssm_b_re, ssm_b_im,
           ssm_c_re, ssm_c_im, ssm_d, w_glu, w_ssm_proj, rel_bias, w_attn_proj, w_out,
           ln1_g, ln1_b, w_up, w_down, ln2_g, ln2_b, tm=512, attn_mt=1024):
    qkv0, qkv1, qkv2, u, gates = _in_proj(h2, _permute_w_in(w_in[l]), b_gate[l][None, :], bsz, seqlen, tm)
    outs, lses = [], []
    for gi, ((window, dilation), qkv) in enumerate(zip(DILATION_PATTERNS, (qkv0, qkv1, qkv2))):
        rb = rel_bias[:, gi * HEADS_PER_GROUP:(gi + 1) * HEADS_PER_GROUP].astype(F32)
        o, s = _attention_group(qkv, rb, window, dilation, attn_mt)
        outs.append(o)
        lses.append(s)
    ssm_w = _ssm_weights(lambda_re[l], lambda_im[l], log_dt[l], ssm_b_re[l], ssm_b_im[l],
                         ssm_c_re[l], ssm_c_im[l], ssm_d[l])
    ys = _ssm(u, ssm_w, nb=4)
    h1 = _merge(outs, lses, ys, gates, h2, w_glu[l].astype(BF16), w_ssm_proj[l].astype(BF16),
                w_attn_proj[l].astype(BF16), w_out[l].astype(BF16),
                ln1_g[l][None, :], ln1_b[l][None, :], seqlen, tm)
    return _ffn(h1, w_up[l].astype(BF16), w_down[l].astype(BF16), ln2_g[l][None, :], ln2_b[l][None, :], tm)


def kernel(x, w_in, b_gate, lambda_re, lambda_im, log_dt, ssm_b_re, ssm_b_im, ssm_c_re, ssm_c_im,
           ssm_d, w_glu, w_ssm_proj, rel_bias, w_attn_proj, w_out, ln1_g, ln1_b, w_up, w_down,
           ln2_g, ln2_b):
    bsz, seqlen, d = x.shape
    h = x.reshape(bsz * seqlen, d)
    for l in range(w_in.shape[0]):
        h = _layer(h, bsz, seqlen, l, w_in, b_gate, lambda_re, lambda_im, log_dt, ssm_b_re, ssm_b_im,
                   ssm_c_re, ssm_c_im, ssm_d, w_glu, w_ssm_proj, rel_bias, w_attn_proj, w_out,
                   ln1_g, ln1_b, w_up, w_down, ln2_g, ln2_b)
    return h.reshape(bsz, seqlen, d)
```

```python
import functools
import math

import jax
import jax.numpy as jnp
from jax import lax
from jax.experimental import pallas as pl
from jax.experimental.pallas import tpu as pltpu

F32 = jnp.float32
BF16 = jnp.bfloat16

D_MODEL = 1024
HEAD_DIM = 64
HEADS_PER_GROUP = 4
GROUP_WIDTH = HEADS_PER_GROUP * HEAD_DIM
DILATION_PATTERNS = ((128, 1), (512, 4), (2048, 16))
N_GROUPS = len(DILATION_PATTERNS)
ATTN_WIDTH = N_GROUPS * GROUP_WIDTH
N_BUCKETS = 32
MAX_DISTANCE = 2048
SSM_WIDTH = 512
SSM_GROUP = 16
SSM_GROUPS = 32
SSM_STATE = 64
D_FF = 4 * D_MODEL
DEPTH = 1
ALPHA = (2.0 * DEPTH) ** 0.25
LN_EPS = 1e-5
NEG_INF = -1e30

ATTN_BLOCK = 128
SSM_CHUNK = 16
SSM_OCT = 8
LANES = 128
VMEM_LIMIT_BYTES =56 * 1024 * 1024


def _params(n_axes):
    return pltpu.CompilerParams(dimension_semantics=("arbitrary",) * n_axes,
                                vmem_limit_bytes=VMEM_LIMIT_BYTES)


def _sigmoid(z):
    return 0.5 * jnp.tanh(0.5 * z) + 0.5


def _const_spec(shape):
    nd = len(shape)
    return pl.BlockSpec(shape, lambda *_: (0,) * nd)


def _in_proj_kernel(x_ref, w_ref, bg_ref, qkv0_ref, qkv1_ref, qkv2_ref, u_ref, g_ref, scr):
    xb = x_ref[...].astype(BF16)
    tm = xb.shape[0]

    def mm(lo, hi):
        return jnp.dot(xb, w_ref[:, lo:hi], preferred_element_type=F32)

    def emit(ref, res, d, col0):
        width = res.shape[1]
        if d == 1:
            ref[0, :, col0:col0 + width] = res.astype(BF16)
            return
        slot0 = next_slot[0]
        next_slot[0] += width // LANES
        for k in range(width // LANES):
            scr[slot0 + k] = res[:, k * LANES:(k + 1) * LANES]
        for r in range(d):
            for k in range(width // LANES):
                col = col0 + k * LANES
                ref[r, :, col:col + LANES] = scr[slot0 + k, pl.ds(r, tm // d, stride=d), :].astype(BF16)

    next_slot = [0]

    gw3 = 3 * GROUP_WIDTH
    for gi, ref in enumerate((qkv0_ref, qkv1_ref, qkv2_ref)):
        for c in range(3):
            lo = gi * gw3 + c * GROUP_WIDTH
            emit(ref, mm(lo, lo + GROUP_WIDTH), DILATION_PATTERNS[gi][1], c * GROUP_WIDTH)
    base = N_GROUPS * gw3
    emit(u_ref, mm(base, base + SSM_WIDTH), SSM_CHUNK, 0)
    base += SSM_WIDTH
    for c in range(4):
        lo, hi = c * 512, (c + 1) * 512
        z = mm(base + lo, base + hi) + bg_ref[:, lo:hi]
        g_ref[:, lo:hi] = _sigmoid(z).astype(BF16)


def _in_proj(x2, w_perm, b_gate, bsz, seqlen, tm):
    t = x2.shape[0]
    n_in = w_perm.shape[1]
    tiles = seqlen // tm
    row = lambda w: pl.BlockSpec((tm, w), lambda i: (i, 0))
    w3 = 3 * GROUP_WIDTH
    dils = [d for _, d in DILATION_PATTERNS]
    n_stage = (sum(d > 1 for d in dils) * w3 + SSM_WIDTH) // LANES
    res_spec = lambda d, w:pl.BlockSpec((None, d, tm // d, w), lambda i: (i // tiles, 0, i % tiles, 0))
    return pl.pallas_call(
        _in_proj_kernel,
        grid=(t // tm,),
        in_specs=[row(D_MODEL), _const_spec((D_MODEL, n_in)), _const_spec((1, 2 * D_MODEL))],
        out_specs=[res_spec(d, w3) for d in dils] + [res_spec(SSM_CHUNK, SSM_WIDTH), row(2 * D_MODEL)],
        out_shape=[jax.ShapeDtypeStruct((bsz, d, seqlen // d, w3), BF16) for d in dils]
        + [jax.ShapeDtypeStruct((bsz, SSM_CHUNK, seqlen // SSM_CHUNK, SSM_WIDTH), BF16),
           jax.ShapeDtypeStruct((t, 2 * D_MODEL), BF16)],
        scratch_shapes=[pltpu.VMEM((n_stage, tm, LANES), F32)],
        compiler_params=_params(1),
        name="in_proj",
    )(x2, w_perm, b_gate)


def _attn_kernel(bucket_ref, relb_ref, cur_ref, prev_ref, out_ref, lse_ref, bias_scr, kv_scr, *, nq, nres):
    blk = ATTN_BLOCK
    first = (pl.program_id(0) == 0) & (pl.program_id(1) == 0) & (pl.program_id(2) == 0)

    @pl.when(first)
    def _build_bias():
        bucket = bucket_ref[...]
        for h in range(HEADS_PER_GROUP):
            acc = jnp.full(bucket.shape, NEG_INF, F32)
            for bkt in range(N_BUCKETS):
                acc = jnp.where(bucket == bkt, relb_ref[bkt, h], acc)
            bias_scr[h * blk:(h + 1) * blk, :] = acc

    lane_head = lax.broadcasted_iota(jnp.int32, (blk, GROUP_WIDTH), 1) // HEAD_DIM
    col = lax.broadcasted_iota(jnp.int32, (HEADS_PER_GROUP * blk, 2 * blk), 1)
    keep_first = (col >= blk) | (pl.program_id(2) > 0)
    nt = (((1,), (1,)), ((), ()))
    for r in range(nres):
        kv_scr[r, 0:blk, :] = prev_ref[r, :, GROUP_WIDTH:3 * GROUP_WIDTH]
        kv_scr[r, blk:, :] = cur_ref[r, :, GROUP_WIDTH:3 * GROUP_WIDTH]
        for j in range(nq):
            q = cur_ref[r, j * blk:(j + 1) * blk, 0:GROUP_WIDTH]
            k2 = kv_scr[r, j * blk:(j + 2) * blk, 0:GROUP_WIDTH]
            v2 = kv_scr[r, j * blk:(j + 2) * blk, GROUP_WIDTH:2 * GROUP_WIDTH]
            qs = jnp.concatenate(
                [jnp.where(lane_head == h, q, jnp.zeros_like(q)) for h in range(HEADS_PER_GROUP)], axis=0)
            s = lax.dot_general(qs, k2, nt, preferred_element_type=F32) + bias_scr[...]
            if j == 0:
                s = jnp.where(keep_first, s, NEG_INF)
            m = jnp.max(s, axis=-1, keepdims=True)
            p = jnp.exp(s - m)
            l = jnp.sum(p, axis=-1, keepdims=True)
            pv = jnp.dot(p.astype(BF16), v2, preferred_element_type=F32) * (1.0 / l)
            lse = m + jnp.log(l)
            o = pv[0:blk]
            e = jnp.broadcast_to(lse[0:blk], (blk, GROUP_WIDTH))
            for h in range(1, HEADS_PER_GROUP):
                sel = lane_head == h
                o = jnp.where(sel, pv[h * blk:(h + 1) * blk], o)
                e = jnp.where(sel, lse[h * blk:(h + 1) * blk], e)
            out_ref[r, j * blk:(j + 1) * blk, :] = o.astype(BF16)
            lse_ref[r, j * blk:(j + 1) * blk, :] = e


def _t5_bucket(dist):
    max_exact = N_BUCKETS // 2
    d = jnp.maximum(dist, 1).astype(F32)
    large = max_exact + (jnp.log(d / max_exact) / math.log(MAX_DISTANCE / max_exact)
                         * (N_BUCKETS - max_exact)).astype(jnp.int32)
    large = jnp.minimum(large, N_BUCKETS - 1)
    return jnp.where(dist < max_exact, dist, large)


def _bucket_table(window, dilation):
    blk = ATTN_BLOCK
    span = window // dilation
    rel = jnp.arange(blk)[:, None] + blk - jnp.arange(2 * blk)[None, :]
    valid = (rel >= 0) & (rel <= span)
    return jnp.where(valid, _t5_bucket(jnp.maximum(rel, 0) * dilation), -1).astype(jnp.int32)


def _attention_group(qkv, rel_bias_g, window, dilation, mt):
    assert window // dilation == ATTN_BLOCK
    bsz, _, n, w3 = qkv.shape
    rows = mt
    mt = min(rows, n)
    nres = rows // mt
    nq = mt // ATTN_BLOCK
    cur = pl.BlockSpec((None, nres, mt, w3), lambda b, r, i: (b, r, i, 0))
    prev = pl.BlockSpec((None, nres, ATTN_BLOCK, w3), lambda b, r, i: (b, r, jnp.maximum(i * nq - 1, 0), 0))
    ospec = pl.BlockSpec((None, nres, mt, GROUP_WIDTH), lambda b, r, i: (b, r, i, 0))
    return pl.pallas_call(
        functools.partial(_attn_kernel, nq=nq, nres=nres),
        grid=(bsz, dilation // nres, n // mt),
        in_specs=[_const_spec((ATTN_BLOCK, 2 * ATTN_BLOCK)),
                  pl.BlockSpec(memory_space=pltpu.SMEM), cur, prev],
        out_specs=[ospec, ospec],
        out_shape=[jax.ShapeDtypeStruct((bsz, dilation, n, GROUP_WIDTH), BF16),
                   jax.ShapeDtypeStruct((bsz, dilation, n, GROUP_WIDTH), F32)],
        scratch_shapes=[pltpu.VMEM((HEADS_PER_GROUP * ATTN_BLOCK, 2 * ATTN_BLOCK), F32),
                        pltpu.VMEM((nres, mt + ATTN_BLOCK, 2 * GROUP_WIDTH), BF16)],
        compiler_params=_params(3),
        name=f"attn_d{dilation}",
    )(_bucket_table(window, dilation), rel_bias_g, qkv, qkv)


def _ssm_prep_kernel(logdt_ref, lr_ref, li_ref, bt_re_ref, bt_im_ref, c_re_ref, c_im_ref, dl_ref,
                     toep_ref, bst_ref, cst_ref, a_ref):
    cs = SSM_CHUNK
    dt = jnp.exp(jnp.full((1, SSM_STATE), logdt_ref[pl.program_id(0)], F32))
    lr, li = lr_ref[...], li_ref[...]
    mag = jnp.exp(lr * dt)
    ab_re, ab_im = mag * jnp.cos(li * dt), mag * jnp.sin(li * dt)
    den = lr * lr + li * li
    nr = ab_re - 1.0
    k_re = (nr * lr + ab_im * li) / den
    k_im = (ab_im * lr - nr * li) / den
    bt_re, bt_im = bt_re_ref[...], bt_im_ref[...]
    bb_re = k_re * bt_re - k_im * bt_im
    bb_im = k_re * bt_im + k_im * bt_re
    j = lax.broadcasted_iota(jnp.int32, (cs + 8, SSM_STATE), 0).astype(F32)
    pmag = jnp.exp(lr * dt * j)
    ang = li * dt * j
    p_re, p_im = pmag * jnp.cos(ang), pmag * jnp.sin(ang)
    c_re, c_im = c_re_ref[...], c_im_ref[...]
    cp_re = [c_re * p_re[i:i + 1] - c_im * p_im[i:i + 1] for i in range(cs + 1)]
    cp_im = [c_re * p_im[i:i + 1] + c_im * p_re[i:i + 1] for i in range(cs + 1)]
    nt = (((1,), (1,)), ((), ()))
    hi = lax.Precision.HIGHEST
    kcat = (lax.dot_general(bb_re, jnp.concatenate(cp_re[:cs], axis=0), nt, precision=hi,
                            preferred_element_type=F32)
            - lax.dot_general(bb_im, jnp.concatenate(cp_im[:cs], axis=0), nt, precision=hi,
                              preferred_element_type=F32))
    lane = lax.broadcasted_iota(jnp.int32, kcat.shape, 1)
    row = lax.broadcasted_iota(jnp.int32, kcat.shape, 0)
    dl = dl_ref[...]
    for s in range(cs):
        off = s * SSM_GROUP
        t_s = kcat if s == 0 else jnp.where(lane >= off, pltpu.roll(kcat, off, 1), 0.0)
        toep_ref[s] = (t_s + jnp.where(lane == off + row, dl, 0.0)).astype(BF16)
        pe_re, pe_im = p_re[cs - 1 - s:cs - s], p_im[cs - 1 - s:cs - s]
        st_re = pe_re * bb_re - pe_im * bb_im
        st_im = pe_re * bb_im + pe_im * bb_re
        bst_ref[s] = jnp.concatenate([st_re, st_im], axis=-1).astype(BF16)
    ro = jnp.concatenate([jnp.concatenate(cp_re[1:], axis=0),
                          -jnp.concatenate(cp_im[1:], axis=0)], axis=-1)
    ro_t = ro.T
    cst_ref[0] = ro_t[:SSM_STATE].astype(BF16)
    cst_ref[1] = ro_t[SSM_STATE:].astype(BF16)
    a_ref[...] = jnp.concatenate([p_re[cs:cs + 1], p_im[cs:cs + 1]], axis=0)


def _ssm_weights(lambda_re, lambda_im, log_dt, b_re, b_im, c_re, c_im, d_skip):
    f32 = F32
    cs, ng, no = SSM_CHUNK, SSM_GROUPS, SSM_GROUPS // SSM_OCT
    grp = lambda *shape: pl.BlockSpec((None,) + shape, lambda g: (g,) + (0,) * len(shape))
    in_oct = lambda *shape: pl.BlockSpec((None, shape[0], None) + shape[1:],
                                         lambda g: (g // SSM_OCT, 0, g % SSM_OCT) + (0,) * (len(shape) - 1))
    toep_c, bst_c, cst_c, a32 = pl.pallas_call(
        _ssm_prep_kernel,
        grid=(ng,),
        in_specs=[pl.BlockSpec(memory_space=pltpu.SMEM), grp(1, SSM_STATE), grp(1, SSM_STATE),
                  grp(SSM_GROUP, SSM_STATE), grp(SSM_GROUP, SSM_STATE),
                  grp(SSM_GROUP, SSM_STATE), grp(SSM_GROUP, SSM_STATE), grp(1, cs * SSM_GROUP)],
        out_specs=[in_oct(cs, SSM_GROUP, cs * SSM_GROUP), in_oct(cs, SSM_GROUP, 2 * SSM_STATE),
                   in_oct(2, SSM_STATE, cs * SSM_GROUP), grp(2, SSM_STATE)],
        out_shape=[jax.ShapeDtypeStruct((no, cs, SSM_OCT, SSM_GROUP, cs * SSM_GROUP), BF16),
                   jax.ShapeDtypeStruct((no, cs, SSM_OCT, SSM_GROUP, 2 * SSM_STATE), BF16),
                   jax.ShapeDtypeStruct((no, 2, SSM_OCT, SSM_STATE, cs * SSM_GROUP), BF16),
                   jax.ShapeDtypeStruct((ng, 2, SSM_STATE), f32)],
        compiler_params=_params(1),
        name="ssm_prep",
    )(log_dt.astype(f32), lambda_re.astype(f32)[:, None, :], lambda_im.astype(f32)[:, None, :],
      b_re.astype(f32).transpose(0, 2, 1), b_im.astype(f32).transpose(0, 2, 1),
      c_re.astype(f32), c_im.astype(f32), jnp.tile(d_skip.astype(f32), (1, cs))[:, None, :])
    avec = a32.reshape(no, SSM_OCT, 2, SSM_STATE).transpose(0, 2, 1, 3).reshape(no, 2, SSM_OCT * SSM_STATE)
    return (toep_c.reshape(no, cs * LANES, cs * SSM_GROUP),
            bst_c.reshape(no, cs * LANES, 2 * SSM_STATE),
            cst_c.reshape(no, 2 * SSM_OCT * SSM_STATE, cs * SSM_GROUP),
            avec)


def _expand_ssm_weights(toep_ref, bstc_ref, cstc_ref, wt_scr, bst_scr, cst_scr):
    tw = 2 * LANES
    sh_g, sh_n = SSM_GROUP.bit_length() - 1, SSM_STATE.bit_length() - 1
    r = lax.broadcasted_iota(jnp.int32, (tw, tw), 0)
    c = lax.broadcasted_iota(jnp.int32, (tw, tw), 1)
    col_grp = (c >> sh_g) & (SSM_OCT - 1)
    row_grp = (r >> sh_g) & (SSM_OCT - 1)
    for tp in range(SSM_CHUNK // 2):
        src_col = (2 * tp + (c >> (sh_g + 3))) * SSM_GROUP + (c & (SSM_GROUP - 1))
        e = jnp.where(r == src_col, 1.0, 0.0).astype(BF16)
        base = tp * (tp + 1) // 2
        for sp in range(tp + 1):
            x = jnp.dot(toep_ref[sp * tw:(sp + 1) * tw, :], e, preferred_element_type=F32)
            wt_scr[base + sp] = jnp.where(row_grp == col_grp, x, 0.0).astype(BF16)
        for j in range(2 * SSM_OCT * SSM_STATE // tw):
            x = jnp.dot(cstc_ref[j * tw:(j + 1) * tw, :], e, preferred_element_type=F32)
            row_grp_n = ((j * tw + r) >> sh_n) & (SSM_OCT - 1)
            cst_scr[j * tw:(j + 1) * tw, tp * tw:(tp + 1) * tw] = jnp.where(row_grp_n == col_grp, x, 0.0).astype(BF16)
    ns = 2 * SSM_OCT * SSM_STATE
    rb = lax.broadcasted_iota(jnp.int32, (2 * SSM_STATE, ns), 0)
    cb = lax.broadcasted_iota(jnp.int32, (2 * SSM_STATE, ns), 1)
    src_col_b = (cb >> (sh_n + 3)) * SSM_STATE + (cb & (SSM_STATE - 1))
    eb = jnp.where(rb == src_col_b, 1.0, 0.0).astype(BF16)
    rr = lax.broadcasted_iota(jnp.int32, (tw, ns), 0)
    cc = lax.broadcasted_iota(jnp.int32, (tw, ns), 1)
    keep = ((rr >> sh_g) & (SSM_OCT - 1)) == ((cc >> sh_n) & (SSM_OCT - 1))
    for j in range(SSM_CHUNK * LANES // tw):
        x = jnp.dot(bstc_ref[j * tw:(j + 1) * tw, :], eb, preferred_element_type=F32)
        bst_scr[j * tw:(j + 1) * tw, :] = jnp.where(keep, x, 0.0).astype(BF16)


def _ssm_kernel(u_ref, toep_ref, bstc_ref, cstc_ref, a_ref, y_ref, wt_ref, bst_ref, cst_ref, zz_scr, hp_scr,
                *, n_chunks, nb):
    @pl.when(pl.program_id(1) == 0)
    def _new_octet():
        _expand_ssm_weights(toep_ref, bstc_ref, cstc_ref, wt_ref, bst_ref, cst_ref)

    rows = nb * n_chunks
    nk = SSM_OCT * SSM_STATE // LANES
    u_t = [u_ref[:, s].reshape(rows, LANES) for s in range(SSM_CHUNK)]
    z = jnp.dot(jnp.concatenate(u_t, axis=-1), bst_ref[...], preferred_element_type=F32)
    for k in range(2 * nk):
        zz_scr[k] = z[:, k * LANES:(k + 1) * LANES]
    a_re = [jnp.broadcast_to(a_ref[0:1, k * LANES:(k + 1) * LANES], (nb, LANES)) for k in range(nk)]
    a_im = [jnp.broadcast_to(a_ref[1:2, k * LANES:(k + 1) * LANES], (nb, LANES)) for k in range(nk)]

    def step(c, carry):
        h_re, h_im = carry
        rows_c = pl.ds(c, nb, stride=n_chunks)
        new_re, new_im = [], []
        for k in range(nk):
            hp_scr[k, rows_c, :] = h_re[k]
            hp_scr[nk + k, rows_c, :] = h_im[k]
            new_re.append(a_re[k] * h_re[k] - a_im[k] * h_im[k] + zz_scr[k, rows_c, :])
            new_im.append(a_re[k] * h_im[k] + a_im[k] * h_re[k] + zz_scr[nk + k, rows_c, :])
        return tuple(new_re), tuple(new_im)

    zero = tuple(jnp.zeros((nb, LANES), F32) for _ in range(nk))
    lax.fori_loop(0, n_chunks, step, (zero, zero), unroll=4)
    hp = jnp.concatenate([hp_scr[k] for k in range(2 * nk)], axis=-1).astype(BF16)
    tw = 2 * LANES
    for tp in range(SSM_CHUNK // 2):
        base = tp * (tp + 1) // 2
        y = jnp.dot(hp, cst_ref[:, tp * tw:(tp + 1) * tw], preferred_element_type=F32)
        y = y + jnp.dot(jnp.concatenate(u_t[:2 * (tp + 1)], axis=-1),
                        wt_ref[base:base + tp + 1].reshape((tp + 1) * tw, tw), preferred_element_type=F32)
        y = jax.nn.gelu(y).astype(BF16)
        y_ref[:, 2 * tp] = y[:, :LANES].reshape(nb, n_chunks, LANES)
        y_ref[:, 2 * tp + 1] = y[:, LANES:].reshape(nb, n_chunks, LANES)


def _ssm(u16, weights, nb):
    toep_c, bst_c, cst_c, avec = weights
    bsz, cs, nc, width = u16.shape
    oct_spec = lambda a: pl.BlockSpec((None,) + a.shape[1:], lambda o, b: (o,) + (0,) * (a.ndim - 1))
    io_spec = pl.BlockSpec((nb, cs, nc, LANES), lambda o, b: (b, 0, 0, o))
    n_state = 2 * SSM_OCT * SSM_STATE
    n_pairs = (cs // 2) * (cs // 2 + 1) // 2
    return pl.pallas_call(
        functools.partial(_ssm_kernel, n_chunks=nc, nb=nb),
        grid=(width // LANES, bsz // nb),
        in_specs=[io_spec, oct_spec(toep_c), oct_spec(bst_c), oct_spec(cst_c), oct_spec(avec)],
        out_specs=io_spec,
        out_shape=jax.ShapeDtypeStruct(u16.shape, BF16),
        scratch_shapes=[pltpu.VMEM((n_pairs, 2 * LANES, 2 * LANES), BF16),
                        pltpu.VMEM((cs * LANES, n_state), BF16),
                        pltpu.VMEM((n_state, cs * LANES), BF16),
                        pltpu.VMEM((n_state // LANES, nb * nc, LANES), F32),
                        pltpu.VMEM((n_state // LANES, nb * nc, LANES), F32)],
        compiler_params=_params(2),
        name="ssm",
    )(u16, toep_c, bst_c, cst_c, avec)


def _layer_norm(v, g, b):
    mu = jnp.mean(v, axis=-1, keepdims=True)
    vc = v - mu
    var = jnp.mean(vc * vc, axis=-1, keepdims=True)
    return vc * lax.rsqrt(var + LN_EPS) * g + b


def _merge_kernel(o0_ref, o1_ref, o2_ref, l0_ref, l1_ref, l2_ref, ys_ref, g_ref, x_ref,
                  wglu_ref, wsp_ref, wap_ref, wout_ref, lng_ref, lnb_ref, h_ref, o_scr, l_scr, y_scr):
    tm = x_ref.shape[0]

    def token_order(ref, scr, d):
        if d == 1:
            return ref[0].astype(F32)
        nk = ref.shape[-1] // LANES
        for r in range(d):
            v = ref[r].astype(F32)
            for k in range(nk):
                scr[k, pl.ds(r, tm // d, stride=d), :] = v[:, k * LANES:(k + 1) * LANES]
        return jnp.concatenate([scr[k] for k in range(nk)], axis=-1)

    ls, outs = [], []
    for gi, (o_ref, l_ref) in enumerate(((o0_ref, l0_ref), (o1_ref, l1_ref), (o2_ref, l2_ref))):
        d = DILATION_PATTERNS[gi][1]
        outs.append(token_order(o_ref, o_scr.at[gi], d))
        ls.append(token_order(l_ref, l_scr.at[gi], d))
    mx = jnp.maximum(jnp.maximum(ls[0], ls[1]), ls[2])
    es = [jnp.exp(l - mx) for l in ls]
    num = es[0] * outs[0] + es[1] * outs[1] + es[2] * outs[2]
    y_attn = (num / (es[0] + es[1] + es[2])).astype(BF16)
    ys = token_order(ys_ref, y_scr, SSM_CHUNK).astype(BF16)
    cw = GROUP_WIDTH
    y_ssm = []
    for c in range(SSM_WIDTH // cw):
        a = jnp.dot(ys, wglu_ref[:, c * cw:(c + 1) * cw], preferred_element_type=F32)
        b = jnp.dot(ys, wglu_ref[:, SSM_WIDTH + c * cw:SSM_WIDTH + (c + 1) * cw], preferred_element_type=F32)
        y_ssm.append((a * _sigmoid(b)).astype(BF16))
    y_ssm = jnp.concatenate(y_ssm, axis=-1)
    gated = []
    for c in range(D_MODEL // cw):
        pa = jnp.dot(y_ssm, wsp_ref[:, c * cw:(c + 1) * cw], preferred_element_type=F32)
        pb = jnp.dot(y_attn, wap_ref[:, c * cw:(c + 1) * cw], preferred_element_type=F32)
        g_ssm = g_ref[:, c * cw:(c + 1) * cw].astype(F32)
        g_attn = g_ref[:, D_MODEL + c * cw:D_MODEL + (c + 1) * cw].astype(F32)
        gated.append((g_ssm * pa + g_attn * pb).astype(BF16))
    gated = jnp.concatenate(gated, axis=-1)
    mix = jnp.dot(gated, wout_ref[...], preferred_element_type=F32)
    h_ref[...] = _layer_norm(ALPHA * x_ref[...] + mix, lng_ref[...], lnb_ref[...])


def _merge(outs, lses, ys, gates, x2, w_glu, w_sp, w_ap, w_out, ln_g, ln_b, seqlen, tm):
    t = x2.shape[0]
    tiles = seqlen // tm
    row = lambda w: pl.BlockSpec((tm, w), lambda i: (i, 0))
    res_spec = lambda d, w: pl.BlockSpec((None, d, tm // d, w), lambda i: (i // tiles, 0, i % tiles, 0))
    res_specs = [res_spec(d, GROUP_WIDTH) for _, d in DILATION_PATTERNS]
    return pl.pallas_call(
        _merge_kernel,
        grid=(t // tm,),
        in_specs=res_specs + res_specs + [res_spec(SSM_CHUNK, SSM_WIDTH), row(2 * D_MODEL), row(D_MODEL),
                  _const_spec(w_glu.shape), _const_spec(w_sp.shape), _const_spec(w_ap.shape),
                  _const_spec(w_out.shape), _const_spec((1, D_MODEL)), _const_spec((1, D_MODEL))],
        out_specs=row(D_MODEL),
        out_shape=jax.ShapeDtypeStruct((t, D_MODEL), F32),
        scratch_shapes=[pltpu.VMEM((N_GROUPS, GROUP_WIDTH // LANES, tm, LANES), F32),
                        pltpu.VMEM((N_GROUPS, GROUP_WIDTH // LANES, tm, LANES), F32),
                        pltpu.VMEM((SSM_WIDTH // LANES, tm, LANES), F32)],
        compiler_params=_params(1),
        name="merge_ln1",
    )(*outs, *lses, ys, gates, x2, w_glu, w_sp, w_ap, w_out, ln_g, ln_b)


FF_CHUNK = 1024


def _ffn_kernel(h_ref, wup_ref, wdn_ref, lng_ref, lnb_ref, o_ref):
    h = h_ref[...]
    hb = h.astype(BF16)
    acc = ALPHA * h
    for c in range(D_FF // FF_CHUNK):
        lo, hi = c * FF_CHUNK, (c + 1) * FF_CHUNK
        up = jnp.dot(hb, wup_ref[:, lo:hi], preferred_element_type=F32)
        act = jnp.square(jnp.maximum(up, 0.0)).astype(BF16)
        acc = acc + jnp.dot(act, wdn_ref[lo:hi, :], preferred_element_type=F32)
    o_ref[...] = _layer_norm(acc, lng_ref[...], lnb_ref[...])


def _ffn(h1, w_up, w_down, ln_g, ln_b, tm):
    t = h1.shape[0]
    row = pl.BlockSpec((tm, D_MODEL), lambda i: (i, 0))
    return pl.pallas_call(
        _ffn_kernel,
        grid=(t // tm,),
        in_specs=[row, _const_spec(w_up.shape), _const_spec(w_down.shape),
                  _const_spec((1, D_MODEL)), _const_spec((1, D_MODEL))],
        out_specs=row,
        out_shape=jax.ShapeDtypeStruct((t, D_MODEL), F32),
        compiler_params=_params(1),
        name="ffn_ln2",
    )(h1, w_up, w_down, ln_g, ln_b)


def _permute_w_in(w):
    aw = ATTN_WIDTH
    cols = []
    for gi in range(N_GROUPS):
        lo, hi = gi * GROUP_WIDTH, (gi + 1) * GROUP_WIDTH
        cols += [w[:, lo:hi] * (HEAD_DIM ** -0.5), w[:, aw + lo:aw + hi], w[:, 2 * aw + lo:2 * aw + hi]]
    cols.append(w[:, 3 * aw:])
    return jnp.concatenate(cols, axis=1).astype(BF16)


def _layer(h2, bsz, seqlen, l, w_in, b_gate, lambda_re, lambda_im, log_dt, ssm_b_re, ssm_b_im,
           ssm_c_re, ssm_c_im, ssm_d, w_glu, w_ssm_proj, rel_bias, w_attn_proj, w_out,
           ln1_g, ln1_b, w_up, w_down, ln2_g, ln2_b, tm=512, attn_mt=1024):
    qkv0, qkv1, qkv2, u, gates = _in_proj(h2, _permute_w_in(w_in[l]), b_gate[l][None, :], bsz, seqlen, tm)
    outs, lses = [], []
    for gi, ((window, dilation), qkv) in enumerate(zip(DILATION_PATTERNS, (qkv0, qkv1, qkv2))):
        rb = rel_bias[:, gi * HEADS_PER_GROUP:(gi + 1) * HEADS_PER_GROUP].astype(F32)
        o, s = _attention_group(qkv, rb, window, dilation, attn_mt)
        outs.append(o)
        lses.append(s)
    ssm_w = _ssm_weights(lambda_re[l], lambda_im[l], log_dt[l], ssm_b_re[l], ssm_b_im[l],
                         ssm_c_re[l], ssm_c_im[l], ssm_d[l])
    ys = _ssm(u, ssm_w, nb=4)
    h1 = _merge(outs, lses, ys, gates, h2, w_glu[l].astype(BF16), w_ssm_proj[l].astype(BF16),
                w_attn_proj[l].astype(BF16), w_out[l].astype(BF16),
                ln1_g[l][None, :], ln1_b[l][None, :], seqlen, tm)
    return _ffn(h1, w_up[l].astype(BF16), w_down[l].astype(BF16), ln2_g[l][None, :], ln2_b[l][None, :], tm)


def kernel(x, w_in, b_gate, lambda_re, lambda_im, log_dt, ssm_b_re, ssm_b_im, ssm_c_re, ssm_c_im,
           ssm_d, w_glu, w_ssm_proj, rel_bias, w_attn_proj, w_out, ln1_g, ln1_b, w_up, w_down,
           ln2_g, ln2_b):
    bsz, seqlen, d = x.shape
    h = x.reshape(bsz * seqlen, d)
    for l in range(w_in.shape[0]):
        h = _layer(h, bsz, seqlen, l, w_in, b_gate, lambda_re, lambda_im, log_dt, ssm_b_re, ssm_b_im,
                   ssm_c_re, ssm_c_im, ssm_d, w_glu, w_ssm_proj, rel_bias, w_attn_proj, w_out,
                   ln1_g, ln1_b, w_up, w_down, ln2_g, ln2_b)
    return h.reshape(bsz, seqlen, d)
```

```python
import functools
import math

import jax
import jax.numpy as jnp
from jax import lax
from jax.experimental import pallas as pl
from jax.experimental.pallas import tpu as pltpu

F32 = jnp.float32
BF16 = jnp.bfloat16

D_MODEL = 1024
HEAD_DIM = 64
HEADS_PER_GROUP = 4
GROUP_WIDTH = HEADS_PER_GROUP * HEAD_DIM
DILATION_PATTERNS = ((128, 1), (512, 4), (2048, 16))
N_GROUPS = len(DILATION_PATTERNS)
ATTN_WIDTH = N_GROUPS * GROUP_WIDTH
N_BUCKETS = 32
MAX_DISTANCE = 2048
SSM_WIDTH = 512
SSM_GROUP = 16
SSM_GROUPS = 32
SSM_STATE = 64
D_FF = 4 * D_MODEL
DEPTH = 1
ALPHA = (2.0 * DEPTH) ** 0.25
LN_EPS = 1e-5
NEG_INF = -1e30

ATTN_BLOCK = 128
SSM_CHUNK = 16
SSM_OCT = 8
LANES = 128
VMEM_LIMIT_BYTES =56 * 1024 * 1024


def _params(n_axes):
    return pltpu.CompilerParams(dimension_semantics=("arbitrary",) * n_axes,
                                vmem_limit_bytes=VMEM_LIMIT_BYTES)


def _sigmoid(z):
    return 0.5 * jnp.tanh(0.5 * z) + 0.5


def _const_spec(shape):
    nd = len(shape)
    return pl.BlockSpec(shape, lambda *_: (0,) * nd, pipeline_mode=pl.Buffered(1))


def _in_proj_kernel(x_ref, w_ref, bg_ref, qkv0_ref, qkv1_ref, qkv2_ref, u_ref, g_ref, scr):
    xb = x_ref[...].astype(BF16)
    tm = xb.shape[0]

    def mm(lo, hi):
        return jnp.dot(xb, w_ref[:, lo:hi], preferred_element_type=F32)

    def emit(ref, res, d, col0):
        width = res.shape[1]
        if d == 1:
            ref[0, :, col0:col0 + width] = res.astype(BF16)
            return
        slot0 = next_slot[0]
        next_slot[0] += width // LANES
        for k in range(width // LANES):
            scr[slot0 + k] = res[:, k * LANES:(k + 1) * LANES]
        for r in range(d):
            for k in range(width // LANES):
                col = col0 + k * LANES
                ref[r, :, col:col + LANES] = scr[slot0 + k, pl.ds(r, tm // d, stride=d), :].astype(BF16)

    next_slot = [0]

    gw3 = 3 * GROUP_WIDTH
    for gi, ref in enumerate((qkv0_ref, qkv1_ref, qkv2_ref)):
        for c in range(3):
            lo = gi * gw3 + c * GROUP_WIDTH
            emit(ref, mm(lo, lo + GROUP_WIDTH), DILATION_PATTERNS[gi][1], c * GROUP_WIDTH)
    base = N_GROUPS * gw3
    emit(u_ref, mm(base, base + SSM_WIDTH), SSM_CHUNK, 0)
    base += SSM_WIDTH
    for c in range(4):
        lo, hi = c * 512, (c + 1) * 512
        z = mm(base + lo, base + hi) + bg_ref[:, lo:hi]
        g_ref[:, lo:hi] = _sigmoid(z).astype(BF16)


def _in_proj(x2, w_perm, b_gate, bsz, seqlen, tm):
    t = x2.shape[0]
    n_in = w_perm.shape[1]
    tiles = seqlen // tm
    row = lambda w: pl.BlockSpec((tm, w), lambda i: (i, 0))
    w3 = 3 * GROUP_WIDTH
    dils = [d for _, d in DILATION_PATTERNS]
    n_stage = (sum(d > 1 for d in dils) * w3 + SSM_WIDTH) // LANES
    res_spec = lambda d, w:pl.BlockSpec((None, d, tm // d, w), lambda i: (i // tiles, 0, i % tiles, 0))
    return pl.pallas_call(
        _in_proj_kernel,
        grid=(t // tm,),
        in_specs=[row(D_MODEL), _const_spec((D_MODEL, n_in)), _const_spec((1, 2 * D_MODEL))],
        out_specs=[res_spec(d, w3) for d in dils] + [res_spec(SSM_CHUNK, SSM_WIDTH), row(2 * D_MODEL)],
        out_shape=[jax.ShapeDtypeStruct((bsz, d, seqlen // d, w3), BF16) for d in dils]
        + [jax.ShapeDtypeStruct((bsz, SSM_CHUNK, seqlen // SSM_CHUNK, SSM_WIDTH), BF16),
           jax.ShapeDtypeStruct((t, 2 * D_MODEL), BF16)],
        scratch_shapes=[pltpu.VMEM((n_stage, tm, LANES), F32)],
        compiler_params=_params(1),
        name="in_proj",
    )(x2, w_perm, b_gate)


def _attn_kernel(bucket_ref, relb_ref, cur_ref, prev_ref, out_ref, lse_ref, bias_scr, kv_scr, *, nq, nres):
    blk = ATTN_BLOCK
    first = (pl.program_id(0) == 0) & (pl.program_id(1) == 0) & (pl.program_id(2) == 0)

    @pl.when(first)
    def _build_bias():
        bucket = bucket_ref[...]
        for h in range(HEADS_PER_GROUP):
            acc = jnp.full(bucket.shape, NEG_INF, F32)
            for bkt in range(N_BUCKETS):
                acc = jnp.where(bucket == bkt, relb_ref[bkt, h], acc)
            bias_scr[h * blk:(h + 1) * blk, :] = acc

    lane_head = lax.broadcasted_iota(jnp.int32, (blk, GROUP_WIDTH), 1) // HEAD_DIM
    col = lax.broadcasted_iota(jnp.int32, (HEADS_PER_GROUP * blk, 2 * blk), 1)
    keep_first = (col >= blk) | (pl.program_id(2) > 0)
    nt = (((1,), (1,)), ((), ()))
    for r in range(nres):
        kv_scr[r, 0:blk, :] = prev_ref[r, :, GROUP_WIDTH:3 * GROUP_WIDTH]
        kv_scr[r, blk:, :] = cur_ref[r, :, GROUP_WIDTH:3 * GROUP_WIDTH]
        for j in range(nq):
            q = cur_ref[r, j * blk:(j + 1) * blk, 0:GROUP_WIDTH]
            k2 = kv_scr[r, j * blk:(j + 2) * blk, 0:GROUP_WIDTH]
            v2 = kv_scr[r, j * blk:(j + 2) * blk, GROUP_WIDTH:2 * GROUP_WIDTH]
            qs = jnp.concatenate(
                [jnp.where(lane_head == h, q, jnp.zeros_like(q)) for h in range(HEADS_PER_GROUP)], axis=0)
            s = lax.dot_general(qs, k2, nt, preferred_element_type=F32) + bias_scr[...]
            if j == 0:
                s = jnp.where(keep_first, s, NEG_INF)
            m = jnp.max(s, axis=-1, keepdims=True)
            p = jnp.exp(s - m)
            l = jnp.sum(p, axis=-1, keepdims=True)
            pv = jnp.dot(p.astype(BF16), v2, preferred_element_type=F32) * (1.0 / l)
            lse = m + jnp.log(l)
            o = pv[0:blk]
            e = jnp.broadcast_to(lse[0:blk], (blk, GROUP_WIDTH))
            for h in range(1, HEADS_PER_GROUP):
                sel = lane_head == h
                o = jnp.where(sel, pv[h * blk:(h + 1) * blk], o)
                e = jnp.where(sel, lse[h * blk:(h + 1) * blk], e)
            out_ref[r, j * blk:(j + 1) * blk, :] = o.astype(BF16)
            lse_ref[r, j * blk:(j + 1) * blk, :] = e


def _t5_bucket(dist):
    max_exact = N_BUCKETS // 2
    d = jnp.maximum(dist, 1).astype(F32)
    large = max_exact + (jnp.log(d / max_exact) / math.log(MAX_DISTANCE / max_exact)
                         * (N_BUCKETS - max_exact)).astype(jnp.int32)
    large = jnp.minimum(large, N_BUCKETS - 1)
    return jnp.where(dist < max_exact, dist, large)


def _bucket_table(window, dilation):
    blk = ATTN_BLOCK
    span = window // dilation
    rel = jnp.arange(blk)[:, None] + blk - jnp.arange(2 * blk)[None, :]
    valid = (rel >= 0) & (rel <= span)
    return jnp.where(valid, _t5_bucket(jnp.maximum(rel, 0) * dilation), -1).astype(jnp.int32)


def _attention_group(qkv, rel_bias_g, window, dilation, mt):
    assert window // dilation == ATTN_BLOCK
    bsz, _, n, w3 = qkv.shape
    rows = mt
    mt = min(rows, n)
    nres = rows // mt
    nq = mt // ATTN_BLOCK
    cur = pl.BlockSpec((None, nres, mt, w3), lambda b, r, i: (b, r, i, 0))
    prev = pl.BlockSpec((None, nres, ATTN_BLOCK, w3), lambda b, r, i: (b, r, jnp.maximum(i * nq - 1, 0), 0))
    ospec = pl.BlockSpec((None, nres, mt, GROUP_WIDTH), lambda b, r, i: (b, r, i, 0))
    return pl.pallas_call(
        functools.partial(_attn_kernel, nq=nq, nres=nres),
        grid=(bsz, dilation // nres, n // mt),
        in_specs=[_const_spec((ATTN_BLOCK, 2 * ATTN_BLOCK)),
                  pl.BlockSpec(memory_space=pltpu.SMEM), cur, prev],
        out_specs=[ospec, ospec],
        out_shape=[jax.ShapeDtypeStruct((bsz, dilation, n, GROUP_WIDTH), BF16),
                   jax.ShapeDtypeStruct((bsz, dilation, n, GROUP_WIDTH), F32)],
        scratch_shapes=[pltpu.VMEM((HEADS_PER_GROUP * ATTN_BLOCK, 2 * ATTN_BLOCK), F32),
                        pltpu.VMEM((nres, mt + ATTN_BLOCK, 2 * GROUP_WIDTH), BF16)],
        compiler_params=_params(3),
        name=f"attn_d{dilation}",
    )(_bucket_table(window, dilation), rel_bias_g, qkv, qkv)


def _ssm_prep_kernel(logdt_ref, lr_ref, li_ref, bt_re_ref, bt_im_ref, c_re_ref, c_im_ref, dl_ref,
                     toep_ref, bst_ref, cst_ref, a_ref):
    cs = SSM_CHUNK
    dt = jnp.exp(jnp.full((1, SSM_STATE), logdt_ref[pl.program_id(0)], F32))
    lr, li = lr_ref[...], li_ref[...]
    mag = jnp.exp(lr * dt)
    ab_re, ab_im = mag * jnp.cos(li * dt), mag * jnp.sin(li * dt)
    den = lr * lr + li * li
    nr = ab_re - 1.0
    k_re = (nr * lr + ab_im * li) / den
    k_im = (ab_im * lr - nr * li) / den
    bt_re, bt_im = bt_re_ref[...], bt_im_ref[...]
    bb_re = k_re * bt_re - k_im * bt_im
    bb_im = k_re * bt_im + k_im * bt_re
    j = lax.broadcasted_iota(jnp.int32, (cs + 8, SSM_STATE), 0).astype(F32)
    pmag = jnp.exp(lr * dt * j)
    ang = li * dt * j
    p_re, p_im = pmag * jnp.cos(ang), pmag * jnp.sin(ang)
    c_re, c_im = c_re_ref[...], c_im_ref[...]
    cp_re = [c_re * p_re[i:i + 1] - c_im * p_im[i:i + 1] for i in range(cs + 1)]
    cp_im = [c_re * p_im[i:i + 1] + c_im * p_re[i:i + 1] for i in range(cs + 1)]
    nt = (((1,), (1,)), ((), ()))
    hi = lax.Precision.HIGHEST
    kcat = (lax.dot_general(bb_re, jnp.concatenate(cp_re[:cs], axis=0), nt, precision=hi,
                            preferred_element_type=F32)
            - lax.dot_general(bb_im, jnp.concatenate(cp_im[:cs], axis=0), nt, precision=hi,
                              preferred_element_type=F32))
    lane = lax.broadcasted_iota(jnp.int32, kcat.shape, 1)
    row = lax.broadcasted_iota(jnp.int32, kcat.shape, 0)
    dl = dl_ref[...]
    for s in range(cs):
        off = s * SSM_GROUP
        t_s = kcat if s == 0 else jnp.where(lane >= off, pltpu.roll(kcat, off, 1), 0.0)
        toep_ref[s] = (t_s + jnp.where(lane == off + row, dl, 0.0)).astype(BF16)
        pe_re, pe_im = p_re[cs - 1 - s:cs - s], p_im[cs - 1 - s:cs - s]
        st_re = pe_re * bb_re - pe_im * bb_im
        st_im = pe_re * bb_im + pe_im * bb_re
        bst_ref[s] = jnp.concatenate([st_re, st_im], axis=-1).astype(BF16)
    ro = jnp.concatenate([jnp.concatenate(cp_re[1:], axis=0),
                          -jnp.concatenate(cp_im[1:], axis=0)], axis=-1)
    ro_t = ro.T
    cst_ref[0] = ro_t[:SSM_STATE].astype(BF16)
    cst_ref[1] = ro_t[SSM_STATE:].astype(BF16)
    a_ref[...] = jnp.concatenate([p_re[cs:cs + 1], p_im[cs:cs + 1]], axis=0)


def _ssm_weights(lambda_re, lambda_im, log_dt, b_re, b_im, c_re, c_im, d_skip):
    f32 = F32
    cs, ng, no = SSM_CHUNK, SSM_GROUPS, SSM_GROUPS // SSM_OCT
    grp = lambda *shape: pl.BlockSpec((None,) + shape, lambda g: (g,) + (0,) * len(shape))
    in_oct = lambda *shape: pl.BlockSpec((None, shape[0], None) + shape[1:],
                                         lambda g: (g // SSM_OCT, 0, g % SSM_OCT) + (0,) * (len(shape) - 1))
    toep_c, bst_c, cst_c, a32 = pl.pallas_call(
        _ssm_prep_kernel,
        grid=(ng,),
        in_specs=[pl.BlockSpec(memory_space=pltpu.SMEM), grp(1, SSM_STATE), grp(1, SSM_STATE),
                  grp(SSM_GROUP, SSM_STATE), grp(SSM_GROUP, SSM_STATE),
                  grp(SSM_GROUP, SSM_STATE), grp(SSM_GROUP, SSM_STATE), grp(1, cs * SSM_GROUP)],
        out_specs=[in_oct(cs, SSM_GROUP, cs * SSM_GROUP), in_oct(cs, SSM_GROUP, 2 * SSM_STATE),
                   in_oct(2, SSM_STATE, cs * SSM_GROUP), grp(2, SSM_STATE)],
        out_shape=[jax.ShapeDtypeStruct((no, cs, SSM_OCT, SSM_GROUP, cs * SSM_GROUP), BF16),
                   jax.ShapeDtypeStruct((no, cs, SSM_OCT, SSM_GROUP, 2 * SSM_STATE), BF16),
                   jax.ShapeDtypeStruct((no, 2, SSM_OCT, SSM_STATE, cs * SSM_GROUP), BF16),
                   jax.ShapeDtypeStruct((ng, 2, SSM_STATE), f32)],
        compiler_params=_params(1),
        name="ssm_prep",
    )(log_dt.astype(f32), lambda_re.astype(f32)[:, None, :], lambda_im.astype(f32)[:, None, :],
      b_re.astype(f32).transpose(0, 2, 1), b_im.astype(f32).transpose(0, 2, 1),
      c_re.astype(f32), c_im.astype(f32), jnp.tile(d_skip.astype(f32), (1, cs))[:, None, :])
    avec = a32.reshape(no, SSM_OCT, 2, SSM_STATE).transpose(0, 2, 1, 3).reshape(no, 2, SSM_OCT * SSM_STATE)
    return (toep_c.reshape(no, cs * LANES, cs * SSM_GROUP),
            bst_c.reshape(no, cs * LANES, 2 * SSM_STATE),
            cst_c.reshape(no, 2 * SSM_OCT * SSM_STATE, cs * SSM_GROUP),
            avec)


def _expand_ssm_weights(toep_ref, bstc_ref, cstc_ref, wt_scr, bst_scr, cst_scr):
    tw = 2 * LANES
    sh_g, sh_n = SSM_GROUP.bit_length() - 1, SSM_STATE.bit_length() - 1
    r = lax.broadcasted_iota(jnp.int32, (tw, tw), 0)
    c = lax.broadcasted_iota(jnp.int32, (tw, tw), 1)
    col_grp = (c >> sh_g) & (SSM_OCT - 1)
    row_grp = (r >> sh_g) & (SSM_OCT - 1)
    for tp in range(SSM_CHUNK // 2):
        src_col = (2 * tp + (c >> (sh_g + 3))) * SSM_GROUP + (c & (SSM_GROUP - 1))
        e = jnp.where(r == src_col, 1.0, 0.0).astype(BF16)
        base = tp * (tp + 1) // 2
        for sp in range(tp + 1):
            x = jnp.dot(toep_ref[sp * tw:(sp + 1) * tw, :], e, preferred_element_type=F32)
            wt_scr[base + sp] = jnp.where(row_grp == col_grp, x, 0.0).astype(BF16)
        for j in range(2 * SSM_OCT * SSM_STATE // tw):
            x = jnp.dot(cstc_ref[j * tw:(j + 1) * tw, :], e, preferred_element_type=F32)
            row_grp_n = ((j * tw + r) >> sh_n) & (SSM_OCT - 1)
            cst_scr[j * tw:(j + 1) * tw, tp * tw:(tp + 1) * tw] = jnp.where(row_grp_n == col_grp, x, 0.0).astype(BF16)
    ns = 2 * SSM_OCT * SSM_STATE
    rb = lax.broadcasted_iota(jnp.int32, (2 * SSM_STATE, ns), 0)
    cb = lax.broadcasted_iota(jnp.int32, (2 * SSM_STATE, ns), 1)
    src_col_b = (cb >> (sh_n + 3)) * SSM_STATE + (cb & (SSM_STATE - 1))
    eb = jnp.where(rb == src_col_b, 1.0, 0.0).astype(BF16)
    rr = lax.broadcasted_iota(jnp.int32, (tw, ns), 0)
    cc = lax.broadcasted_iota(jnp.int32, (tw, ns), 1)
    keep = ((rr >> sh_g) & (SSM_OCT - 1)) == ((cc >> sh_n) & (SSM_OCT - 1))
    for j in range(SSM_CHUNK * LANES // tw):
        x = jnp.dot(bstc_ref[j * tw:(j + 1) * tw, :], eb, preferred_element_type=F32)
        bst_scr[j * tw:(j + 1) * tw, :] = jnp.where(keep, x, 0.0).astype(BF16)


def _ssm_kernel(u_ref, toep_ref, bstc_ref, cstc_ref, a_ref, y_ref, wt_ref, bst_ref, cst_ref, zz_scr, hp_scr,
                *, n_chunks, nb):
    @pl.when(pl.program_id(1) == 0)
    def _new_octet():
        _expand_ssm_weights(toep_ref, bstc_ref, cstc_ref, wt_ref, bst_ref, cst_ref)

    rows = nb * n_chunks
    nk = SSM_OCT * SSM_STATE // LANES
    u_t = [u_ref[:, s].reshape(rows, LANES) for s in range(SSM_CHUNK)]
    z = jnp.dot(jnp.concatenate(u_t, axis=-1), bst_ref[...], preferred_element_type=F32)
    for k in range(2 * nk):
        zz_scr[k] = z[:, k * LANES:(k + 1) * LANES]
    a_re = [jnp.broadcast_to(a_ref[0:1, k * LANES:(k + 1) * LANES], (nb, LANES)) for k in range(nk)]
    a_im = [jnp.broadcast_to(a_ref[1:2, k * LANES:(k + 1) * LANES], (nb, LANES)) for k in range(nk)]

    def step(c, carry):
        h_re, h_im = carry
        rows_c = pl.ds(c, nb, stride=n_chunks)
        new_re, new_im = [], []
        for k in range(nk):
            hp_scr[k, rows_c, :] = h_re[k]
            hp_scr[nk + k, rows_c, :] = h_im[k]
            new_re.append(a_re[k] * h_re[k] - a_im[k] * h_im[k] + zz_scr[k, rows_c, :])
            new_im.append(a_re[k] * h_im[k] + a_im[k] * h_re[k] + zz_scr[nk + k, rows_c, :])
        return tuple(new_re), tuple(new_im)

    zero = tuple(jnp.zeros((nb, LANES), F32) for _ in range(nk))
    lax.fori_loop(0, n_chunks, step, (zero, zero), unroll=4)
    hp = jnp.concatenate([hp_scr[k] for k in range(2 * nk)], axis=-1).astype(BF16)
    tw = 2 * LANES
    for tp in range(SSM_CHUNK // 2):
        base = tp * (tp + 1) // 2
        y = jnp.dot(hp, cst_ref[:, tp * tw:(tp + 1) * tw], preferred_element_type=F32)
        y = y + jnp.dot(jnp.concatenate(u_t[:2 * (tp + 1)], axis=-1),
                        wt_ref[base:base + tp + 1].reshape((tp + 1) * tw, tw), preferred_element_type=F32)
        y = jax.nn.gelu(y).astype(BF16)
        y_ref[:, 2 * tp] = y[:, :LANES].reshape(nb, n_chunks, LANES)
        y_ref[:, 2 * tp + 1] = y[:, LANES:].reshape(nb, n_chunks, LANES)


def _ssm(u16, weights, nb):
    toep_c, bst_c, cst_c, avec = weights
    bsz, cs, nc, width = u16.shape
    oct_spec = lambda a: pl.BlockSpec((None,) + a.shape[1:], lambda o, b: (o,) + (0,) * (a.ndim - 1))
    io_spec = pl.BlockSpec((nb, cs, nc, LANES), lambda o, b: (b, 0, 0, o))
    n_state = 2 * SSM_OCT * SSM_STATE
    n_pairs = (cs // 2) * (cs // 2 + 1) // 2
    return pl.pallas_call(
        functools.partial(_ssm_kernel, n_chunks=nc, nb=nb),
        grid=(width // LANES, bsz // nb),
        in_specs=[io_spec, oct_spec(toep_c), oct_spec(bst_c), oct_spec(cst_c), oct_spec(avec)],
        out_specs=io_spec,
        out_shape=jax.ShapeDtypeStruct(u16.shape, BF16),
        scratch_shapes=[pltpu.VMEM((n_pairs, 2 * LANES, 2 * LANES), BF16),
                        pltpu.VMEM((cs * LANES, n_state), BF16),
                        pltpu.VMEM((n_state, cs * LANES), BF16),
                        pltpu.VMEM((n_state // LANES, nb * nc, LANES), F32),
                        pltpu.VMEM((n_state // LANES, nb * nc, LANES), F32)],
        compiler_params=_params(2),
        name="ssm",
    )(u16, toep_c, bst_c, cst_c, avec)


def _layer_norm(v, g, b):
    mu = jnp.mean(v, axis=-1, keepdims=True)
    vc = v - mu
    var = jnp.mean(vc * vc, axis=-1, keepdims=True)
    return vc * lax.rsqrt(var + LN_EPS) * g + b


def _merge_kernel(o0_ref, o1_ref, o2_ref, l0_ref, l1_ref, l2_ref, ys_ref, g_ref, x_ref,
                  wglu_ref, wsp_ref, wap_ref, wout_ref, lng_ref, lnb_ref, h_ref, o_scr, l_scr, y_scr):
    tm = x_ref.shape[0]

    def token_order(ref, scr, d):
        if d == 1:
            return ref[0].astype(F32)
        nk = ref.shape[-1] // LANES
        for r in range(d):
            v = ref[r].astype(F32)
            for k in range(nk):
                scr[k, pl.ds(r, tm // d, stride=d), :] = v[:, k * LANES:(k + 1) * LANES]
        return jnp.concatenate([scr[k] for k in range(nk)], axis=-1)

    ls, outs = [], []
    for gi, (o_ref, l_ref) in enumerate(((o0_ref, l0_ref), (o1_ref, l1_ref), (o2_ref, l2_ref))):
        d = DILATION_PATTERNS[gi][1]
        outs.append(token_order(o_ref, o_scr.at[gi], d))
        ls.append(token_order(l_ref, l_scr.at[gi], d))
    mx = jnp.maximum(jnp.maximum(ls[0], ls[1]), ls[2])
    es = [jnp.exp(l - mx) for l in ls]
    num = es[0] * outs[0] + es[1] * outs[1] + es[2] * outs[2]
    y_attn = (num / (es[0] + es[1] + es[2])).astype(BF16)
    ys = token_order(ys_ref, y_scr, SSM_CHUNK).astype(BF16)
    cw = GROUP_WIDTH
    y_ssm = []
    for c in range(SSM_WIDTH // cw):
        a = jnp.dot(ys, wglu_ref[:, c * cw:(c + 1) * cw], preferred_element_type=F32)
        b = jnp.dot(ys, wglu_ref[:, SSM_WIDTH + c * cw:SSM_WIDTH + (c + 1) * cw], preferred_element_type=F32)
        y_ssm.append((a * _sigmoid(b)).astype(BF16))
    y_ssm = jnp.concatenate(y_ssm, axis=-1)
    gated = []
    for c in range(D_MODEL // cw):
        pa = jnp.dot(y_ssm, wsp_ref[:, c * cw:(c + 1) * cw], preferred_element_type=F32)
        pb = jnp.dot(y_attn, wap_ref[:, c * cw:(c + 1) * cw], preferred_element_type=F32)
        g_ssm = g_ref[:, c * cw:(c + 1) * cw].astype(F32)
        g_attn = g_ref[:, D_MODEL + c * cw:D_MODEL + (c + 1) * cw].astype(F32)
        gated.append((g_ssm * pa + g_attn * pb).astype(BF16))
    gated = jnp.concatenate(gated, axis=-1)
    mix = jnp.dot(gated, wout_ref[...], preferred_element_type=F32)
    h_ref[...] = _layer_norm(ALPHA * x_ref[...] + mix, lng_ref[...], lnb_ref[...])


def _merge(outs, lses, ys, gates, x2, w_glu, w_sp, w_ap, w_out, ln_g, ln_b, seqlen, tm):
    t = x2.shape[0]
    tiles = seqlen // tm
    row = lambda w: pl.BlockSpec((tm, w), lambda i: (i, 0))
    res_spec = lambda d, w: pl.BlockSpec((None, d, tm // d, w), lambda i: (i // tiles, 0, i % tiles, 0))
    res_specs = [res_spec(d, GROUP_WIDTH) for _, d in DILATION_PATTERNS]
    return pl.pallas_call(
        _merge_kernel,
        grid=(t // tm,),
        in_specs=res_specs + res_specs + [res_spec(SSM_CHUNK, SSM_WIDTH), row(2 * D_MODEL), row(D_MODEL),
                  _const_spec(w_glu.shape), _const_spec(w_sp.shape), _const_spec(w_ap.shape),
                  _const_spec(w_out.shape), _const_spec((1, D_MODEL)), _const_spec((1, D_MODEL))],
        out_specs=row(D_MODEL),
        out_shape=jax.ShapeDtypeStruct((t, D_MODEL), F32),
        scratch_shapes=[pltpu.VMEM((N_GROUPS, GROUP_WIDTH // LANES, tm, LANES), F32),
                        pltpu.VMEM((N_GROUPS, GROUP_WIDTH // LANES, tm, LANES), F32),
                        pltpu.VMEM((SSM_WIDTH // LANES, tm, LANES), F32)],
        compiler_params=_params(1),
        name="merge_ln1",
    )(*outs, *lses, ys, gates, x2, w_glu, w_sp, w_ap, w_out, ln_g, ln_b)


FF_CHUNK = 1024


def _ffn_kernel(h_ref, wup_ref, wdn_ref, lng_ref, lnb_ref, o_ref):
    h = h_ref[...]
    hb = h.astype(BF16)
    acts = []
    for c in range(D_FF // FF_CHUNK):
        lo, hi = c * FF_CHUNK, (c + 1) * FF_CHUNK
        up = jnp.dot(hb, wup_ref[:, lo:hi], preferred_element_type=F32)
        acts.append(jnp.square(jnp.maximum(up, 0.0)).astype(BF16))
    ff = jnp.dot(jnp.concatenate(acts, axis=-1), wdn_ref[...], preferred_element_type=F32)
    o_ref[...] = _layer_norm(ALPHA * h + ff, lng_ref[...], lnb_ref[...])


def _ffn(h1, w_up, w_down, ln_g, ln_b, tm):
    t = h1.shape[0]
    row = pl.BlockSpec((tm, D_MODEL), lambda i: (i, 0))
    return pl.pallas_call(
        _ffn_kernel,
        grid=(t // tm,),
        in_specs=[row, _const_spec(w_up.shape), _const_spec(w_down.shape),
                  _const_spec((1, D_MODEL)), _const_spec((1, D_MODEL))],
        out_specs=row,
        out_shape=jax.ShapeDtypeStruct((t, D_MODEL), F32),
        compiler_params=_params(1),
        name="ffn_ln2",
    )(h1, w_up, w_down, ln_g, ln_b)


def _permute_w_in(w):
    aw = ATTN_WIDTH
    cols = []
    for gi in range(N_GROUPS):
        lo, hi = gi * GROUP_WIDTH, (gi + 1) * GROUP_WIDTH
        cols += [w[:, lo:hi] * (HEAD_DIM ** -0.5), w[:, aw + lo:aw + hi], w[:, 2 * aw + lo:2 * aw + hi]]
    cols.append(w[:, 3 * aw:])
    return jnp.concatenate(cols, axis=1).astype(BF16)


def _layer(h2, bsz, seqlen, l, w_in, b_gate, lambda_re, lambda_im, log_dt, ssm_b_re, ssm_b_im,
           ssm_c_re, ssm_c_im, ssm_d, w_glu, w_ssm_proj, rel_bias, w_attn_proj, w_out,
           ln1_g, ln1_b, w_up, w_down, ln2_g, ln2_b, tm=1024, attn_mt=1024, ffn_tm=1024):
    qkv0, qkv1, qkv2, u, gates = _in_proj(h2, _permute_w_in(w_in[l]), b_gate[l][None, :], bsz, seqlen, tm)
    outs, lses = [], []
    for gi, ((window, dilation), qkv) in enumerate(zip(DILATION_PATTERNS, (qkv0, qkv1, qkv2))):
        rb = rel_bias[:, gi * HEADS_PER_GROUP:(gi + 1) * HEADS_PER_GROUP].astype(F32)
        o, s = _attention_group(qkv, rb, window, dilation, attn_mt)
        outs.append(o)
        lses.append(s)
    ssm_w = _ssm_weights(lambda_re[l], lambda_im[l], log_dt[l], ssm_b_re[l], ssm_b_im[l],
                         ssm_c_re[l], ssm_c_im[l], ssm_d[l])
    ys = _ssm(u, ssm_w, nb=4)
    h1 = _merge(outs, lses, ys, gates, h2, w_glu[l].astype(BF16), w_ssm_proj[l].astype(BF16),
                w_attn_proj[l].astype(BF16), w_out[l].astype(BF16),
                ln1_g[l][None, :], ln1_b[l][None, :], seqlen, tm)
    return _ffn(h1, w_up[l].astype(BF16), w_down[l].astype(BF16), ln2_g[l][None, :], ln2_b[l][None, :], ffn_tm)


def kernel(x, w_in, b_gate, lambda_re, lambda_im, log_dt, ssm_b_re, ssm_b_im, ssm_c_re, ssm_c_im,
           ssm_d, w_glu, w_ssm_proj, rel_bias, w_attn_proj, w_out, ln1_g, ln1_b, w_up, w_down,
           ln2_g, ln2_b):
    bsz, seqlen, d = x.shape
    h = x.reshape(bsz * seqlen, d)
    for l in range(w_in.shape[0]):
        h = _layer(h, bsz, seqlen, l, w_in, b_gate, lambda_re, lambda_im, log_dt, ssm_b_re, ssm_b_im,
                   ssm_c_re, ssm_c_im, ssm_d, w_glu, w_ssm_proj, rel_bias, w_attn_proj, w_out,
                   ln1_g, ln1_b, w_up, w_down, ln2_g, ln2_b)
    return h.reshape(bsz, seqlen, d)
```

```python
import functools
import math

import jax
import jax.numpy as jnp
from jax import lax
from jax.experimental import pallas as pl
from jax.experimental.pallas import tpu as pltpu

F32 = jnp.float32
BF16 = jnp.bfloat16

D_MODEL = 1024
HEAD_DIM = 64
HEADS_PER_GROUP = 4
GROUP_WIDTH = HEADS_PER_GROUP * HEAD_DIM
DILATION_PATTERNS = ((128, 1), (512, 4), (2048, 16))
N_GROUPS = len(DILATION_PATTERNS)
ATTN_WIDTH = N_GROUPS * GROUP_WIDTH
N_BUCKETS = 32
MAX_DISTANCE = 2048
SSM_WIDTH = 512
SSM_GROUP = 16
SSM_GROUPS = 32
SSM_STATE = 64
D_FF = 4 * D_MODEL
DEPTH = 1
ALPHA = (2.0 * DEPTH) ** 0.25
LN_EPS = 1e-5
NEG_INF = -1e30

ATTN_BLOCK = 128
ATTN_SKEW = (3, 5)
SSM_CHUNK = 16
SSM_OCT = 8
LANES = 128
VMEM_LIMIT_BYTES =56 * 1024 * 1024


def _params(n_axes):
    return pltpu.CompilerParams(dimension_semantics=("arbitrary",) * n_axes,
                                vmem_limit_bytes=VMEM_LIMIT_BYTES)


def _sigmoid(z):
    return 0.5 * jnp.tanh(0.5 * z) + 0.5


def _const_spec(shape):
    nd = len(shape)
    return pl.BlockSpec(shape, lambda *_: (0,) * nd, pipeline_mode=pl.Buffered(1))


def _in_proj_kernel(x_ref, w_ref, bg_ref, qkv0_ref, qkv1_ref, qkv2_ref, u_ref, g_ref, scr):
    xb = x_ref[...].astype(BF16)
    tm = xb.shape[0]

    def mm(lo, hi):
        return jnp.dot(xb, w_ref[:, lo:hi], preferred_element_type=F32)

    def emit(ref, res, d, col0):
        width = res.shape[1]
        if d == 1:
            ref[0, :, col0:col0 + width] = res.astype(BF16)
            return
        slot0 = next_slot[0]
        next_slot[0] += width // LANES
        for k in range(width // LANES):
            scr[slot0 + k] = res[:, k * LANES:(k + 1) * LANES]
        for r in range(d):
            for k in range(width // LANES):
                col = col0 + k * LANES
                ref[r, :, col:col + LANES] = scr[slot0 + k, pl.ds(r, tm // d, stride=d), :].astype(BF16)

    next_slot = [0]

    gw3 = 3 * GROUP_WIDTH
    for gi, ref in enumerate((qkv0_ref, qkv1_ref, qkv2_ref)):
        for c in range(3):
            lo = gi * gw3 + c * GROUP_WIDTH
            emit(ref, mm(lo, lo + GROUP_WIDTH), DILATION_PATTERNS[gi][1], c * GROUP_WIDTH)
    base = N_GROUPS * gw3
    emit(u_ref, mm(base, base + SSM_WIDTH), SSM_CHUNK, 0)
    base += SSM_WIDTH
    for c in range(4):
        lo, hi = c * 512, (c + 1) * 512
        z = mm(base + lo, base + hi) + bg_ref[:, lo:hi]
        g_ref[:, lo:hi] = _sigmoid(z).astype(BF16)


def _in_proj(x2, w_perm, b_gate, bsz, seqlen, tm):
    t = x2.shape[0]
    n_in = w_perm.shape[1]
    tiles = seqlen // tm
    row = lambda w: pl.BlockSpec((tm, w), lambda i: (i, 0))
    w3 = 3 * GROUP_WIDTH
    dils = [d for _, d in DILATION_PATTERNS]
    n_stage = (sum(d > 1 for d in dils) * w3 + SSM_WIDTH) // LANES
    res_spec = lambda d, w:pl.BlockSpec((None, d, tm // d, w), lambda i: (i // tiles, 0, i % tiles, 0))
    return pl.pallas_call(
        _in_proj_kernel,
        grid=(t // tm,),
        in_specs=[row(D_MODEL), _const_spec((D_MODEL, n_in)), _const_spec((1, 2 * D_MODEL))],
        out_specs=[res_spec(d, w3) for d in dils] + [res_spec(SSM_CHUNK, SSM_WIDTH), row(2 * D_MODEL)],
        out_shape=[jax.ShapeDtypeStruct((bsz, d, seqlen // d, w3), BF16) for d in dils]
        + [jax.ShapeDtypeStruct((bsz, SSM_CHUNK, seqlen // SSM_CHUNK, SSM_WIDTH), BF16),
           jax.ShapeDtypeStruct((t, 2 * D_MODEL), BF16)],
        scratch_shapes=[pltpu.VMEM((n_stage, tm, LANES), F32)],
        compiler_params=_params(1),
        name="in_proj",
    )(x2, w_perm, b_gate)


def _attn_kernel(bucket_ref, relb_ref, cur_ref, prev_ref, out_ref, lse_ref, bias_scr, k_scr, vt_scr, *, nq, nres):
    blk = ATTN_BLOCK
    first = (pl.program_id(0) == 0) & (pl.program_id(1) == 0) & (pl.program_id(2) == 0)

    @pl.when(first)
    def _build_bias():
        bucket = bucket_ref[...]
        for h in range(HEADS_PER_GROUP):
            acc = jnp.full(bucket.shape, NEG_INF, F32)
            for bkt in range(N_BUCKETS):
                acc = jnp.where(bucket == bkt, relb_ref[bkt, h], acc)
            bias_scr[h // 2, :, (h % 2) * blk:(h % 2 + 1) * blk] = acc

    row = lax.broadcasted_iota(jnp.int32, (2 * blk, blk), 0)
    row_head = row // HEAD_DIM
    keep_first = (row >= blk) | (pl.program_id(2) > 0)
    for r in range(nres):
        k_scr[r, 0:blk, :] = prev_ref[r, :, GROUP_WIDTH:2 * GROUP_WIDTH]
        k_scr[r, blk:, :] = cur_ref[r, :, GROUP_WIDTH:2 * GROUP_WIDTH]
        vt_scr[r, :, 0:blk] = prev_ref[r, :, 2 * GROUP_WIDTH:3 * GROUP_WIDTH].T
        for j in range(nq):
            vt_scr[r, :, (j + 1) * blk:(j + 2) * blk] = cur_ref[r, j * blk:(j + 1) * blk,
                                                                 2 * GROUP_WIDTH:3 * GROUP_WIDTH].T

    npair = HEADS_PER_GROUP // 2
    units = [(r, j, hp) for r in range(nres) for j in range(nq) for hp in range(npair)]
    qts = {}

    def scores(r, j, hp):
        if (r, j) not in qts:
            qts[(r, j)] = cur_ref[r, j * blk:(j + 1) * blk, 0:GROUP_WIDTH].T
        qt = qts[(r, j)]
        k2 = k_scr[r, j * blk:(j + 2) * blk, :]
        qh = jnp.concatenate([jnp.where(row_head == 2 * hp + i, qt, jnp.zeros_like(qt)) for i in range(2)], axis=1)
        return jnp.dot(k2, qh, preferred_element_type=F32)

    def softmax(r, j, hp, s2):
        ps, ms, ls = [], [], []
        for i in range(2):
            lanes = slice(i * blk, (i + 1) * blk)
            s = s2[:, lanes] + bias_scr[hp, :, lanes]
            if j == 0:
                s = jnp.where(keep_first, s, NEG_INF)
            m = jnp.max(s, axis=0, keepdims=True)
            p = jnp.exp(s - m)
            ls.append(jnp.sum(p, axis=0, keepdims=True))
            ps.append(p.astype(BF16))
            ms.append(m)
        return jnp.concatenate(ps, axis=1), jnp.concatenate(ms, axis=1), jnp.concatenate(ls, axis=1)

    def values(r, j, hp, p, m, l):
        vt = vt_scr[r, 2 * hp * HEAD_DIM:2 * (hp + 1) * HEAD_DIM, j * blk:(j + 2) * blk]
        ot = jnp.dot(vt, p, preferred_element_type=F32) * (1.0 / l)
        lse = jnp.broadcast_to(m + jnp.log(l), (HEAD_DIM, 2 * blk))
        return [(ot[i * HEAD_DIM:(i + 1) * HEAD_DIM, i * blk:(i + 1) * blk], lse[:, i * blk:(i + 1) * blk])
                for i in range(2)]

    raw, soft, done = {}, {}, []
    lag_soft, lag_val = ATTN_SKEW
    for step in range(len(units) + lag_val):
        if step < len(units):
            raw[step] = scores(*units[step])
        if lag_soft <= step < len(units) + lag_soft:
            soft[step - lag_soft] = softmax(*units[step - lag_soft], raw.pop(step - lag_soft))
        if step >= lag_val:
            i = step - lag_val
            r, j, hp = units[i]
            done += values(r, j, hp, *soft.pop(i))
            if hp == npair - 1:
                rows = slice(j * blk, (j + 1) * blk)
                out_ref[r, rows, :] = jnp.concatenate([d[0] for d in done], axis=0).T.astype(BF16)
                lse_ref[r, rows, :] = jnp.concatenate([d[1] for d in done], axis=0).T
                done = []


def _t5_bucket(dist):
    max_exact = N_BUCKETS // 2
    d = jnp.maximum(dist, 1).astype(F32)
    large = max_exact + (jnp.log(d / max_exact) / math.log(MAX_DISTANCE / max_exact)
                         * (N_BUCKETS - max_exact)).astype(jnp.int32)
    large = jnp.minimum(large, N_BUCKETS - 1)
    return jnp.where(dist < max_exact, dist, large)


def _bucket_table(window, dilation):
    blk = ATTN_BLOCK
    span = window // dilation
    rel = jnp.arange(blk)[None, :] + blk - jnp.arange(2 * blk)[:, None]
    valid = (rel >= 0) & (rel <= span)
    return jnp.where(valid, _t5_bucket(jnp.maximum(rel, 0) * dilation), -1).astype(jnp.int32)


def _attention_group(qkv, rel_bias_g, window, dilation, mt):
    assert window // dilation == ATTN_BLOCK
    bsz, _, n, w3 = qkv.shape
    rows = mt
    mt = min(rows, n)
    nres = rows // mt
    nq = mt // ATTN_BLOCK
    cur = pl.BlockSpec((None, nres, mt, w3), lambda b, r, i: (b, r, i, 0))
    prev = pl.BlockSpec((None, nres, ATTN_BLOCK, w3), lambda b, r, i: (b, r, jnp.maximum(i * nq - 1, 0), 0))
    ospec = pl.BlockSpec((None, nres, mt, GROUP_WIDTH), lambda b, r, i: (b, r, i, 0))
    return pl.pallas_call(
        functools.partial(_attn_kernel, nq=nq, nres=nres),
        grid=(bsz, dilation // nres, n // mt),
        in_specs=[_const_spec((2 * ATTN_BLOCK, ATTN_BLOCK)),
                  pl.BlockSpec(memory_space=pltpu.SMEM), cur, prev],
        out_specs=[ospec, ospec],
        out_shape=[jax.ShapeDtypeStruct((bsz, dilation, n, GROUP_WIDTH), BF16),
                   jax.ShapeDtypeStruct((bsz, dilation, n, GROUP_WIDTH), F32)],
        scratch_shapes=[pltpu.VMEM((HEADS_PER_GROUP // 2, 2 * ATTN_BLOCK, 2 * ATTN_BLOCK), F32),
                        pltpu.VMEM((nres, mt + ATTN_BLOCK, GROUP_WIDTH), BF16),
                        pltpu.VMEM((nres, GROUP_WIDTH, mt + ATTN_BLOCK), BF16)],
        compiler_params=_params(3),
        name=f"attn_d{dilation}",
    )(_bucket_table(window, dilation), rel_bias_g, qkv, qkv)


def _ssm_prep_kernel(logdt_ref, lr_ref, li_ref, bt_re_ref, bt_im_ref, c_re_ref, c_im_ref, dl_ref,
                     toep_ref, bst_ref, cst_ref, a_ref):
    cs = SSM_CHUNK
    dt = jnp.exp(jnp.full((1, SSM_STATE), logdt_ref[pl.program_id(0)], F32))
    lr, li = lr_ref[...], li_ref[...]
    mag = jnp.exp(lr * dt)
    ab_re, ab_im = mag * jnp.cos(li * dt), mag * jnp.sin(li * dt)
    den = lr * lr + li * li
    nr = ab_re - 1.0
    k_re = (nr * lr + ab_im * li) / den
    k_im = (ab_im * lr - nr * li) / den
    bt_re, bt_im = bt_re_ref[...], bt_im_ref[...]
    bb_re = k_re * bt_re - k_im * bt_im
    bb_im = k_re * bt_im + k_im * bt_re
    j = lax.broadcasted_iota(jnp.int32, (cs + 8, SSM_STATE), 0).astype(F32)
    pmag = jnp.exp(lr * dt * j)
    ang = li * dt * j
    p_re, p_im = pmag * jnp.cos(ang), pmag * jnp.sin(ang)
    c_re, c_im = c_re_ref[...], c_im_ref[...]
    cp_re = [c_re * p_re[i:i + 1] - c_im * p_im[i:i + 1] for i in range(cs + 1)]
    cp_im = [c_re * p_im[i:i + 1] + c_im * p_re[i:i + 1] for i in range(cs + 1)]
    nt = (((1,), (1,)), ((), ()))
    hi = lax.Precision.HIGHEST
    kcat = (lax.dot_general(bb_re, jnp.concatenate(cp_re[:cs], axis=0), nt, precision=hi,
                            preferred_element_type=F32)
            - lax.dot_general(bb_im, jnp.concatenate(cp_im[:cs], axis=0), nt, precision=hi,
                              preferred_element_type=F32))
    lane = lax.broadcasted_iota(jnp.int32, kcat.shape, 1)
    row = lax.broadcasted_iota(jnp.int32, kcat.shape, 0)
    dl = dl_ref[...]
    for s in range(cs):
        off = s * SSM_GROUP
        t_s = kcat if s == 0 else jnp.where(lane >= off, pltpu.roll(kcat, off, 1), 0.0)
        toep_ref[s] = (t_s + jnp.where(lane == off + row, dl, 0.0)).astype(BF16)
        pe_re, pe_im = p_re[cs - 1 - s:cs - s], p_im[cs - 1 - s:cs - s]
        st_re = pe_re * bb_re - pe_im * bb_im
        st_im = pe_re * bb_im + pe_im * bb_re
        bst_ref[s] = jnp.concatenate([st_re, st_im], axis=-1).astype(BF16)
    ro = jnp.concatenate([jnp.concatenate(cp_re[1:], axis=0),
                          -jnp.concatenate(cp_im[1:], axis=0)], axis=-1)
    ro_t = ro.T
    cst_ref[0] = ro_t[:SSM_STATE].astype(BF16)
    cst_ref[1] = ro_t[SSM_STATE:].astype(BF16)
    a_ref[...] = jnp.concatenate([p_re[cs:cs + 1], p_im[cs:cs + 1]], axis=0)


def _ssm_weights(lambda_re, lambda_im, log_dt, b_re, b_im, c_re, c_im, d_skip):
    f32 = F32
    cs, ng, no = SSM_CHUNK, SSM_GROUPS, SSM_GROUPS // SSM_OCT
    grp = lambda *shape: pl.BlockSpec((None,) + shape, lambda g: (g,) + (0,) * len(shape))
    in_oct = lambda *shape: pl.BlockSpec((None, shape[0], None) + shape[1:],
                                         lambda g: (g // SSM_OCT, 0, g % SSM_OCT) + (0,) * (len(shape) - 1))
    toep_c, bst_c, cst_c, a32 = pl.pallas_call(
        _ssm_prep_kernel,
        grid=(ng,),
        in_specs=[pl.BlockSpec(memory_space=pltpu.SMEM), grp(1, SSM_STATE), grp(1, SSM_STATE),
                  grp(SSM_GROUP, SSM_STATE), grp(SSM_GROUP, SSM_STATE),
                  grp(SSM_GROUP, SSM_STATE), grp(SSM_GROUP, SSM_STATE), grp(1, cs * SSM_GROUP)],
        out_specs=[in_oct(cs, SSM_GROUP, cs * SSM_GROUP), in_oct(cs, SSM_GROUP, 2 * SSM_STATE),
                   in_oct(2, SSM_STATE, cs * SSM_GROUP), grp(2, SSM_STATE)],
        out_shape=[jax.ShapeDtypeStruct((no, cs, SSM_OCT, SSM_GROUP, cs * SSM_GROUP), BF16),
                   jax.ShapeDtypeStruct((no, cs, SSM_OCT, SSM_GROUP, 2 * SSM_STATE), BF16),
                   jax.ShapeDtypeStruct((no, 2, SSM_OCT, SSM_STATE, cs * SSM_GROUP), BF16),
                   jax.ShapeDtypeStruct((ng, 2, SSM_STATE), f32)],
        compiler_params=_params(1),
        name="ssm_prep",
    )(log_dt.astype(f32), lambda_re.astype(f32)[:, None, :], lambda_im.astype(f32)[:, None, :],
      b_re.astype(f32).transpose(0, 2, 1), b_im.astype(f32).transpose(0, 2, 1),
      c_re.astype(f32), c_im.astype(f32), jnp.tile(d_skip.astype(f32), (1, cs))[:, None, :])
    avec = a32.reshape(no, SSM_OCT, 2, SSM_STATE).transpose(0, 2, 1, 3).reshape(no, 2, SSM_OCT * SSM_STATE)
    return (toep_c.reshape(no, cs * LANES, cs * SSM_GROUP),
            bst_c.reshape(no, cs * LANES, 2 * SSM_STATE),
            cst_c.reshape(no, 2 * SSM_OCT * SSM_STATE, cs * SSM_GROUP),
            avec)


def _expand_ssm_weights(toep_ref, bstc_ref, cstc_ref, wt_scr, bst_scr, cst_scr):
    tw = 2 * LANES
    sh_g, sh_n = SSM_GROUP.bit_length() - 1, SSM_STATE.bit_length() - 1
    r = lax.broadcasted_iota(jnp.int32, (tw, tw), 0)
    c = lax.broadcasted_iota(jnp.int32, (tw, tw), 1)
    col_grp = (c >> sh_g) & (SSM_OCT - 1)
    row_grp = (r >> sh_g) & (SSM_OCT - 1)
    for tp in range(SSM_CHUNK // 2):
        src_col = (2 * tp + (c >> (sh_g + 3))) * SSM_GROUP + (c & (SSM_GROUP - 1))
        e = jnp.where(r == src_col, 1.0, 0.0).astype(BF16)
        base = tp * (tp + 1) // 2
        for sp in range(tp + 1):
            x = jnp.dot(toep_ref[sp * tw:(sp + 1) * tw, :], e, preferred_element_type=F32)
            wt_scr[base + sp] = jnp.where(row_grp == col_grp, x, 0.0).astype(BF16)
        for j in range(2 * SSM_OCT * SSM_STATE // tw):
            x = jnp.dot(cstc_ref[j * tw:(j + 1) * tw, :], e, preferred_element_type=F32)
            row_grp_n = ((j * tw + r) >> sh_n) & (SSM_OCT - 1)
            cst_scr[j * tw:(j + 1) * tw, tp * tw:(tp + 1) * tw] = jnp.where(row_grp_n == col_grp, x, 0.0).astype(BF16)
    ns = 2 * SSM_OCT * SSM_STATE
    rb = lax.broadcasted_iota(jnp.int32, (2 * SSM_STATE, ns), 0)
    cb = lax.broadcasted_iota(jnp.int32, (2 * SSM_STATE, ns), 1)
    src_col_b = (cb >> (sh_n + 3)) * SSM_STATE + (cb & (SSM_STATE - 1))
    eb = jnp.where(rb == src_col_b, 1.0, 0.0).astype(BF16)
    rr = lax.broadcasted_iota(jnp.int32, (tw, ns), 0)
    cc = lax.broadcasted_iota(jnp.int32, (tw, ns), 1)
    keep = ((rr >> sh_g) & (SSM_OCT - 1)) == ((cc >> sh_n) & (SSM_OCT - 1))
    for j in range(SSM_CHUNK * LANES // tw):
        x = jnp.dot(bstc_ref[j * tw:(j + 1) * tw, :], eb, preferred_element_type=F32)
        bst_scr[j * tw:(j + 1) * tw, :] = jnp.where(keep, x, 0.0).astype(BF16)


def _ssm_kernel(u_ref, toep_ref, bstc_ref, cstc_ref, a_ref, y_ref, wt_ref, bst_ref, cst_ref, zz_scr, hp_scr,
                *, n_chunks, nb):
    @pl.when(pl.program_id(1) == 0)
    def _new_octet():
        _expand_ssm_weights(toep_ref, bstc_ref, cstc_ref, wt_ref, bst_ref, cst_ref)

    rows = nb * n_chunks
    nk = SSM_OCT * SSM_STATE // LANES
    u_t = [u_ref[:, s].reshape(rows, LANES) for s in range(SSM_CHUNK)]
    z = jnp.dot(jnp.concatenate(u_t, axis=-1), bst_ref[...], preferred_element_type=F32)
    for k in range(2 * nk):
        zz_scr[k] = z[:, k * LANES:(k + 1) * LANES]
    a_re = [jnp.broadcast_to(a_ref[0:1, k * LANES:(k + 1) * LANES], (nb, LANES)) for k in range(nk)]
    a_im = [jnp.broadcast_to(a_ref[1:2, k * LANES:(k + 1) * LANES], (nb, LANES)) for k in range(nk)]

    def step(c, carry):
        h_re, h_im = carry
        rows_c = pl.ds(c, nb, stride=n_chunks)
        new_re, new_im = [], []
        for k in range(nk):
            hp_scr[k, rows_c, :] = h_re[k]
            hp_scr[nk + k, rows_c, :] = h_im[k]
            new_re.append(a_re[k] * h_re[k] - a_im[k] * h_im[k] + zz_scr[k, rows_c, :])
            new_im.append(a_re[k] * h_im[k] + a_im[k] * h_re[k] + zz_scr[nk + k, rows_c, :])
        return tuple(new_re), tuple(new_im)

    zero = tuple(jnp.zeros((nb, LANES), F32) for _ in range(nk))
    lax.fori_loop(0, n_chunks, step, (zero, zero), unroll=4)
    hp = jnp.concatenate([hp_scr[k] for k in range(2 * nk)], axis=-1).astype(BF16)
    tw = 2 * LANES
    for tp in range(SSM_CHUNK // 2):
        base = tp * (tp + 1) // 2
        y = jnp.dot(hp, cst_ref[:, tp * tw:(tp + 1) * tw], preferred_element_type=F32)
        y = y + jnp.dot(jnp.concatenate(u_t[:2 * (tp + 1)], axis=-1),
                        wt_ref[base:base + tp + 1].reshape((tp + 1) * tw, tw), preferred_element_type=F32)
        y = jax.nn.gelu(y).astype(BF16)
        y_ref[:, 2 * tp] = y[:, :LANES].reshape(nb, n_chunks, LANES)
        y_ref[:, 2 * tp + 1] = y[:, LANES:].reshape(nb, n_chunks, LANES)


def _ssm(u16, weights, nb):
    toep_c, bst_c, cst_c, avec = weights
    bsz, cs, nc, width = u16.shape
    oct_spec = lambda a: pl.BlockSpec((None,) + a.shape[1:], lambda o, b: (o,) + (0,) * (a.ndim - 1))
    io_spec = pl.BlockSpec((nb, cs, nc, LANES), lambda o, b: (b, 0, 0, o))
    n_state = 2 * SSM_OCT * SSM_STATE
    n_pairs = (cs // 2) * (cs // 2 + 1) // 2
    return pl.pallas_call(
        functools.partial(_ssm_kernel, n_chunks=nc, nb=nb),
        grid=(width // LANES, bsz // nb),
        in_specs=[io_spec, oct_spec(toep_c), oct_spec(bst_c), oct_spec(cst_c), oct_spec(avec)],
        out_specs=io_spec,
        out_shape=jax.ShapeDtypeStruct(u16.shape, BF16),
        scratch_shapes=[pltpu.VMEM((n_pairs, 2 * LANES, 2 * LANES), BF16),
                        pltpu.VMEM((cs * LANES, n_state), BF16),
                        pltpu.VMEM((n_state, cs * LANES), BF16),
                        pltpu.VMEM((n_state // LANES, nb * nc, LANES), F32),
                        pltpu.VMEM((n_state // LANES, nb * nc, LANES), F32)],
        compiler_params=_params(2),
        name="ssm",
    )(u16, toep_c, bst_c, cst_c, avec)


def _layer_norm(v, g, b):
    mu = jnp.mean(v, axis=-1, keepdims=True)
    vc = v - mu
    var = jnp.mean(vc * vc, axis=-1, keepdims=True)
    return vc * lax.rsqrt(var + LN_EPS) * g + b


def _merge_kernel(o0_ref, o1_ref, o2_ref, l0_ref, l1_ref, l2_ref, ys_ref, g_ref, x_ref,
                  wglu_ref, wsp_ref, wap_ref, wout_ref, lng_ref, lnb_ref, h_ref, o_scr, l_scr, y_scr):
    tm = x_ref.shape[0]

    def token_order(ref, scr, d):
        if d == 1:
            return ref[0].astype(F32)
        nk = ref.shape[-1] // LANES
        for r in range(d):
            v = ref[r].astype(F32)
            for k in range(nk):
                scr[k, pl.ds(r, tm // d, stride=d), :] = v[:, k * LANES:(k + 1) * LANES]
        return jnp.concatenate([scr[k] for k in range(nk)], axis=-1)

    ls, outs = [], []
    for gi, (o_ref, l_ref) in enumerate(((o0_ref, l0_ref), (o1_ref, l1_ref), (o2_ref, l2_ref))):
        d = DILATION_PATTERNS[gi][1]
        outs.append(token_order(o_ref, o_scr.at[gi], d))
        ls.append(token_order(l_ref, l_scr.at[gi], d))
    mx = jnp.maximum(jnp.maximum(ls[0], ls[1]), ls[2])
    es = [jnp.exp(l - mx) for l in ls]
    num = es[0] * outs[0] + es[1] * outs[1] + es[2] * outs[2]
    y_attn = (num / (es[0] + es[1] + es[2])).astype(BF16)
    ys = token_order(ys_ref, y_scr, SSM_CHUNK).astype(BF16)
    cw = GROUP_WIDTH
    y_ssm = []
    for c in range(SSM_WIDTH // cw):
        a = jnp.dot(ys, wglu_ref[:, c * cw:(c + 1) * cw], preferred_element_type=F32)
        b = jnp.dot(ys, wglu_ref[:, SSM_WIDTH + c * cw:SSM_WIDTH + (c + 1) * cw], preferred_element_type=F32)
        y_ssm.append((a * _sigmoid(b)).astype(BF16))
    y_ssm = jnp.concatenate(y_ssm, axis=-1)
    gated = []
    for c in range(D_MODEL // cw):
        pa = jnp.dot(y_ssm, wsp_ref[:, c * cw:(c + 1) * cw], preferred_element_type=F32)
        pb = jnp.dot(y_attn, wap_ref[:, c * cw:(c + 1) * cw], preferred_element_type=F32)
        g_ssm = g_ref[:, c * cw:(c + 1) * cw].astype(F32)
        g_attn = g_ref[:, D_MODEL + c * cw:D_MODEL + (c + 1) * cw].astype(F32)
        gated.append((g_ssm * pa + g_attn * pb).astype(BF16))
    gated = jnp.concatenate(gated, axis=-1)
    mix = jnp.dot(gated, wout_ref[...], preferred_element_type=F32)
    h_ref[...] = _layer_norm(ALPHA * x_ref[...] + mix, lng_ref[...], lnb_ref[...])


def _merge(outs, lses, ys, gates, x2, w_glu, w_sp, w_ap, w_out, ln_g, ln_b, seqlen, tm):
    t = x2.shape[0]
    tiles = seqlen // tm
    row = lambda w: pl.BlockSpec((tm, w), lambda i: (i, 0))
    res_spec = lambda d, w: pl.BlockSpec((None, d, tm // d, w), lambda i: (i // tiles, 0, i % tiles, 0))
    res_specs = [res_spec(d, GROUP_WIDTH) for _, d in DILATION_PATTERNS]
    return pl.pallas_call(
        _merge_kernel,
        grid=(t // tm,),
        in_specs=res_specs + res_specs + [res_spec(SSM_CHUNK, SSM_WIDTH), row(2 * D_MODEL), row(D_MODEL),
                  _const_spec(w_glu.shape), _const_spec(w_sp.shape), _const_spec(w_ap.shape),
                  _const_spec(w_out.shape), _const_spec((1, D_MODEL)), _const_spec((1, D_MODEL))],
        out_specs=row(D_MODEL),
        out_shape=jax.ShapeDtypeStruct((t, D_MODEL), F32),
        scratch_shapes=[pltpu.VMEM((N_GROUPS, GROUP_WIDTH // LANES, tm, LANES), F32),
                        pltpu.VMEM((N_GROUPS, GROUP_WIDTH // LANES, tm, LANES), F32),
                        pltpu.VMEM((SSM_WIDTH // LANES, tm, LANES), F32)],
        compiler_params=_params(1),
        name="merge_ln1",
    )(*outs, *lses, ys, gates, x2, w_glu, w_sp, w_ap, w_out, ln_g, ln_b)


FF_CHUNK = 1024


def _ffn_kernel(h_ref, wup_ref, wdn_ref, lng_ref, lnb_ref, o_ref):
    h = h_ref[...]
    hb = h.astype(BF16)
    acts = []
    for c in range(D_FF // FF_CHUNK):
        lo, hi = c * FF_CHUNK, (c + 1) * FF_CHUNK
        up = jnp.dot(hb, wup_ref[:, lo:hi], preferred_element_type=F32)
        acts.append(jnp.square(jnp.maximum(up, 0.0)).astype(BF16))
    ff = jnp.dot(jnp.concatenate(acts, axis=-1), wdn_ref[...], preferred_element_type=F32)
    o_ref[...] = _layer_norm(ALPHA * h + ff, lng_ref[...], lnb_ref[...])


def _ffn(h1, w_up, w_down, ln_g, ln_b, tm):
    t = h1.shape[0]
    row = pl.BlockSpec((tm, D_MODEL), lambda i: (i, 0))
    return pl.pallas_call(
        _ffn_kernel,
        grid=(t // tm,),
        in_specs=[row, _const_spec(w_up.shape), _const_spec(w_down.shape),
                  _const_spec((1, D_MODEL)), _const_spec((1, D_MODEL))],
        out_specs=row,
        out_shape=jax.ShapeDtypeStruct((t, D_MODEL), F32),
        compiler_params=_params(1),
        name="ffn_ln2",
    )(h1, w_up, w_down, ln_g, ln_b)


def _permute_w_in(w):
    aw = ATTN_WIDTH
    cols = []
    for gi in range(N_GROUPS):
        lo, hi = gi * GROUP_WIDTH, (gi + 1) * GROUP_WIDTH
        cols += [w[:, lo:hi] * (HEAD_DIM ** -0.5), w[:, aw + lo:aw + hi], w[:, 2 * aw + lo:2 * aw + hi]]
    cols.append(w[:, 3 * aw:])
    return jnp.concatenate(cols, axis=1).astype(BF16)


def _layer(h2, bsz, seqlen, l, w_in, b_gate, lambda_re, lambda_im, log_dt, ssm_b_re, ssm_b_im,
           ssm_c_re, ssm_c_im, ssm_d, w_glu, w_ssm_proj, rel_bias, w_attn_proj, w_out,
           ln1_g, ln1_b, w_up, w_down, ln2_g, ln2_b, tm=1024, attn_mt=1024, ffn_tm=1024):
    qkv0, qkv1, qkv2, u, gates = _in_proj(h2, _permute_w_in(w_in[l]), b_gate[l][None, :], bsz, seqlen, tm)
    outs, lses = [], []
    for gi, ((window, dilation), qkv) in enumerate(zip(DILATION_PATTERNS, (qkv0, qkv1, qkv2))):
        rb = rel_bias[:, gi * HEADS_PER_GROUP:(gi + 1) * HEADS_PER_GROUP].astype(F32)
        o, s = _attention_group(qkv, rb, window, dilation, attn_mt)
        outs.append(o)
        lses.append(s)
    ssm_w = _ssm_weights(lambda_re[l], lambda_im[l], log_dt[l], ssm_b_re[l], ssm_b_im[l],
                         ssm_c_re[l], ssm_c_im[l], ssm_d[l])
    ys = _ssm(u, ssm_w, nb=4)
    h1 = _merge(outs, lses, ys, gates, h2, w_glu[l].astype(BF16), w_ssm_proj[l].astype(BF16),
                w_attn_proj[l].astype(BF16), w_out[l].astype(BF16),
                ln1_g[l][None, :], ln1_b[l][None, :], seqlen, tm)
    return _ffn(h1, w_up[l].astype(BF16), w_down[l].astype(BF16), ln2_g[l][None, :], ln2_b[l][None, :], ffn_tm)


def kernel(x, w_in, b_gate, lambda_re, lambda_im, log_dt, ssm_b_re, ssm_b_im, ssm_c_re, ssm_c_im,
           ssm_d, w_glu, w_ssm_proj, rel_bias, w_attn_proj, w_out, ln1_g, ln1_b, w_up, w_down,
           ln2_g, ln2_b):
    bsz, seqlen, d = x.shape
    h = x.reshape(bsz * seqlen, d)
    for l in range(w_in.shape[0]):
        h = _layer(h, bsz, seqlen, l, w_in, b_gate, lambda_re, lambda_im, log_dt, ssm_b_re, ssm_b_im,
                   ssm_c_re, ssm_c_im, ssm_d, w_glu, w_ssm_proj, rel_bias, w_attn_proj, w_out,
                   ln1_g, ln1_b, w_up, w_down, ln2_g, ln2_b)
    return h.reshape(bsz, seqlen, d)
```

```python
import functools
import math

import jax
import jax.numpy as jnp
from jax import lax
from jax.experimental import pallas as pl
from jax.experimental.pallas import tpu as pltpu

F32 = jnp.float32
BF16 = jnp.bfloat16

D_MODEL = 1024
HEAD_DIM = 64
HEADS_PER_GROUP = 4
GROUP_WIDTH = HEADS_PER_GROUP * HEAD_DIM
DILATION_PATTERNS = ((128, 1), (512, 4), (2048, 16))
N_GROUPS = len(DILATION_PATTERNS)
ATTN_WIDTH = N_GROUPS * GROUP_WIDTH
N_BUCKETS = 32
MAX_DISTANCE = 2048
SSM_WIDTH = 512
SSM_GROUP = 16
SSM_GROUPS = 32
SSM_STATE = 64
D_FF = 4 * D_MODEL
DEPTH = 1
ALPHA = (2.0 * DEPTH) ** 0.25
LN_EPS = 1e-5
NEG_INF = -1e30

ATTN_BLOCK = 128
ATTN_SKEW = (3, 5)
SSM_CHUNK = 16
SSM_OCT = 8
LANES = 128
VMEM_LIMIT_BYTES = 56 * 1024 * 1024


def _params(n_axes):
    return pltpu.CompilerParams(dimension_semantics=("arbitrary",) * n_axes,
                                vmem_limit_bytes=VMEM_LIMIT_BYTES)


def _sigmoid(z):
    return 0.5 * jnp.tanh(0.5 * z) + 0.5


def _const_spec(shape):
    nd = len(shape)
    return pl.BlockSpec(shape, lambda *_: (0,) * nd, pipeline_mode=pl.Buffered(1))


def _in_proj_kernel(x_ref, w_ref, bg_ref, qkv0_ref, qkv1_ref, qkv2_ref, u_ref, g_ref, scr):
    xb = x_ref[...].astype(BF16)
    tm = xb.shape[0]

    def mm(lo, hi):
        return jnp.dot(xb, w_ref[:, lo:hi], preferred_element_type=F32)

    def emit(ref, res, d, col0):
        width = res.shape[1]
        if d == 1:
            ref[0, :, col0:col0 + width] = res.astype(BF16)
            return
        slot0 = next_slot[0]
        next_slot[0] += width // LANES
        for k in range(width // LANES):
            scr[slot0 + k] = res[:, k * LANES:(k + 1) * LANES]
        for r in range(d):
            for k in range(width // LANES):
                col = col0 + k * LANES
                ref[r, :, col:col + LANES] = scr[slot0 + k, pl.ds(r, tm // d, stride=d), :].astype(BF16)

    next_slot = [0]

    gw3 = 3 * GROUP_WIDTH
    for gi, ref in enumerate((qkv0_ref, qkv1_ref, qkv2_ref)):
        for c in range(3):
            lo = gi * gw3 + c * GROUP_WIDTH
            emit(ref, mm(lo, lo + GROUP_WIDTH), DILATION_PATTERNS[gi][1], c * GROUP_WIDTH)
    base = N_GROUPS * gw3
    emit(u_ref, mm(base, base + SSM_WIDTH), SSM_CHUNK, 0)
    base += SSM_WIDTH
    for c in range(2 * D_MODEL // GROUP_WIDTH):
        lo, hi = c * GROUP_WIDTH, (c + 1) * GROUP_WIDTH
        z = mm(base + lo, base + hi) + bg_ref[:, lo:hi]
        g_ref[:, lo:hi] = _sigmoid(z).astype(BF16)


def _in_proj(x2, w_perm, b_gate, bsz, seqlen, tm):
    t = x2.shape[0]
    n_in = w_perm.shape[1]
    tiles = seqlen // tm
    row = lambda w: pl.BlockSpec((tm, w), lambda i: (i, 0))
    w3 = 3 * GROUP_WIDTH
    dils = [d for _, d in DILATION_PATTERNS]
    n_stage = (sum(d > 1 for d in dils) * w3 + SSM_WIDTH) // LANES
    res_spec = lambda d, w:pl.BlockSpec((None, d, tm // d, w), lambda i: (i // tiles, 0, i % tiles, 0))
    return pl.pallas_call(
        _in_proj_kernel,
        grid=(t // tm,),
        in_specs=[row(D_MODEL), _const_spec((D_MODEL, n_in)), _const_spec((1, 2 * D_MODEL))],
        out_specs=[res_spec(d, w3) for d in dils] + [res_spec(SSM_CHUNK, SSM_WIDTH), row(2 * D_MODEL)],
        out_shape=[jax.ShapeDtypeStruct((bsz, d, seqlen // d, w3), BF16) for d in dils]
        + [jax.ShapeDtypeStruct((bsz, SSM_CHUNK, seqlen // SSM_CHUNK, SSM_WIDTH), BF16),
           jax.ShapeDtypeStruct((t, 2 * D_MODEL), BF16)],
        scratch_shapes=[pltpu.VMEM((n_stage, tm, LANES), F32)],
        compiler_params=_params(1),
        name="in_proj",
    )(x2, w_perm, b_gate)


def _attn_kernel(bucket_ref, relb_ref, cur_ref, prev_ref, out_ref, lse_ref, bias_scr, k_scr, vt_scr, *, nq, nres):
    blk = ATTN_BLOCK
    first = (pl.program_id(0) == 0) & (pl.program_id(1) == 0) & (pl.program_id(2) == 0)

    @pl.when(first)
    def _build_bias():
        bucket = bucket_ref[...]
        for h in range(HEADS_PER_GROUP):
            acc = jnp.full(bucket.shape, NEG_INF, F32)
            for bkt in range(N_BUCKETS):
                acc = jnp.where(bucket == bkt, relb_ref[bkt, h], acc)
            bias_scr[h // 2, :, (h % 2) * blk:(h % 2 + 1) * blk] = acc

    row = lax.broadcasted_iota(jnp.int32, (2 * blk, blk), 0)
    row_head = row // HEAD_DIM
    keep_first = (row >= blk) | (pl.program_id(2) > 0)
    for r in range(nres):
        k_scr[r, 0:blk, :] = prev_ref[r, :, GROUP_WIDTH:2 * GROUP_WIDTH]
        k_scr[r, blk:, :] = cur_ref[r, :, GROUP_WIDTH:2 * GROUP_WIDTH]
        vt_scr[r, :, 0:blk] = prev_ref[r, :, 2 * GROUP_WIDTH:3 * GROUP_WIDTH].T
        for j in range(nq):
            vt_scr[r, :, (j + 1) * blk:(j + 2) * blk] = cur_ref[r, j * blk:(j + 1) * blk,
                                                                 2 * GROUP_WIDTH:3 * GROUP_WIDTH].T

    npair = HEADS_PER_GROUP // 2
    units = [(r, j, hp) for r in range(nres) for j in range(nq) for hp in range(npair)]
    qts = {}

    def scores(r, j, hp):
        if (r, j) not in qts:
            qts[(r, j)] = cur_ref[r, j * blk:(j + 1) * blk, 0:GROUP_WIDTH].T
        qt = qts[(r, j)]
        k2 = k_scr[r, j * blk:(j + 2) * blk, :]
        qh = jnp.concatenate([jnp.where(row_head == 2 * hp + i, qt, jnp.zeros_like(qt)) for i in range(2)], axis=1)
        return jnp.dot(k2, qh, preferred_element_type=F32)

    def softmax(r, j, hp, s2):
        ps, ms, ls = [], [], []
        for i in range(2):
            lanes = slice(i * blk, (i + 1) * blk)
            s = s2[:, lanes] + bias_scr[hp, :, lanes]
            if j == 0:
                s = jnp.where(keep_first, s, NEG_INF)
            m = jnp.max(s, axis=0, keepdims=True)
            p = jnp.exp(s - m)
            ls.append(jnp.sum(p, axis=0, keepdims=True))
            ps.append(p.astype(BF16))
            ms.append(m)
        return jnp.concatenate(ps, axis=1), jnp.concatenate(ms, axis=1), jnp.concatenate(ls, axis=1)

    def values(r, j, hp, p, m, l):
        vt = vt_scr[r, 2 * hp * HEAD_DIM:2 * (hp + 1) * HEAD_DIM, j * blk:(j + 2) * blk]
        ot = jnp.dot(vt, p, preferred_element_type=F32) * (1.0 / l)
        lse = jnp.broadcast_to(m + jnp.log(l), (HEAD_DIM, 2 * blk))
        return [(ot[i * HEAD_DIM:(i + 1) * HEAD_DIM, i * blk:(i + 1) * blk], lse[:, i * blk:(i + 1) * blk])
                for i in range(2)]

    raw, soft, done = {}, {}, []
    lag_soft, lag_val = ATTN_SKEW
    for step in range(len(units) + lag_val):
        if step < len(units):
            raw[step] = scores(*units[step])
        if lag_soft <= step < len(units) + lag_soft:
            soft[step - lag_soft] = softmax(*units[step - lag_soft], raw.pop(step - lag_soft))
        if step >= lag_val:
            i = step - lag_val
            r, j, hp = units[i]
            done += values(r, j, hp, *soft.pop(i))
            if hp == npair - 1:
                rows = slice(j * blk, (j + 1) * blk)
                out_ref[r, rows, :] = jnp.concatenate([d[0] for d in done], axis=0).T.astype(BF16)
                lse_ref[r, rows, :] = jnp.concatenate([d[1] for d in done], axis=0).T
                done = []


def _t5_bucket(dist):
    max_exact = N_BUCKETS // 2
    d = jnp.maximum(dist, 1).astype(F32)
    large = max_exact + (jnp.log(d / max_exact) / math.log(MAX_DISTANCE / max_exact)
                         * (N_BUCKETS - max_exact)).astype(jnp.int32)
    large = jnp.minimum(large, N_BUCKETS - 1)
    return jnp.where(dist < max_exact, dist, large)


def _bucket_table(window, dilation):
    blk = ATTN_BLOCK
    span = window // dilation
    rel = jnp.arange(blk)[None, :] + blk - jnp.arange(2 * blk)[:, None]
    valid = (rel >= 0) & (rel <= span)
    return jnp.where(valid, _t5_bucket(jnp.maximum(rel, 0) * dilation), -1).astype(jnp.int32)


def _attention_group(qkv, rel_bias_g, window, dilation, mt):
    assert window // dilation == ATTN_BLOCK
    bsz, _, n, w3 = qkv.shape
    rows = mt
    mt = min(rows, n)
    nres = rows // mt
    nq = mt // ATTN_BLOCK
    cur = pl.BlockSpec((None, nres, mt, w3), lambda b, r, i: (b, r, i, 0))
    prev = pl.BlockSpec((None, nres, ATTN_BLOCK, w3), lambda b, r, i: (b, r, jnp.maximum(i * nq - 1, 0), 0))
    ospec = pl.BlockSpec((None, nres, mt, GROUP_WIDTH), lambda b, r, i: (b, r, i, 0))
    return pl.pallas_call(
        functools.partial(_attn_kernel, nq=nq, nres=nres),
        grid=(bsz, dilation // nres, n // mt),
        in_specs=[_const_spec((2 * ATTN_BLOCK, ATTN_BLOCK)),
                  pl.BlockSpec(memory_space=pltpu.SMEM), cur, prev],
        out_specs=[ospec, ospec],
        out_shape=[jax.ShapeDtypeStruct((bsz, dilation, n, GROUP_WIDTH), BF16),
                   jax.ShapeDtypeStruct((bsz, dilation, n, GROUP_WIDTH), F32)],
        scratch_shapes=[pltpu.VMEM((HEADS_PER_GROUP // 2, 2 * ATTN_BLOCK, 2 * ATTN_BLOCK), F32),
                        pltpu.VMEM((nres, mt + ATTN_BLOCK, GROUP_WIDTH), BF16),
                        pltpu.VMEM((nres, GROUP_WIDTH, mt + ATTN_BLOCK), BF16)],
        compiler_params=_params(3),
        name=f"attn_d{dilation}",
    )(_bucket_table(window, dilation), rel_bias_g, qkv, qkv)


def _ssm_prep_kernel(logdt_ref, lr_ref, li_ref, bt_re_ref, bt_im_ref, c_re_ref, c_im_ref, dl_ref,
                     toep_ref, bst_ref, cst_ref, a_ref):
    cs = SSM_CHUNK
    dt = jnp.exp(jnp.full((1, SSM_STATE), logdt_ref[pl.program_id(0)], F32))
    lr, li = lr_ref[...], li_ref[...]
    mag = jnp.exp(lr * dt)
    ab_re, ab_im = mag * jnp.cos(li * dt), mag * jnp.sin(li * dt)
    den = lr * lr + li * li
    nr = ab_re - 1.0
    k_re = (nr * lr + ab_im * li) / den
    k_im = (ab_im * lr - nr * li) / den
    bt_re, bt_im = bt_re_ref[...], bt_im_ref[...]
    bb_re = k_re * bt_re - k_im * bt_im
    bb_im = k_re * bt_im + k_im * bt_re
    j = lax.broadcasted_iota(jnp.int32, (cs + 8, SSM_STATE), 0).astype(F32)
    pmag = jnp.exp(lr * dt * j)
    ang = li * dt * j
    p_re, p_im = pmag * jnp.cos(ang), pmag * jnp.sin(ang)
    c_re, c_im = c_re_ref[...], c_im_ref[...]
    cp_re = [c_re * p_re[i:i + 1] - c_im * p_im[i:i + 1] for i in range(cs + 1)]
    cp_im = [c_re * p_im[i:i + 1] + c_im * p_re[i:i + 1] for i in range(cs + 1)]
    nt = (((1,), (1,)), ((), ()))
    hi = lax.Precision.HIGHEST
    kcat = (lax.dot_general(bb_re, jnp.concatenate(cp_re[:cs], axis=0), nt, precision=hi,
                            preferred_element_type=F32)
            - lax.dot_general(bb_im, jnp.concatenate(cp_im[:cs], axis=0), nt, precision=hi,
                              preferred_element_type=F32))
    lane = lax.broadcasted_iota(jnp.int32, kcat.shape, 1)
    row = lax.broadcasted_iota(jnp.int32, kcat.shape, 0)
    dl = dl_ref[...]
    for s in range(cs):
        off = s * SSM_GROUP
        t_s = kcat if s == 0 else jnp.where(lane >= off, pltpu.roll(kcat, off, 1), 0.0)
        toep_ref[s] = (t_s + jnp.where(lane == off + row, dl, 0.0)).astype(BF16)
        pe_re, pe_im = p_re[cs - 1 - s:cs - s], p_im[cs - 1 - s:cs - s]
        st_re = pe_re * bb_re - pe_im * bb_im
        st_im = pe_re * bb_im + pe_im * bb_re
        bst_ref[s] = jnp.concatenate([st_re, st_im], axis=-1).astype(BF16)
    ro = jnp.concatenate([jnp.concatenate(cp_re[1:], axis=0),
                          -jnp.concatenate(cp_im[1:], axis=0)], axis=-1)
    ro_t = ro.T
    cst_ref[0] = ro_t[:SSM_STATE].astype(BF16)
    cst_ref[1] = ro_t[SSM_STATE:].astype(BF16)
    a_ref[...] = jnp.concatenate([p_re[cs:cs + 1], p_im[cs:cs + 1]], axis=0)


def _ssm_weights(lambda_re, lambda_im, log_dt, b_re, b_im, c_re, c_im, d_skip):
    f32 = F32
    cs, ng, no = SSM_CHUNK, SSM_GROUPS, SSM_GROUPS // SSM_OCT
    grp = lambda *shape: pl.BlockSpec((None,) + shape, lambda g: (g,) + (0,) * len(shape))
    in_oct = lambda *shape: pl.BlockSpec((None, shape[0], None) + shape[1:],
                                         lambda g: (g // SSM_OCT, 0, g % SSM_OCT) + (0,) * (len(shape) - 1))
    toep_c, bst_c, cst_c, a32 = pl.pallas_call(
        _ssm_prep_kernel,
        grid=(ng,),
        in_specs=[pl.BlockSpec(memory_space=pltpu.SMEM), grp(1, SSM_STATE), grp(1, SSM_STATE),
                  grp(SSM_GROUP, SSM_STATE), grp(SSM_GROUP, SSM_STATE),
                  grp(SSM_GROUP, SSM_STATE), grp(SSM_GROUP, SSM_STATE), grp(1, cs * SSM_GROUP)],
        out_specs=[in_oct(cs, SSM_GROUP, cs * SSM_GROUP), in_oct(cs, SSM_GROUP, 2 * SSM_STATE),
                   in_oct(2, SSM_STATE, cs * SSM_GROUP), grp(2, SSM_STATE)],
        out_shape=[jax.ShapeDtypeStruct((no, cs, SSM_OCT, SSM_GROUP, cs * SSM_GROUP), BF16),
                   jax.ShapeDtypeStruct((no, cs, SSM_OCT, SSM_GROUP, 2 * SSM_STATE), BF16),
                   jax.ShapeDtypeStruct((no, 2, SSM_OCT, SSM_STATE, cs * SSM_GROUP), BF16),
                   jax.ShapeDtypeStruct((ng, 2, SSM_STATE), f32)],
        compiler_params=_params(1),
        name="ssm_prep",
    )(log_dt.astype(f32), lambda_re.astype(f32)[:, None, :], lambda_im.astype(f32)[:, None, :],
      b_re.astype(f32).transpose(0, 2, 1), b_im.astype(f32).transpose(0, 2, 1),
      c_re.astype(f32), c_im.astype(f32), jnp.tile(d_skip.astype(f32), (1, cs))[:, None, :])
    avec = a32.reshape(no, SSM_OCT, 2, SSM_STATE).transpose(0, 2, 1, 3).reshape(no, 2, SSM_OCT * SSM_STATE)
    return (toep_c.reshape(no, cs * LANES, cs * SSM_GROUP),
            bst_c.reshape(no, cs * LANES, 2 * SSM_STATE),
            cst_c.reshape(no, 2 * SSM_OCT * SSM_STATE, cs * SSM_GROUP),
            avec)


def _expand_ssm_weights(toep_ref, bstc_ref, cstc_ref, wt_scr, bst_scr, cst_scr):
    tw = 2 * LANES
    sh_g, sh_n = SSM_GROUP.bit_length() - 1, SSM_STATE.bit_length() - 1
    r = lax.broadcasted_iota(jnp.int32, (tw, tw), 0)
    c = lax.broadcasted_iota(jnp.int32, (tw, tw), 1)
    col_grp = (c >> sh_g) & (SSM_OCT - 1)
    row_grp = (r >> sh_g) & (SSM_OCT - 1)
    for tp in range(SSM_CHUNK // 2):
        src_col = (2 * tp + (c >> (sh_g + 3))) * SSM_GROUP + (c & (SSM_GROUP - 1))
        e = jnp.where(r == src_col, 1.0, 0.0).astype(BF16)
        base = tp * (tp + 1) // 2
        for sp in range(tp + 1):
            x = jnp.dot(toep_ref[sp * tw:(sp + 1) * tw, :], e, preferred_element_type=F32)
            wt_scr[base + sp] = jnp.where(row_grp == col_grp, x, 0.0).astype(BF16)
        for j in range(2 * SSM_OCT * SSM_STATE // tw):
            x = jnp.dot(cstc_ref[j * tw:(j + 1) * tw, :], e, preferred_element_type=F32)
            row_grp_n = ((j * tw + r) >> sh_n) & (SSM_OCT - 1)
            cst_scr[j * tw:(j + 1) * tw, tp * tw:(tp + 1) * tw] = jnp.where(row_grp_n == col_grp, x, 0.0).astype(BF16)
    ns = 2 * SSM_OCT * SSM_STATE
    rb = lax.broadcasted_iota(jnp.int32, (2 * SSM_STATE, ns), 0)
    cb = lax.broadcasted_iota(jnp.int32, (2 * SSM_STATE, ns), 1)
    src_col_b = (cb >> (sh_n + 3)) * SSM_STATE + (cb & (SSM_STATE - 1))
    eb = jnp.where(rb == src_col_b, 1.0, 0.0).astype(BF16)
    rr = lax.broadcasted_iota(jnp.int32, (tw, ns), 0)
    cc = lax.broadcasted_iota(jnp.int32, (tw, ns), 1)
    keep = ((rr >> sh_g) & (SSM_OCT - 1)) == ((cc >> sh_n) & (SSM_OCT - 1))
    for j in range(SSM_CHUNK * LANES // tw):
        x = jnp.dot(bstc_ref[j * tw:(j + 1) * tw, :], eb, preferred_element_type=F32)
        bst_scr[j * tw:(j + 1) * tw, :] = jnp.where(keep, x, 0.0).astype(BF16)


def _ssm_kernel(u_ref, toep_ref, bstc_ref, cstc_ref, a_ref, y_ref, wt_ref, bst_ref, cst_ref, zz_scr, hp_scr,
                *, n_chunks, nb):
    @pl.when(pl.program_id(1) == 0)
    def _new_octet():
        _expand_ssm_weights(toep_ref, bstc_ref, cstc_ref, wt_ref, bst_ref, cst_ref)

    rows = nb * n_chunks
    nk = SSM_OCT * SSM_STATE // LANES
    u_t = [u_ref[:, s].reshape(rows, LANES) for s in range(SSM_CHUNK)]
    z = jnp.dot(jnp.concatenate(u_t, axis=-1), bst_ref[...], preferred_element_type=F32)
    for k in range(2 * nk):
        zz_scr[k] = z[:, k * LANES:(k + 1) * LANES]
    a_re = [jnp.broadcast_to(a_ref[0:1, k * LANES:(k + 1) * LANES], (nb, LANES)) for k in range(nk)]
    a_im = [jnp.broadcast_to(a_ref[1:2, k * LANES:(k + 1) * LANES], (nb, LANES)) for k in range(nk)]

    def step(c, carry):
        h_re, h_im = carry
        rows_c = pl.ds(c, nb, stride=n_chunks)
        new_re, new_im = [], []
        for k in range(nk):
            hp_scr[k, rows_c, :] = h_re[k]
            hp_scr[nk + k, rows_c, :] = h_im[k]
            new_re.append(a_re[k] * h_re[k] - a_im[k] * h_im[k] + zz_scr[k, rows_c, :])
            new_im.append(a_re[k] * h_im[k] + a_im[k] * h_re[k] + zz_scr[nk + k, rows_c, :])
        return tuple(new_re), tuple(new_im)

    zero = tuple(jnp.zeros((nb, LANES), F32) for _ in range(nk))
    lax.fori_loop(0, n_chunks, step, (zero, zero), unroll=4)
    hp = jnp.concatenate([hp_scr[k] for k in range(2 * nk)], axis=-1).astype(BF16)
    tw = 2 * LANES
    for tp in range(SSM_CHUNK // 2):
        base = tp * (tp + 1) // 2
        y = jnp.dot(hp, cst_ref[:, tp * tw:(tp + 1) * tw], preferred_element_type=F32)
        y = y + jnp.dot(jnp.concatenate(u_t[:2 * (tp + 1)], axis=-1),
                        wt_ref[base:base + tp + 1].reshape((tp + 1) * tw, tw), preferred_element_type=F32)
        y = jax.nn.gelu(y).astype(BF16)
        y_ref[:, 2 * tp] = y[:, :LANES].reshape(nb, n_chunks, LANES)
        y_ref[:, 2 * tp + 1] = y[:, LANES:].reshape(nb, n_chunks, LANES)


def _ssm(u16, weights, nb):
    toep_c, bst_c, cst_c, avec = weights
    bsz, cs, nc, width = u16.shape
    oct_spec = lambda a: pl.BlockSpec((None,) + a.shape[1:], lambda o, b: (o,) + (0,) * (a.ndim - 1))
    io_spec = pl.BlockSpec((nb, cs, nc, LANES), lambda o, b: (b, 0, 0, o))
    n_state = 2 * SSM_OCT * SSM_STATE
    n_pairs = (cs // 2) * (cs // 2 + 1) // 2
    return pl.pallas_call(
        functools.partial(_ssm_kernel, n_chunks=nc, nb=nb),
        grid=(width // LANES, bsz // nb),
        in_specs=[io_spec, oct_spec(toep_c), oct_spec(bst_c), oct_spec(cst_c), oct_spec(avec)],
        out_specs=io_spec,
        out_shape=jax.ShapeDtypeStruct(u16.shape, BF16),
        scratch_shapes=[pltpu.VMEM((n_pairs, 2 * LANES, 2 * LANES), BF16),
                        pltpu.VMEM((cs * LANES, n_state), BF16),
                        pltpu.VMEM((n_state, cs * LANES), BF16),
                        pltpu.VMEM((n_state // LANES, nb * nc, LANES), F32),
                        pltpu.VMEM((n_state // LANES, nb * nc, LANES), F32)],
        compiler_params=_params(2),
        name="ssm",
    )(u16, toep_c, bst_c, cst_c, avec)


def _layer_norm(v, g, b):
    mu = jnp.mean(v, axis=-1, keepdims=True)
    vc = v - mu
    var = jnp.mean(vc * vc, axis=-1, keepdims=True)
    return vc * lax.rsqrt(var + LN_EPS) * g + b


def _merge_kernel(o0_ref, o1_ref, o2_ref, l0_ref, l1_ref, l2_ref, ys_ref, g_ref, x_ref,
                  wglu_ref, wsp_ref, wap_ref, wout_ref, lng_ref, lnb_ref, h_ref, o_scr, l_scr, y_scr):
    tm = x_ref.shape[0]

    def token_order(ref, scr, d):
        if d == 1:
            return ref[0].astype(F32)
        nk = ref.shape[-1] // LANES
        for r in range(d):
            v = ref[r].astype(F32)
            for k in range(nk):
                scr[k, pl.ds(r, tm // d, stride=d), :] = v[:, k * LANES:(k + 1) * LANES]
        return jnp.concatenate([scr[k] for k in range(nk)], axis=-1)

    ls, outs = [], []
    for gi, (o_ref, l_ref) in enumerate(((o0_ref, l0_ref), (o1_ref, l1_ref), (o2_ref, l2_ref))):
        d = DILATION_PATTERNS[gi][1]
        outs.append(token_order(o_ref, o_scr.at[gi], d))
        ls.append(token_order(l_ref, l_scr.at[gi], d))
    mx = jnp.maximum(jnp.maximum(ls[0], ls[1]), ls[2])
    es = [jnp.exp(l - mx) for l in ls]
    num = es[0] * outs[0] + es[1] * outs[1] + es[2] * outs[2]
    y_attn = (num / (es[0] + es[1] + es[2])).astype(BF16)
    ys = token_order(ys_ref, y_scr, SSM_CHUNK).astype(BF16)
    cw = GROUP_WIDTH
    y_ssm = []
    for c in range(SSM_WIDTH // cw):
        a = jnp.dot(ys, wglu_ref[:, c * cw:(c + 1) * cw], preferred_element_type=F32)
        b = jnp.dot(ys, wglu_ref[:, SSM_WIDTH + c * cw:SSM_WIDTH + (c + 1) * cw], preferred_element_type=F32)
        y_ssm.append((a * _sigmoid(b)).astype(BF16))
    y_ssm = jnp.concatenate(y_ssm, axis=-1)
    gated = []
    for c in range(D_MODEL // cw):
        pa = jnp.dot(y_ssm, wsp_ref[:, c * cw:(c + 1) * cw], preferred_element_type=F32)
        pb = jnp.dot(y_attn, wap_ref[:, c * cw:(c + 1) * cw], preferred_element_type=F32)
        g_ssm = g_ref[:, c * cw:(c + 1) * cw].astype(F32)
        g_attn = g_ref[:, D_MODEL + c * cw:D_MODEL + (c + 1) * cw].astype(F32)
        gated.append((g_ssm * pa + g_attn * pb).astype(BF16))
    gated = jnp.concatenate(gated, axis=-1)
    mix = jnp.dot(gated, wout_ref[...], preferred_element_type=F32)
    h_ref[...] = _layer_norm(ALPHA * x_ref[...] + mix, lng_ref[...], lnb_ref[...])


def _merge(outs, lses, ys, gates, x2, w_glu, w_sp, w_ap, w_out, ln_g, ln_b, seqlen, tm):
    t = x2.shape[0]
    tiles = seqlen // tm
    row = lambda w: pl.BlockSpec((tm, w), lambda i: (i, 0))
    res_spec = lambda d, w: pl.BlockSpec((None, d, tm // d, w), lambda i: (i // tiles, 0, i % tiles, 0))
    res_specs = [res_spec(d, GROUP_WIDTH) for _, d in DILATION_PATTERNS]
    return pl.pallas_call(
        _merge_kernel,
        grid=(t // tm,),
        in_specs=res_specs + res_specs + [res_spec(SSM_CHUNK, SSM_WIDTH), row(2 * D_MODEL), row(D_MODEL),
                  _const_spec(w_glu.shape), _const_spec(w_sp.shape), _const_spec(w_ap.shape),
                  _const_spec(w_out.shape), _const_spec((1, D_MODEL)), _const_spec((1, D_MODEL))],
        out_specs=row(D_MODEL),
        out_shape=jax.ShapeDtypeStruct((t, D_MODEL), F32),
        scratch_shapes=[pltpu.VMEM((N_GROUPS, GROUP_WIDTH // LANES, tm, LANES), F32),
                        pltpu.VMEM((N_GROUPS, GROUP_WIDTH // LANES, tm, LANES), F32),
                        pltpu.VMEM((SSM_WIDTH // LANES, tm, LANES), F32)],
        compiler_params=_params(1),
        name="merge_ln1",
    )(*outs, *lses, ys, gates, x2, w_glu, w_sp, w_ap, w_out, ln_g, ln_b)


FF_CHUNK = 1024


def _ffn_kernel(h_ref, wup_ref, wdn_ref, lng_ref, lnb_ref, o_ref):
    h = h_ref[...]
    hb = h.astype(BF16)
    acts = []
    for c in range(D_FF // FF_CHUNK):
        lo, hi = c * FF_CHUNK, (c + 1) * FF_CHUNK
        up = jnp.dot(hb, wup_ref[:, lo:hi], preferred_element_type=F32)
        acts.append(jnp.square(jnp.maximum(up, 0.0)).astype(BF16))
    ff = jnp.dot(jnp.concatenate(acts, axis=-1), wdn_ref[...], preferred_element_type=F32)
    o_ref[...] = _layer_norm(ALPHA * h + ff, lng_ref[...], lnb_ref[...])


def _ffn(h1, w_up, w_down, ln_g, ln_b, tm):
    t = h1.shape[0]
    row = pl.BlockSpec((tm, D_MODEL), lambda i: (i, 0))
    return pl.pallas_call(
        _ffn_kernel,
        grid=(t // tm,),
        in_specs=[row, _const_spec(w_up.shape), _const_spec(w_down.shape),
                  _const_spec((1, D_MODEL)), _const_spec((1, D_MODEL))],
        out_specs=row,
        out_shape=jax.ShapeDtypeStruct((t, D_MODEL), F32),
        compiler_params=_params(1),
        name="ffn_ln2",
    )(h1, w_up, w_down, ln_g, ln_b)


def _permute_w_in(w):
    aw = ATTN_WIDTH
    cols = []
    for gi in range(N_GROUPS):
        lo, hi = gi * GROUP_WIDTH, (gi + 1) * GROUP_WIDTH
        cols += [w[:, lo:hi] * (HEAD_DIM ** -0.5), w[:, aw + lo:aw + hi], w[:, 2 * aw + lo:2 * aw + hi]]
    cols.append(w[:, 3 * aw:])
    return jnp.concatenate(cols, axis=1).astype(BF16)


def _layer(h2, bsz, seqlen, l, w_in, b_gate, lambda_re, lambda_im, log_dt, ssm_b_re, ssm_b_im,
           ssm_c_re, ssm_c_im, ssm_d, w_glu, w_ssm_proj, rel_bias, w_attn_proj, w_out,
           ln1_g, ln1_b, w_up, w_down, ln2_g, ln2_b, tm=1024, attn_mt=1024, ffn_tm=1024):
    qkv0, qkv1, qkv2, u, gates = _in_proj(h2, _permute_w_in(w_in[l]), b_gate[l][None, :], bsz, seqlen, tm)
    outs, lses = [], []
    for gi, ((window, dilation), qkv) in enumerate(zip(DILATION_PATTERNS, (qkv0, qkv1, qkv2))):
        rb = rel_bias[:, gi * HEADS_PER_GROUP:(gi + 1) * HEADS_PER_GROUP].astype(F32)
        o, s = _attention_group(qkv, rb, window, dilation, attn_mt)
        outs.append(o)
        lses.append(s)
    ssm_w = _ssm_weights(lambda_re[l], lambda_im[l], log_dt[l], ssm_b_re[l], ssm_b_im[l],
                         ssm_c_re[l], ssm_c_im[l], ssm_d[l])
    ys = _ssm(u, ssm_w, nb=4)
    h1 = _merge(outs, lses, ys, gates, h2, w_glu[l].astype(BF16), w_ssm_proj[l].astype(BF16),
                w_attn_proj[l].astype(BF16), w_out[l].astype(BF16),
                ln1_g[l][None, :], ln1_b[l][None, :], seqlen, tm)
    return _ffn(h1, w_up[l].astype(BF16), w_down[l].astype(BF16), ln2_g[l][None, :], ln2_b[l][None, :], ffn_tm)


def kernel(x, w_in, b_gate, lambda_re, lambda_im, log_dt, ssm_b_re, ssm_b_im, ssm_c_re, ssm_c_im,
           ssm_d, w_glu, w_ssm_proj, rel_bias, w_attn_proj, w_out, ln1_g, ln1_b, w_up, w_down,
           ln2_g, ln2_b):
    bsz, seqlen, d = x.shape
    h = x.reshape(bsz * seqlen, d)
    for l in range(w_in.shape[0]):
        h = _layer(h, bsz, seqlen, l, w_in, b_gate, lambda_re, lambda_im, log_dt, ssm_b_re, ssm_b_im,
                   ssm_c_re, ssm_c_im, ssm_d, w_glu, w_ssm_proj, rel_bias, w_attn_proj, w_out,
                   ln1_g, ln1_b, w_up, w_down, ln2_g, ln2_b)
    return h.reshape(bsz, seqlen, d)
```

```python
import functools
import math

import jax
import jax.numpy as jnp
from jax import lax
from jax.experimental import pallas as pl
from jax.experimental.pallas import tpu as pltpu

F32 = jnp.float32
BF16 = jnp.bfloat16

D_MODEL = 1024
HEAD_DIM = 64
HEADS_PER_GROUP = 4
GROUP_WIDTH = HEADS_PER_GROUP * HEAD_DIM
DILATION_PATTERNS = ((128, 1), (512, 4), (2048, 16))
N_GROUPS = len(DILATION_PATTERNS)
ATTN_WIDTH = N_GROUPS * GROUP_WIDTH
N_BUCKETS = 32
MAX_DISTANCE = 2048
SSM_WIDTH = 512
SSM_GROUP = 16
SSM_GROUPS = 32
SSM_STATE = 64
D_FF = 4 * D_MODEL
DEPTH = 1
ALPHA = (2.0 * DEPTH) ** 0.25
LN_EPS = 1e-5
NEG_INF = -1e30

ATTN_BLOCK = 128
ATTN_SKEW = (3, 5)
SSM_CHUNK = 16
SSM_OCT = 8
LANES = 128
SUBLANES = 8
VMEM_LIMIT_BYTES = 56 * 1024 * 1024


def _params(n_axes):
    return pltpu.CompilerParams(dimension_semantics=("arbitrary",) * n_axes,
                                vmem_limit_bytes=VMEM_LIMIT_BYTES)


def _sigmoid(z):
    return 0.5 * jnp.tanh(0.5 * z) + 0.5


def _row_pitch(stride):
    return stride + SUBLANES if stride % (2 * SUBLANES) == 0 else stride


def _const_spec(shape):
    nd = len(shape)
    return pl.BlockSpec(shape, lambda *_: (0,) * nd, pipeline_mode=pl.Buffered(1))


def _in_proj_kernel(x_ref, w_ref, bg_ref, qkv0_ref, qkv1_ref, qkv2_ref, u_ref, g_ref, scr):
    xb = x_ref[...].astype(BF16)
    tm = xb.shape[0]

    def mm(lo, hi):
        return jnp.dot(xb, w_ref[:, lo:hi], preferred_element_type=F32)

    def emit(ref, res, d, col0):
        width = res.shape[1]
        if d == 1:
            ref[0, :, col0:col0 + width] = res.astype(BF16)
            return
        slot0 = next_slot[0]
        next_slot[0] += width // LANES
        pitch = _row_pitch(d)
        for k in range(width // LANES):
            tile = res[:, k * LANES:(k + 1) * LANES]
            if pitch == d:
                scr[slot0 + k, 0:tm, :] = tile
            else:
                for g in range(tm // d):
                    scr[slot0 + k, g * pitch:g * pitch + d, :] = tile[g * d:(g + 1) * d]
        for r in range(d):
            for k in range(width // LANES):
                col = col0 + k * LANES
                ref[r, :, col:col + LANES] = scr[slot0 + k, pl.ds(r, tm // d, stride=pitch), :].astype(BF16)

    next_slot = [0]

    gw3 = 3 * GROUP_WIDTH
    for gi, ref in enumerate((qkv0_ref, qkv1_ref, qkv2_ref)):
        for c in range(3):
            lo = gi * gw3 + c * GROUP_WIDTH
            emit(ref, mm(lo, lo + GROUP_WIDTH), DILATION_PATTERNS[gi][1], c * GROUP_WIDTH)
    base = N_GROUPS * gw3
    emit(u_ref, mm(base, base + SSM_WIDTH), SSM_CHUNK, 0)
    base += SSM_WIDTH
    for c in range(2 * D_MODEL // GROUP_WIDTH):
        lo, hi = c * GROUP_WIDTH, (c + 1) * GROUP_WIDTH
        z = mm(base + lo, base + hi) + bg_ref[:, lo:hi]
        g_ref[:, lo:hi] = _sigmoid(z).astype(BF16)


def _in_proj(x2, w_perm, b_gate, bsz, seqlen, tm):
    t = x2.shape[0]
    n_in = w_perm.shape[1]
    tiles = seqlen // tm
    row = lambda w: pl.BlockSpec((tm, w), lambda i: (i, 0))
    w3 = 3 * GROUP_WIDTH
    dils = [d for _, d in DILATION_PATTERNS]
    n_stage = (sum(d > 1 for d in dils) * w3 + SSM_WIDTH) // LANES
    res_spec = lambda d, w:pl.BlockSpec((None, d, tm // d, w), lambda i: (i // tiles, 0, i % tiles, 0))
    return pl.pallas_call(
        _in_proj_kernel,
        grid=(t // tm,),
        in_specs=[row(D_MODEL), _const_spec((D_MODEL, n_in)), _const_spec((1, 2 * D_MODEL))],
        out_specs=[res_spec(d, w3) for d in dils] + [res_spec(SSM_CHUNK, SSM_WIDTH), row(2 * D_MODEL)],
        out_shape=[jax.ShapeDtypeStruct((bsz, d, seqlen // d, w3), BF16) for d in dils]
        + [jax.ShapeDtypeStruct((bsz, SSM_CHUNK, seqlen // SSM_CHUNK, SSM_WIDTH), BF16),
           jax.ShapeDtypeStruct((t, 2 * D_MODEL), BF16)],
        scratch_shapes=[pltpu.VMEM((n_stage, max(tm // d * _row_pitch(d) for d in dils + [SSM_CHUNK]), LANES), F32)],
        compiler_params=_params(1),
        name="in_proj",
    )(x2, w_perm, b_gate)


def _attn_kernel(bucket_ref, relb_ref, cur_ref, prev_ref, out_ref, lse_ref, bias_scr, k_scr, vt_scr, *, nq, nres):
    blk = ATTN_BLOCK
    first = (pl.program_id(0) == 0) & (pl.program_id(1) == 0) & (pl.program_id(2) == 0)

    @pl.when(first)
    def _build_bias():
        bucket = bucket_ref[...]
        for h in range(HEADS_PER_GROUP):
            acc = jnp.full(bucket.shape, NEG_INF, F32)
            for bkt in range(N_BUCKETS):
                acc = jnp.where(bucket == bkt, relb_ref[bkt, h], acc)
            bias_scr[h // 2, :, (h % 2) * blk:(h % 2 + 1) * blk] = acc

    row = lax.broadcasted_iota(jnp.int32, (2 * blk, blk), 0)
    row_head = row // HEAD_DIM
    keep_first = (row >= blk) | (pl.program_id(2) > 0)
    for r in range(nres):
        k_scr[r, 0:blk, :] = prev_ref[r, :, GROUP_WIDTH:2 * GROUP_WIDTH]
        k_scr[r, blk:, :] = cur_ref[r, :, GROUP_WIDTH:2 * GROUP_WIDTH]
        vt_scr[r, :, 0:blk] = prev_ref[r, :, 2 * GROUP_WIDTH:3 * GROUP_WIDTH].T
        for j in range(nq):
            vt_scr[r, :, (j + 1) * blk:(j + 2) * blk] = cur_ref[r, j * blk:(j + 1) * blk,
                                                                 2 * GROUP_WIDTH:3 * GROUP_WIDTH].T

    npair = HEADS_PER_GROUP // 2
    units = [(r, j, hp) for r in range(nres) for j in range(nq) for hp in range(npair)]
    qts = {}

    def scores(r, j, hp):
        if (r, j) not in qts:
            qts[(r, j)] = cur_ref[r, j * blk:(j + 1) * blk, 0:GROUP_WIDTH].T
        qt = qts[(r, j)]
        k2 = k_scr[r, j * blk:(j + 2) * blk, :]
        qh = jnp.concatenate([jnp.where(row_head == 2 * hp + i, qt, jnp.zeros_like(qt)) for i in range(2)], axis=1)
        return jnp.dot(k2, qh, preferred_element_type=F32)

    def softmax(r, j, hp, s2):
        ps, ms, ls = [], [], []
        for i in range(2):
            lanes = slice(i * blk, (i + 1) * blk)
            s = s2[:, lanes] + bias_scr[hp, :, lanes]
            if j == 0:
                s = jnp.where(keep_first, s, NEG_INF)
            m = jnp.max(s, axis=0, keepdims=True)
            p = jnp.exp(s - m)
            ls.append(jnp.sum(p, axis=0, keepdims=True))
            ps.append(p.astype(BF16))
            ms.append(m)
        return jnp.concatenate(ps, axis=1), jnp.concatenate(ms, axis=1), jnp.concatenate(ls, axis=1)

    def values(r, j, hp, p, m, l):
        vt = vt_scr[r, 2 * hp * HEAD_DIM:2 * (hp + 1) * HEAD_DIM, j * blk:(j + 2) * blk]
        ot = jnp.dot(vt, p, preferred_element_type=F32) * (1.0 / l)
        lse = jnp.broadcast_to(m + jnp.log(l), (HEAD_DIM, 2 * blk))
        return [(ot[i * HEAD_DIM:(i + 1) * HEAD_DIM, i * blk:(i + 1) * blk], lse[:, i * blk:(i + 1) * blk])
                for i in range(2)]

    raw, soft, done = {}, {}, []
    lag_soft, lag_val = ATTN_SKEW
    for step in range(len(units) + lag_val):
        if step < len(units):
            raw[step] = scores(*units[step])
        if lag_soft <= step < len(units) + lag_soft:
            soft[step - lag_soft] = softmax(*units[step - lag_soft], raw.pop(step - lag_soft))
        if step >= lag_val:
            i = step - lag_val
            r, j, hp = units[i]
            done += values(r, j, hp, *soft.pop(i))
            if hp == npair - 1:
                rows = slice(j * blk, (j + 1) * blk)
                out_ref[r, rows, :] = jnp.concatenate([d[0] for d in done], axis=0).T.astype(BF16)
                lse_ref[r, rows, :] = jnp.concatenate([d[1] for d in done], axis=0).T
                done = []


def _t5_bucket(dist):
    max_exact = N_BUCKETS // 2
    d = jnp.maximum(dist, 1).astype(F32)
    large = max_exact + (jnp.log(d / max_exact) / math.log(MAX_DISTANCE / max_exact)
                         * (N_BUCKETS - max_exact)).astype(jnp.int32)
    large = jnp.minimum(large, N_BUCKETS - 1)
    return jnp.where(dist < max_exact, dist, large)


def _bucket_table(window, dilation):
    blk = ATTN_BLOCK
    span = window // dilation
    rel = jnp.arange(blk)[None, :] + blk - jnp.arange(2 * blk)[:, None]
    valid = (rel >= 0) & (rel <= span)
    return jnp.where(valid, _t5_bucket(jnp.maximum(rel, 0) * dilation), -1).astype(jnp.int32)


def _attention_group(qkv, rel_bias_g, window, dilation, mt):
    assert window // dilation == ATTN_BLOCK
    bsz, _, n, w3 = qkv.shape
    rows = mt
    mt = min(rows, n)
    nres = rows // mt
    nq = mt // ATTN_BLOCK
    cur = pl.BlockSpec((None, nres, mt, w3), lambda b, r, i: (b, r, i, 0))
    prev = pl.BlockSpec((None, nres, ATTN_BLOCK, w3), lambda b, r, i: (b, r, jnp.maximum(i * nq - 1, 0), 0))
    ospec = pl.BlockSpec((None, nres, mt, GROUP_WIDTH), lambda b, r, i: (b, r, i, 0))
    return pl.pallas_call(
        functools.partial(_attn_kernel, nq=nq, nres=nres),
        grid=(bsz, dilation // nres, n // mt),
        in_specs=[_const_spec((2 * ATTN_BLOCK, ATTN_BLOCK)),
                  pl.BlockSpec(memory_space=pltpu.SMEM), cur, prev],
        out_specs=[ospec, ospec],
        out_shape=[jax.ShapeDtypeStruct((bsz, dilation, n, GROUP_WIDTH), BF16),
                   jax.ShapeDtypeStruct((bsz, dilation, n, GROUP_WIDTH), F32)],
        scratch_shapes=[pltpu.VMEM((HEADS_PER_GROUP // 2, 2 * ATTN_BLOCK, 2 * ATTN_BLOCK), F32),
                        pltpu.VMEM((nres, mt + ATTN_BLOCK, GROUP_WIDTH), BF16),
                        pltpu.VMEM((nres, GROUP_WIDTH, mt + ATTN_BLOCK), BF16)],
        compiler_params=_params(3),
        name=f"attn_d{dilation}",
    )(_bucket_table(window, dilation), rel_bias_g, qkv, qkv)


def _ssm_prep_kernel(logdt_ref, lr_ref, li_ref, bt_re_ref, bt_im_ref, c_re_ref, c_im_ref, dl_ref,
                     toep_ref, bst_ref, cst_ref, a_ref):
    cs = SSM_CHUNK
    dt = jnp.exp(jnp.full((1, SSM_STATE), logdt_ref[pl.program_id(0)], F32))
    lr, li = lr_ref[...], li_ref[...]
    mag = jnp.exp(lr * dt)
    ab_re, ab_im = mag * jnp.cos(li * dt), mag * jnp.sin(li * dt)
    den = lr * lr + li * li
    nr = ab_re - 1.0
    k_re = (nr * lr + ab_im * li) / den
    k_im = (ab_im * lr - nr * li) / den
    bt_re, bt_im = bt_re_ref[...], bt_im_ref[...]
    bb_re = k_re * bt_re - k_im * bt_im
    bb_im = k_re * bt_im + k_im * bt_re
    j = lax.broadcasted_iota(jnp.int32, (cs + 8, SSM_STATE), 0).astype(F32)
    pmag = jnp.exp(lr * dt * j)
    ang = li * dt * j
    p_re, p_im = pmag * jnp.cos(ang), pmag * jnp.sin(ang)
    c_re, c_im = c_re_ref[...], c_im_ref[...]
    cp_re = [c_re * p_re[i:i + 1] - c_im * p_im[i:i + 1] for i in range(cs + 1)]
    cp_im = [c_re * p_im[i:i + 1] + c_im * p_re[i:i + 1] for i in range(cs + 1)]
    nt = (((1,), (1,)), ((), ()))
    hi = lax.Precision.HIGHEST
    kcat = (lax.dot_general(bb_re, jnp.concatenate(cp_re[:cs], axis=0), nt, precision=hi,
                            preferred_element_type=F32)
            - lax.dot_general(bb_im, jnp.concatenate(cp_im[:cs], axis=0), nt, precision=hi,
                              preferred_element_type=F32))
    lane = lax.broadcasted_iota(jnp.int32, kcat.shape, 1)
    row = lax.broadcasted_iota(jnp.int32, kcat.shape, 0)
    dl = dl_ref[...]
    for s in range(cs):
        off = s * SSM_GROUP
        t_s = kcat if s == 0 else jnp.where(lane >= off, pltpu.roll(kcat, off, 1), 0.0)
        toep_ref[s] = (t_s + jnp.where(lane == off + row, dl, 0.0)).astype(BF16)
        pe_re, pe_im = p_re[cs - 1 - s:cs - s], p_im[cs - 1 - s:cs - s]
        st_re = pe_re * bb_re - pe_im * bb_im
        st_im = pe_re * bb_im + pe_im * bb_re
        bst_ref[s] = jnp.concatenate([st_re, st_im], axis=-1).astype(BF16)
    ro = jnp.concatenate([jnp.concatenate(cp_re[1:], axis=0),
                          -jnp.concatenate(cp_im[1:], axis=0)], axis=-1)
    ro_t = ro.T
    cst_ref[0] = ro_t[:SSM_STATE].astype(BF16)
    cst_ref[1] = ro_t[SSM_STATE:].astype(BF16)
    a_ref[...] = jnp.concatenate([p_re[cs:cs + 1], p_im[cs:cs + 1]], axis=0)


def _ssm_weights(lambda_re, lambda_im, log_dt, b_re, b_im, c_re, c_im, d_skip):
    f32 = F32
    cs, ng, no = SSM_CHUNK, SSM_GROUPS, SSM_GROUPS // SSM_OCT
    grp = lambda *shape: pl.BlockSpec((None,) + shape, lambda g: (g,) + (0,) * len(shape))
    in_oct = lambda *shape: pl.BlockSpec((None, shape[0], None) + shape[1:],
                                         lambda g: (g // SSM_OCT, 0, g % SSM_OCT) + (0,) * (len(shape) - 1))
    toep_c, bst_c, cst_c, a32 = pl.pallas_call(
        _ssm_prep_kernel,
        grid=(ng,),
        in_specs=[pl.BlockSpec(memory_space=pltpu.SMEM), grp(1, SSM_STATE), grp(1, SSM_STATE),
                  grp(SSM_GROUP, SSM_STATE), grp(SSM_GROUP, SSM_STATE),
                  grp(SSM_GROUP, SSM_STATE), grp(SSM_GROUP, SSM_STATE), grp(1, cs * SSM_GROUP)],
        out_specs=[in_oct(cs, SSM_GROUP, cs * SSM_GROUP), in_oct(cs, SSM_GROUP, 2 * SSM_STATE),
                   in_oct(2, SSM_STATE, cs * SSM_GROUP), grp(2, SSM_STATE)],
        out_shape=[jax.ShapeDtypeStruct((no, cs, SSM_OCT, SSM_GROUP, cs * SSM_GROUP), BF16),
                   jax.ShapeDtypeStruct((no, cs, SSM_OCT, SSM_GROUP, 2 * SSM_STATE), BF16),
                   jax.ShapeDtypeStruct((no, 2, SSM_OCT, SSM_STATE, cs * SSM_GROUP), BF16),
                   jax.ShapeDtypeStruct((ng, 2, SSM_STATE), f32)],
        compiler_params=_params(1),
        name="ssm_prep",
    )(log_dt.astype(f32), lambda_re.astype(f32)[:, None, :], lambda_im.astype(f32)[:, None, :],
      b_re.astype(f32).transpose(0, 2, 1), b_im.astype(f32).transpose(0, 2, 1),
      c_re.astype(f32), c_im.astype(f32), jnp.tile(d_skip.astype(f32), (1, cs))[:, None, :])
    avec = a32.reshape(no, SSM_OCT, 2, SSM_STATE).transpose(0, 2, 1, 3).reshape(no, 2, SSM_OCT * SSM_STATE)
    return (toep_c.reshape(no, cs * LANES, cs * SSM_GROUP),
            bst_c.reshape(no, cs * LANES, 2 * SSM_STATE),
            cst_c.reshape(no, 2 * SSM_OCT * SSM_STATE, cs * SSM_GROUP),
            avec)


def _expand_ssm_weights(toep_ref, bstc_ref, cstc_ref, wt_scr, bst_scr, cst_scr):
    tw = 2 * LANES
    sh_g, sh_n = SSM_GROUP.bit_length() - 1, SSM_STATE.bit_length() - 1
    r = lax.broadcasted_iota(jnp.int32, (tw, tw), 0)
    c = lax.broadcasted_iota(jnp.int32, (tw, tw), 1)
    col_grp = (c >> sh_g) & (SSM_OCT - 1)
    row_grp = (r >> sh_g) & (SSM_OCT - 1)
    for tp in range(SSM_CHUNK // 2):
        src_col = (2 * tp + (c >> (sh_g + 3))) * SSM_GROUP + (c & (SSM_GROUP - 1))
        e = jnp.where(r == src_col, 1.0, 0.0).astype(BF16)
        base = tp * (tp + 1) // 2
        for sp in range(tp + 1):
            x = jnp.dot(toep_ref[sp * tw:(sp + 1) * tw, :], e, preferred_element_type=F32)
            wt_scr[base + sp] = jnp.where(row_grp == col_grp, x, 0.0).astype(BF16)
        for j in range(2 * SSM_OCT * SSM_STATE // tw):
            x = jnp.dot(cstc_ref[j * tw:(j + 1) * tw, :], e, preferred_element_type=F32)
            row_grp_n = ((j * tw + r) >> sh_n) & (SSM_OCT - 1)
            cst_scr[j * tw:(j + 1) * tw, tp * tw:(tp + 1) * tw] = jnp.where(row_grp_n == col_grp, x, 0.0).astype(BF16)
    ns = 2 * SSM_OCT * SSM_STATE
    rb = lax.broadcasted_iota(jnp.int32, (2 * SSM_STATE, ns), 0)
    cb = lax.broadcasted_iota(jnp.int32, (2 * SSM_STATE, ns), 1)
    src_col_b = (cb >> (sh_n + 3)) * SSM_STATE + (cb & (SSM_STATE - 1))
    eb = jnp.where(rb == src_col_b, 1.0, 0.0).astype(BF16)
    rr = lax.broadcasted_iota(jnp.int32, (tw, ns), 0)
    cc = lax.broadcasted_iota(jnp.int32, (tw, ns), 1)
    keep = ((rr >> sh_g) & (SSM_OCT - 1)) == ((cc >> sh_n) & (SSM_OCT - 1))
    for j in range(SSM_CHUNK * LANES // tw):
        x = jnp.dot(bstc_ref[j * tw:(j + 1) * tw, :], eb, preferred_element_type=F32)
        bst_scr[j * tw:(j + 1) * tw, :] = jnp.where(keep, x, 0.0).astype(BF16)


def _ssm_kernel(u_ref, toep_ref, bstc_ref, cstc_ref, a_ref, y_ref, wt_ref, bst_ref, cst_ref, zz_scr, hp_scr,
                *, n_chunks, nb):
    @pl.when(pl.program_id(1) == 0)
    def _new_octet():
        _expand_ssm_weights(toep_ref, bstc_ref, cstc_ref, wt_ref, bst_ref, cst_ref)

    rows = nb * n_chunks
    nk = SSM_OCT * SSM_STATE // LANES
    u_t = [u_ref[:, s].reshape(rows, LANES) for s in range(SSM_CHUNK)]
    z = jnp.dot(jnp.concatenate(u_t, axis=-1), bst_ref[...], preferred_element_type=F32)
    pitch = _row_pitch(n_chunks)
    for k in range(2 * nk):
        for b in range(nb):
            zz_scr[k, b * pitch:b * pitch + n_chunks, :] = z[b * n_chunks:(b + 1) * n_chunks, k * LANES:(k + 1) * LANES]
    a_re = [jnp.broadcast_to(a_ref[0:1, k * LANES:(k + 1) * LANES], (nb, LANES)) for k in range(nk)]
    a_im = [jnp.broadcast_to(a_ref[1:2, k * LANES:(k + 1) * LANES], (nb, LANES)) for k in range(nk)]

    def step(c, carry):
        h_re, h_im = carry
        rows_c = pl.ds(c, nb, stride=pitch)
        new_re, new_im = [], []
        for k in range(nk):
            hp_scr[k, rows_c, :] = h_re[k]
            hp_scr[nk + k, rows_c, :] = h_im[k]
            new_re.append(a_re[k] * h_re[k] - a_im[k] * h_im[k] + zz_scr[k, rows_c, :])
            new_im.append(a_re[k] * h_im[k] + a_im[k] * h_re[k] + zz_scr[nk + k, rows_c, :])
        return tuple(new_re), tuple(new_im)

    zero = tuple(jnp.zeros((nb, LANES), F32) for _ in range(nk))
    lax.fori_loop(0, n_chunks, step, (zero, zero), unroll=4)
    hp = jnp.concatenate(
        [jnp.concatenate([hp_scr[k, b * pitch:b * pitch + n_chunks, :] for b in range(nb)], axis=0)
         for k in range(2 * nk)], axis=-1).astype(BF16)
    tw = 2 * LANES
    for tp in range(SSM_CHUNK // 2):
        base = tp * (tp + 1) // 2
        y = jnp.dot(hp, cst_ref[:, tp * tw:(tp + 1) * tw], preferred_element_type=F32)
        y = y + jnp.dot(jnp.concatenate(u_t[:2 * (tp + 1)], axis=-1),
                        wt_ref[base:base + tp + 1].reshape((tp + 1) * tw, tw), preferred_element_type=F32)
        y = jax.nn.gelu(y).astype(BF16)
        y_ref[:, 2 * tp] = y[:, :LANES].reshape(nb, n_chunks, LANES)
        y_ref[:, 2 * tp + 1] = y[:, LANES:].reshape(nb, n_chunks, LANES)


def _ssm(u16, weights, nb):
    toep_c, bst_c, cst_c, avec = weights
    bsz, cs, nc, width = u16.shape
    oct_spec = lambda a: pl.BlockSpec((None,) + a.shape[1:], lambda o, b: (o,) + (0,) * (a.ndim - 1))
    io_spec = pl.BlockSpec((nb, cs, nc, LANES), lambda o, b: (b, 0, 0, o))
    n_state = 2 * SSM_OCT * SSM_STATE
    n_pairs = (cs // 2) * (cs // 2 + 1) // 2
    return pl.pallas_call(
        functools.partial(_ssm_kernel, n_chunks=nc, nb=nb),
        grid=(width // LANES, bsz // nb),
        in_specs=[io_spec, oct_spec(toep_c), oct_spec(bst_c), oct_spec(cst_c), oct_spec(avec)],
        out_specs=io_spec,
        out_shape=jax.ShapeDtypeStruct(u16.shape, BF16),
        scratch_shapes=[pltpu.VMEM((n_pairs, 2 * LANES, 2 * LANES), BF16),
                        pltpu.VMEM((cs * LANES, n_state), BF16),
                        pltpu.VMEM((n_state, cs * LANES), BF16),
                        pltpu.VMEM((n_state // LANES, nb * _row_pitch(nc), LANES), F32),
                        pltpu.VMEM((n_state // LANES, nb * _row_pitch(nc), LANES), F32)],
        compiler_params=_params(2),
        name="ssm",
    )(u16, toep_c, bst_c, cst_c, avec)


def _layer_norm(v, g, b):
    mu = jnp.mean(v, axis=-1, keepdims=True)
    vc = v - mu
    var = jnp.mean(vc * vc, axis=-1, keepdims=True)
    return vc * lax.rsqrt(var + LN_EPS) * g + b


def _merge_kernel(o0_ref, o1_ref, o2_ref, l0_ref, l1_ref, l2_ref, ys_ref, g_ref, x_ref,
                  wglu_ref, wsp_ref, wap_ref, wout_ref, lng_ref, lnb_ref, h_ref, o_scr, l_scr, y_scr):
    tm = x_ref.shape[0]

    def token_order(ref, scr, d):
        if d == 1:
            return ref[0].astype(F32)
        nk = ref.shape[-1] // LANES
        pitch = _row_pitch(d)
        for r in range(d):
            v = ref[r].astype(F32)
            for k in range(nk):
                scr[k, pl.ds(r, tm // d, stride=pitch), :] = v[:, k * LANES:(k + 1) * LANES]
        if pitch == d:
            return jnp.concatenate([scr[k, 0:tm, :] for k in range(nk)], axis=-1)
        return jnp.concatenate(
            [jnp.concatenate([scr[k, g * pitch:g * pitch + d, :] for g in range(tm // d)], axis=0)
             for k in range(nk)], axis=-1)

    ls, outs = [], []
    for gi, (o_ref, l_ref) in enumerate(((o0_ref, l0_ref), (o1_ref, l1_ref), (o2_ref, l2_ref))):
        d = DILATION_PATTERNS[gi][1]
        outs.append(token_order(o_ref, o_scr.at[gi], d))
        ls.append(token_order(l_ref, l_scr.at[gi], d))
    mx = jnp.maximum(jnp.maximum(ls[0], ls[1]), ls[2])
    es = [jnp.exp(l - mx) for l in ls]
    num = es[0] * outs[0] + es[1] * outs[1] + es[2] * outs[2]
    y_attn = (num / (es[0] + es[1] + es[2])).astype(BF16)
    ys = token_order(ys_ref, y_scr, SSM_CHUNK).astype(BF16)
    cw = GROUP_WIDTH
    y_ssm = []
    for c in range(SSM_WIDTH // cw):
        a = jnp.dot(ys, wglu_ref[:, c * cw:(c + 1) * cw], preferred_element_type=F32)
        b = jnp.dot(ys, wglu_ref[:, SSM_WIDTH + c * cw:SSM_WIDTH + (c + 1) * cw], preferred_element_type=F32)
        y_ssm.append((a * _sigmoid(b)).astype(BF16))
    y_ssm = jnp.concatenate(y_ssm, axis=-1)
    gated = []
    for c in range(D_MODEL // cw):
        pa = jnp.dot(y_ssm, wsp_ref[:, c * cw:(c + 1) * cw], preferred_element_type=F32)
        pb = jnp.dot(y_attn, wap_ref[:, c * cw:(c + 1) * cw], preferred_element_type=F32)
        g_ssm = g_ref[:, c * cw:(c + 1) * cw].astype(F32)
        g_attn = g_ref[:, D_MODEL + c * cw:D_MODEL + (c + 1) * cw].astype(F32)
        gated.append((g_ssm * pa + g_attn * pb).astype(BF16))
    gated = jnp.concatenate(gated, axis=-1)
    mix = jnp.dot(gated, wout_ref[...], preferred_element_type=F32)
    h_ref[...] = _layer_norm(ALPHA * x_ref[...] + mix, lng_ref[...], lnb_ref[...])


def _merge(outs, lses, ys, gates, x2, w_glu, w_sp, w_ap, w_out, ln_g, ln_b, seqlen, tm):
    t = x2.shape[0]
    tiles = seqlen // tm
    stage_rows = max(tm // d * _row_pitch(d) for d in [d for _, d in DILATION_PATTERNS] + [SSM_CHUNK])
    row = lambda w: pl.BlockSpec((tm, w), lambda i: (i, 0))
    res_spec = lambda d, w: pl.BlockSpec((None, d, tm // d, w), lambda i: (i // tiles, 0, i % tiles, 0))
    res_specs = [res_spec(d, GROUP_WIDTH) for _, d in DILATION_PATTERNS]
    return pl.pallas_call(
        _merge_kernel,
        grid=(t // tm,),
        in_specs=res_specs + res_specs + [res_spec(SSM_CHUNK, SSM_WIDTH), row(2 * D_MODEL), row(D_MODEL),
                  _const_spec(w_glu.shape), _const_spec(w_sp.shape), _const_spec(w_ap.shape),
                  _const_spec(w_out.shape), _const_spec((1, D_MODEL)), _const_spec((1, D_MODEL))],
        out_specs=row(D_MODEL),
        out_shape=jax.ShapeDtypeStruct((t, D_MODEL), F32),
        scratch_shapes=[pltpu.VMEM((N_GROUPS, GROUP_WIDTH // LANES, stage_rows, LANES), F32),
                        pltpu.VMEM((N_GROUPS, GROUP_WIDTH // LANES, stage_rows, LANES), F32),
                        pltpu.VMEM((SSM_WIDTH // LANES, stage_rows, LANES), F32)],
        compiler_params=_params(1),
        name="merge_ln1",
    )(*outs, *lses, ys, gates, x2, w_glu, w_sp, w_ap, w_out, ln_g, ln_b)


FF_CHUNK = 1024


def _ffn_kernel(h_ref, wup_ref, wdn_ref, lng_ref, lnb_ref, o_ref):
    h = h_ref[...]
    hb = h.astype(BF16)
    acts = []
    for c in range(D_FF // FF_CHUNK):
        lo, hi = c * FF_CHUNK, (c + 1) * FF_CHUNK
        up = jnp.dot(hb, wup_ref[:, lo:hi], preferred_element_type=F32)
        acts.append(jnp.square(jnp.maximum(up, 0.0)).astype(BF16))
    ff = jnp.dot(jnp.concatenate(acts, axis=-1), wdn_ref[...], preferred_element_type=F32)
    o_ref[...] = _layer_norm(ALPHA * h + ff, lng_ref[...], lnb_ref[...])


def _ffn(h1, w_up, w_down, ln_g, ln_b, tm):
    t = h1.shape[0]
    row = pl.BlockSpec((tm, D_MODEL), lambda i: (i, 0))
    return pl.pallas_call(
        _ffn_kernel,
        grid=(t // tm,),
        in_specs=[row, _const_spec(w_up.shape), _const_spec(w_down.shape),
                  _const_spec((1, D_MODEL)), _const_spec((1, D_MODEL))],
        out_specs=row,
        out_shape=jax.ShapeDtypeStruct((t, D_MODEL), F32),
        compiler_params=_params(1),
        name="ffn_ln2",
    )(h1, w_up, w_down, ln_g, ln_b)


def _permute_w_in(w):
    aw = ATTN_WIDTH
    cols = []
    for gi in range(N_GROUPS):
        lo, hi = gi * GROUP_WIDTH, (gi + 1) * GROUP_WIDTH
        cols += [w[:, lo:hi] * (HEAD_DIM ** -0.5), w[:, aw + lo:aw + hi], w[:, 2 * aw + lo:2 * aw + hi]]
    cols.append(w[:, 3 * aw:])
    return jnp.concatenate(cols, axis=1).astype(BF16)


def _layer(h2, bsz, seqlen, l, w_in, b_gate, lambda_re, lambda_im, log_dt, ssm_b_re, ssm_b_im,
           ssm_c_re, ssm_c_im, ssm_d, w_glu, w_ssm_proj, rel_bias, w_attn_proj, w_out,
           ln1_g, ln1_b, w_up, w_down, ln2_g, ln2_b, tm=1024, attn_mt=1024, ffn_tm=1024):
    qkv0, qkv1, qkv2, u, gates = _in_proj(h2, _permute_w_in(w_in[l]), b_gate[l][None, :], bsz, seqlen, tm)
    outs, lses = [], []
    for gi, ((window, dilation), qkv) in enumerate(zip(DILATION_PATTERNS, (qkv0, qkv1, qkv2))):
        rb = rel_bias[:, gi * HEADS_PER_GROUP:(gi + 1) * HEADS_PER_GROUP].astype(F32)
        o, s = _attention_group(qkv, rb, window, dilation, attn_mt)
        outs.append(o)
        lses.append(s)
    ssm_w = _ssm_weights(lambda_re[l], lambda_im[l], log_dt[l], ssm_b_re[l], ssm_b_im[l],
                         ssm_c_re[l], ssm_c_im[l], ssm_d[l])
    ys = _ssm(u, ssm_w, nb=4)
    h1 = _merge(outs, lses, ys, gates, h2, w_glu[l].astype(BF16), w_ssm_proj[l].astype(BF16),
                w_attn_proj[l].astype(BF16), w_out[l].astype(BF16),
                ln1_g[l][None, :], ln1_b[l][None, :], seqlen, tm)
    return _ffn(h1, w_up[l].astype(BF16), w_down[l].astype(BF16), ln2_g[l][None, :], ln2_b[l][None, :], ffn_tm)


def kernel(x, w_in, b_gate, lambda_re, lambda_im, log_dt, ssm_b_re, ssm_b_im, ssm_c_re, ssm_c_im,
           ssm_d, w_glu, w_ssm_proj, rel_bias, w_attn_proj, w_out, ln1_g, ln1_b, w_up, w_down,
           ln2_g, ln2_b):
    bsz, seqlen, d = x.shape
    h = x.reshape(bsz * seqlen, d)
    for l in range(w_in.shape[0]):
        h = _layer(h, bsz, seqlen, l, w_in, b_gate, lambda_re, lambda_im, log_dt, ssm_b_re, ssm_b_im,
                   ssm_c_re, ssm_c_im, ssm_d, w_glu, w_ssm_proj, rel_bias, w_attn_proj, w_out,
                   ln1_g, ln1_b, w_up, w_down, ln2_g, ln2_b)
    return h.reshape(bsz, seqlen, d)
```

```python
import functools
import math

import jax
import jax.numpy as jnp
from jax import lax
from jax.experimental import pallas as pl
from jax.experimental.pallas import tpu as pltpu

F32 = jnp.float32
BF16 = jnp.bfloat16

D_MODEL = 1024
HEAD_DIM = 64
HEADS_PER_GROUP = 4
GROUP_WIDTH = HEADS_PER_GROUP * HEAD_DIM
DILATION_PATTERNS = ((128, 1), (512, 4), (2048, 16))
N_GROUPS = len(DILATION_PATTERNS)
ATTN_WIDTH = N_GROUPS * GROUP_WIDTH
N_BUCKETS = 32
MAX_DISTANCE = 2048
SSM_WIDTH = 512
SSM_GROUP = 16
SSM_GROUPS = 32
SSM_STATE = 64
D_FF = 4 * D_MODEL
DEPTH = 1
ALPHA = (2.0 * DEPTH) ** 0.25
LN_EPS = 1e-5
NEG_INF = -1e30

ATTN_BLOCK = 128
ATTN_SKEW = (3, 5)
SSM_CHUNK = 16
SSM_OCT = 8
LANES = 128
SUBLANES = 8
VMEM_LIMIT_BYTES = 56 * 1024 * 1024


def _params(n_axes):
    return pltpu.CompilerParams(dimension_semantics=("arbitrary",) * n_axes,
                                vmem_limit_bytes=VMEM_LIMIT_BYTES)


def _sigmoid(z):
    return 0.5 * jnp.tanh(0.5 * z) + 0.5


def _row_pitch(stride):
    return stride + SUBLANES if stride % (2 * SUBLANES) == 0 else stride


def _const_spec(shape):
    nd = len(shape)
    return pl.BlockSpec(shape, lambda *_: (0,) * nd, pipeline_mode=pl.Buffered(1))


def _in_proj_kernel(x_ref, w_ref, bg_ref, qkv0_ref, qkv1_ref, qkv2_ref, u_ref, g_ref, scr):
    xb = x_ref[...].astype(BF16)
    tm = xb.shape[0]

    def mm(lo, hi):
        return jnp.dot(xb, w_ref[:, lo:hi], preferred_element_type=F32)

    def emit(ref, res, d, col0):
        width = res.shape[1]
        if d == 1:
            ref[0, :, col0:col0 + width] = res.astype(BF16)
            return
        slot0 = next_slot[0]
        next_slot[0] += width // LANES
        pitch = _row_pitch(d)
        for k in range(width // LANES):
            tile = res[:, k * LANES:(k + 1) * LANES]
            if pitch == d:
                scr[slot0 + k, 0:tm, :] = tile
            else:
                for g in range(tm // d):
                    scr[slot0 + k, g * pitch:g * pitch + d, :] = tile[g * d:(g + 1) * d]
        for r in range(d):
            for k in range(width // LANES):
                col = col0 + k * LANES
                ref[r, :, col:col + LANES] = scr[slot0 + k, pl.ds(r, tm // d, stride=pitch), :].astype(BF16)

    next_slot = [0]

    gw3 = 3 * GROUP_WIDTH
    for gi, ref in enumerate((qkv0_ref, qkv1_ref, qkv2_ref)):
        for c in range(3):
            lo = gi * gw3 + c * GROUP_WIDTH
            emit(ref, mm(lo, lo + GROUP_WIDTH), DILATION_PATTERNS[gi][1], c * GROUP_WIDTH)
    base = N_GROUPS * gw3
    emit(u_ref, mm(base, base + SSM_WIDTH), SSM_CHUNK, 0)
    base += SSM_WIDTH
    for c in range(2 * D_MODEL // GROUP_WIDTH):
        lo, hi = c * GROUP_WIDTH, (c + 1) * GROUP_WIDTH
        z = mm(base + lo, base + hi) + bg_ref[:, lo:hi]
        g_ref[:, lo:hi] = _sigmoid(z).astype(BF16)


def _in_proj(x2, w_perm, b_gate, bsz, seqlen, tm):
    t = x2.shape[0]
    n_in = w_perm.shape[1]
    tiles = seqlen // tm
    row = lambda w: pl.BlockSpec((tm, w), lambda i: (i, 0))
    w3 = 3 * GROUP_WIDTH
    dils = [d for _, d in DILATION_PATTERNS]
    n_stage = (sum(d > 1 for d in dils) * w3 + SSM_WIDTH) // LANES
    res_spec = lambda d, w:pl.BlockSpec((None, d, tm // d, w), lambda i: (i // tiles, 0, i % tiles, 0))
    return pl.pallas_call(
        _in_proj_kernel,
        grid=(t // tm,),
        in_specs=[row(D_MODEL), _const_spec((D_MODEL, n_in)), _const_spec((1, 2 * D_MODEL))],
        out_specs=[res_spec(d, w3) for d in dils] + [res_spec(SSM_CHUNK, SSM_WIDTH), row(2 * D_MODEL)],
        out_shape=[jax.ShapeDtypeStruct((bsz, d, seqlen // d, w3), BF16) for d in dils]
        + [jax.ShapeDtypeStruct((bsz, SSM_CHUNK, seqlen // SSM_CHUNK, SSM_WIDTH), BF16),
           jax.ShapeDtypeStruct((t, 2 * D_MODEL), BF16)],
        scratch_shapes=[pltpu.VMEM((n_stage, max(tm // d * _row_pitch(d) for d in dils + [SSM_CHUNK]), LANES), F32)],
        compiler_params=_params(1),
        name="in_proj",
    )(x2, w_perm, b_gate)


def _attn_kernel(bucket_ref, relb_ref, cur_ref, prev_ref, out_ref, lse_ref, bias_scr, k_scr, vt_scr, *, nq, nres):
    blk = ATTN_BLOCK
    first = (pl.program_id(0) == 0) & (pl.program_id(1) == 0) & (pl.program_id(2) == 0)

    @pl.when(first)
    def _build_bias():
        bucket = bucket_ref[...]
        for h in range(HEADS_PER_GROUP):
            acc = jnp.full(bucket.shape, NEG_INF, F32)
            for bkt in range(N_BUCKETS):
                acc = jnp.where(bucket == bkt, relb_ref[bkt, h], acc)
            bias_scr[h // 2, :, (h % 2) * blk:(h % 2 + 1) * blk] = acc

    row = lax.broadcasted_iota(jnp.int32, (2 * blk, blk), 0)
    row_head = row // HEAD_DIM
    keep_first = (row >= blk) | (pl.program_id(2) > 0)
    for r in range(nres):
        k_scr[r, 0:blk, :] = prev_ref[r, :, GROUP_WIDTH:2 * GROUP_WIDTH]
        k_scr[r, blk:, :] = cur_ref[r, :, GROUP_WIDTH:2 * GROUP_WIDTH]
        vt_scr[r, :, 0:blk] = prev_ref[r, :, 2 * GROUP_WIDTH:3 * GROUP_WIDTH].T
        for j in range(nq):
            vt_scr[r, :, (j + 1) * blk:(j + 2) * blk] = cur_ref[r, j * blk:(j + 1) * blk,
                                                                 2 * GROUP_WIDTH:3 * GROUP_WIDTH].T

    npair = HEADS_PER_GROUP // 2
    units = [(r, j, hp) for r in range(nres) for j in range(nq) for hp in range(npair)]
    qts = {}

    def scores(r, j, hp):
        if (r, j) not in qts:
            qts[(r, j)] = cur_ref[r, j * blk:(j + 1) * blk, 0:GROUP_WIDTH].T
        qt = qts[(r, j)]
        k2 = k_scr[r, j * blk:(j + 2) * blk, :]
        qh = jnp.concatenate([jnp.where(row_head == 2 * hp + i, qt, jnp.zeros_like(qt)) for i in range(2)], axis=1)
        return jnp.dot(k2, qh, preferred_element_type=F32)

    def softmax(r, j, hp, s2):
        ps, ms, ls = [], [], []
        for i in range(2):
            lanes = slice(i * blk, (i + 1) * blk)
            s = s2[:, lanes] + bias_scr[hp, :, lanes]
            if j == 0:
                s = jnp.where(keep_first, s, NEG_INF)
            m = jnp.max(s, axis=0, keepdims=True)
            p = jnp.exp(s - m)
            ls.append(jnp.sum(p, axis=0, keepdims=True))
            ps.append(p.astype(BF16))
            ms.append(m)
        return jnp.concatenate(ps, axis=1), jnp.concatenate(ms, axis=1), jnp.concatenate(ls, axis=1)

    def values(r, j, hp, p, m, l):
        vt = vt_scr[r, 2 * hp * HEAD_DIM:2 * (hp + 1) * HEAD_DIM, j * blk:(j + 2) * blk]
        ot = jnp.dot(vt, p, preferred_element_type=F32) * (1.0 / l)
        lse = jnp.broadcast_to(m + jnp.log(l), (HEAD_DIM, 2 * blk))
        return [(ot[i * HEAD_DIM:(i + 1) * HEAD_DIM, i * blk:(i + 1) * blk], lse[:, i * blk:(i + 1) * blk])
                for i in range(2)]

    raw, soft, done = {}, {}, []
    lag_soft, lag_val = ATTN_SKEW
    for step in range(len(units) + lag_val):
        if step < len(units):
            raw[step] = scores(*units[step])
        if lag_soft <= step < len(units) + lag_soft:
            soft[step - lag_soft] = softmax(*units[step - lag_soft], raw.pop(step - lag_soft))
        if step >= lag_val:
            i = step - lag_val
            r, j, hp = units[i]
            done += values(r, j, hp, *soft.pop(i))
            if hp == npair - 1:
                rows = slice(j * blk, (j + 1) * blk)
                out_ref[r, rows, :] = jnp.concatenate([d[0] for d in done], axis=0).T.astype(BF16)
                lse_ref[r, rows, :] = jnp.concatenate([d[1] for d in done], axis=0).T
                done = []


def _t5_bucket(dist):
    max_exact = N_BUCKETS // 2
    n_log = N_BUCKETS - max_exact
    thresholds = [math.ceil(max_exact * (MAX_DISTANCE / max_exact) ** (k / n_log)) for k in range(1, n_log)]
    large = max_exact + sum((dist >= t).astype(jnp.int32) for t in thresholds)
    return jnp.where(dist < max_exact, dist, large)


def _bucket_table(window, dilation):
    blk = ATTN_BLOCK
    span = window // dilation
    rel = jnp.arange(blk, dtype=jnp.int32)[None, :] + blk - jnp.arange(2 * blk, dtype=jnp.int32)[:, None]
    valid = (rel >= 0) & (rel <= span)
    return jnp.where(valid, _t5_bucket(jnp.maximum(rel, 0) * dilation), -1)


def _attention_group(qkv, rel_bias_g, window, dilation, mt):
    assert window // dilation == ATTN_BLOCK
    bsz, _, n, w3 = qkv.shape
    rows = mt
    mt = min(rows, n)
    nres = rows // mt
    nq = mt // ATTN_BLOCK
    cur = pl.BlockSpec((None, nres, mt, w3), lambda b, r, i: (b, r, i, 0))
    prev = pl.BlockSpec((None, nres, ATTN_BLOCK, w3), lambda b, r, i: (b, r, jnp.maximum(i * nq - 1, 0), 0))
    ospec = pl.BlockSpec((None, nres, mt, GROUP_WIDTH), lambda b, r, i: (b, r, i, 0))
    return pl.pallas_call(
        functools.partial(_attn_kernel, nq=nq, nres=nres),
        grid=(bsz, dilation // nres, n // mt),
        in_specs=[_const_spec((2 * ATTN_BLOCK, ATTN_BLOCK)),
                  pl.BlockSpec(memory_space=pltpu.SMEM), cur, prev],
        out_specs=[ospec, ospec],
        out_shape=[jax.ShapeDtypeStruct((bsz, dilation, n, GROUP_WIDTH), BF16),
                   jax.ShapeDtypeStruct((bsz, dilation, n, GROUP_WIDTH), F32)],
        scratch_shapes=[pltpu.VMEM((HEADS_PER_GROUP // 2, 2 * ATTN_BLOCK, 2 * ATTN_BLOCK), F32),
                        pltpu.VMEM((nres, mt + ATTN_BLOCK, GROUP_WIDTH), BF16),
                        pltpu.VMEM((nres, GROUP_WIDTH, mt + ATTN_BLOCK), BF16)],
        compiler_params=_params(3),
        name=f"attn_d{dilation}",
    )(_bucket_table(window, dilation), rel_bias_g, qkv, qkv)


def _ssm_prep_kernel(logdt_ref, lr_ref, li_ref, bt_re_ref, bt_im_ref, c_re_ref, c_im_ref, dl_ref,
                     toep_ref, bst_ref, cst_ref, a_ref):
    cs = SSM_CHUNK
    dt = jnp.exp(jnp.full((1, SSM_STATE), logdt_ref[pl.program_id(0)], F32))
    lr, li = lr_ref[...], li_ref[...]
    mag = jnp.exp(lr * dt)
    ab_re, ab_im = mag * jnp.cos(li * dt), mag * jnp.sin(li * dt)
    den = lr * lr + li * li
    nr = ab_re - 1.0
    k_re = (nr * lr + ab_im * li) / den
    k_im = (ab_im * lr - nr * li) / den
    bt_re, bt_im = bt_re_ref[...], bt_im_ref[...]
    bb_re = k_re * bt_re - k_im * bt_im
    bb_im = k_re * bt_im + k_im * bt_re
    j = lax.broadcasted_iota(jnp.int32, (cs + 8, SSM_STATE), 0).astype(F32)
    pmag = jnp.exp(lr * dt * j)
    ang = li * dt * j
    p_re, p_im = pmag * jnp.cos(ang), pmag * jnp.sin(ang)
    c_re, c_im = c_re_ref[...], c_im_ref[...]
    cp_re = [c_re * p_re[i:i + 1] - c_im * p_im[i:i + 1] for i in range(cs + 1)]
    cp_im = [c_re * p_im[i:i + 1] + c_im * p_re[i:i + 1] for i in range(cs + 1)]
    nt = (((1,), (1,)), ((), ()))
    hi = lax.Precision.HIGHEST
    kcat = (lax.dot_general(bb_re, jnp.concatenate(cp_re[:cs], axis=0), nt, precision=hi,
                            preferred_element_type=F32)
            - lax.dot_general(bb_im, jnp.concatenate(cp_im[:cs], axis=0), nt, precision=hi,
                              preferred_element_type=F32))
    lane = lax.broadcasted_iota(jnp.int32, kcat.shape, 1)
    row = lax.broadcasted_iota(jnp.int32, kcat.shape, 0)
    dl = dl_ref[...]
    for s in range(cs):
        off = s * SSM_GROUP
        t_s = kcat if s == 0 else jnp.where(lane >= off, pltpu.roll(kcat, off, 1), 0.0)
        toep_ref[s] = (t_s + jnp.where(lane == off + row, dl, 0.0)).astype(BF16)
        pe_re, pe_im = p_re[cs - 1 - s:cs - s], p_im[cs - 1 - s:cs - s]
        st_re = pe_re * bb_re - pe_im * bb_im
        st_im = pe_re * bb_im + pe_im * bb_re
        bst_ref[s] = jnp.concatenate([st_re, st_im], axis=-1).astype(BF16)
    ro = jnp.concatenate([jnp.concatenate(cp_re[1:], axis=0),
                          -jnp.concatenate(cp_im[1:], axis=0)], axis=-1)
    ro_t = ro.T
    cst_ref[0] = ro_t[:SSM_STATE].astype(BF16)
    cst_ref[1] = ro_t[SSM_STATE:].astype(BF16)
    a_ref[...] = jnp.concatenate([p_re[cs:cs + 1], p_im[cs:cs + 1]], axis=0)


def _ssm_weights(lambda_re, lambda_im, log_dt, b_re, b_im, c_re, c_im, d_skip):
    f32 = F32
    cs, ng, no = SSM_CHUNK, SSM_GROUPS, SSM_GROUPS // SSM_OCT
    grp = lambda *shape: pl.BlockSpec((None,) + shape, lambda g: (g,) + (0,) * len(shape))
    in_oct = lambda *shape: pl.BlockSpec((None, shape[0], None) + shape[1:],
                                         lambda g: (g // SSM_OCT, 0, g % SSM_OCT) + (0,) * (len(shape) - 1))
    toep_c, bst_c, cst_c, a32 = pl.pallas_call(
        _ssm_prep_kernel,
        grid=(ng,),
        in_specs=[pl.BlockSpec(memory_space=pltpu.SMEM), grp(1, SSM_STATE), grp(1, SSM_STATE),
                  grp(SSM_GROUP, SSM_STATE), grp(SSM_GROUP, SSM_STATE),
                  grp(SSM_GROUP, SSM_STATE), grp(SSM_GROUP, SSM_STATE), grp(1, cs * SSM_GROUP)],
        out_specs=[in_oct(cs, SSM_GROUP, cs * SSM_GROUP), in_oct(cs, SSM_GROUP, 2 * SSM_STATE),
                   in_oct(2, SSM_STATE, cs * SSM_GROUP), grp(2, SSM_STATE)],
        out_shape=[jax.ShapeDtypeStruct((no, cs, SSM_OCT, SSM_GROUP, cs * SSM_GROUP), BF16),
                   jax.ShapeDtypeStruct((no, cs, SSM_OCT, SSM_GROUP, 2 * SSM_STATE), BF16),
                   jax.ShapeDtypeStruct((no, 2, SSM_OCT, SSM_STATE, cs * SSM_GROUP), BF16),
                   jax.ShapeDtypeStruct((ng, 2, SSM_STATE), f32)],
        compiler_params=_params(1),
        name="ssm_prep",
    )(log_dt.astype(f32), lambda_re.astype(f32)[:, None, :], lambda_im.astype(f32)[:, None, :],
      b_re.astype(f32).transpose(0, 2, 1), b_im.astype(f32).transpose(0, 2, 1),
      c_re.astype(f32), c_im.astype(f32), jnp.tile(d_skip.astype(f32), (1, cs))[:, None, :])
    avec = a32.reshape(no, SSM_OCT, 2, SSM_STATE).transpose(0, 2, 1, 3).reshape(no, 2, SSM_OCT * SSM_STATE)
    return (toep_c.reshape(no, cs * LANES, cs * SSM_GROUP),
            bst_c.reshape(no, cs * LANES, 2 * SSM_STATE),
            cst_c.reshape(no, 2 * SSM_OCT * SSM_STATE, cs * SSM_GROUP),
            avec)


def _expand_ssm_weights(toep_ref, bstc_ref, cstc_ref, wt_scr, bst_scr, cst_scr):
    tw = 2 * LANES
    sh_g, sh_n = SSM_GROUP.bit_length() - 1, SSM_STATE.bit_length() - 1
    r = lax.broadcasted_iota(jnp.int32, (tw, tw), 0)
    c = lax.broadcasted_iota(jnp.int32, (tw, tw), 1)
    col_grp = (c >> sh_g) & (SSM_OCT - 1)
    row_grp = (r >> sh_g) & (SSM_OCT - 1)
    for tp in range(SSM_CHUNK // 2):
        src_col = (2 * tp + (c >> (sh_g + 3))) * SSM_GROUP + (c & (SSM_GROUP - 1))
        e = jnp.where(r == src_col, 1.0, 0.0).astype(BF16)
        base = tp * (tp + 1) // 2
        for sp in range(tp + 1):
            x = jnp.dot(toep_ref[sp * tw:(sp + 1) * tw, :], e, preferred_element_type=F32)
            wt_scr[base + sp] = jnp.where(row_grp == col_grp, x, 0.0).astype(BF16)
        for j in range(2 * SSM_OCT * SSM_STATE // tw):
            x = jnp.dot(cstc_ref[j * tw:(j + 1) * tw, :], e, preferred_element_type=F32)
            row_grp_n = ((j * tw + r) >> sh_n) & (SSM_OCT - 1)
            cst_scr[j * tw:(j + 1) * tw, tp * tw:(tp + 1) * tw] = jnp.where(row_grp_n == col_grp, x, 0.0).astype(BF16)
    ns = 2 * SSM_OCT * SSM_STATE
    rb = lax.broadcasted_iota(jnp.int32, (2 * SSM_STATE, ns), 0)
    cb = lax.broadcasted_iota(jnp.int32, (2 * SSM_STATE, ns), 1)
    src_col_b = (cb >> (sh_n + 3)) * SSM_STATE + (cb & (SSM_STATE - 1))
    eb = jnp.where(rb == src_col_b, 1.0, 0.0).astype(BF16)
    rr = lax.broadcasted_iota(jnp.int32, (tw, ns), 0)
    cc = lax.broadcasted_iota(jnp.int32, (tw, ns), 1)
    keep = ((rr >> sh_g) & (SSM_OCT - 1)) == ((cc >> sh_n) & (SSM_OCT - 1))
    for j in range(SSM_CHUNK * LANES // tw):
        x = jnp.dot(bstc_ref[j * tw:(j + 1) * tw, :], eb, preferred_element_type=F32)
        bst_scr[j * tw:(j + 1) * tw, :] = jnp.where(keep, x, 0.0).astype(BF16)


def _ssm_kernel(u_ref, toep_ref, bstc_ref, cstc_ref, a_ref, y_ref, wt_ref, bst_ref, cst_ref, zz_scr, hp_scr,
                *, n_chunks, nb):
    @pl.when(pl.program_id(1) == 0)
    def _new_octet():
        _expand_ssm_weights(toep_ref, bstc_ref, cstc_ref, wt_ref, bst_ref, cst_ref)

    rows = nb * n_chunks
    nk = SSM_OCT * SSM_STATE // LANES
    u_t = [u_ref[:, s].reshape(rows, LANES) for s in range(SSM_CHUNK)]
    z = jnp.dot(jnp.concatenate(u_t, axis=-1), bst_ref[...], preferred_element_type=F32)
    pitch = _row_pitch(n_chunks)
    for k in range(2 * nk):
        for b in range(nb):
            zz_scr[k, b * pitch:b * pitch + n_chunks, :] = z[b * n_chunks:(b + 1) * n_chunks, k * LANES:(k + 1) * LANES]
    a_re = [jnp.broadcast_to(a_ref[0:1, k * LANES:(k + 1) * LANES], (nb, LANES)) for k in range(nk)]
    a_im = [jnp.broadcast_to(a_ref[1:2, k * LANES:(k + 1) * LANES], (nb, LANES)) for k in range(nk)]

    def step(c, carry):
        h_re, h_im = carry
        rows_c = pl.ds(c, nb, stride=pitch)
        new_re, new_im = [], []
        for k in range(nk):
            hp_scr[k, rows_c, :] = h_re[k]
            hp_scr[nk + k, rows_c, :] = h_im[k]
            new_re.append(a_re[k] * h_re[k] - a_im[k] * h_im[k] + zz_scr[k, rows_c, :])
            new_im.append(a_re[k] * h_im[k] + a_im[k] * h_re[k] + zz_scr[nk + k, rows_c, :])
        return tuple(new_re), tuple(new_im)

    zero = tuple(jnp.zeros((nb, LANES), F32) for _ in range(nk))
    lax.fori_loop(0, n_chunks, step, (zero, zero), unroll=4)
    hp = jnp.concatenate(
        [jnp.concatenate([hp_scr[k, b * pitch:b * pitch + n_chunks, :] for b in range(nb)], axis=0)
         for k in range(2 * nk)], axis=-1).astype(BF16)
    tw = 2 * LANES
    for tp in range(SSM_CHUNK // 2):
        base = tp * (tp + 1) // 2
        y = jnp.dot(hp, cst_ref[:, tp * tw:(tp + 1) * tw], preferred_element_type=F32)
        y = y + jnp.dot(jnp.concatenate(u_t[:2 * (tp + 1)], axis=-1),
                        wt_ref[base:base + tp + 1].reshape((tp + 1) * tw, tw), preferred_element_type=F32)
        y = jax.nn.gelu(y).astype(BF16)
        y_ref[:, 2 * tp] = y[:, :LANES].reshape(nb, n_chunks, LANES)
        y_ref[:, 2 * tp + 1] = y[:, LANES:].reshape(nb, n_chunks, LANES)


def _ssm(u16, weights, nb):
    toep_c, bst_c, cst_c, avec = weights
    bsz, cs, nc, width = u16.shape
    oct_spec = lambda a: pl.BlockSpec((None,) + a.shape[1:], lambda o, b: (o,) + (0,) * (a.ndim - 1))
    io_spec = pl.BlockSpec((nb, cs, nc, LANES), lambda o, b: (b, 0, 0, o))
    n_state = 2 * SSM_OCT * SSM_STATE
    n_pairs = (cs // 2) * (cs // 2 + 1) // 2
    return pl.pallas_call(
        functools.partial(_ssm_kernel, n_chunks=nc, nb=nb),
        grid=(width // LANES, bsz // nb),
        in_specs=[io_spec, oct_spec(toep_c), oct_spec(bst_c), oct_spec(cst_c), oct_spec(avec)],
        out_specs=io_spec,
        out_shape=jax.ShapeDtypeStruct(u16.shape, BF16),
        scratch_shapes=[pltpu.VMEM((n_pairs, 2 * LANES, 2 * LANES), BF16),
                        pltpu.VMEM((cs * LANES, n_state), BF16),
                        pltpu.VMEM((n_state, cs * LANES), BF16),
                        pltpu.VMEM((n_state // LANES, nb * _row_pitch(nc), LANES), F32),
                        pltpu.VMEM((n_state // LANES, nb * _row_pitch(nc), LANES), F32)],
        compiler_params=_params(2),
        name="ssm",
    )(u16, toep_c, bst_c, cst_c, avec)


def _layer_norm(v, g, b):
    mu = jnp.mean(v, axis=-1, keepdims=True)
    vc = v - mu
    var = jnp.mean(vc * vc, axis=-1, keepdims=True)
    return vc * lax.rsqrt(var + LN_EPS) * g + b


def _merge_kernel(o0_ref, o1_ref, o2_ref, l0_ref, l1_ref, l2_ref, ys_ref, g_ref, x_ref,
                  wglu_ref, wsp_ref, wap_ref, wout_ref, lng_ref, lnb_ref, h_ref, o_scr, l_scr, y_scr):
    tm = x_ref.shape[0]

    def token_order(ref, scr, d):
        if d == 1:
            return ref[0].astype(F32)
        nk = ref.shape[-1] // LANES
        pitch = _row_pitch(d)
        for r in range(d):
            v = ref[r].astype(F32)
            for k in range(nk):
                scr[k, pl.ds(r, tm // d, stride=pitch), :] = v[:, k * LANES:(k + 1) * LANES]
        if pitch == d:
            return jnp.concatenate([scr[k, 0:tm, :] for k in range(nk)], axis=-1)
        return jnp.concatenate(
            [jnp.concatenate([scr[k, g * pitch:g * pitch + d, :] for g in range(tm // d)], axis=0)
             for k in range(nk)], axis=-1)

    ls, outs = [], []
    for gi, (o_ref, l_ref) in enumerate(((o0_ref, l0_ref), (o1_ref, l1_ref), (o2_ref, l2_ref))):
        d = DILATION_PATTERNS[gi][1]
        outs.append(token_order(o_ref, o_scr.at[gi], d))
        ls.append(token_order(l_ref, l_scr.at[gi], d))
    mx = jnp.maximum(jnp.maximum(ls[0], ls[1]), ls[2])
    es = [jnp.exp(l - mx) for l in ls]
    num = es[0] * outs[0] + es[1] * outs[1] + es[2] * outs[2]
    y_attn = (num / (es[0] + es[1] + es[2])).astype(BF16)
    ys = token_order(ys_ref, y_scr, SSM_CHUNK).astype(BF16)
    cw = GROUP_WIDTH
    y_ssm = []
    for c in range(SSM_WIDTH // cw):
        a = jnp.dot(ys, wglu_ref[:, c * cw:(c + 1) * cw], preferred_element_type=F32)
        b = jnp.dot(ys, wglu_ref[:, SSM_WIDTH + c * cw:SSM_WIDTH + (c + 1) * cw], preferred_element_type=F32)
        y_ssm.append((a * _sigmoid(b)).astype(BF16))
    y_ssm = jnp.concatenate(y_ssm, axis=-1)
    gated = []
    for c in range(D_MODEL // cw):
        pa = jnp.dot(y_ssm, wsp_ref[:, c * cw:(c + 1) * cw], preferred_element_type=F32)
        pb = jnp.dot(y_attn, wap_ref[:, c * cw:(c + 1) * cw], preferred_element_type=F32)
        g_ssm = g_ref[:, c * cw:(c + 1) * cw].astype(F32)
        g_attn = g_ref[:, D_MODEL + c * cw:D_MODEL + (c + 1) * cw].astype(F32)
        gated.append((g_ssm * pa + g_attn * pb).astype(BF16))
    gated = jnp.concatenate(gated, axis=-1)
    mix = jnp.dot(gated, wout_ref[...], preferred_element_type=F32)
    h_ref[...] = _layer_norm(ALPHA * x_ref[...] + mix, lng_ref[...], lnb_ref[...])


def _merge(outs, lses, ys, gates, x2, w_glu, w_sp, w_ap, w_out, ln_g, ln_b, seqlen, tm):
    t = x2.shape[0]
    tiles = seqlen // tm
    stage_rows = max(tm // d * _row_pitch(d) for d in [d for _, d in DILATION_PATTERNS] + [SSM_CHUNK])
    row = lambda w: pl.BlockSpec((tm, w), lambda i: (i, 0))
    res_spec = lambda d, w: pl.BlockSpec((None, d, tm // d, w), lambda i: (i // tiles, 0, i % tiles, 0))
    res_specs = [res_spec(d, GROUP_WIDTH) for _, d in DILATION_PATTERNS]
    return pl.pallas_call(
        _merge_kernel,
        grid=(t // tm,),
        in_specs=res_specs + res_specs + [res_spec(SSM_CHUNK, SSM_WIDTH), row(2 * D_MODEL), row(D_MODEL),
                  _const_spec(w_glu.shape), _const_spec(w_sp.shape), _const_spec(w_ap.shape),
                  _const_spec(w_out.shape), _const_spec((1, D_MODEL)), _const_spec((1, D_MODEL))],
        out_specs=row(D_MODEL),
        out_shape=jax.ShapeDtypeStruct((t, D_MODEL), F32),
        scratch_shapes=[pltpu.VMEM((N_GROUPS, GROUP_WIDTH // LANES, stage_rows, LANES), F32),
                        pltpu.VMEM((N_GROUPS, GROUP_WIDTH // LANES, stage_rows, LANES), F32),
                        pltpu.VMEM((SSM_WIDTH // LANES, stage_rows, LANES), F32)],
        compiler_params=_params(1),
        name="merge_ln1",
    )(*outs, *lses, ys, gates, x2, w_glu, w_sp, w_ap, w_out, ln_g, ln_b)


FF_CHUNK = 1024


def _ffn_kernel(h_ref, wup_ref, wdn_ref, lng_ref, lnb_ref, o_ref):
    h = h_ref[...]
    hb = h.astype(BF16)
    acts = []
    for c in range(D_FF // FF_CHUNK):
        lo, hi = c * FF_CHUNK, (c + 1) * FF_CHUNK
        up = jnp.dot(hb, wup_ref[:, lo:hi], preferred_element_type=F32)
        acts.append(jnp.square(jnp.maximum(up, 0.0)).astype(BF16))
    ff = jnp.dot(jnp.concatenate(acts, axis=-1), wdn_ref[...], preferred_element_type=F32)
    o_ref[...] = _layer_norm(ALPHA * h + ff, lng_ref[...], lnb_ref[...])


def _ffn(h1, w_up, w_down, ln_g, ln_b, tm):
    t = h1.shape[0]
    row = pl.BlockSpec((tm, D_MODEL), lambda i: (i, 0))
    return pl.pallas_call(
        _ffn_kernel,
        grid=(t // tm,),
        in_specs=[row, _const_spec(w_up.shape), _const_spec(w_down.shape),
                  _const_spec((1, D_MODEL)), _const_spec((1, D_MODEL))],
        out_specs=row,
        out_shape=jax.ShapeDtypeStruct((t, D_MODEL), F32),
        compiler_params=_params(1),
        name="ffn_ln2",
    )(h1, w_up, w_down, ln_g, ln_b)


def _permute_w_in(w):
    aw = ATTN_WIDTH
    cols = []
    for gi in range(N_GROUPS):
        lo, hi = gi * GROUP_WIDTH, (gi + 1) * GROUP_WIDTH
        cols += [w[:, lo:hi] * (HEAD_DIM ** -0.5), w[:, aw + lo:aw + hi], w[:, 2 * aw + lo:2 * aw + hi]]
    cols.append(w[:, 3 * aw:])
    return jnp.concatenate(cols, axis=1).astype(BF16)


def _layer(h2, bsz, seqlen, l, w_in, b_gate, lambda_re, lambda_im, log_dt, ssm_b_re, ssm_b_im,
           ssm_c_re, ssm_c_im, ssm_d, w_glu, w_ssm_proj, rel_bias, w_attn_proj, w_out,
           ln1_g, ln1_b, w_up, w_down, ln2_g, ln2_b, tm=1024, attn_mt=2048, ffn_tm=1024):
    qkv0, qkv1, qkv2, u, gates = _in_proj(h2, _permute_w_in(w_in[l]), b_gate[l][None, :], bsz, seqlen, tm)
    outs, lses = [], []
    for gi, ((window, dilation), qkv) in enumerate(zip(DILATION_PATTERNS, (qkv0, qkv1, qkv2))):
        rb = rel_bias[:, gi * HEADS_PER_GROUP:(gi + 1) * HEADS_PER_GROUP].astype(F32)
        o, s = _attention_group(qkv, rb, window, dilation, attn_mt)
        outs.append(o)
        lses.append(s)
    ssm_w = _ssm_weights(lambda_re[l], lambda_im[l], log_dt[l], ssm_b_re[l], ssm_b_im[l],
                         ssm_c_re[l], ssm_c_im[l], ssm_d[l])
    ys = _ssm(u, ssm_w, nb=4)
    h1 = _merge(outs, lses, ys, gates, h2, w_glu[l].astype(BF16), w_ssm_proj[l].astype(BF16),
                w_attn_proj[l].astype(BF16), w_out[l].astype(BF16),
                ln1_g[l][None, :], ln1_b[l][None, :], seqlen, tm)
    return _ffn(h1, w_up[l].astype(BF16), w_down[l].astype(BF16), ln2_g[l][None, :], ln2_b[l][None, :], ffn_tm)


def kernel(x, w_in, b_gate, lambda_re, lambda_im, log_dt, ssm_b_re, ssm_b_im, ssm_c_re, ssm_c_im,
           ssm_d, w_glu, w_ssm_proj, rel_bias, w_attn_proj, w_out, ln1_g, ln1_b, w_up, w_down,
           ln2_g, ln2_b):
    bsz, seqlen, d = x.shape
    h = x.reshape(bsz * seqlen, d)
    for l in range(w_in.shape[0]):
        h = _layer(h, bsz, seqlen, l, w_in, b_gate, lambda_re, lambda_im, log_dt, ssm_b_re, ssm_b_im,
                   ssm_c_re, ssm_c_im, ssm_d, w_glu, w_ssm_proj, rel_bias, w_attn_proj, w_out,
                   ln1_g, ln1_b, w_up, w_down, ln2_g, ln2_b)
    return h.reshape(bsz, seqlen, d)
```

```python
import functools
import math

import jax
import jax.numpy as jnp
from jax import lax
from jax.experimental import pallas as pl
from jax.experimental.pallas import tpu as pltpu

F32 = jnp.float32
BF16 = jnp.bfloat16

D_MODEL = 1024
HEAD_DIM = 64
HEADS_PER_GROUP = 4
GROUP_WIDTH = HEADS_PER_GROUP * HEAD_DIM
DILATION_PATTERNS = ((128, 1), (512, 4), (2048, 16))
N_GROUPS = len(DILATION_PATTERNS)
ATTN_WIDTH = N_GROUPS * GROUP_WIDTH
N_BUCKETS = 32
MAX_DISTANCE = 2048
SSM_WIDTH = 512
SSM_GROUP = 16
SSM_GROUPS = 32
SSM_STATE = 64
D_FF = 4 * D_MODEL
DEPTH = 1
ALPHA = (2.0 * DEPTH) ** 0.25
LN_EPS = 1e-5
NEG_INF = -1e30

ATTN_BLOCK = 128
ATTN_SKEW = (4, 7)
SSM_CHUNK = 16
SSM_OCT = 8
LANES = 128
SUBLANES = 8
VMEM_LIMIT_BYTES = 56 * 1024 * 1024


def _params(n_axes):
    return pltpu.CompilerParams(dimension_semantics=("arbitrary",) * n_axes,
                                vmem_limit_bytes=VMEM_LIMIT_BYTES)


def _sigmoid(z):
    return 0.5 * jnp.tanh(0.5 * z) + 0.5


def _row_pitch(stride):
    return stride + SUBLANES if stride % (2 * SUBLANES) == 0 else stride


def _const_spec(shape):
    nd = len(shape)
    return pl.BlockSpec(shape, lambda *_: (0,) * nd, pipeline_mode=pl.Buffered(1))


def _in_proj_kernel(x_ref, w_ref, bg_ref, qkv0_ref, qkv1_ref, qkv2_ref, u_ref, g_ref, scr):
    xb = x_ref[...].astype(BF16)
    tm = xb.shape[0]

    def mm(lo, hi):
        return jnp.dot(xb, w_ref[:, lo:hi], preferred_element_type=F32)

    def emit(ref, res, d, col0):
        width = res.shape[1]
        if d == 1:
            ref[0, :, col0:col0 + width] = res.astype(BF16)
            return
        slot0 = next_slot[0]
        next_slot[0] += width // LANES
        pitch = _row_pitch(d)
        for k in range(width // LANES):
            tile = res[:, k * LANES:(k + 1) * LANES]
            if pitch == d:
                scr[slot0 + k, 0:tm, :] = tile
            else:
                for g in range(tm // d):
                    scr[slot0 + k, g * pitch:g * pitch + d, :] = tile[g * d:(g + 1) * d]
        for r in range(d):
            for k in range(width // LANES):
                col = col0 + k * LANES
                ref[r, :, col:col + LANES] = scr[slot0 + k, pl.ds(r, tm // d, stride=pitch), :].astype(BF16)

    next_slot = [0]

    gw3 = 3 * GROUP_WIDTH
    for gi, ref in enumerate((qkv0_ref, qkv1_ref, qkv2_ref)):
        for c in range(3):
            lo = gi * gw3 + c * GROUP_WIDTH
            emit(ref, mm(lo, lo + GROUP_WIDTH), DILATION_PATTERNS[gi][1], c * GROUP_WIDTH)
    base = N_GROUPS * gw3
    emit(u_ref, mm(base, base + SSM_WIDTH), SSM_CHUNK, 0)
    base += SSM_WIDTH
    for c in range(2 * D_MODEL // GROUP_WIDTH):
        lo, hi = c * GROUP_WIDTH, (c + 1) * GROUP_WIDTH
        z = mm(base + lo, base + hi) + bg_ref[:, lo:hi]
        g_ref[:, lo:hi] = _sigmoid(z).astype(BF16)


def _in_proj(x2, w_perm, b_gate, bsz, seqlen, tm):
    t = x2.shape[0]
    n_in = w_perm.shape[1]
    tiles = seqlen // tm
    row = lambda w: pl.BlockSpec((tm, w), lambda i: (i, 0))
    w3 = 3 * GROUP_WIDTH
    dils = [d for _, d in DILATION_PATTERNS]
    n_stage = (sum(d > 1 for d in dils) * w3 + SSM_WIDTH) // LANES
    res_spec = lambda d, w:pl.BlockSpec((None, d, tm // d, w), lambda i: (i // tiles, 0, i % tiles, 0))
    return pl.pallas_call(
        _in_proj_kernel,
        grid=(t // tm,),
        in_specs=[row(D_MODEL), _const_spec((D_MODEL, n_in)), _const_spec((1, 2 * D_MODEL))],
        out_specs=[res_spec(d, w3) for d in dils] + [res_spec(SSM_CHUNK, SSM_WIDTH), row(2 * D_MODEL)],
        out_shape=[jax.ShapeDtypeStruct((bsz, d, seqlen // d, w3), BF16) for d in dils]
        + [jax.ShapeDtypeStruct((bsz, SSM_CHUNK, seqlen // SSM_CHUNK, SSM_WIDTH), BF16),
           jax.ShapeDtypeStruct((t, 2 * D_MODEL), BF16)],
        scratch_shapes=[pltpu.VMEM((n_stage, max(tm // d * _row_pitch(d) for d in dils + [SSM_CHUNK]), LANES), F32)],
        compiler_params=_params(1),
        name="in_proj",
    )(x2, w_perm, b_gate)


def _attn_kernel(bucket_ref, relb_ref, cur_ref, prev_ref, out_ref, lse_ref, bias_scr, k_scr, vt_scr, *, nq, nres):
    blk = ATTN_BLOCK
    first = (pl.program_id(0) == 0) & (pl.program_id(1) == 0) & (pl.program_id(2) == 0)

    @pl.when(first)
    def _build_bias():
        bucket = bucket_ref[...]
        for h in range(HEADS_PER_GROUP):
            acc = jnp.full(bucket.shape, NEG_INF, F32)
            for bkt in range(N_BUCKETS):
                acc = jnp.where(bucket == bkt, relb_ref[bkt, h], acc)
            bias_scr[h // 2, :, (h % 2) * blk:(h % 2 + 1) * blk] = acc

    row = lax.broadcasted_iota(jnp.int32, (2 * blk, blk), 0)
    row_head = row // HEAD_DIM
    keep_first = (row >= blk) | (pl.program_id(2) > 0)
    for r in range(nres):
        k_scr[r, 0:blk, :] = prev_ref[r, :, GROUP_WIDTH:2 * GROUP_WIDTH]
        k_scr[r, blk:, :] = cur_ref[r, :, GROUP_WIDTH:2 * GROUP_WIDTH]
        vt_scr[r, :, 0:blk] = prev_ref[r, :, 2 * GROUP_WIDTH:3 * GROUP_WIDTH].T
        for j in range(nq):
            vt_scr[r, :, (j + 1) * blk:(j + 2) * blk] = cur_ref[r, j * blk:(j + 1) * blk,
                                                                 2 * GROUP_WIDTH:3 * GROUP_WIDTH].T

    npair = HEADS_PER_GROUP // 2
    units = [(r, j, hp) for r in range(nres) for j in range(nq) for hp in range(npair)]
    qts = {}

    def scores(r, j, hp):
        if (r, j) not in qts:
            qts[(r, j)] = cur_ref[r, j * blk:(j + 1) * blk, 0:GROUP_WIDTH].T
        qt = qts[(r, j)]
        k2 = k_scr[r, j * blk:(j + 2) * blk, :]
        qh = jnp.concatenate([jnp.where(row_head == 2 * hp + i, qt, jnp.zeros_like(qt)) for i in range(2)], axis=1)
        return jnp.dot(k2, qh, preferred_element_type=F32)

    def softmax(r, j, hp, s2):
        ps, ms, ls = [], [], []
        for i in range(2):
            lanes = slice(i * blk, (i + 1) * blk)
            s = s2[:, lanes] + bias_scr[hp, :, lanes]
            if j == 0:
                s = jnp.where(keep_first, s, NEG_INF)
            m = jnp.max(s, axis=0, keepdims=True)
            p = jnp.exp(s - m)
            ls.append(jnp.sum(p, axis=0, keepdims=True))
            ps.append(p.astype(BF16))
            ms.append(m)
        return jnp.concatenate(ps, axis=1), jnp.concatenate(ms, axis=1), jnp.concatenate(ls, axis=1)

    def values(r, j, hp, p, m, l):
        vt = vt_scr[r, 2 * hp * HEAD_DIM:2 * (hp + 1) * HEAD_DIM, j * blk:(j + 2) * blk]
        ot = jnp.dot(vt, p, preferred_element_type=F32) * (1.0 / l)
        lse = jnp.broadcast_to(m + jnp.log(l), (HEAD_DIM, 2 * blk))
        return [(ot[i * HEAD_DIM:(i + 1) * HEAD_DIM, i * blk:(i + 1) * blk], lse[:, i * blk:(i + 1) * blk])
                for i in range(2)]

    raw, soft, done = {}, {}, []
    lag_soft, lag_val = ATTN_SKEW
    for step in range(len(units) + lag_val):
        if step < len(units):
            raw[step] = scores(*units[step])
        if lag_soft <= step < len(units) + lag_soft:
            soft[step - lag_soft] = softmax(*units[step - lag_soft], raw.pop(step - lag_soft))
        if step >= lag_val:
            i = step - lag_val
            r, j, hp = units[i]
            done += values(r, j, hp, *soft.pop(i))
            if hp == npair - 1:
                rows = slice(j * blk, (j + 1) * blk)
                out_ref[r, rows, :] = jnp.concatenate([d[0] for d in done], axis=0).T.astype(BF16)
                lse_ref[r, rows, :] = jnp.concatenate([d[1] for d in done], axis=0).T
                done = []


def _t5_bucket(dist):
    max_exact = N_BUCKETS // 2
    n_log = N_BUCKETS - max_exact
    thresholds = [math.ceil(max_exact * (MAX_DISTANCE / max_exact) ** (k / n_log)) for k in range(1, n_log)]
    large = max_exact + sum((dist >= t).astype(jnp.int32) for t in thresholds)
    return jnp.where(dist < max_exact, dist, large)


def _bucket_table(window, dilation):
    blk = ATTN_BLOCK
    span = window // dilation
    rel = jnp.arange(blk, dtype=jnp.int32)[None, :] + blk - jnp.arange(2 * blk, dtype=jnp.int32)[:, None]
    valid = (rel >= 0) & (rel <= span)
    return jnp.where(valid, _t5_bucket(jnp.maximum(rel, 0) * dilation), -1)


def _attention_group(qkv, rel_bias_g, window, dilation, mt):
    assert window // dilation == ATTN_BLOCK
    bsz, _, n, w3 = qkv.shape
    rows = mt
    mt = min(rows, n)
    nres = rows // mt
    nq = mt // ATTN_BLOCK
    cur = pl.BlockSpec((None, nres, mt, w3), lambda b, r, i: (b, r, i, 0))
    prev = pl.BlockSpec((None, nres, ATTN_BLOCK, w3), lambda b, r, i: (b, r, jnp.maximum(i * nq - 1, 0), 0))
    ospec = pl.BlockSpec((None, nres, mt, GROUP_WIDTH), lambda b, r, i: (b, r, i, 0))
    return pl.pallas_call(
        functools.partial(_attn_kernel, nq=nq, nres=nres),
        grid=(bsz, dilation // nres, n // mt),
        in_specs=[_const_spec((2 * ATTN_BLOCK, ATTN_BLOCK)),
                  pl.BlockSpec(memory_space=pltpu.SMEM), cur, prev],
        out_specs=[ospec, ospec],
        out_shape=[jax.ShapeDtypeStruct((bsz, dilation, n, GROUP_WIDTH), BF16),
                   jax.ShapeDtypeStruct((bsz, dilation, n, GROUP_WIDTH), F32)],
        scratch_shapes=[pltpu.VMEM((HEADS_PER_GROUP // 2, 2 * ATTN_BLOCK, 2 * ATTN_BLOCK), F32),
                        pltpu.VMEM((nres, mt + ATTN_BLOCK, GROUP_WIDTH), BF16),
                        pltpu.VMEM((nres, GROUP_WIDTH, mt + ATTN_BLOCK), BF16)],
        compiler_params=_params(3),
        name=f"attn_d{dilation}",
    )(_bucket_table(window, dilation), rel_bias_g, qkv, qkv)


def _ssm_prep_kernel(logdt_ref, lr_ref, li_ref, bt_re_ref, bt_im_ref, c_re_ref, c_im_ref, dl_ref,
                     toep_ref, bst_ref, cst_ref, a_ref):
    cs = SSM_CHUNK
    dt = jnp.exp(jnp.full((1, SSM_STATE), logdt_ref[pl.program_id(0)], F32))
    lr, li = lr_ref[...], li_ref[...]
    mag = jnp.exp(lr * dt)
    ab_re, ab_im = mag * jnp.cos(li * dt), mag * jnp.sin(li * dt)
    den = lr * lr + li * li
    nr = ab_re - 1.0
    k_re = (nr * lr + ab_im * li) / den
    k_im = (ab_im * lr - nr * li) / den
    bt_re, bt_im = bt_re_ref[...], bt_im_ref[...]
    bb_re = k_re * bt_re - k_im * bt_im
    bb_im = k_re * bt_im + k_im * bt_re
    j = lax.broadcasted_iota(jnp.int32, (cs + SUBLANES, SSM_STATE), 0).astype(F32)
    pmag = jnp.exp(lr * dt * j)
    ang = li * dt * j
    p_re, p_im = pmag * jnp.cos(ang), pmag * jnp.sin(ang)
    c_re, c_im = c_re_ref[...], c_im_ref[...]
    cp_re = [c_re * p_re[i:i + 1] - c_im * p_im[i:i + 1] for i in range(cs + 1)]
    cp_im = [c_re * p_im[i:i + 1] + c_im * p_re[i:i + 1] for i in range(cs + 1)]
    nt = (((1,), (1,)), ((), ()))
    hi = lax.Precision.HIGHEST
    kcat = (lax.dot_general(bb_re, jnp.concatenate(cp_re[:cs], axis=0), nt, precision=hi,
                            preferred_element_type=F32)
            - lax.dot_general(bb_im, jnp.concatenate(cp_im[:cs], axis=0), nt, precision=hi,
                              preferred_element_type=F32))
    lane = lax.broadcasted_iota(jnp.int32, kcat.shape, 1)
    row = lax.broadcasted_iota(jnp.int32, kcat.shape, 0)
    dl = dl_ref[...]
    for s in range(cs):
        off = s * SSM_GROUP
        t_s = kcat if s == 0 else jnp.where(lane >= off, pltpu.roll(kcat, off, 1), 0.0)
        toep_ref[s] = (t_s + jnp.where(lane == off + row, dl, 0.0)).astype(BF16)
        pe_re, pe_im = p_re[cs - 1 - s:cs - s], p_im[cs - 1 - s:cs - s]
        st_re = pe_re * bb_re - pe_im * bb_im
        st_im = pe_re * bb_im + pe_im * bb_re
        bst_ref[s] = jnp.concatenate([st_re, st_im], axis=-1).astype(BF16)
    ro = jnp.concatenate([jnp.concatenate(cp_re[1:], axis=0),
                          -jnp.concatenate(cp_im[1:], axis=0)], axis=-1)
    ro_t = ro.T
    cst_ref[0] = ro_t[:SSM_STATE].astype(BF16)
    cst_ref[1] = ro_t[SSM_STATE:].astype(BF16)
    a_ref[...] = jnp.concatenate([p_re[cs:cs + 1], p_im[cs:cs + 1]], axis=0)


def _ssm_weights(lambda_re, lambda_im, log_dt, b_re, b_im, c_re, c_im, d_skip):
    f32 = F32
    cs, ng, no = SSM_CHUNK, SSM_GROUPS, SSM_GROUPS // SSM_OCT
    grp = lambda *shape: pl.BlockSpec((None,) + shape, lambda g: (g,) + (0,) * len(shape))
    in_oct = lambda *shape: pl.BlockSpec((None, shape[0], None) + shape[1:],
                                         lambda g: (g // SSM_OCT, 0, g % SSM_OCT) + (0,) * (len(shape) - 1))
    toep_c, bst_c, cst_c, a32 = pl.pallas_call(
        _ssm_prep_kernel,
        grid=(ng,),
        in_specs=[pl.BlockSpec(memory_space=pltpu.SMEM), grp(1, SSM_STATE), grp(1, SSM_STATE),
                  grp(SSM_GROUP, SSM_STATE), grp(SSM_GROUP, SSM_STATE),
                  grp(SSM_GROUP, SSM_STATE), grp(SSM_GROUP, SSM_STATE), grp(1, cs * SSM_GROUP)],
        out_specs=[in_oct(cs, SSM_GROUP, cs * SSM_GROUP), in_oct(cs, SSM_GROUP, 2 * SSM_STATE),
                   in_oct(2, SSM_STATE, cs * SSM_GROUP), grp(2, SSM_STATE)],
        out_shape=[jax.ShapeDtypeStruct((no, cs, SSM_OCT, SSM_GROUP, cs * SSM_GROUP), BF16),
                   jax.ShapeDtypeStruct((no, cs, SSM_OCT, SSM_GROUP, 2 * SSM_STATE), BF16),
                   jax.ShapeDtypeStruct((no, 2, SSM_OCT, SSM_STATE, cs * SSM_GROUP), BF16),
                   jax.ShapeDtypeStruct((ng, 2, SSM_STATE), f32)],
        compiler_params=_params(1),
        name="ssm_prep",
    )(log_dt.astype(f32), lambda_re.astype(f32)[:, None, :], lambda_im.astype(f32)[:, None, :],
      b_re.astype(f32).transpose(0, 2, 1), b_im.astype(f32).transpose(0, 2, 1),
      c_re.astype(f32), c_im.astype(f32), jnp.tile(d_skip.astype(f32), (1, cs))[:, None, :])
    avec = a32.reshape(no, SSM_OCT, 2, SSM_STATE).transpose(0, 2, 1, 3).reshape(no, 2, SSM_OCT * SSM_STATE)
    return (toep_c.reshape(no, cs * LANES, cs * SSM_GROUP),
            bst_c.reshape(no, cs * LANES, 2 * SSM_STATE),
            cst_c.reshape(no, 2 * SSM_OCT * SSM_STATE, cs * SSM_GROUP),
            avec)


def _expand_ssm_weights(toep_ref, bstc_ref, cstc_ref, wt_scr, bst_scr, cst_scr):
    tw = 2 * LANES
    sh_g, sh_n, sh_o = (v.bit_length() - 1 for v in (SSM_GROUP, SSM_STATE, SSM_OCT))
    r = lax.broadcasted_iota(jnp.int32, (tw, tw), 0)
    c = lax.broadcasted_iota(jnp.int32, (tw, tw), 1)
    col_grp = (c >> sh_g) & (SSM_OCT - 1)
    row_grp = (r >> sh_g) & (SSM_OCT - 1)
    for tp in range(SSM_CHUNK // 2):
        src_col = (2 * tp + (c >> (sh_g + sh_o))) * SSM_GROUP + (c & (SSM_GROUP - 1))
        e = jnp.where(r == src_col, 1.0, 0.0).astype(BF16)
        base = tp * (tp + 1) // 2
        for sp in range(tp + 1):
            x = jnp.dot(toep_ref[sp * tw:(sp + 1) * tw, :], e, preferred_element_type=F32)
            wt_scr[base + sp] = jnp.where(row_grp == col_grp, x, 0.0).astype(BF16)
        for j in range(2 * SSM_OCT * SSM_STATE // tw):
            x = jnp.dot(cstc_ref[j * tw:(j + 1) * tw, :], e, preferred_element_type=F32)
            row_grp_n = ((j * tw + r) >> sh_n) & (SSM_OCT - 1)
            cst_scr[j * tw:(j + 1) * tw, tp * tw:(tp + 1) * tw] = jnp.where(row_grp_n == col_grp, x, 0.0).astype(BF16)
    ns = 2 * SSM_OCT * SSM_STATE
    rb = lax.broadcasted_iota(jnp.int32, (2 * SSM_STATE, ns), 0)
    cb = lax.broadcasted_iota(jnp.int32, (2 * SSM_STATE, ns), 1)
    src_col_b = (cb >> (sh_n + sh_o)) * SSM_STATE + (cb & (SSM_STATE - 1))
    eb = jnp.where(rb == src_col_b, 1.0, 0.0).astype(BF16)
    rr = lax.broadcasted_iota(jnp.int32, (tw, ns), 0)
    cc = lax.broadcasted_iota(jnp.int32, (tw, ns), 1)
    keep = ((rr >> sh_g) & (SSM_OCT - 1)) == ((cc >> sh_n) & (SSM_OCT - 1))
    for j in range(SSM_CHUNK * LANES // tw):
        x = jnp.dot(bstc_ref[j * tw:(j + 1) * tw, :], eb, preferred_element_type=F32)
        bst_scr[j * tw:(j + 1) * tw, :] = jnp.where(keep, x, 0.0).astype(BF16)


def _ssm_kernel(u_ref, toep_ref, bstc_ref, cstc_ref, a_ref, y_ref, wt_ref, bst_ref, cst_ref, zz_scr, hp_scr,
                *, n_chunks, nb):
    @pl.when(pl.program_id(1) == 0)
    def _new_octet():
        _expand_ssm_weights(toep_ref, bstc_ref, cstc_ref, wt_ref, bst_ref, cst_ref)

    rows = nb * n_chunks
    nk = SSM_OCT * SSM_STATE // LANES
    u_t = [u_ref[:, s].reshape(rows, LANES) for s in range(SSM_CHUNK)]
    z = jnp.dot(jnp.concatenate(u_t, axis=-1), bst_ref[...], preferred_element_type=F32)
    pitch = _row_pitch(n_chunks)
    for k in range(2 * nk):
        for b in range(nb):
            zz_scr[k, b * pitch:b * pitch + n_chunks, :] = z[b * n_chunks:(b + 1) * n_chunks, k * LANES:(k + 1) * LANES]
    a_re = [jnp.broadcast_to(a_ref[0:1, k * LANES:(k + 1) * LANES], (nb, LANES)) for k in range(nk)]
    a_im = [jnp.broadcast_to(a_ref[1:2, k * LANES:(k + 1) * LANES], (nb, LANES)) for k in range(nk)]

    def step(c, carry):
        h_re, h_im = carry
        rows_c = pl.ds(c, nb, stride=pitch)
        new_re, new_im = [], []
        for k in range(nk):
            hp_scr[k, rows_c, :] = h_re[k]
            hp_scr[nk + k, rows_c, :] = h_im[k]
            new_re.append(a_re[k] * h_re[k] - a_im[k] * h_im[k] + zz_scr[k, rows_c, :])
            new_im.append(a_re[k] * h_im[k] + a_im[k] * h_re[k] + zz_scr[nk + k, rows_c, :])
        return tuple(new_re), tuple(new_im)

    zero = tuple(jnp.zeros((nb, LANES), F32) for _ in range(nk))
    lax.fori_loop(0, n_chunks, step, (zero, zero), unroll=4)
    hp = jnp.concatenate(
        [jnp.concatenate([hp_scr[k, b * pitch:b * pitch + n_chunks, :] for b in range(nb)], axis=0)
         for k in range(2 * nk)], axis=-1).astype(BF16)
    tw = 2 * LANES
    for tp in range(SSM_CHUNK // 2):
        base = tp * (tp + 1) // 2
        y = jnp.dot(hp, cst_ref[:, tp * tw:(tp + 1) * tw], preferred_element_type=F32)
        y = y + jnp.dot(jnp.concatenate(u_t[:2 * (tp + 1)], axis=-1),
                        wt_ref[base:base + tp + 1].reshape((tp + 1) * tw, tw), preferred_element_type=F32)
        y = jax.nn.gelu(y).astype(BF16)
        y_ref[:, 2 * tp] = y[:, :LANES].reshape(nb, n_chunks, LANES)
        y_ref[:, 2 * tp + 1] = y[:, LANES:].reshape(nb, n_chunks, LANES)


def _ssm(u16, weights, nb):
    toep_c, bst_c, cst_c, avec = weights
    bsz, cs, nc, width = u16.shape
    oct_spec = lambda a: pl.BlockSpec((None,) + a.shape[1:], lambda o, b: (o,) + (0,) * (a.ndim - 1))
    io_spec = pl.BlockSpec((nb, cs, nc, LANES), lambda o, b: (b, 0, 0, o))
    n_state = 2 * SSM_OCT * SSM_STATE
    n_pairs = (cs // 2) * (cs // 2 + 1) // 2
    return pl.pallas_call(
        functools.partial(_ssm_kernel, n_chunks=nc, nb=nb),
        grid=(width // LANES, bsz // nb),
        in_specs=[io_spec, oct_spec(toep_c), oct_spec(bst_c), oct_spec(cst_c), oct_spec(avec)],
        out_specs=io_spec,
        out_shape=jax.ShapeDtypeStruct(u16.shape, BF16),
        scratch_shapes=[pltpu.VMEM((n_pairs, 2 * LANES, 2 * LANES), BF16),
                        pltpu.VMEM((cs * LANES, n_state), BF16),
                        pltpu.VMEM((n_state, cs * LANES), BF16),
                        pltpu.VMEM((n_state // LANES, nb * _row_pitch(nc), LANES), F32),
                        pltpu.VMEM((n_state // LANES, nb * _row_pitch(nc), LANES), F32)],
        compiler_params=_params(2),
        name="ssm",
    )(u16, toep_c, bst_c, cst_c, avec)


def _layer_norm(v, g, b):
    mu = jnp.mean(v, axis=-1, keepdims=True)
    vc = v - mu
    var = jnp.mean(vc * vc, axis=-1, keepdims=True)
    return vc * lax.rsqrt(var + LN_EPS) * g + b


def _merge_kernel(o0_ref, o1_ref, o2_ref, l0_ref, l1_ref, l2_ref, ys_ref, g_ref, x_ref,
                  wglu_ref, wsp_ref, wap_ref, wout_ref, lng_ref, lnb_ref, h_ref, o_scr, l_scr, y_scr):
    tm = x_ref.shape[0]

    def token_order(ref, scr, d):
        if d == 1:
            return ref[0].astype(F32)
        nk = ref.shape[-1] // LANES
        pitch = _row_pitch(d)
        for r in range(d):
            v = ref[r].astype(F32)
            for k in range(nk):
                scr[k, pl.ds(r, tm // d, stride=pitch), :] = v[:, k * LANES:(k + 1) * LANES]
        if pitch == d:
            return jnp.concatenate([scr[k, 0:tm, :] for k in range(nk)], axis=-1)
        return jnp.concatenate(
            [jnp.concatenate([scr[k, g * pitch:g * pitch + d, :] for g in range(tm // d)], axis=0)
             for k in range(nk)], axis=-1)

    ls, outs = [], []
    for gi, (o_ref, l_ref) in enumerate(((o0_ref, l0_ref), (o1_ref, l1_ref), (o2_ref, l2_ref))):
        d = DILATION_PATTERNS[gi][1]
        outs.append(token_order(o_ref, o_scr.at[gi], d))
        ls.append(token_order(l_ref, l_scr.at[gi], d))
    mx = jnp.maximum(jnp.maximum(ls[0], ls[1]), ls[2])
    es = [jnp.exp(l - mx) for l in ls]
    num = es[0] * outs[0] + es[1] * outs[1] + es[2] * outs[2]
    y_attn = (num / (es[0] + es[1] + es[2])).astype(BF16)
    ys = token_order(ys_ref, y_scr, SSM_CHUNK).astype(BF16)
    cw = GROUP_WIDTH
    y_ssm = []
    for c in range(SSM_WIDTH // cw):
        a = jnp.dot(ys, wglu_ref[:, c * cw:(c + 1) * cw], preferred_element_type=F32)
        b = jnp.dot(ys, wglu_ref[:, SSM_WIDTH + c * cw:SSM_WIDTH + (c + 1) * cw], preferred_element_type=F32)
        y_ssm.append((a * _sigmoid(b)).astype(BF16))
    y_ssm = jnp.concatenate(y_ssm, axis=-1)
    gated = []
    for c in range(D_MODEL // cw):
        pa = jnp.dot(y_ssm, wsp_ref[:, c * cw:(c + 1) * cw], preferred_element_type=F32)
        pb = jnp.dot(y_attn, wap_ref[:, c * cw:(c + 1) * cw], preferred_element_type=F32)
        g_ssm = g_ref[:, c * cw:(c + 1) * cw].astype(F32)
        g_attn = g_ref[:, D_MODEL + c * cw:D_MODEL + (c + 1) * cw].astype(F32)
        gated.append((g_ssm * pa + g_attn * pb).astype(BF16))
    gated = jnp.concatenate(gated, axis=-1)
    mix = jnp.dot(gated, wout_ref[...], preferred_element_type=F32)
    h_ref[...] = _layer_norm(ALPHA * x_ref[...] + mix, lng_ref[...], lnb_ref[...])


def _merge(outs, lses, ys, gates, x2, w_glu, w_sp, w_ap, w_out, ln_g, ln_b, seqlen, tm):
    t = x2.shape[0]
    tiles = seqlen // tm
    stage_rows = max(tm // d * _row_pitch(d) for d in [d for _, d in DILATION_PATTERNS] + [SSM_CHUNK])
    row = lambda w: pl.BlockSpec((tm, w), lambda i: (i, 0))
    res_spec = lambda d, w: pl.BlockSpec((None, d, tm // d, w), lambda i: (i // tiles, 0, i % tiles, 0))
    res_specs = [res_spec(d, GROUP_WIDTH) for _, d in DILATION_PATTERNS]
    return pl.pallas_call(
        _merge_kernel,
        grid=(t // tm,),
        in_specs=res_specs + res_specs + [res_spec(SSM_CHUNK, SSM_WIDTH), row(2 * D_MODEL), row(D_MODEL),
                  _const_spec(w_glu.shape), _const_spec(w_sp.shape), _const_spec(w_ap.shape),
                  _const_spec(w_out.shape), _const_spec((1, D_MODEL)), _const_spec((1, D_MODEL))],
        out_specs=row(D_MODEL),
        out_shape=jax.ShapeDtypeStruct((t, D_MODEL), F32),
        scratch_shapes=[pltpu.VMEM((N_GROUPS, GROUP_WIDTH // LANES, stage_rows, LANES), F32),
                        pltpu.VMEM((N_GROUPS, GROUP_WIDTH // LANES, stage_rows, LANES), F32),
                        pltpu.VMEM((SSM_WIDTH // LANES, stage_rows, LANES), F32)],
        compiler_params=_params(1),
        name="merge_ln1",
    )(*outs, *lses, ys, gates, x2, w_glu, w_sp, w_ap, w_out, ln_g, ln_b)


FF_CHUNK = 1024


def _ffn_kernel(h_ref, wup_ref, wdn_ref, lng_ref, lnb_ref, o_ref):
    h = h_ref[...]
    hb = h.astype(BF16)
    acts = []
    for c in range(D_FF // FF_CHUNK):
        lo, hi = c * FF_CHUNK, (c + 1) * FF_CHUNK
        up = jnp.dot(hb, wup_ref[:, lo:hi], preferred_element_type=F32)
        acts.append(jnp.square(jnp.maximum(up, 0.0)).astype(BF16))
    ff = jnp.dot(jnp.concatenate(acts, axis=-1), wdn_ref[...], preferred_element_type=F32)
    o_ref[...] = _layer_norm(ALPHA * h + ff, lng_ref[...], lnb_ref[...])


def _ffn(h1, w_up, w_down, ln_g, ln_b, tm):
    t = h1.shape[0]
    row = pl.BlockSpec((tm, D_MODEL), lambda i: (i, 0))
    return pl.pallas_call(
        _ffn_kernel,
        grid=(t // tm,),
        in_specs=[row, _const_spec(w_up.shape), _const_spec(w_down.shape),
                  _const_spec((1, D_MODEL)), _const_spec((1, D_MODEL))],
        out_specs=row,
        out_shape=jax.ShapeDtypeStruct((t, D_MODEL), F32),
        compiler_params=_params(1),
        name="ffn_ln2",
    )(h1, w_up, w_down, ln_g, ln_b)


def _permute_w_in(w):
    aw = ATTN_WIDTH
    cols = []
    for gi in range(N_GROUPS):
        lo, hi = gi * GROUP_WIDTH, (gi + 1) * GROUP_WIDTH
        cols += [w[:, lo:hi] * (HEAD_DIM ** -0.5), w[:, aw + lo:aw + hi], w[:, 2 * aw + lo:2 * aw + hi]]
    cols.append(w[:, 3 * aw:])
    return jnp.concatenate(cols, axis=1).astype(BF16)


def _layer(h2, bsz, seqlen, l, w_in, b_gate, lambda_re, lambda_im, log_dt, ssm_b_re, ssm_b_im,
           ssm_c_re, ssm_c_im, ssm_d, w_glu, w_ssm_proj, rel_bias, w_attn_proj, w_out,
           ln1_g, ln1_b, w_up, w_down, ln2_g, ln2_b, tm=1024, attn_mt=2048, ffn_tm=1024):
    qkv0, qkv1, qkv2, u, gates = _in_proj(h2, _permute_w_in(w_in[l]), b_gate[l][None, :], bsz, seqlen, tm)
    outs, lses = [], []
    for gi, ((window, dilation), qkv) in enumerate(zip(DILATION_PATTERNS, (qkv0, qkv1, qkv2))):
        rb = rel_bias[:, gi * HEADS_PER_GROUP:(gi + 1) * HEADS_PER_GROUP].astype(F32)
        o, s = _attention_group(qkv, rb, window, dilation, attn_mt)
        outs.append(o)
        lses.append(s)
    ssm_w = _ssm_weights(lambda_re[l], lambda_im[l], log_dt[l], ssm_b_re[l], ssm_b_im[l],
                         ssm_c_re[l], ssm_c_im[l], ssm_d[l])
    ys = _ssm(u, ssm_w, nb=4)
    h1 = _merge(outs, lses, ys, gates, h2, w_glu[l].astype(BF16), w_ssm_proj[l].astype(BF16),
                w_attn_proj[l].astype(BF16), w_out[l].astype(BF16),
                ln1_g[l][None, :], ln1_b[l][None, :], seqlen, tm)
    return _ffn(h1, w_up[l].astype(BF16), w_down[l].astype(BF16), ln2_g[l][None, :], ln2_b[l][None, :], ffn_tm)


def kernel(x, w_in, b_gate, lambda_re, lambda_im, log_dt, ssm_b_re, ssm_b_im, ssm_c_re, ssm_c_im,
           ssm_d, w_glu, w_ssm_proj, rel_bias, w_attn_proj, w_out, ln1_g, ln1_b, w_up, w_down,
           ln2_g, ln2_b):
    bsz, seqlen, d = x.shape
    h = x.reshape(bsz * seqlen, d)
    for l in range(w_in.shape[0]):
        h = _layer(h, bsz, seqlen, l, w_in, b_gate, lambda_re, lambda_im, log_dt, ssm_b_re, ssm_b_im,
                   ssm_c_re, ssm_c_im, ssm_d, w_glu, w_ssm_proj, rel_bias, w_attn_proj, w_out,
                   ln1_g, ln1_b, w_up, w_down, ln2_g, ln2_b)
    return h.reshape(bsz, seqlen, d)
```

```python
import functools
import math

import jax
import jax.numpy as jnp
from jax import lax
from jax.experimental import pallas as pl
from jax.experimental.pallas import tpu as pltpu

F32 = jnp.float32
BF16 = jnp.bfloat16

D_MODEL = 1024
HEAD_DIM = 64
HEADS_PER_GROUP = 4
GROUP_WIDTH = HEADS_PER_GROUP * HEAD_DIM
DILATION_PATTERNS = ((128, 1), (512, 4), (2048, 16))
N_GROUPS = len(DILATION_PATTERNS)
ATTN_WIDTH = N_GROUPS * GROUP_WIDTH
N_BUCKETS = 32
MAX_DISTANCE = 2048
SSM_WIDTH = 512
SSM_GROUP = 16
SSM_GROUPS = 32
SSM_STATE = 64
D_FF = 4 * D_MODEL
DEPTH = 1
ALPHA = (2.0 * DEPTH) ** 0.25
LN_EPS = 1e-5
NEG_INF = -1e30

ATTN_BLOCK = 128
ATTN_SKEW = (4, 7)
SSM_CHUNK = 16
SSM_OCT = 8
LANES = 128
SUBLANES = 8
VMEM_LIMIT_BYTES = 56 * 1024 * 1024


def _params(n_axes):
    return pltpu.CompilerParams(dimension_semantics=("arbitrary",) * n_axes,
                                vmem_limit_bytes=VMEM_LIMIT_BYTES)


def _sigmoid(z):
    return 0.5 * jnp.tanh(0.5 * z) + 0.5


def _row_pitch(stride):
    return stride + SUBLANES if stride % (2 * SUBLANES) == 0 else stride


def _const_spec(shape):
    nd = len(shape)
    return pl.BlockSpec(shape, lambda *_: (0,) * nd, pipeline_mode=pl.Buffered(1))


def _in_proj_kernel(x_ref, w_ref, bg_ref, qkv0_ref, qkv1_ref, qkv2_ref, u_ref, g_ref, scr):
    xb = x_ref[...].astype(BF16)
    tm = xb.shape[0]

    def mm(lo, hi):
        return jnp.dot(xb, w_ref[:, lo:hi], preferred_element_type=F32)

    def emit(ref, res, d, col0):
        width = res.shape[1]
        if d == 1:
            ref[0, :, col0:col0 + width] = res.astype(BF16)
            return
        slot0 = next_slot[0]
        next_slot[0] += width // LANES
        pitch = _row_pitch(d)
        for k in range(width // LANES):
            tile = res[:, k * LANES:(k + 1) * LANES]
            if pitch == d:
                scr[slot0 + k, 0:tm, :] = tile
            else:
                for g in range(tm // d):
                    scr[slot0 + k, g * pitch:g * pitch + d, :] = tile[g * d:(g + 1) * d]
        for r in range(d):
            for k in range(width // LANES):
                col = col0 + k * LANES
                ref[r, :, col:col + LANES] = scr[slot0 + k, pl.ds(r, tm // d, stride=pitch), :].astype(BF16)

    next_slot = [0]

    gw3 = 3 * GROUP_WIDTH
    for gi, ref in enumerate((qkv0_ref, qkv1_ref, qkv2_ref)):
        for c in range(3):
            lo = gi * gw3 + c * GROUP_WIDTH
            emit(ref, mm(lo, lo + GROUP_WIDTH), DILATION_PATTERNS[gi][1], c * GROUP_WIDTH)
    base = N_GROUPS * gw3
    emit(u_ref, mm(base, base + SSM_WIDTH), SSM_CHUNK, 0)
    base += SSM_WIDTH
    for c in range(2 * D_MODEL // GROUP_WIDTH):
        lo, hi = c * GROUP_WIDTH, (c + 1) * GROUP_WIDTH
        z = mm(base + lo, base + hi) + bg_ref[:, lo:hi]
        g_ref[:, lo:hi] = _sigmoid(z).astype(BF16)


def _in_proj(x2, w_perm, b_gate, bsz, seqlen, tm):
    t = x2.shape[0]
    n_in = w_perm.shape[1]
    tiles = seqlen // tm
    row = lambda w: pl.BlockSpec((tm, w), lambda i: (i, 0))
    w3 = 3 * GROUP_WIDTH
    dils = [d for _, d in DILATION_PATTERNS]
    n_stage = (sum(d > 1 for d in dils) * w3 + SSM_WIDTH) // LANES
    res_spec = lambda d, w:pl.BlockSpec((None, d, tm // d, w), lambda i: (i // tiles, 0, i % tiles, 0))
    return pl.pallas_call(
        _in_proj_kernel,
        grid=(t // tm,),
        in_specs=[row(D_MODEL), _const_spec((D_MODEL, n_in)), _const_spec((1, 2 * D_MODEL))],
        out_specs=[res_spec(d, w3) for d in dils] + [res_spec(SSM_CHUNK, SSM_WIDTH), row(2 * D_MODEL)],
        out_shape=[jax.ShapeDtypeStruct((bsz, d, seqlen // d, w3), BF16) for d in dils]
        + [jax.ShapeDtypeStruct((bsz, SSM_CHUNK, seqlen // SSM_CHUNK, SSM_WIDTH), BF16),
           jax.ShapeDtypeStruct((t, 2 * D_MODEL), BF16)],
        scratch_shapes=[pltpu.VMEM((n_stage, max(tm // d * _row_pitch(d) for d in dils + [SSM_CHUNK]), LANES), F32)],
        compiler_params=_params(1),
        name="in_proj",
    )(x2, w_perm, b_gate)


def _attn_kernel(bucket_ref, relb_ref, cur_ref, prev_ref, out_ref, lse_ref, bias_scr, k_scr, vt_scr, *, nq, nres):
    blk = ATTN_BLOCK
    first = (pl.program_id(0) == 0) & (pl.program_id(1) == 0) & (pl.program_id(2) == 0)

    @pl.when(first)
    def _build_bias():
        bucket = bucket_ref[...]
        for h in range(HEADS_PER_GROUP):
            acc = jnp.full(bucket.shape, NEG_INF, F32)
            for bkt in range(N_BUCKETS):
                acc = jnp.where(bucket == bkt, relb_ref[bkt, h], acc)
            bias_scr[h // 2, :, (h % 2) * blk:(h % 2 + 1) * blk] = acc

    row = lax.broadcasted_iota(jnp.int32, (2 * blk, blk), 0)
    row_head = row // HEAD_DIM
    keep_first = (row >= blk) | (pl.program_id(2) > 0)
    for r in range(nres):
        k_scr[r, 0:blk, :] = prev_ref[r, :, GROUP_WIDTH:2 * GROUP_WIDTH]
        k_scr[r, blk:, :] = cur_ref[r, :, GROUP_WIDTH:2 * GROUP_WIDTH]
        vt_scr[r, :, 0:blk] = prev_ref[r, :, 2 * GROUP_WIDTH:3 * GROUP_WIDTH].T
        for j in range(nq):
            vt_scr[r, :, (j + 1) * blk:(j + 2) * blk] = cur_ref[r, j * blk:(j + 1) * blk,
                                                                 2 * GROUP_WIDTH:3 * GROUP_WIDTH].T

    npair = HEADS_PER_GROUP // 2
    units = [(r, j, hp) for r in range(nres) for j in range(nq) for hp in range(npair)]
    qts = {}

    def scores(r, j, hp):
        if (r, j) not in qts:
            qts[(r, j)] = cur_ref[r, j * blk:(j + 1) * blk, 0:GROUP_WIDTH].T
        qt = qts[(r, j)]
        k2 = k_scr[r, j * blk:(j + 2) * blk, :]
        qh = jnp.concatenate([jnp.where(row_head == 2 * hp + i, qt, jnp.zeros_like(qt)) for i in range(2)], axis=1)
        return jnp.dot(k2, qh, preferred_element_type=F32)

    def softmax(r, j, hp, s2):
        ps, ms, ls = [], [], []
        for i in range(2):
            lanes = slice(i * blk, (i + 1) * blk)
            s = s2[:, lanes] + bias_scr[hp, :, lanes]
            if j == 0:
                s = jnp.where(keep_first, s, NEG_INF)
            m = jnp.max(s, axis=0, keepdims=True)
            p = jnp.exp(s - m)
            ls.append(jnp.sum(p, axis=0, keepdims=True))
            ps.append(p.astype(BF16))
            ms.append(m)
        return jnp.concatenate(ps, axis=1), jnp.concatenate(ms, axis=1), jnp.concatenate(ls, axis=1)

    def values(r, j, hp, p, m, l):
        vt = vt_scr[r, 2 * hp * HEAD_DIM:2 * (hp + 1) * HEAD_DIM, j * blk:(j + 2) * blk]
        ot = jnp.dot(vt, p, preferred_element_type=F32) * (1.0 / l)
        lse = jnp.broadcast_to(m + jnp.log(l), (HEAD_DIM, 2 * blk))
        return [(ot[i * HEAD_DIM:(i + 1) * HEAD_DIM, i * blk:(i + 1) * blk], lse[:, i * blk:(i + 1) * blk])
                for i in range(2)]

    raw, soft, done = {}, {}, []
    lag_soft, lag_val = ATTN_SKEW
    for step in range(len(units) + lag_val):
        if step < len(units):
            raw[step] = scores(*units[step])
        if lag_soft <= step < len(units) + lag_soft:
            soft[step - lag_soft] = softmax(*units[step - lag_soft], raw.pop(step - lag_soft))
        if step >= lag_val:
            i = step - lag_val
            r, j, hp = units[i]
            done += values(r, j, hp, *soft.pop(i))
            if hp == npair - 1:
                rows = slice(j * blk, (j + 1) * blk)
                out_ref[r, rows, :] = jnp.concatenate([d[0] for d in done], axis=0).T.astype(BF16)
                lse_ref[r, rows, :] = jnp.concatenate([d[1] for d in done], axis=0).T
                done = []


def _t5_bucket(dist):
    max_exact = N_BUCKETS // 2
    n_log = N_BUCKETS - max_exact
    thresholds = [math.ceil(max_exact * (MAX_DISTANCE / max_exact) ** (k / n_log)) for k in range(1, n_log)]
    large = max_exact + sum((dist >= t).astype(jnp.int32) for t in thresholds)
    return jnp.where(dist < max_exact, dist, large)


def _bucket_table(window, dilation):
    blk = ATTN_BLOCK
    span = window // dilation
    rel = jnp.arange(blk, dtype=jnp.int32)[None, :] + blk - jnp.arange(2 * blk, dtype=jnp.int32)[:, None]
    valid = (rel >= 0) & (rel <= span)
    return jnp.where(valid, _t5_bucket(jnp.maximum(rel, 0) * dilation), -1)


def _attention_group(qkv, rel_bias_g, window, dilation, mt):
    assert window // dilation == ATTN_BLOCK
    bsz, _, n, w3 = qkv.shape
    rows = mt
    mt = min(rows, n)
    nres = rows // mt
    nq = mt // ATTN_BLOCK
    cur = pl.BlockSpec((None, nres, mt, w3), lambda b, r, i: (b, r, i, 0))
    prev = pl.BlockSpec((None, nres, ATTN_BLOCK, w3), lambda b, r, i: (b, r, jnp.maximum(i * nq - 1, 0), 0))
    ospec = pl.BlockSpec((None, nres, mt, GROUP_WIDTH), lambda b, r, i: (b, r, i, 0))
    return pl.pallas_call(
        functools.partial(_attn_kernel, nq=nq, nres=nres),
        grid=(bsz, dilation // nres, n // mt),
        in_specs=[_const_spec((2 * ATTN_BLOCK, ATTN_BLOCK)),
                  pl.BlockSpec(memory_space=pltpu.SMEM), cur, prev],
        out_specs=[ospec, ospec],
        out_shape=[jax.ShapeDtypeStruct((bsz, dilation, n, GROUP_WIDTH), BF16),
                   jax.ShapeDtypeStruct((bsz, dilation, n, GROUP_WIDTH), F32)],
        scratch_shapes=[pltpu.VMEM((HEADS_PER_GROUP // 2, 2 * ATTN_BLOCK, 2 * ATTN_BLOCK), F32),
                        pltpu.VMEM((nres, mt + ATTN_BLOCK, GROUP_WIDTH), BF16),
                        pltpu.VMEM((nres, GROUP_WIDTH, mt + ATTN_BLOCK), BF16)],
        compiler_params=_params(3),
        name=f"attn_d{dilation}",
    )(_bucket_table(window, dilation), rel_bias_g, qkv, qkv)


def _ssm_prep_kernel(logdt_ref, lr_ref, li_ref, bt_re_ref, bt_im_ref, c_re_ref, c_im_ref, dl_ref,
                     toep_ref, bst_ref, cst_ref, a_ref):
    cs = SSM_CHUNK
    dt = jnp.exp(jnp.full((1, SSM_STATE), logdt_ref[pl.program_id(0)], F32))
    lr, li = lr_ref[...], li_ref[...]
    mag = jnp.exp(lr * dt)
    ab_re, ab_im = mag * jnp.cos(li * dt), mag * jnp.sin(li * dt)
    den = lr * lr + li * li
    nr = ab_re - 1.0
    k_re = (nr * lr + ab_im * li) / den
    k_im = (ab_im * lr - nr * li) / den
    bt_re, bt_im = bt_re_ref[...], bt_im_ref[...]
    bb_re = k_re * bt_re - k_im * bt_im
    bb_im = k_re * bt_im + k_im * bt_re
    j = lax.broadcasted_iota(jnp.int32, (cs + SUBLANES, SSM_STATE), 0).astype(F32)
    pmag = jnp.exp(lr * dt * j)
    ang = li * dt * j
    p_re, p_im = pmag * jnp.cos(ang), pmag * jnp.sin(ang)
    c_re, c_im = c_re_ref[...], c_im_ref[...]
    cp_re = [c_re * p_re[i:i + 1] - c_im * p_im[i:i + 1] for i in range(cs + 1)]
    cp_im = [c_re * p_im[i:i + 1] + c_im * p_re[i:i + 1] for i in range(cs + 1)]
    nt = (((1,), (1,)), ((), ()))
    hi = lax.Precision.HIGHEST
    kcat = (lax.dot_general(bb_re, jnp.concatenate(cp_re[:cs], axis=0), nt, precision=hi,
                            preferred_element_type=F32)
            - lax.dot_general(bb_im, jnp.concatenate(cp_im[:cs], axis=0), nt, precision=hi,
                              preferred_element_type=F32))
    lane = lax.broadcasted_iota(jnp.int32, kcat.shape, 1)
    row = lax.broadcasted_iota(jnp.int32, kcat.shape, 0)
    dl = dl_ref[...]
    for s in range(cs):
        off = s * SSM_GROUP
        t_s = kcat if s == 0 else jnp.where(lane >= off, pltpu.roll(kcat, off, 1), 0.0)
        toep_ref[s] = (t_s + jnp.where(lane == off + row, dl, 0.0)).astype(BF16)
        pe_re, pe_im = p_re[cs - 1 - s:cs - s], p_im[cs - 1 - s:cs - s]
        st_re = pe_re * bb_re - pe_im * bb_im
        st_im = pe_re * bb_im + pe_im * bb_re
        bst_ref[s] = jnp.concatenate([st_re, st_im], axis=-1).astype(BF16)
    ro = jnp.concatenate([jnp.concatenate(cp_re[1:], axis=0),
                          -jnp.concatenate(cp_im[1:], axis=0)], axis=-1)
    ro_t = ro.T
    cst_ref[0] = ro_t[:SSM_STATE].astype(BF16)
    cst_ref[1] = ro_t[SSM_STATE:].astype(BF16)
    a_ref[...] = jnp.concatenate([p_re[cs:cs + 1], p_im[cs:cs + 1]], axis=0)


def _ssm_weights(lambda_re, lambda_im, log_dt, b_re, b_im, c_re, c_im, d_skip):
    f32 = F32
    cs, ng, no = SSM_CHUNK, SSM_GROUPS, SSM_GROUPS // SSM_OCT
    grp = lambda *shape: pl.BlockSpec((None,) + shape, lambda g: (g,) + (0,) * len(shape))
    in_oct = lambda *shape: pl.BlockSpec((None, shape[0], None) + shape[1:],
                                         lambda g: (g // SSM_OCT, 0, g % SSM_OCT) + (0,) * (len(shape) - 1))
    toep_c, bst_c, cst_c, a32 = pl.pallas_call(
        _ssm_prep_kernel,
        grid=(ng,),
        in_specs=[pl.BlockSpec(memory_space=pltpu.SMEM), grp(1, SSM_STATE), grp(1, SSM_STATE),
                  grp(SSM_GROUP, SSM_STATE), grp(SSM_GROUP, SSM_STATE),
                  grp(SSM_GROUP, SSM_STATE), grp(SSM_GROUP, SSM_STATE), grp(1, cs * SSM_GROUP)],
        out_specs=[in_oct(cs, SSM_GROUP, cs * SSM_GROUP), in_oct(cs, SSM_GROUP, 2 * SSM_STATE),
                   in_oct(2, SSM_STATE, cs * SSM_GROUP), grp(2, SSM_STATE)],
        out_shape=[jax.ShapeDtypeStruct((no, cs, SSM_OCT, SSM_GROUP, cs * SSM_GROUP), BF16),
                   jax.ShapeDtypeStruct((no, cs, SSM_OCT, SSM_GROUP, 2 * SSM_STATE), BF16),
                   jax.ShapeDtypeStruct((no, 2, SSM_OCT, SSM_STATE, cs * SSM_GROUP), BF16),
                   jax.ShapeDtypeStruct((ng, 2, SSM_STATE), f32)],
        compiler_params=_params(1),
        name="ssm_prep",
    )(log_dt.astype(f32), lambda_re.astype(f32)[:, None, :], lambda_im.astype(f32)[:, None, :],
      b_re.astype(f32).transpose(0, 2, 1), b_im.astype(f32).transpose(0, 2, 1),
      c_re.astype(f32), c_im.astype(f32), jnp.tile(d_skip.astype(f32), (1, cs))[:, None, :])
    avec = a32.reshape(no, SSM_OCT, 2, SSM_STATE).transpose(0, 2, 1, 3).reshape(no, 2, SSM_OCT * SSM_STATE)
    return (toep_c.reshape(no, cs * LANES, cs * SSM_GROUP),
            bst_c.reshape(no, cs * LANES, 2 * SSM_STATE),
            cst_c.reshape(no, 2 * SSM_OCT * SSM_STATE, cs * SSM_GROUP),
            avec)


def _expand_ssm_weights(toep_ref, bstc_ref, cstc_ref, wt_scr, bst_scr, cst_scr):
    tw = 2 * LANES
    sh_g, sh_n, sh_o = (v.bit_length() - 1 for v in (SSM_GROUP, SSM_STATE, SSM_OCT))
    r = lax.broadcasted_iota(jnp.int32, (tw, tw), 0)
    c = lax.broadcasted_iota(jnp.int32, (tw, tw), 1)
    col_grp = (c >> sh_g) & (SSM_OCT - 1)
    row_grp = (r >> sh_g) & (SSM_OCT - 1)
    for tp in range(SSM_CHUNK // 2):
        src_col = (2 * tp + (c >> (sh_g + sh_o))) * SSM_GROUP + (c & (SSM_GROUP - 1))
        e = jnp.where(r == src_col, 1.0, 0.0).astype(BF16)
        base = tp * (tp + 1) // 2
        for sp in range(tp + 1):
            x = jnp.dot(toep_ref[sp * tw:(sp + 1) * tw, :], e, preferred_element_type=F32)
            wt_scr[base + sp] = jnp.where(row_grp == col_grp, x, 0.0).astype(BF16)
        for j in range(2 * SSM_OCT * SSM_STATE // tw):
            x = jnp.dot(cstc_ref[j * tw:(j + 1) * tw, :], e, preferred_element_type=F32)
            row_grp_n = ((j * tw + r) >> sh_n) & (SSM_OCT - 1)
            cst_scr[j * tw:(j + 1) * tw, tp * tw:(tp + 1) * tw] = jnp.where(row_grp_n == col_grp, x, 0.0).astype(BF16)
    ns = 2 * SSM_OCT * SSM_STATE
    rb = lax.broadcasted_iota(jnp.int32, (2 * SSM_STATE, ns), 0)
    cb = lax.broadcasted_iota(jnp.int32, (2 * SSM_STATE, ns), 1)
    src_col_b = (cb >> (sh_n + sh_o)) * SSM_STATE + (cb & (SSM_STATE - 1))
    eb = jnp.where(rb == src_col_b, 1.0, 0.0).astype(BF16)
    rr = lax.broadcasted_iota(jnp.int32, (tw, ns), 0)
    cc = lax.broadcasted_iota(jnp.int32, (tw, ns), 1)
    keep = ((rr >> sh_g) & (SSM_OCT - 1)) == ((cc >> sh_n) & (SSM_OCT - 1))
    for j in range(SSM_CHUNK * LANES // tw):
        x = jnp.dot(bstc_ref[j * tw:(j + 1) * tw, :], eb, preferred_element_type=F32)
        bst_scr[j * tw:(j + 1) * tw, :] = jnp.where(keep, x, 0.0).astype(BF16)


def _ssm_kernel(u_ref, toep_ref, bstc_ref, cstc_ref, a_ref, y_ref, wt_ref, bst_ref, cst_ref, zz_scr, hp_scr,
                *, n_chunks, nb):
    @pl.when(pl.program_id(1) == 0)
    def _new_octet():
        _expand_ssm_weights(toep_ref, bstc_ref, cstc_ref, wt_ref, bst_ref, cst_ref)

    rows = nb * n_chunks
    nk = SSM_OCT * SSM_STATE // LANES
    u_t = [u_ref[:, s].reshape(rows, LANES) for s in range(SSM_CHUNK)]
    z = jnp.dot(jnp.concatenate(u_t, axis=-1), bst_ref[...], preferred_element_type=F32)
    pitch = _row_pitch(n_chunks)
    for k in range(2 * nk):
        for b in range(nb):
            zz_scr[k, b * pitch:b * pitch + n_chunks, :] = z[b * n_chunks:(b + 1) * n_chunks, k * LANES:(k + 1) * LANES]
    a_re = [jnp.broadcast_to(a_ref[0:1, k * LANES:(k + 1) * LANES], (nb, LANES)) for k in range(nk)]
    a_im = [jnp.broadcast_to(a_ref[1:2, k * LANES:(k + 1) * LANES], (nb, LANES)) for k in range(nk)]

    def step(c, carry):
        h_re, h_im = carry
        rows_c = pl.ds(c, nb, stride=pitch)
        new_re, new_im = [], []
        for k in range(nk):
            hp_scr[k, rows_c, :] = h_re[k]
            hp_scr[nk + k, rows_c, :] = h_im[k]
            new_re.append(a_re[k] * h_re[k] - a_im[k] * h_im[k] + zz_scr[k, rows_c, :])
            new_im.append(a_re[k] * h_im[k] + a_im[k] * h_re[k] + zz_scr[nk + k, rows_c, :])
        return tuple(new_re), tuple(new_im)

    zero = tuple(jnp.zeros((nb, LANES), F32) for _ in range(nk))
    lax.fori_loop(0, n_chunks, step, (zero, zero), unroll=4)
    hp = jnp.concatenate(
        [jnp.concatenate([hp_scr[k, b * pitch:b * pitch + n_chunks, :] for b in range(nb)], axis=0)
         for k in range(2 * nk)], axis=-1).astype(BF16)
    tw = 2 * LANES
    for tp in range(SSM_CHUNK // 2):
        base = tp * (tp + 1) // 2
        y = jnp.dot(hp, cst_ref[:, tp * tw:(tp + 1) * tw], preferred_element_type=F32)
        y = y + jnp.dot(jnp.concatenate(u_t[:2 * (tp + 1)], axis=-1),
                        wt_ref[base:base + tp + 1].reshape((tp + 1) * tw, tw), preferred_element_type=F32)
        y = jax.nn.gelu(y).astype(BF16)
        y_ref[:, 2 * tp] = y[:, :LANES].reshape(nb, n_chunks, LANES)
        y_ref[:, 2 * tp + 1] = y[:, LANES:].reshape(nb, n_chunks, LANES)


def _ssm(u16, weights, nb):
    toep_c, bst_c, cst_c, avec = weights
    bsz, cs, nc, width = u16.shape
    oct_spec = lambda a: pl.BlockSpec((None,) + a.shape[1:], lambda o, b: (o,) + (0,) * (a.ndim - 1))
    io_spec = pl.BlockSpec((nb, cs, nc, LANES), lambda o, b: (b, 0, 0, o))
    n_state = 2 * SSM_OCT * SSM_STATE
    n_pairs = (cs // 2) * (cs // 2 + 1) // 2
    return pl.pallas_call(
        functools.partial(_ssm_kernel, n_chunks=nc, nb=nb),
        grid=(width // LANES, bsz // nb),
        in_specs=[io_spec, oct_spec(toep_c), oct_spec(bst_c), oct_spec(cst_c), oct_spec(avec)],
        out_specs=io_spec,
        out_shape=jax.ShapeDtypeStruct(u16.shape, BF16),
        scratch_shapes=[pltpu.VMEM((n_pairs, 2 * LANES, 2 * LANES), BF16),
                        pltpu.VMEM((cs * LANES, n_state), BF16),
                        pltpu.VMEM((n_state, cs * LANES), BF16),
                        pltpu.VMEM((n_state // LANES, nb * _row_pitch(nc), LANES), F32),
                        pltpu.VMEM((n_state // LANES, nb * _row_pitch(nc), LANES), F32)],
        compiler_params=_params(2),
        name="ssm",
    )(u16, toep_c, bst_c, cst_c, avec)


def _post_norm(v, g, b):
    mu = jnp.mean(v, axis=-1, keepdims=True)
    vc = v - mu
    var = jnp.mean(vc * vc, axis=-1, keepdims=True)
    return vc * lax.rsqrt(var + LN_EPS / ALPHA ** 2) * g + b


def _merge_kernel(o0_ref, o1_ref, o2_ref, l0_ref, l1_ref, l2_ref, ys_ref, g_ref, x_ref,
                  wglu_ref, wsp_ref, wap_ref, wout_ref, lng_ref, lnb_ref, h_ref, o_scr, l_scr, y_scr):
    tm = x_ref.shape[0]

    def token_order(ref, scr, d):
        if d == 1:
            return ref[0].astype(F32)
        nk = ref.shape[-1] // LANES
        pitch = _row_pitch(d)
        for r in range(d):
            v = ref[r].astype(F32)
            for k in range(nk):
                scr[k, pl.ds(r, tm // d, stride=pitch), :] = v[:, k * LANES:(k + 1) * LANES]
        if pitch == d:
            return jnp.concatenate([scr[k, 0:tm, :] for k in range(nk)], axis=-1)
        return jnp.concatenate(
            [jnp.concatenate([scr[k, g * pitch:g * pitch + d, :] for g in range(tm // d)], axis=0)
             for k in range(nk)], axis=-1)

    ls, outs = [], []
    for gi, (o_ref, l_ref) in enumerate(((o0_ref, l0_ref), (o1_ref, l1_ref), (o2_ref, l2_ref))):
        d = DILATION_PATTERNS[gi][1]
        outs.append(token_order(o_ref, o_scr.at[gi], d))
        ls.append(token_order(l_ref, l_scr.at[gi], d))
    mx = jnp.maximum(jnp.maximum(ls[0], ls[1]), ls[2])
    es = [jnp.exp(l - mx) for l in ls]
    num = es[0] * outs[0] + es[1] * outs[1] + es[2] * outs[2]
    y_attn = (num / (es[0] + es[1] + es[2])).astype(BF16)
    ys = token_order(ys_ref, y_scr, SSM_CHUNK).astype(BF16)
    cw = GROUP_WIDTH
    y_ssm = []
    for c in range(SSM_WIDTH // cw):
        a = jnp.dot(ys, wglu_ref[:, c * cw:(c + 1) * cw], preferred_element_type=F32)
        b = jnp.dot(ys, wglu_ref[:, SSM_WIDTH + c * cw:SSM_WIDTH + (c + 1) * cw], preferred_element_type=F32)
        y_ssm.append((a * _sigmoid(b)).astype(BF16))
    y_ssm = jnp.concatenate(y_ssm, axis=-1)
    gated = []
    for c in range(D_MODEL // cw):
        pa = jnp.dot(y_ssm, wsp_ref[:, c * cw:(c + 1) * cw], preferred_element_type=F32)
        pb = jnp.dot(y_attn, wap_ref[:, c * cw:(c + 1) * cw], preferred_element_type=F32)
        g_ssm = g_ref[:, c * cw:(c + 1) * cw].astype(F32)
        g_attn = g_ref[:, D_MODEL + c * cw:D_MODEL + (c + 1) * cw].astype(F32)
        gated.append((g_ssm * pa + g_attn * pb).astype(BF16))
    gated = jnp.concatenate(gated, axis=-1)
    mix = jnp.dot(gated, wout_ref[...], preferred_element_type=F32)
    h_ref[...] = _post_norm(x_ref[...] + mix, lng_ref[...], lnb_ref[...])


def _merge(outs, lses, ys, gates, x2, w_glu, w_sp, w_ap, w_out, ln_g, ln_b, seqlen, tm):
    t = x2.shape[0]
    tiles = seqlen // tm
    stage_rows = max(tm // d * _row_pitch(d) for d in [d for _, d in DILATION_PATTERNS] + [SSM_CHUNK])
    row = lambda w: pl.BlockSpec((tm, w), lambda i: (i, 0))
    res_spec = lambda d, w: pl.BlockSpec((None, d, tm // d, w), lambda i: (i // tiles, 0, i % tiles, 0))
    res_specs = [res_spec(d, GROUP_WIDTH) for _, d in DILATION_PATTERNS]
    return pl.pallas_call(
        _merge_kernel,
        grid=(t // tm,),
        in_specs=res_specs + res_specs + [res_spec(SSM_CHUNK, SSM_WIDTH), row(2 * D_MODEL), row(D_MODEL),
                  _const_spec(w_glu.shape), _const_spec(w_sp.shape), _const_spec(w_ap.shape),
                  _const_spec(w_out.shape), _const_spec((1, D_MODEL)), _const_spec((1, D_MODEL))],
        out_specs=row(D_MODEL),
        out_shape=jax.ShapeDtypeStruct((t, D_MODEL), F32),
        scratch_shapes=[pltpu.VMEM((N_GROUPS, GROUP_WIDTH // LANES, stage_rows, LANES), F32),
                        pltpu.VMEM((N_GROUPS, GROUP_WIDTH // LANES, stage_rows, LANES), F32),
                        pltpu.VMEM((SSM_WIDTH // LANES, stage_rows, LANES), F32)],
        compiler_params=_params(1),
        name="merge_ln1",
    )(*outs, *lses, ys, gates, x2, w_glu, w_sp, w_ap, w_out, ln_g, ln_b)


FF_CHUNK = 1024


def _ffn_kernel(h_ref, wup_ref, wdn_ref, lng_ref, lnb_ref, o_ref):
    h = h_ref[...]
    hb = h.astype(BF16)
    acts = []
    for c in range(D_FF // FF_CHUNK):
        lo, hi = c * FF_CHUNK, (c + 1) * FF_CHUNK
        up = jnp.dot(hb, wup_ref[:, lo:hi], preferred_element_type=F32)
        acts.append(jnp.square(jnp.maximum(up, 0.0)).astype(BF16))
    ff = jnp.dot(jnp.concatenate(acts, axis=-1), wdn_ref[...], preferred_element_type=F32)
    o_ref[...] = _post_norm(h + ff, lng_ref[...], lnb_ref[...])


def _ffn(h1, w_up, w_down, ln_g, ln_b, tm):
    t = h1.shape[0]
    row = pl.BlockSpec((tm, D_MODEL), lambda i: (i, 0))
    return pl.pallas_call(
        _ffn_kernel,
        grid=(t // tm,),
        in_specs=[row, _const_spec(w_up.shape), _const_spec(w_down.shape),
                  _const_spec((1, D_MODEL)), _const_spec((1, D_MODEL))],
        out_specs=row,
        out_shape=jax.ShapeDtypeStruct((t, D_MODEL), F32),
        compiler_params=_params(1),
        name="ffn_ln2",
    )(h1, w_up, w_down, ln_g, ln_b)


def _permute_w_in(w):
    aw = ATTN_WIDTH
    cols = []
    for gi in range(N_GROUPS):
        lo, hi = gi * GROUP_WIDTH, (gi + 1) * GROUP_WIDTH
        cols += [w[:, lo:hi] * (HEAD_DIM ** -0.5), w[:, aw + lo:aw + hi], w[:, 2 * aw + lo:2 * aw + hi]]
    cols.append(w[:, 3 * aw:])
    return jnp.concatenate(cols, axis=1).astype(BF16)


def _layer(h2, bsz, seqlen, l, w_in, b_gate, lambda_re, lambda_im, log_dt, ssm_b_re, ssm_b_im,
           ssm_c_re, ssm_c_im, ssm_d, w_glu, w_ssm_proj, rel_bias, w_attn_proj, w_out,
           ln1_g, ln1_b, w_up, w_down, ln2_g, ln2_b, tm=1024, attn_mt=2048, ffn_tm=1024):
    qkv0, qkv1, qkv2, u, gates = _in_proj(h2, _permute_w_in(w_in[l]), b_gate[l][None, :], bsz, seqlen, tm)
    outs, lses = [], []
    for gi, ((window, dilation), qkv) in enumerate(zip(DILATION_PATTERNS, (qkv0, qkv1, qkv2))):
        rb = rel_bias[:, gi * HEADS_PER_GROUP:(gi + 1) * HEADS_PER_GROUP].astype(F32)
        o, s = _attention_group(qkv, rb, window, dilation, attn_mt)
        outs.append(o)
        lses.append(s)
    ssm_w = _ssm_weights(lambda_re[l], lambda_im[l], log_dt[l], ssm_b_re[l], ssm_b_im[l],
                         ssm_c_re[l], ssm_c_im[l], ssm_d[l])
    ys = _ssm(u, ssm_w, nb=4)
    h1 = _merge(outs, lses, ys, gates, h2, w_glu[l].astype(BF16), w_ssm_proj[l].astype(BF16),
                w_attn_proj[l].astype(BF16), (w_out[l] * (1.0 / ALPHA)).astype(BF16),
                ln1_g[l][None, :], ln1_b[l][None, :], seqlen, tm)
    return _ffn(h1, w_up[l].astype(BF16), (w_down[l] * (1.0 / ALPHA)).astype(BF16),
                ln2_g[l][None, :], ln2_b[l][None, :], ffn_tm)


def kernel(x, w_in, b_gate, lambda_re, lambda_im, log_dt, ssm_b_re, ssm_b_im, ssm_c_re, ssm_c_im,
           ssm_d, w_glu, w_ssm_proj, rel_bias, w_attn_proj, w_out, ln1_g, ln1_b, w_up, w_down,
           ln2_g, ln2_b):
    bsz, seqlen, d = x.shape
    h = x.reshape(bsz * seqlen, d)
    for l in range(w_in.shape[0]):
        h = _layer(h, bsz, seqlen, l, w_in, b_gate, lambda_re, lambda_im, log_dt, ssm_b_re, ssm_b_im,
                   ssm_c_re, ssm_c_im, ssm_d, w_glu, w_ssm_proj, rel_bias, w_attn_proj, w_out,
                   ln1_g, ln1_b, w_up, w_down, ln2_g, ln2_b)
    return h.reshape(bsz, seqlen, d)
```

```python
import functools
import math

import jax
import jax.numpy as jnp
from jax import lax
from jax.experimental import pallas as pl
from jax.experimental.pallas import tpu as pltpu

F32 = jnp.float32
BF16 = jnp.bfloat16

D_MODEL = 1024
HEAD_DIM = 64
HEADS_PER_GROUP = 4
GROUP_WIDTH = HEADS_PER_GROUP * HEAD_DIM
DILATION_PATTERNS = ((128, 1), (512, 4), (2048, 16))
N_GROUPS = len(DILATION_PATTERNS)
ATTN_WIDTH = N_GROUPS * GROUP_WIDTH
N_BUCKETS = 32
MAX_DISTANCE = 2048
SSM_WIDTH = 512
SSM_GROUP = 16
SSM_GROUPS = 32
SSM_STATE = 64
D_FF = 4 * D_MODEL
DEPTH = 1
ALPHA = (2.0 * DEPTH) ** 0.25
LN_EPS = 1e-5
NEG_INF = -1e30

ATTN_BLOCK = 128
ATTN_SKEW = (4, 7)
SSM_CHUNK = 16
SSM_OCT = 8
LANES = 128
SUBLANES = 8
VMEM_LIMIT_BYTES = 56 * 1024 * 1024


def _params(n_axes):
    return pltpu.CompilerParams(dimension_semantics=("arbitrary",) * n_axes,
                                vmem_limit_bytes=VMEM_LIMIT_BYTES)


def _sigmoid(z):
    return 0.5 * jnp.tanh(0.5 * z) + 0.5


def _row_pitch(stride):
    return stride + SUBLANES if stride % (2 * SUBLANES) == 0 else stride


def _const_spec(shape):
    nd = len(shape)
    return pl.BlockSpec(shape, lambda *_: (0,) * nd, pipeline_mode=pl.Buffered(1))


def _in_proj_kernel(x_ref, w_ref, bg_ref, qkv0_ref, qkv1_ref, qkv2_ref, u_ref, g_ref, scr):
    xb = x_ref[...].astype(BF16)
    tm = xb.shape[0]

    def mm(lo, hi):
        return jnp.dot(xb, w_ref[:, lo:hi], preferred_element_type=F32)

    def emit(ref, res, d, col0):
        width = res.shape[1]
        if d == 1:
            ref[0, :, col0:col0 + width] = res.astype(BF16)
            return
        slot0 = next_slot[0]
        next_slot[0] += width // LANES
        pitch = _row_pitch(d)
        for k in range(width // LANES):
            tile = res[:, k * LANES:(k + 1) * LANES]
            if pitch == d:
                scr[slot0 + k, 0:tm, :] = tile
            else:
                for g in range(tm // d):
                    scr[slot0 + k, g * pitch:g * pitch + d, :] = tile[g * d:(g + 1) * d]
        for r in range(d):
            for k in range(width // LANES):
                col = col0 + k * LANES
                ref[r, :, col:col + LANES] = scr[slot0 + k, pl.ds(r, tm // d, stride=pitch), :].astype(BF16)

    next_slot = [0]

    gw3 = 3 * GROUP_WIDTH
    for gi, ref in enumerate((qkv0_ref, qkv1_ref, qkv2_ref)):
        for c in range(3):
            lo = gi * gw3 + c * GROUP_WIDTH
            emit(ref, mm(lo, lo + GROUP_WIDTH), DILATION_PATTERNS[gi][1], c * GROUP_WIDTH)
    base = N_GROUPS * gw3
    emit(u_ref, mm(base, base + SSM_WIDTH), SSM_CHUNK, 0)
    base += SSM_WIDTH
    for c in range(2 * D_MODEL // GROUP_WIDTH):
        lo, hi = c * GROUP_WIDTH, (c + 1) * GROUP_WIDTH
        z = mm(base + lo, base + hi) + bg_ref[:, lo:hi]
        g_ref[:, lo:hi] = _sigmoid(z).astype(BF16)


def _in_proj(x2, w_perm, b_gate, bsz, seqlen, tm):
    t = x2.shape[0]
    n_in = w_perm.shape[1]
    tiles = seqlen // tm
    row = lambda w: pl.BlockSpec((tm, w), lambda i: (i, 0))
    w3 = 3 * GROUP_WIDTH
    dils = [d for _, d in DILATION_PATTERNS]
    n_stage = (sum(d > 1 for d in dils) * w3 + SSM_WIDTH) // LANES
    res_spec = lambda d, w:pl.BlockSpec((None, d, tm // d, w), lambda i: (i // tiles, 0, i % tiles, 0))
    return pl.pallas_call(
        _in_proj_kernel,
        grid=(t // tm,),
        in_specs=[row(D_MODEL), _const_spec((D_MODEL, n_in)), _const_spec((1, 2 * D_MODEL))],
        out_specs=[res_spec(d, w3) for d in dils] + [res_spec(SSM_CHUNK, SSM_WIDTH), row(2 * D_MODEL)],
        out_shape=[jax.ShapeDtypeStruct((bsz, d, seqlen // d, w3), BF16) for d in dils]
        + [jax.ShapeDtypeStruct((bsz, SSM_CHUNK, seqlen // SSM_CHUNK, SSM_WIDTH), BF16),
           jax.ShapeDtypeStruct((t, 2 * D_MODEL), BF16)],
        scratch_shapes=[pltpu.VMEM((n_stage, max(tm // d * _row_pitch(d) for d in dils + [SSM_CHUNK]), LANES), F32)],
        compiler_params=_params(1),
        name="in_proj",
    )(x2, w_perm, b_gate)


def _attn_kernel(bucket_ref, relb_ref, cur_ref, prev_ref, out_ref, lse_ref, bias_scr, k_scr, vt_scr, *, nq, nres):
    blk = ATTN_BLOCK
    first = (pl.program_id(0) == 0) & (pl.program_id(1) == 0) & (pl.program_id(2) == 0)

    @pl.when(first)
    def _build_bias():
        bucket = bucket_ref[...]
        for h in range(HEADS_PER_GROUP):
            acc = jnp.full(bucket.shape, NEG_INF, F32)
            for bkt in range(N_BUCKETS):
                acc = jnp.where(bucket == bkt, relb_ref[bkt, h], acc)
            bias_scr[h // 2, :, (h % 2) * blk:(h % 2 + 1) * blk] = acc

    row = lax.broadcasted_iota(jnp.int32, (2 * blk, blk), 0)
    row_head = row // HEAD_DIM
    keep_first = (row >= blk) | (pl.program_id(2) > 0)
    for r in range(nres):
        k_scr[r, 0:blk, :] = prev_ref[r, :, GROUP_WIDTH:2 * GROUP_WIDTH]
        k_scr[r, blk:, :] = cur_ref[r, :, GROUP_WIDTH:2 * GROUP_WIDTH]
        vt_scr[r, :, 0:blk] = prev_ref[r, :, 2 * GROUP_WIDTH:3 * GROUP_WIDTH].T
        for j in range(nq):
            vt_scr[r, :, (j + 1) * blk:(j + 2) * blk] = cur_ref[r, j * blk:(j + 1) * blk,
                                                                 2 * GROUP_WIDTH:3 * GROUP_WIDTH].T

    npair = HEADS_PER_GROUP // 2
    units = [(r, j, hp) for r in range(nres) for j in range(nq) for hp in range(npair)]
    qts = {}

    def scores(r, j, hp):
        if (r, j) not in qts:
            qts[(r, j)] = cur_ref[r, j * blk:(j + 1) * blk, 0:GROUP_WIDTH].T
        qt = qts[(r, j)]
        k2 = k_scr[r, j * blk:(j + 2) * blk, :]
        qh = jnp.concatenate([jnp.where(row_head == 2 * hp + i, qt, jnp.zeros_like(qt)) for i in range(2)], axis=1)
        return jnp.dot(k2, qh, preferred_element_type=F32)

    def softmax(r, j, hp, s2):
        ps, ms, ls = [], [], []
        for i in range(2):
            lanes = slice(i * blk, (i + 1) * blk)
            s = s2[:, lanes] + bias_scr[hp, :, lanes]
            if j == 0:
                s = jnp.where(keep_first, s, NEG_INF)
            m = jnp.max(s, axis=0, keepdims=True)
            p = jnp.exp(s - m)
            ls.append(jnp.sum(p, axis=0, keepdims=True))
            ps.append(p.astype(BF16))
            ms.append(m)
        return jnp.concatenate(ps, axis=1), jnp.concatenate(ms, axis=1), jnp.concatenate(ls, axis=1)

    def values(r, j, hp, p, m, l):
        vt = vt_scr[r, 2 * hp * HEAD_DIM:2 * (hp + 1) * HEAD_DIM, j * blk:(j + 2) * blk]
        ot = jnp.dot(vt, p, preferred_element_type=F32) * (1.0 / l)
        lse = jnp.broadcast_to(m + jnp.log(l), (HEAD_DIM, 2 * blk))
        return [(ot[i * HEAD_DIM:(i + 1) * HEAD_DIM, i * blk:(i + 1) * blk], lse[:, i * blk:(i + 1) * blk])
                for i in range(2)]

    raw, soft, done = {}, {}, []
    lag_soft, lag_val = ATTN_SKEW
    for step in range(len(units) + lag_val):
        if step < len(units):
            raw[step] = scores(*units[step])
        if lag_soft <= step < len(units) + lag_soft:
            soft[step - lag_soft] = softmax(*units[step - lag_soft], raw.pop(step - lag_soft))
        if step >= lag_val:
            i = step - lag_val
            r, j, hp = units[i]
            done += values(r, j, hp, *soft.pop(i))
            if hp == npair - 1:
                rows = slice(j * blk, (j + 1) * blk)
                out_ref[r, rows, :] = jnp.concatenate([d[0] for d in done], axis=0).T.astype(BF16)
                lse_ref[r, rows, :] = jnp.concatenate([d[1] for d in done], axis=0).T
                done = []


def _t5_bucket(dist):
    max_exact = N_BUCKETS // 2
    n_log = N_BUCKETS - max_exact
    thresholds = [math.ceil(max_exact * (MAX_DISTANCE / max_exact) ** (k / n_log)) for k in range(1, n_log)]
    large = max_exact + sum((dist >= t).astype(jnp.int32) for t in thresholds)
    return jnp.where(dist < max_exact, dist, large)


def _bucket_table(window, dilation):
    blk = ATTN_BLOCK
    span = window // dilation
    rel = jnp.arange(blk, dtype=jnp.int32)[None, :] + blk - jnp.arange(2 * blk, dtype=jnp.int32)[:, None]
    valid = (rel >= 0) & (rel <= span)
    return jnp.where(valid, _t5_bucket(jnp.maximum(rel, 0) * dilation), -1)


def _attention_group(qkv, rel_bias_g, window, dilation, mt):
    assert window // dilation == ATTN_BLOCK
    bsz, _, n, w3 = qkv.shape
    rows = mt
    mt = min(rows, n)
    nres = rows // mt
    nq = mt // ATTN_BLOCK
    cur = pl.BlockSpec((None, nres, mt, w3), lambda b, r, i: (b, r, i, 0))
    prev = pl.BlockSpec((None, nres, ATTN_BLOCK, w3), lambda b, r, i: (b, r, jnp.maximum(i * nq - 1, 0), 0))
    ospec = pl.BlockSpec((None, nres, mt, GROUP_WIDTH), lambda b, r, i: (b, r, i, 0))
    return pl.pallas_call(
        functools.partial(_attn_kernel, nq=nq, nres=nres),
        grid=(bsz, dilation // nres, n // mt),
        in_specs=[_const_spec((2 * ATTN_BLOCK, ATTN_BLOCK)),
                  pl.BlockSpec(memory_space=pltpu.SMEM), cur, prev],
        out_specs=[ospec, ospec],
        out_shape=[jax.ShapeDtypeStruct((bsz, dilation, n, GROUP_WIDTH), BF16),
                   jax.ShapeDtypeStruct((bsz, dilation, n, GROUP_WIDTH), F32)],
        scratch_shapes=[pltpu.VMEM((HEADS_PER_GROUP // 2, 2 * ATTN_BLOCK, 2 * ATTN_BLOCK), F32),
                        pltpu.VMEM((nres, mt + ATTN_BLOCK, GROUP_WIDTH), BF16),
                        pltpu.VMEM((nres, GROUP_WIDTH, mt + ATTN_BLOCK), BF16)],
        compiler_params=_params(3),
        name=f"attn_d{dilation}",
    )(_bucket_table(window, dilation), rel_bias_g, qkv, qkv)


def _ssm_prep_kernel(logdt_ref, lr_ref, li_ref, bt_re_ref, bt_im_ref, c_re_ref, c_im_ref, dl_ref,
                     toep_ref, bst_ref, cst_ref, a_ref):
    for g in range(SSM_OCT):
        _ssm_prep_group(g, logdt_ref[pl.program_id(0) * SSM_OCT + g], lr_ref, li_ref, bt_re_ref, bt_im_ref,
                        c_re_ref, c_im_ref, dl_ref, toep_ref, bst_ref, cst_ref, a_ref)


def _ssm_prep_group(g, log_dt, lr_ref, li_ref, bt_re_ref, bt_im_ref, c_re_ref, c_im_ref, dl_ref,
                    toep_ref, bst_ref, cst_ref, a_ref):
    cs = SSM_CHUNK
    dt = jnp.exp(jnp.full((1, SSM_STATE), log_dt, F32))
    lr, li = lr_ref[g], li_ref[g]
    mag = jnp.exp(lr * dt)
    ab_re, ab_im = mag * jnp.cos(li * dt), mag * jnp.sin(li * dt)
    den = lr * lr + li * li
    nr = ab_re - 1.0
    k_re = (nr * lr + ab_im * li) / den
    k_im = (ab_im * lr - nr * li) / den
    bt_re, bt_im = bt_re_ref[g], bt_im_ref[g]
    bb_re = k_re * bt_re - k_im * bt_im
    bb_im = k_re * bt_im + k_im * bt_re
    j = lax.broadcasted_iota(jnp.int32, (cs + SUBLANES, SSM_STATE), 0).astype(F32)
    pmag = jnp.exp(lr * dt * j)
    ang = li * dt * j
    p_re, p_im = pmag * jnp.cos(ang), pmag * jnp.sin(ang)
    c_re, c_im = c_re_ref[g], c_im_ref[g]
    cp_re = [c_re * p_re[i:i + 1] - c_im * p_im[i:i + 1] for i in range(cs + 1)]
    cp_im = [c_re * p_im[i:i + 1] + c_im * p_re[i:i + 1] for i in range(cs + 1)]
    nt = (((1,), (1,)), ((), ()))
    hi = lax.Precision.HIGHEST
    kcat = (lax.dot_general(bb_re, jnp.concatenate(cp_re[:cs], axis=0), nt, precision=hi,
                            preferred_element_type=F32)
            - lax.dot_general(bb_im, jnp.concatenate(cp_im[:cs], axis=0), nt, precision=hi,
                              preferred_element_type=F32))
    lane = lax.broadcasted_iota(jnp.int32, kcat.shape, 1)
    row = lax.broadcasted_iota(jnp.int32, kcat.shape, 0)
    dl = dl_ref[g]
    for s in range(cs):
        off = s * SSM_GROUP
        t_s = kcat if s == 0 else jnp.where(lane >= off, pltpu.roll(kcat, off, 1), 0.0)
        toep_ref[s, g] = (t_s + jnp.where(lane == off + row, dl, 0.0)).astype(BF16)
        pe_re, pe_im = p_re[cs - 1 - s:cs - s], p_im[cs - 1 - s:cs - s]
        st_re = pe_re * bb_re - pe_im * bb_im
        st_im = pe_re * bb_im + pe_im * bb_re
        bst_ref[s, g] = jnp.concatenate([st_re, st_im], axis=-1).astype(BF16)
    ro = jnp.concatenate([jnp.concatenate(cp_re[1:], axis=0),
                          -jnp.concatenate(cp_im[1:], axis=0)], axis=-1)
    ro_t = ro.T
    cst_ref[0, g] = ro_t[:SSM_STATE].astype(BF16)
    cst_ref[1, g] = ro_t[SSM_STATE:].astype(BF16)
    a_ref[g] = jnp.concatenate([p_re[cs:cs + 1], p_im[cs:cs + 1]], axis=0)


def _ssm_weights(lambda_re, lambda_im, log_dt, b_re, b_im, c_re, c_im, d_skip):
    f32 = F32
    cs, ng, no = SSM_CHUNK, SSM_GROUPS, SSM_GROUPS // SSM_OCT
    grp = lambda *shape: pl.BlockSpec((SSM_OCT,) + shape, lambda o: (o,) + (0,) * len(shape))
    in_oct = lambda *shape: pl.BlockSpec((None, shape[0], SSM_OCT) + shape[1:],
                                         lambda o: (o,) + (0,) * (len(shape) + 1))
    toep_c, bst_c, cst_c, a32 = pl.pallas_call(
        _ssm_prep_kernel,
        grid=(no,),
        in_specs=[pl.BlockSpec(memory_space=pltpu.SMEM), grp(1, SSM_STATE), grp(1, SSM_STATE),
                  grp(SSM_GROUP, SSM_STATE), grp(SSM_GROUP, SSM_STATE),
                  grp(SSM_GROUP, SSM_STATE), grp(SSM_GROUP, SSM_STATE), grp(1, cs * SSM_GROUP)],
        out_specs=[in_oct(cs, SSM_GROUP, cs * SSM_GROUP), in_oct(cs, SSM_GROUP, 2 * SSM_STATE),
                   in_oct(2, SSM_STATE, cs * SSM_GROUP), grp(2, SSM_STATE)],
        out_shape=[jax.ShapeDtypeStruct((no, cs, SSM_OCT, SSM_GROUP, cs * SSM_GROUP), BF16),
                   jax.ShapeDtypeStruct((no, cs, SSM_OCT, SSM_GROUP, 2 * SSM_STATE), BF16),
                   jax.ShapeDtypeStruct((no, 2, SSM_OCT, SSM_STATE, cs * SSM_GROUP), BF16),
                   jax.ShapeDtypeStruct((ng, 2, SSM_STATE), f32)],
        compiler_params=_params(1),
        name="ssm_prep",
    )(log_dt.astype(f32), lambda_re.astype(f32)[:, None, :], lambda_im.astype(f32)[:, None, :],
      b_re.astype(f32).transpose(0, 2, 1), b_im.astype(f32).transpose(0, 2, 1),
      c_re.astype(f32), c_im.astype(f32), jnp.tile(d_skip.astype(f32), (1, cs))[:, None, :])
    avec = a32.reshape(no, SSM_OCT, 2, SSM_STATE).transpose(0, 2, 1, 3).reshape(no, 2, SSM_OCT * SSM_STATE)
    return (toep_c.reshape(no, cs * LANES, cs * SSM_GROUP),
            bst_c.reshape(no, cs * LANES, 2 * SSM_STATE),
            cst_c.reshape(no, 2 * SSM_OCT * SSM_STATE, cs * SSM_GROUP),
            avec)


def _expand_ssm_weights(toep_ref, bstc_ref, cstc_ref, wt_scr, bst_scr, cst_scr):
    tw = 2 * LANES
    sh_g, sh_n, sh_o = (v.bit_length() - 1 for v in (SSM_GROUP, SSM_STATE, SSM_OCT))
    r = lax.broadcasted_iota(jnp.int32, (tw, tw), 0)
    c = lax.broadcasted_iota(jnp.int32, (tw, tw), 1)
    col_grp = (c >> sh_g) & (SSM_OCT - 1)
    row_grp = (r >> sh_g) & (SSM_OCT - 1)
    for tp in range(SSM_CHUNK // 2):
        src_col = (2 * tp + (c >> (sh_g + sh_o))) * SSM_GROUP + (c & (SSM_GROUP - 1))
        e = jnp.where(r == src_col, 1.0, 0.0).astype(BF16)
        base = tp * (tp + 1) // 2
        for sp in range(tp + 1):
            x = jnp.dot(toep_ref[sp * tw:(sp + 1) * tw, :], e, preferred_element_type=F32)
            wt_scr[base + sp] = jnp.where(row_grp == col_grp, x, 0.0).astype(BF16)
        for j in range(2 * SSM_OCT * SSM_STATE // tw):
            x = jnp.dot(cstc_ref[j * tw:(j + 1) * tw, :], e, preferred_element_type=F32)
            row_grp_n = ((j * tw + r) >> sh_n) & (SSM_OCT - 1)
            cst_scr[j * tw:(j + 1) * tw, tp * tw:(tp + 1) * tw] = jnp.where(row_grp_n == col_grp, x, 0.0).astype(BF16)
    ns = 2 * SSM_OCT * SSM_STATE
    rb = lax.broadcasted_iota(jnp.int32, (2 * SSM_STATE, ns), 0)
    cb = lax.broadcasted_iota(jnp.int32, (2 * SSM_STATE, ns), 1)
    src_col_b = (cb >> (sh_n + sh_o)) * SSM_STATE + (cb & (SSM_STATE - 1))
    eb = jnp.where(rb == src_col_b, 1.0, 0.0).astype(BF16)
    rr = lax.broadcasted_iota(jnp.int32, (tw, ns), 0)
    cc = lax.broadcasted_iota(jnp.int32, (tw, ns), 1)
    keep = ((rr >> sh_g) & (SSM_OCT - 1)) == ((cc >> sh_n) & (SSM_OCT - 1))
    for j in range(SSM_CHUNK * LANES // tw):
        x = jnp.dot(bstc_ref[j * tw:(j + 1) * tw, :], eb, preferred_element_type=F32)
        bst_scr[j * tw:(j + 1) * tw, :] = jnp.where(keep, x, 0.0).astype(BF16)


def _ssm_kernel(u_ref, toep_ref, bstc_ref, cstc_ref, a_ref, y_ref, wt_ref, bst_ref, cst_ref, zz_scr, hp_scr,
                *, n_chunks, nb):
    @pl.when(pl.program_id(1) == 0)
    def _new_octet():
        _expand_ssm_weights(toep_ref, bstc_ref, cstc_ref, wt_ref, bst_ref, cst_ref)

    rows = nb * n_chunks
    nk = SSM_OCT * SSM_STATE // LANES
    u_t = [u_ref[:, s].reshape(rows, LANES) for s in range(SSM_CHUNK)]
    z = jnp.dot(jnp.concatenate(u_t, axis=-1), bst_ref[...], preferred_element_type=F32)
    pitch = _row_pitch(n_chunks)
    for k in range(2 * nk):
        for b in range(nb):
            zz_scr[k, b * pitch:b * pitch + n_chunks, :] = z[b * n_chunks:(b + 1) * n_chunks, k * LANES:(k + 1) * LANES]
    a_re = [jnp.broadcast_to(a_ref[0:1, k * LANES:(k + 1) * LANES], (nb, LANES)) for k in range(nk)]
    a_im = [jnp.broadcast_to(a_ref[1:2, k * LANES:(k + 1) * LANES], (nb, LANES)) for k in range(nk)]

    def step(c, carry):
        h_re, h_im = carry
        rows_c = pl.ds(c, nb, stride=pitch)
        new_re, new_im = [], []
        for k in range(nk):
            hp_scr[k, rows_c, :] = h_re[k]
            hp_scr[nk + k, rows_c, :] = h_im[k]
            new_re.append(a_re[k] * h_re[k] - a_im[k] * h_im[k] + zz_scr[k, rows_c, :])
            new_im.append(a_re[k] * h_im[k] + a_im[k] * h_re[k] + zz_scr[nk + k, rows_c, :])
        return tuple(new_re), tuple(new_im)

    zero = tuple(jnp.zeros((nb, LANES), F32) for _ in range(nk))
    lax.fori_loop(0, n_chunks, step, (zero, zero), unroll=4)
    hp = jnp.concatenate(
        [jnp.concatenate([hp_scr[k, b * pitch:b * pitch + n_chunks, :] for b in range(nb)], axis=0)
         for k in range(2 * nk)], axis=-1).astype(BF16)
    tw = 2 * LANES
    for tp in range(SSM_CHUNK // 2):
        base = tp * (tp + 1) // 2
        y = jnp.dot(hp, cst_ref[:, tp * tw:(tp + 1) * tw], preferred_element_type=F32)
        y = y + jnp.dot(jnp.concatenate(u_t[:2 * (tp + 1)], axis=-1),
                        wt_ref[base:base + tp + 1].reshape((tp + 1) * tw, tw), preferred_element_type=F32)
        y = jax.nn.gelu(y).astype(BF16)
        y_ref[:, 2 * tp] = y[:, :LANES].reshape(nb, n_chunks, LANES)
        y_ref[:, 2 * tp + 1] = y[:, LANES:].reshape(nb, n_chunks, LANES)


def _ssm(u16, weights, nb):
    toep_c, bst_c, cst_c, avec = weights
    bsz, cs, nc, width = u16.shape
    oct_spec = lambda a: pl.BlockSpec((None,) + a.shape[1:], lambda o, b: (o,) + (0,) * (a.ndim - 1))
    io_spec = pl.BlockSpec((nb, cs, nc, LANES), lambda o, b: (b, 0, 0, o))
    n_state = 2 * SSM_OCT * SSM_STATE
    n_pairs = (cs // 2) * (cs // 2 + 1) // 2
    return pl.pallas_call(
        functools.partial(_ssm_kernel, n_chunks=nc, nb=nb),
        grid=(width // LANES, bsz // nb),
        in_specs=[io_spec, oct_spec(toep_c), oct_spec(bst_c), oct_spec(cst_c), oct_spec(avec)],
        out_specs=io_spec,
        out_shape=jax.ShapeDtypeStruct(u16.shape, BF16),
        scratch_shapes=[pltpu.VMEM((n_pairs, 2 * LANES, 2 * LANES), BF16),
                        pltpu.VMEM((cs * LANES, n_state), BF16),
                        pltpu.VMEM((n_state, cs * LANES), BF16),
                        pltpu.VMEM((n_state // LANES, nb * _row_pitch(nc), LANES), F32),
                        pltpu.VMEM((n_state // LANES, nb * _row_pitch(nc), LANES), F32)],
        compiler_params=_params(2),
        name="ssm",
    )(u16, toep_c, bst_c, cst_c, avec)


def _layer_norm(v, g, b):
    mu = jnp.mean(v, axis=-1, keepdims=True)
    vc = v - mu
    var = jnp.mean(vc * vc, axis=-1, keepdims=True)
    return vc * lax.rsqrt(var + LN_EPS) * g + b


def _merge_kernel(o0_ref, o1_ref, o2_ref, l0_ref, l1_ref, l2_ref, ys_ref, g_ref, x_ref,
                  wglu_ref, wsp_ref, wap_ref, wout_ref, lng_ref, lnb_ref, h_ref, o_scr, l_scr, y_scr):
    tm = x_ref.shape[0]

    def token_order(ref, scr, d):
        if d == 1:
            return ref[0].astype(F32)
        nk = ref.shape[-1] // LANES
        pitch = _row_pitch(d)
        for r in range(d):
            v = ref[r].astype(F32)
            for k in range(nk):
                scr[k, pl.ds(r, tm // d, stride=pitch), :] = v[:, k * LANES:(k + 1) * LANES]
        if pitch == d:
            return jnp.concatenate([scr[k, 0:tm, :] for k in range(nk)], axis=-1)
        return jnp.concatenate(
            [jnp.concatenate([scr[k, g * pitch:g * pitch + d, :] for g in range(tm // d)], axis=0)
             for k in range(nk)], axis=-1)

    ls, outs = [], []
    for gi, (o_ref, l_ref) in enumerate(((o0_ref, l0_ref), (o1_ref, l1_ref), (o2_ref, l2_ref))):
        d = DILATION_PATTERNS[gi][1]
        outs.append(token_order(o_ref, o_scr.at[gi], d))
        ls.append(token_order(l_ref, l_scr.at[gi], d))
    mx = jnp.maximum(jnp.maximum(ls[0], ls[1]), ls[2])
    es = [jnp.exp(l - mx) for l in ls]
    num = es[0] * outs[0] + es[1] * outs[1] + es[2] * outs[2]
    y_attn = (num / (es[0] + es[1] + es[2])).astype(BF16)
    ys = token_order(ys_ref, y_scr, SSM_CHUNK).astype(BF16)
    cw = GROUP_WIDTH
    y_ssm = []
    for c in range(SSM_WIDTH // cw):
        a = jnp.dot(ys, wglu_ref[:, c * cw:(c + 1) * cw], preferred_element_type=F32)
        b = jnp.dot(ys, wglu_ref[:, SSM_WIDTH + c * cw:SSM_WIDTH + (c + 1) * cw], preferred_element_type=F32)
        y_ssm.append((a * _sigmoid(b)).astype(BF16))
    y_ssm = jnp.concatenate(y_ssm, axis=-1)
    gated = []
    for c in range(D_MODEL // cw):
        pa = jnp.dot(y_ssm, wsp_ref[:, c * cw:(c + 1) * cw], preferred_element_type=F32)
        pb = jnp.dot(y_attn, wap_ref[:, c * cw:(c + 1) * cw], preferred_element_type=F32)
        g_ssm = g_ref[:, c * cw:(c + 1) * cw].astype(F32)
        g_attn = g_ref[:, D_MODEL + c * cw:D_MODEL + (c + 1) * cw].astype(F32)
        gated.append((g_ssm * pa + g_attn * pb).astype(BF16))
    gated = jnp.concatenate(gated, axis=-1)
    mix = jnp.dot(gated, wout_ref[...], preferred_element_type=F32)
    h_ref[...] = _layer_norm(ALPHA * x_ref[...] + mix, lng_ref[...], lnb_ref[...])


def _merge(outs, lses, ys, gates, x2, w_glu, w_sp, w_ap, w_out, ln_g, ln_b, seqlen, tm):
    t = x2.shape[0]
    tiles = seqlen // tm
    stage_rows = max(tm // d * _row_pitch(d) for d in [d for _, d in DILATION_PATTERNS] + [SSM_CHUNK])
    row = lambda w: pl.BlockSpec((tm, w), lambda i: (i, 0))
    res_spec = lambda d, w: pl.BlockSpec((None, d, tm // d, w), lambda i: (i // tiles, 0, i % tiles, 0))
    res_specs = [res_spec(d, GROUP_WIDTH) for _, d in DILATION_PATTERNS]
    return pl.pallas_call(
        _merge_kernel,
        grid=(t // tm,),
        in_specs=res_specs + res_specs + [res_spec(SSM_CHUNK, SSM_WIDTH), row(2 * D_MODEL), row(D_MODEL),
                  _const_spec(w_glu.shape), _const_spec(w_sp.shape), _const_spec(w_ap.shape),
                  _const_spec(w_out.shape), _const_spec((1, D_MODEL)), _const_spec((1, D_MODEL))],
        out_specs=row(D_MODEL),
        out_shape=jax.ShapeDtypeStruct((t, D_MODEL), F32),
        scratch_shapes=[pltpu.VMEM((N_GROUPS, GROUP_WIDTH // LANES, stage_rows, LANES), F32),
                        pltpu.VMEM((N_GROUPS, GROUP_WIDTH // LANES, stage_rows, LANES), F32),
                        pltpu.VMEM((SSM_WIDTH // LANES, stage_rows, LANES), F32)],
        compiler_params=_params(1),
        name="merge_ln1",
    )(*outs, *lses, ys, gates, x2, w_glu, w_sp, w_ap, w_out, ln_g, ln_b)


FF_CHUNK = 1024


def _ffn_kernel(h_ref, wup_ref, wdn_ref, lng_ref, lnb_ref, o_ref):
    h = h_ref[...]
    hb = h.astype(BF16)
    acts = []
    for c in range(D_FF // FF_CHUNK):
        lo, hi = c * FF_CHUNK, (c + 1) * FF_CHUNK
        up = jnp.dot(hb, wup_ref[:, lo:hi], preferred_element_type=F32)
        acts.append(jnp.square(jnp.maximum(up, 0.0)).astype(BF16))
    ff = jnp.dot(jnp.concatenate(acts, axis=-1), wdn_ref[...], preferred_element_type=F32)
    o_ref[...] = _layer_norm(ALPHA * h + ff, lng_ref[...], lnb_ref[...])


def _ffn(h1, w_up, w_down, ln_g, ln_b, tm):
    t = h1.shape[0]
    row = pl.BlockSpec((tm, D_MODEL), lambda i: (i, 0))
    return pl.pallas_call(
        _ffn_kernel,
        grid=(t // tm,),
        in_specs=[row, _const_spec(w_up.shape), _const_spec(w_down.shape),
                  _const_spec((1, D_MODEL)), _const_spec((1, D_MODEL))],
        out_specs=row,
        out_shape=jax.ShapeDtypeStruct((t, D_MODEL), F32),
        compiler_params=_params(1),
        name="ffn_ln2",
    )(h1, w_up, w_down, ln_g, ln_b)


def _permute_w_in(w):
    aw = ATTN_WIDTH
    cols = []
    for gi in range(N_GROUPS):
        lo, hi = gi * GROUP_WIDTH, (gi + 1) * GROUP_WIDTH
        cols += [w[:, lo:hi] * (HEAD_DIM ** -0.5), w[:, aw + lo:aw + hi], w[:, 2 * aw + lo:2 * aw + hi]]
    cols.append(w[:, 3 * aw:])
    return jnp.concatenate(cols, axis=1).astype(BF16)


def _layer(h2, bsz, seqlen, l, w_in, b_gate, lambda_re, lambda_im, log_dt, ssm_b_re, ssm_b_im,
           ssm_c_re, ssm_c_im, ssm_d, w_glu, w_ssm_proj, rel_bias, w_attn_proj, w_out,
           ln1_g, ln1_b, w_up, w_down, ln2_g, ln2_b, tm=1024, attn_mt=2048, ffn_tm=1024):
    qkv0, qkv1, qkv2, u, gates = _in_proj(h2, _permute_w_in(w_in[l]), b_gate[l][None, :], bsz, seqlen, tm)
    outs, lses = [], []
    for gi, ((window, dilation), qkv) in enumerate(zip(DILATION_PATTERNS, (qkv0, qkv1, qkv2))):
        rb = rel_bias[:, gi * HEADS_PER_GROUP:(gi + 1) * HEADS_PER_GROUP].astype(F32)
        o, s = _attention_group(qkv, rb, window, dilation, attn_mt)
        outs.append(o)
        lses.append(s)
    ssm_w = _ssm_weights(lambda_re[l], lambda_im[l], log_dt[l], ssm_b_re[l], ssm_b_im[l],
                         ssm_c_re[l], ssm_c_im[l], ssm_d[l])
    ys = _ssm(u, ssm_w, nb=4)
    h1 = _merge(outs, lses, ys, gates, h2, w_glu[l].astype(BF16), w_ssm_proj[l].astype(BF16),
                w_attn_proj[l].astype(BF16), w_out[l].astype(BF16),
                ln1_g[l][None, :], ln1_b[l][None, :], seqlen, tm)
    return _ffn(h1, w_up[l].astype(BF16), w_down[l].astype(BF16), ln2_g[l][None, :], ln2_b[l][None, :], ffn_tm)


def kernel(x, w_in, b_gate, lambda_re, lambda_im, log_dt, ssm_b_re, ssm_b_im, ssm_c_re, ssm_c_im,
           ssm_d, w_glu, w_ssm_proj, rel_bias, w_attn_proj, w_out, ln1_g, ln1_b, w_up, w_down,
           ln2_g, ln2_b):
    bsz, seqlen, d = x.shape
    h = x.reshape(bsz * seqlen, d)
    for l in range(w_in.shape[0]):
        h = _layer(h, bsz, seqlen, l, w_in, b_gate, lambda_re, lambda_im, log_dt, ssm_b_re, ssm_b_im,
                   ssm_c_re, ssm_c_im, ssm_d, w_glu, w_ssm_proj, rel_bias, w_attn_proj, w_out,
                   ln1_g, ln1_b, w_up, w_down, ln2_g, ln2_b)
    return h.reshape(bsz, seqlen, d)
```

```python
import functools
import math

import jax
import jax.numpy as jnp
from jax import lax
from jax.experimental import pallas as pl
from jax.experimental.pallas import tpu as pltpu

F32 = jnp.float32
BF16 = jnp.bfloat16

D_MODEL = 1024
HEAD_DIM = 64
HEADS_PER_GROUP = 4
GROUP_WIDTH = HEADS_PER_GROUP * HEAD_DIM
DILATION_PATTERNS = ((128, 1), (512, 4), (2048, 16))
N_GROUPS = len(DILATION_PATTERNS)
ATTN_WIDTH = N_GROUPS * GROUP_WIDTH
N_BUCKETS = 32
MAX_DISTANCE = 2048
SSM_WIDTH = 512
SSM_GROUP = 16
SSM_GROUPS = 32
SSM_STATE = 64
D_FF = 4 * D_MODEL
DEPTH = 1
ALPHA = (2.0 * DEPTH) ** 0.25
LN_EPS = 1e-5
NEG_INF = -1e30

ATTN_BLOCK = 128
ATTN_SKEW = (4, 7)
SSM_CHUNK = 16
SSM_OCT = 8
LANES = 128
SUBLANES = 8
VMEM_LIMIT_BYTES = 56 * 1024 * 1024


def _params(n_axes):
    return pltpu.CompilerParams(dimension_semantics=("arbitrary",) * n_axes,
                                vmem_limit_bytes=VMEM_LIMIT_BYTES)


def _sigmoid(z):
    return 0.5 * jnp.tanh(0.5 * z) + 0.5


def _row_pitch(stride):
    return stride + SUBLANES if stride % (2 * SUBLANES) == 0 else stride


def _const_spec(shape):
    nd = len(shape)
    return pl.BlockSpec(shape, lambda *_: (0,) * nd, pipeline_mode=pl.Buffered(1))


def _in_proj_kernel(x_ref, w_ref, bg_ref, qkv0_ref, qkv1_ref, qkv2_ref, u_ref, g_ref, scr):
    xb = x_ref[...].astype(BF16)
    tm = xb.shape[0]

    def mm(lo, hi):
        return jnp.dot(xb, w_ref[:, lo:hi], preferred_element_type=F32)

    def emit(ref, res, d, col0):
        width = res.shape[1]
        if d == 1:
            ref[0, :, col0:col0 + width] = res.astype(BF16)
            return
        slot0 = next_slot[0]
        next_slot[0] += width // LANES
        pitch = _row_pitch(d)
        for k in range(width // LANES):
            tile = res[:, k * LANES:(k + 1) * LANES]
            if pitch == d:
                scr[slot0 + k, 0:tm, :] = tile
            else:
                for g in range(tm // d):
                    scr[slot0 + k, g * pitch:g * pitch + d, :] = tile[g * d:(g + 1) * d]
        for r in range(d):
            for k in range(width // LANES):
                col = col0 + k * LANES
                ref[r, :, col:col + LANES] = scr[slot0 + k, pl.ds(r, tm // d, stride=pitch), :].astype(BF16)

    next_slot = [0]

    gw3 = 3 * GROUP_WIDTH
    for gi, ref in enumerate((qkv0_ref, qkv1_ref, qkv2_ref)):
        for c in range(3):
            lo = gi * gw3 + c * GROUP_WIDTH
            emit(ref, mm(lo, lo + GROUP_WIDTH), DILATION_PATTERNS[gi][1], c * GROUP_WIDTH)
    base = N_GROUPS * gw3
    emit(u_ref, mm(base, base + SSM_WIDTH), SSM_CHUNK, 0)
    base += SSM_WIDTH
    for c in range(2 * D_MODEL // GROUP_WIDTH):
        lo, hi = c * GROUP_WIDTH, (c + 1) * GROUP_WIDTH
        z = mm(base + lo, base + hi) + bg_ref[:, lo:hi]
        g_ref[:, lo:hi] = _sigmoid(z).astype(BF16)


def _in_proj(x2, w_perm, b_gate, bsz, seqlen, tm):
    t = x2.shape[0]
    n_in = w_perm.shape[1]
    tiles = seqlen // tm
    row = lambda w: pl.BlockSpec((tm, w), lambda i: (i, 0))
    w3 = 3 * GROUP_WIDTH
    dils = [d for _, d in DILATION_PATTERNS]
    n_stage = (sum(d > 1 for d in dils) * w3 + SSM_WIDTH) // LANES
    res_spec = lambda d, w:pl.BlockSpec((None, d, tm // d, w), lambda i: (i // tiles, 0, i % tiles, 0))
    return pl.pallas_call(
        _in_proj_kernel,
        grid=(t // tm,),
        in_specs=[row(D_MODEL), _const_spec((D_MODEL, n_in)), _const_spec((1, 2 * D_MODEL))],
        out_specs=[res_spec(d, w3) for d in dils] + [res_spec(SSM_CHUNK, SSM_WIDTH), row(2 * D_MODEL)],
        out_shape=[jax.ShapeDtypeStruct((bsz, d, seqlen // d, w3), BF16) for d in dils]
        + [jax.ShapeDtypeStruct((bsz, SSM_CHUNK, seqlen // SSM_CHUNK, SSM_WIDTH), BF16),
           jax.ShapeDtypeStruct((t, 2 * D_MODEL), BF16)],
        scratch_shapes=[pltpu.VMEM((n_stage, max(tm // d * _row_pitch(d) for d in dils + [SSM_CHUNK]), LANES), F32)],
        compiler_params=_params(1),
        name="in_proj",
    )(x2, w_perm, b_gate)


def _attn_kernel(bucket_ref, relb_ref, cur_ref, prev_ref, out_ref, lse_ref, bias_scr, k_scr, vt_scr, *, nq, nres):
    blk = ATTN_BLOCK
    first = (pl.program_id(0) == 0) & (pl.program_id(1) == 0) & (pl.program_id(2) == 0)

    @pl.when(first)
    def _build_bias():
        bucket = bucket_ref[...]
        for h in range(HEADS_PER_GROUP):
            acc = jnp.full(bucket.shape, NEG_INF, F32)
            for bkt in range(N_BUCKETS):
                acc = jnp.where(bucket == bkt, relb_ref[bkt, h], acc)
            bias_scr[h // 2, :, (h % 2) * blk:(h % 2 + 1) * blk] = acc

    row = lax.broadcasted_iota(jnp.int32, (2 * blk, blk), 0)
    row_head = row // HEAD_DIM
    keep_first = (row >= blk) | (pl.program_id(2) > 0)
    for r in range(nres):
        k_scr[r, 0:blk, :] = prev_ref[r, :, GROUP_WIDTH:2 * GROUP_WIDTH]
        k_scr[r, blk:, :] = cur_ref[r, :, GROUP_WIDTH:2 * GROUP_WIDTH]
        vt_scr[r, :, 0:blk] = prev_ref[r, :, 2 * GROUP_WIDTH:3 * GROUP_WIDTH].T
        for j in range(nq):
            vt_scr[r, :, (j + 1) * blk:(j + 2) * blk] = cur_ref[r, j * blk:(j + 1) * blk,
                                                                 2 * GROUP_WIDTH:3 * GROUP_WIDTH].T

    npair = HEADS_PER_GROUP // 2
    units = [(r, j, hp) for r in range(nres) for j in range(nq) for hp in range(npair)]
    qts = {}

    def scores(r, j, hp):
        if (r, j) not in qts:
            qts[(r, j)] = cur_ref[r, j * blk:(j + 1) * blk, 0:GROUP_WIDTH].T
        qt = qts[(r, j)]
        k2 = k_scr[r, j * blk:(j + 2) * blk, :]
        qh = jnp.concatenate([jnp.where(row_head == 2 * hp + i, qt, jnp.zeros_like(qt)) for i in range(2)], axis=1)
        return jnp.dot(k2, qh, preferred_element_type=F32)

    def softmax(r, j, hp, s2):
        ps, ms, ls = [], [], []
        for i in range(2):
            lanes = slice(i * blk, (i + 1) * blk)
            s = s2[:, lanes] + bias_scr[hp, :, lanes]
            if j == 0:
                s = jnp.where(keep_first, s, NEG_INF)
            m = jnp.max(s, axis=0, keepdims=True)
            p = jnp.exp(s - m)
            ls.append(jnp.sum(p, axis=0, keepdims=True))
            ps.append(p.astype(BF16))
            ms.append(m)
        return jnp.concatenate(ps, axis=1), jnp.concatenate(ms, axis=1), jnp.concatenate(ls, axis=1)

    def values(r, j, hp, p, m, l):
        vt = vt_scr[r, 2 * hp * HEAD_DIM:2 * (hp + 1) * HEAD_DIM, j * blk:(j + 2) * blk]
        ot = jnp.dot(vt, p, preferred_element_type=F32) * (1.0 / l)
        lse = jnp.broadcast_to(m + jnp.log(l), (HEAD_DIM, 2 * blk))
        return [(ot[i * HEAD_DIM:(i + 1) * HEAD_DIM, i * blk:(i + 1) * blk], lse[:, i * blk:(i + 1) * blk])
                for i in range(2)]

    raw, soft, done = {}, {}, []
    lag_soft, lag_val = ATTN_SKEW
    for step in range(len(units) + lag_val):
        if step < len(units):
            raw[step] = scores(*units[step])
        if lag_soft <= step < len(units) + lag_soft:
            soft[step - lag_soft] = softmax(*units[step - lag_soft], raw.pop(step - lag_soft))
        if step >= lag_val:
            i = step - lag_val
            r, j, hp = units[i]
            done += values(r, j, hp, *soft.pop(i))
            if hp == npair - 1:
                rows = slice(j * blk, (j + 1) * blk)
                out_ref[r, rows, :] = jnp.concatenate([d[0] for d in done], axis=0).T.astype(BF16)
                lse_ref[r, rows, :] = jnp.concatenate([d[1] for d in done], axis=0).T
                done = []


def _t5_bucket(dist):
    max_exact = N_BUCKETS // 2
    n_log = N_BUCKETS - max_exact
    thresholds = [math.ceil(max_exact * (MAX_DISTANCE / max_exact) ** (k / n_log)) for k in range(1, n_log)]
    large = max_exact + sum((dist >= t).astype(jnp.int32) for t in thresholds)
    return jnp.where(dist < max_exact, dist, large)


def _bucket_table(window, dilation):
    blk = ATTN_BLOCK
    span = window // dilation
    rel = jnp.arange(blk, dtype=jnp.int32)[None, :] + blk - jnp.arange(2 * blk, dtype=jnp.int32)[:, None]
    valid = (rel >= 0) & (rel <= span)
    return jnp.where(valid, _t5_bucket(jnp.maximum(rel, 0) * dilation), -1)


def _attention_group(qkv, rel_bias_g, window, dilation, mt):
    assert window // dilation == ATTN_BLOCK
    bsz, _, n, w3 = qkv.shape
    rows = mt
    mt = min(rows, n)
    nres = rows // mt
    nq = mt // ATTN_BLOCK
    cur = pl.BlockSpec((None, nres, mt, w3), lambda b, r, i: (b, r, i, 0))
    prev = pl.BlockSpec((None, nres, ATTN_BLOCK, w3), lambda b, r, i: (b, r, jnp.maximum(i * nq - 1, 0), 0))
    ospec = pl.BlockSpec((None, nres, mt, GROUP_WIDTH), lambda b, r, i: (b, r, i, 0))
    return pl.pallas_call(
        functools.partial(_attn_kernel, nq=nq, nres=nres),
        grid=(bsz, dilation // nres, n // mt),
        in_specs=[_const_spec((2 * ATTN_BLOCK, ATTN_BLOCK)),
                  pl.BlockSpec(memory_space=pltpu.SMEM), cur, prev],
        out_specs=[ospec, ospec],
        out_shape=[jax.ShapeDtypeStruct((bsz, dilation, n, GROUP_WIDTH), BF16),
                   jax.ShapeDtypeStruct((bsz, dilation, n, GROUP_WIDTH), F32)],
        scratch_shapes=[pltpu.VMEM((HEADS_PER_GROUP // 2, 2 * ATTN_BLOCK, 2 * ATTN_BLOCK), F32),
                        pltpu.VMEM((nres, mt + ATTN_BLOCK, GROUP_WIDTH), BF16),
                        pltpu.VMEM((nres, GROUP_WIDTH, mt + ATTN_BLOCK), BF16)],
        compiler_params=_params(3),
        name=f"attn_d{dilation}",
    )(_bucket_table(window, dilation), rel_bias_g, qkv, qkv)


def _ssm_prep_kernel(logdt_ref, lr_ref, li_ref, bt_re_ref, bt_im_ref, c_re_ref, c_im_ref, dl_ref,
                     toep_ref, bst_ref, cst_ref, a_ref):
    for g in range(SSM_OCT):
        _ssm_prep_group(g, logdt_ref[pl.program_id(0) * SSM_OCT + g], lr_ref, li_ref, bt_re_ref, bt_im_ref,
                        c_re_ref, c_im_ref, dl_ref, toep_ref, bst_ref, cst_ref, a_ref)


def _ssm_prep_group(g, log_dt, lr_ref, li_ref, bt_re_ref, bt_im_ref, c_re_ref, c_im_ref, dl_ref,
                    toep_ref, bst_ref, cst_ref, a_ref):
    cs = SSM_CHUNK
    dt = jnp.exp(jnp.full((1, SSM_STATE), log_dt, F32))
    lr, li = lr_ref[g], li_ref[g]
    mag = jnp.exp(lr * dt)
    ab_re, ab_im = mag * jnp.cos(li * dt), mag * jnp.sin(li * dt)
    den = lr * lr + li * li
    nr = ab_re - 1.0
    k_re = (nr * lr + ab_im * li) / den
    k_im = (ab_im * lr - nr * li) / den
    bt_re, bt_im = bt_re_ref[g], bt_im_ref[g]
    bb_re = k_re * bt_re - k_im * bt_im
    bb_im = k_re * bt_im + k_im * bt_re
    j = lax.broadcasted_iota(jnp.int32, (cs + SUBLANES, SSM_STATE), 0).astype(F32)
    pmag = jnp.exp(lr * dt * j)
    ang = li * dt * j
    p_re, p_im = pmag * jnp.cos(ang), pmag * jnp.sin(ang)
    c_re, c_im = c_re_ref[g], c_im_ref[g]
    cp_re = [c_re * p_re[i:i + 1] - c_im * p_im[i:i + 1] for i in range(cs + 1)]
    cp_im = [c_re * p_im[i:i + 1] + c_im * p_re[i:i + 1] for i in range(cs + 1)]
    nt = (((1,), (1,)), ((), ()))
    hi = lax.Precision.HIGHEST
    kcat = (lax.dot_general(bb_re, jnp.concatenate(cp_re[:cs], axis=0), nt, precision=hi,
                            preferred_element_type=F32)
            - lax.dot_general(bb_im, jnp.concatenate(cp_im[:cs], axis=0), nt, precision=hi,
                              preferred_element_type=F32))
    lane = lax.broadcasted_iota(jnp.int32, kcat.shape, 1)
    row = lax.broadcasted_iota(jnp.int32, kcat.shape, 0)
    dl = dl_ref[g]
    for s in range(cs):
        off = s * SSM_GROUP
        t_s = kcat if s == 0 else jnp.where(lane >= off, pltpu.roll(kcat, off, 1), 0.0)
        toep_ref[s, g] = (t_s + jnp.where(lane == off + row, dl, 0.0)).astype(BF16)
        pe_re, pe_im = p_re[cs - 1 - s:cs - s], p_im[cs - 1 - s:cs - s]
        st_re = pe_re * bb_re - pe_im * bb_im
        st_im = pe_re * bb_im + pe_im * bb_re
        bst_ref[s, g] = jnp.concatenate([st_re, st_im], axis=-1).astype(BF16)
    ro = jnp.concatenate([jnp.concatenate(cp_re[1:], axis=0),
                          -jnp.concatenate(cp_im[1:], axis=0)], axis=-1)
    ro_t = ro.T
    cst_ref[0, g] = ro_t[:SSM_STATE].astype(BF16)
    cst_ref[1, g] = ro_t[SSM_STATE:].astype(BF16)
    a_ref[g] = jnp.concatenate([p_re[cs:cs + 1], p_im[cs:cs + 1]], axis=0)


def _ssm_weights(lambda_re, lambda_im, log_dt, b_re, b_im, c_re, c_im, d_skip):
    f32 = F32
    cs, ng, no = SSM_CHUNK, SSM_GROUPS, SSM_GROUPS // SSM_OCT
    grp = lambda *shape: pl.BlockSpec((SSM_OCT,) + shape, lambda o: (o,) + (0,) * len(shape))
    in_oct = lambda *shape: pl.BlockSpec((None, shape[0], SSM_OCT) + shape[1:],
                                         lambda o: (o,) + (0,) * (len(shape) + 1))
    toep_c, bst_c, cst_c, a32 = pl.pallas_call(
        _ssm_prep_kernel,
        grid=(no,),
        in_specs=[pl.BlockSpec(memory_space=pltpu.SMEM), grp(1, SSM_STATE), grp(1, SSM_STATE),
                  grp(SSM_GROUP, SSM_STATE), grp(SSM_GROUP, SSM_STATE),
                  grp(SSM_GROUP, SSM_STATE), grp(SSM_GROUP, SSM_STATE), grp(1, cs * SSM_GROUP)],
        out_specs=[in_oct(cs, SSM_GROUP, cs * SSM_GROUP), in_oct(cs, SSM_GROUP, 2 * SSM_STATE),
                   in_oct(2, SSM_STATE, cs * SSM_GROUP), grp(2, SSM_STATE)],
        out_shape=[jax.ShapeDtypeStruct((no, cs, SSM_OCT, SSM_GROUP, cs * SSM_GROUP), BF16),
                   jax.ShapeDtypeStruct((no, cs, SSM_OCT, SSM_GROUP, 2 * SSM_STATE), BF16),
                   jax.ShapeDtypeStruct((no, 2, SSM_OCT, SSM_STATE, cs * SSM_GROUP), BF16),
                   jax.ShapeDtypeStruct((ng, 2, SSM_STATE), f32)],
        compiler_params=_params(1),
        name="ssm_prep",
    )(log_dt.astype(f32), lambda_re.astype(f32)[:, None, :], lambda_im.astype(f32)[:, None, :],
      b_re.astype(f32).transpose(0, 2, 1), b_im.astype(f32).transpose(0, 2, 1),
      c_re.astype(f32), c_im.astype(f32), jnp.tile(d_skip.astype(f32), (1, cs))[:, None, :])
    avec = a32.reshape(no, SSM_OCT, 2, SSM_STATE).transpose(0, 2, 1, 3).reshape(no, 2, SSM_OCT * SSM_STATE)
    return (toep_c.reshape(no, cs * LANES, cs * SSM_GROUP),
            bst_c.reshape(no, cs * LANES, 2 * SSM_STATE),
            cst_c.reshape(no, 2 * SSM_OCT * SSM_STATE, cs * SSM_GROUP),
            avec)


def _expand_ssm_weights(toep_ref, bstc_ref, cstc_ref, wt_scr, bst_scr, cst_scr):
    tw = 2 * LANES
    sh_g, sh_n, sh_o = (v.bit_length() - 1 for v in (SSM_GROUP, SSM_STATE, SSM_OCT))
    r = lax.broadcasted_iota(jnp.int32, (tw, tw), 0)
    c = lax.broadcasted_iota(jnp.int32, (tw, tw), 1)
    col_grp = (c >> sh_g) & (SSM_OCT - 1)
    row_grp = (r >> sh_g) & (SSM_OCT - 1)
    for tp in range(SSM_CHUNK // 2):
        src_col = (2 * tp + (c >> (sh_g + sh_o))) * SSM_GROUP + (c & (SSM_GROUP - 1))
        e = jnp.where(r == src_col, 1.0, 0.0).astype(BF16)
        base = tp * (tp + 1) // 2
        for sp in range(tp + 1):
            x = jnp.dot(toep_ref[sp * tw:(sp + 1) * tw, :], e, preferred_element_type=F32)
            wt_scr[base + sp] = jnp.where(row_grp == col_grp, x, 0.0).astype(BF16)
        for j in range(2 * SSM_OCT * SSM_STATE // tw):
            x = jnp.dot(cstc_ref[j * tw:(j + 1) * tw, :], e, preferred_element_type=F32)
            row_grp_n = ((j * tw + r) >> sh_n) & (SSM_OCT - 1)
            cst_scr[j * tw:(j + 1) * tw, tp * tw:(tp + 1) * tw] = jnp.where(row_grp_n == col_grp, x, 0.0).astype(BF16)
    ns = 2 * SSM_OCT * SSM_STATE
    rb = lax.broadcasted_iota(jnp.int32, (2 * SSM_STATE, ns), 0)
    cb = lax.broadcasted_iota(jnp.int32, (2 * SSM_STATE, ns), 1)
    src_col_b = (cb >> (sh_n + sh_o)) * SSM_STATE + (cb & (SSM_STATE - 1))
    eb = jnp.where(rb == src_col_b, 1.0, 0.0).astype(BF16)
    rr = lax.broadcasted_iota(jnp.int32, (tw, ns), 0)
    cc = lax.broadcasted_iota(jnp.int32, (tw, ns), 1)
    keep = ((rr >> sh_g) & (SSM_OCT - 1)) == ((cc >> sh_n) & (SSM_OCT - 1))
    for j in range(SSM_CHUNK * LANES // tw):
        x = jnp.dot(bstc_ref[j * tw:(j + 1) * tw, :], eb, preferred_element_type=F32)
        bst_scr[j * tw:(j + 1) * tw, :] = jnp.where(keep, x, 0.0).astype(BF16)


def _ssm_kernel(u_ref, toep_ref, bstc_ref, cstc_ref, a_ref, y_ref, wt_ref, bst_ref, cst_ref, zz_scr, hp_scr,
                *, n_chunks, nb):
    @pl.when(pl.program_id(1) == 0)
    def _new_octet():
        _expand_ssm_weights(toep_ref, bstc_ref, cstc_ref, wt_ref, bst_ref, cst_ref)

    rows = nb * n_chunks
    nk = SSM_OCT * SSM_STATE // LANES
    u_t = [u_ref[:, s].reshape(rows, LANES) for s in range(SSM_CHUNK)]
    z = jnp.dot(jnp.concatenate(u_t, axis=-1), bst_ref[...], preferred_element_type=F32)
    pitch = _row_pitch(n_chunks)
    for k in range(2 * nk):
        for b in range(nb):
            zz_scr[k, b * pitch:b * pitch + n_chunks, :] = z[b * n_chunks:(b + 1) * n_chunks, k * LANES:(k + 1) * LANES]
    a_re = [jnp.broadcast_to(a_ref[0:1, k * LANES:(k + 1) * LANES], (nb, LANES)) for k in range(nk)]
    a_im = [jnp.broadcast_to(a_ref[1:2, k * LANES:(k + 1) * LANES], (nb, LANES)) for k in range(nk)]

    def step(c, carry):
        h_re, h_im = carry
        rows_c = pl.ds(c, nb, stride=pitch)
        new_re, new_im = [], []
        for k in range(nk):
            hp_scr[k, rows_c, :] = h_re[k]
            hp_scr[nk + k, rows_c, :] = h_im[k]
            new_re.append(a_re[k] * h_re[k] - a_im[k] * h_im[k] + zz_scr[k, rows_c, :])
            new_im.append(a_re[k] * h_im[k] + a_im[k] * h_re[k] + zz_scr[nk + k, rows_c, :])
        return tuple(new_re), tuple(new_im)

    zero = tuple(jnp.zeros((nb, LANES), F32) for _ in range(nk))
    lax.fori_loop(0, n_chunks, step, (zero, zero), unroll=4)
    hp = jnp.concatenate(
        [jnp.concatenate([hp_scr[k, b * pitch:b * pitch + n_chunks, :] for b in range(nb)], axis=0)
         for k in range(2 * nk)], axis=-1).astype(BF16)
    tw = 2 * LANES
    for tp in range(SSM_CHUNK // 2):
        base = tp * (tp + 1) // 2
        y = jnp.dot(hp, cst_ref[:, tp * tw:(tp + 1) * tw], preferred_element_type=F32)
        y = y + jnp.dot(jnp.concatenate(u_t[:2 * (tp + 1)], axis=-1),
                        wt_ref[base:base + tp + 1].reshape((tp + 1) * tw, tw), preferred_element_type=F32)
        y = jax.nn.gelu(y).astype(BF16)
        y_ref[:, 2 * tp] = y[:, :LANES].reshape(nb, n_chunks, LANES)
        y_ref[:, 2 * tp + 1] = y[:, LANES:].reshape(nb, n_chunks, LANES)


def _ssm(u16, weights, nb):
    toep_c, bst_c, cst_c, avec = weights
    bsz, cs, nc, width = u16.shape
    oct_spec = lambda a: pl.BlockSpec((None,) + a.shape[1:], lambda o, b: (o,) + (0,) * (a.ndim - 1))
    io_spec = pl.BlockSpec((nb, cs, nc, LANES), lambda o, b: (b, 0, 0, o))
    n_state = 2 * SSM_OCT * SSM_STATE
    n_pairs = (cs // 2) * (cs // 2 + 1) // 2
    return pl.pallas_call(
        functools.partial(_ssm_kernel, n_chunks=nc, nb=nb),
        grid=(width // LANES, bsz // nb),
        in_specs=[io_spec, oct_spec(toep_c), oct_spec(bst_c), oct_spec(cst_c), oct_spec(avec)],
        out_specs=io_spec,
        out_shape=jax.ShapeDtypeStruct(u16.shape, BF16),
        scratch_shapes=[pltpu.VMEM((n_pairs, 2 * LANES, 2 * LANES), BF16),
                        pltpu.VMEM((cs * LANES, n_state), BF16),
                        pltpu.VMEM((n_state, cs * LANES), BF16),
                        pltpu.VMEM((n_state // LANES, nb * _row_pitch(nc), LANES), F32),
                        pltpu.VMEM((n_state // LANES, nb * _row_pitch(nc), LANES), F32)],
        compiler_params=_params(2),
        name="ssm",
    )(u16, toep_c, bst_c, cst_c, avec)


def _layer_norm(v, g, b):
    mu = jnp.mean(v, axis=-1, keepdims=True)
    vc = v - mu
    var = jnp.mean(vc * vc, axis=-1, keepdims=True)
    return vc * lax.rsqrt(var + LN_EPS) * g + b


def _project_and_norm(o_ref, resid, lhs, w_ref, g_ref, b_ref):
    slab = resid.shape[0] // NORM_SLABS
    vs = [ALPHA * resid[i * slab:(i + 1) * slab]
          + jnp.dot(lhs[i * slab:(i + 1) * slab], w_ref[...], preferred_element_type=F32)
          for i in range(NORM_SLABS)]
    for i, v in enumerate(vs):
        o_ref[i * slab:(i + 1) * slab, :] = _layer_norm(v, g_ref[...], b_ref[...])


def _merge_kernel(o0_ref, o1_ref, o2_ref, l0_ref, l1_ref, l2_ref, ys_ref, g_ref, x_ref,
                  wglu_ref, wsp_ref, wap_ref, wout_ref, lng_ref, lnb_ref, h_ref, o_scr, l_scr, y_scr):
    tm = x_ref.shape[0]

    def token_order(ref, scr, d):
        if d == 1:
            return ref[0].astype(F32)
        nk = ref.shape[-1] // LANES
        pitch = _row_pitch(d)
        for r in range(d):
            v = ref[r].astype(F32)
            for k in range(nk):
                scr[k, pl.ds(r, tm // d, stride=pitch), :] = v[:, k * LANES:(k + 1) * LANES]
        if pitch == d:
            return jnp.concatenate([scr[k, 0:tm, :] for k in range(nk)], axis=-1)
        return jnp.concatenate(
            [jnp.concatenate([scr[k, g * pitch:g * pitch + d, :] for g in range(tm // d)], axis=0)
             for k in range(nk)], axis=-1)

    ls, outs = [], []
    for gi, (o_ref, l_ref) in enumerate(((o0_ref, l0_ref), (o1_ref, l1_ref), (o2_ref, l2_ref))):
        d = DILATION_PATTERNS[gi][1]
        outs.append(token_order(o_ref, o_scr.at[gi], d))
        ls.append(token_order(l_ref, l_scr.at[gi], d))
    mx = jnp.maximum(jnp.maximum(ls[0], ls[1]), ls[2])
    es = [jnp.exp(l - mx) for l in ls]
    num = es[0] * outs[0] + es[1] * outs[1] + es[2] * outs[2]
    y_attn = (num / (es[0] + es[1] + es[2])).astype(BF16)
    ys = token_order(ys_ref, y_scr, SSM_CHUNK).astype(BF16)
    cw = GROUP_WIDTH
    y_ssm = []
    for c in range(SSM_WIDTH // cw):
        a = jnp.dot(ys, wglu_ref[:, c * cw:(c + 1) * cw], preferred_element_type=F32)
        b = jnp.dot(ys, wglu_ref[:, SSM_WIDTH + c * cw:SSM_WIDTH + (c + 1) * cw], preferred_element_type=F32)
        y_ssm.append((a * _sigmoid(b)).astype(BF16))
    y_ssm = jnp.concatenate(y_ssm, axis=-1)
    gated = []
    for c in range(D_MODEL // cw):
        pa = jnp.dot(y_ssm, wsp_ref[:, c * cw:(c + 1) * cw], preferred_element_type=F32)
        pb = jnp.dot(y_attn, wap_ref[:, c * cw:(c + 1) * cw], preferred_element_type=F32)
        g_ssm = g_ref[:, c * cw:(c + 1) * cw].astype(F32)
        g_attn = g_ref[:, D_MODEL + c * cw:D_MODEL + (c + 1) * cw].astype(F32)
        gated.append((g_ssm * pa + g_attn * pb).astype(BF16))
    gated = jnp.concatenate(gated, axis=-1)
    mix = jnp.dot(gated, wout_ref[...], preferred_element_type=F32)
    h_ref[...] = _layer_norm(ALPHA * x_ref[...] + mix, lng_ref[...], lnb_ref[...])


def _merge(outs, lses, ys, gates, x2, w_glu, w_sp, w_ap, w_out, ln_g, ln_b, seqlen, tm):
    t = x2.shape[0]
    tiles = seqlen // tm
    stage_rows = max(tm // d * _row_pitch(d) for d in [d for _, d in DILATION_PATTERNS] + [SSM_CHUNK])
    row = lambda w: pl.BlockSpec((tm, w), lambda i: (i, 0))
    res_spec = lambda d, w: pl.BlockSpec((None, d, tm // d, w), lambda i: (i // tiles, 0, i % tiles, 0))
    res_specs = [res_spec(d, GROUP_WIDTH) for _, d in DILATION_PATTERNS]
    return pl.pallas_call(
        _merge_kernel,
        grid=(t // tm,),
        in_specs=res_specs + res_specs + [res_spec(SSM_CHUNK, SSM_WIDTH), row(2 * D_MODEL), row(D_MODEL),
                  _const_spec(w_glu.shape), _const_spec(w_sp.shape), _const_spec(w_ap.shape),
                  _const_spec(w_out.shape), _const_spec((1, D_MODEL)), _const_spec((1, D_MODEL))],
        out_specs=row(D_MODEL),
        out_shape=jax.ShapeDtypeStruct((t, D_MODEL), F32),
        scratch_shapes=[pltpu.VMEM((N_GROUPS, GROUP_WIDTH // LANES, stage_rows, LANES), F32),
                        pltpu.VMEM((N_GROUPS, GROUP_WIDTH // LANES, stage_rows, LANES), F32),
                        pltpu.VMEM((SSM_WIDTH // LANES, stage_rows, LANES), F32)],
        compiler_params=_params(1),
        name="merge_ln1",
    )(*outs, *lses, ys, gates, x2, w_glu, w_sp, w_ap, w_out, ln_g, ln_b)


FF_CHUNK = 1024
NORM_SLABS = 4


def _ffn_kernel(h_ref, wup_ref, wdn_ref, lng_ref, lnb_ref, o_ref):
    h = h_ref[...]
    hb = h.astype(BF16)
    acts = []
    for c in range(D_FF // FF_CHUNK):
        lo, hi = c * FF_CHUNK, (c + 1) * FF_CHUNK
        up = jnp.dot(hb, wup_ref[:, lo:hi], preferred_element_type=F32)
        acts.append(jnp.square(jnp.maximum(up, 0.0)).astype(BF16))
    act = jnp.concatenate(acts, axis=-1)
    _project_and_norm(o_ref, h, act, wdn_ref, lng_ref, lnb_ref)


def _ffn(h1, w_up, w_down, ln_g, ln_b, tm):
    t = h1.shape[0]
    row = pl.BlockSpec((tm, D_MODEL), lambda i: (i, 0))
    return pl.pallas_call(
        _ffn_kernel,
        grid=(t // tm,),
        in_specs=[row, _const_spec(w_up.shape), _const_spec(w_down.shape),
                  _const_spec((1, D_MODEL)), _const_spec((1, D_MODEL))],
        out_specs=row,
        out_shape=jax.ShapeDtypeStruct((t, D_MODEL), F32),
        compiler_params=_params(1),
        name="ffn_ln2",
    )(h1, w_up, w_down, ln_g, ln_b)


def _permute_w_in(w):
    aw = ATTN_WIDTH
    cols = []
    for gi in range(N_GROUPS):
        lo, hi = gi * GROUP_WIDTH, (gi + 1) * GROUP_WIDTH
        cols += [w[:, lo:hi] * (HEAD_DIM ** -0.5), w[:, aw + lo:aw + hi], w[:, 2 * aw + lo:2 * aw + hi]]
    cols.append(w[:, 3 * aw:])
    return jnp.concatenate(cols, axis=1).astype(BF16)


def _layer(h2, bsz, seqlen, l, w_in, b_gate, lambda_re, lambda_im, log_dt, ssm_b_re, ssm_b_im,
           ssm_c_re, ssm_c_im, ssm_d, w_glu, w_ssm_proj, rel_bias, w_attn_proj, w_out,
           ln1_g, ln1_b, w_up, w_down, ln2_g, ln2_b, tm=1024, attn_mt=2048, ffn_tm=1024):
    qkv0, qkv1, qkv2, u, gates = _in_proj(h2, _permute_w_in(w_in[l]), b_gate[l][None, :], bsz, seqlen, tm)
    outs, lses = [], []
    for gi, ((window, dilation), qkv) in enumerate(zip(DILATION_PATTERNS, (qkv0, qkv1, qkv2))):
        rb = rel_bias[:, gi * HEADS_PER_GROUP:(gi + 1) * HEADS_PER_GROUP].astype(F32)
        o, s = _attention_group(qkv, rb, window, dilation, attn_mt)
        outs.append(o)
        lses.append(s)
    ssm_w = _ssm_weights(lambda_re[l], lambda_im[l], log_dt[l], ssm_b_re[l], ssm_b_im[l],
                         ssm_c_re[l], ssm_c_im[l], ssm_d[l])
    ys = _ssm(u, ssm_w, nb=4)
    h1 = _merge(outs, lses, ys, gates, h2, w_glu[l].astype(BF16), w_ssm_proj[l].astype(BF16),
                w_attn_proj[l].astype(BF16), w_out[l].astype(BF16),
                ln1_g[l][None, :], ln1_b[l][None, :], seqlen, tm)
    return _ffn(h1, w_up[l].astype(BF16), w_down[l].astype(BF16), ln2_g[l][None, :], ln2_b[l][None, :], ffn_tm)


def kernel(x, w_in, b_gate, lambda_re, lambda_im, log_dt, ssm_b_re, ssm_b_im, ssm_c_re, ssm_c_im,
           ssm_d, w_glu, w_ssm_proj, rel_bias, w_attn_proj, w_out, ln1_g, ln1_b, w_up, w_down,
           ln2_g, ln2_b):
    bsz, seqlen, d = x.shape
    h = x.reshape(bsz * seqlen, d)
    for l in range(w_in.shape[0]):
        h = _layer(h, bsz, seqlen, l, w_in, b_gate, lambda_re, lambda_im, log_dt, ssm_b_re, ssm_b_im,
                   ssm_c_re, ssm_c_im, ssm_d, w_glu, w_ssm_proj, rel_bias, w_attn_proj, w_out,
                   ln1_g, ln1_b, w_up, w_down, ln2_g, ln2_b)
    return h.reshape(bsz, seqlen, d)
```

```python
import functools
import math

import jax
import jax.numpy as jnp
from jax import lax
from jax.experimental import pallas as pl
from jax.experimental.pallas import tpu as pltpu

F32 = jnp.float32
BF16 = jnp.bfloat16

D_MODEL = 1024
HEAD_DIM = 64
HEADS_PER_GROUP = 4
GROUP_WIDTH = HEADS_PER_GROUP * HEAD_DIM
DILATION_PATTERNS = ((128, 1), (512, 4), (2048, 16))
N_GROUPS = len(DILATION_PATTERNS)
ATTN_WIDTH = N_GROUPS * GROUP_WIDTH
N_BUCKETS = 32
MAX_DISTANCE = 2048
SSM_WIDTH = 512
SSM_GROUP = 16
SSM_GROUPS = 32
SSM_STATE = 64
D_FF = 4 * D_MODEL
DEPTH = 1
ALPHA = (2.0 * DEPTH) ** 0.25
LN_EPS = 1e-5
NEG_INF = -1e30

ATTN_BLOCK = 128
ATTN_SKEW = (4, 7)
SSM_CHUNK = 16
SSM_OCT = 8
LANES = 128
SUBLANES = 8
VMEM_LIMIT_BYTES = 56 * 1024 * 1024


def _params(n_axes):
    return pltpu.CompilerParams(dimension_semantics=("arbitrary",) * n_axes,
                                vmem_limit_bytes=VMEM_LIMIT_BYTES)


def _sigmoid(z):
    return 0.5 * jnp.tanh(0.5 * z) + 0.5


def _row_pitch(stride):
    return stride + SUBLANES if stride % (2 * SUBLANES) == 0 else stride


def _const_spec(shape):
    nd = len(shape)
    return pl.BlockSpec(shape, lambda *_: (0,) * nd, pipeline_mode=pl.Buffered(1))


def _in_proj_kernel(x_ref, w_ref, bg_ref, qkv0_ref, qkv1_ref, qkv2_ref, u_ref, g_ref, scr):
    xb = x_ref[...].astype(BF16)
    tm = xb.shape[0]

    def mm(lo, hi):
        return jnp.dot(xb, w_ref[:, lo:hi], preferred_element_type=F32)

    def emit(ref, res, d, col0):
        width = res.shape[1]
        if d == 1:
            ref[0, :, col0:col0 + width] = res.astype(BF16)
            return
        slot0 = next_slot[0]
        next_slot[0] += width // LANES
        pitch = _row_pitch(d)
        for k in range(width // LANES):
            tile = res[:, k * LANES:(k + 1) * LANES]
            if pitch == d:
                scr[slot0 + k, 0:tm, :] = tile
            else:
                for g in range(tm // d):
                    scr[slot0 + k, g * pitch:g * pitch + d, :] = tile[g * d:(g + 1) * d]
        for r in range(d):
            for k in range(width // LANES):
                col = col0 + k * LANES
                ref[r, :, col:col + LANES] = scr[slot0 + k, pl.ds(r, tm // d, stride=pitch), :].astype(BF16)

    next_slot = [0]

    gw3 = 3 * GROUP_WIDTH
    for gi, ref in enumerate((qkv0_ref, qkv1_ref, qkv2_ref)):
        for c in range(3):
            lo = gi * gw3 + c * GROUP_WIDTH
            emit(ref, mm(lo, lo + GROUP_WIDTH), DILATION_PATTERNS[gi][1], c * GROUP_WIDTH)
    base = N_GROUPS * gw3
    emit(u_ref, mm(base, base + SSM_WIDTH), SSM_CHUNK, 0)
    base += SSM_WIDTH
    for c in range(2 * D_MODEL // GROUP_WIDTH):
        lo, hi = c * GROUP_WIDTH, (c + 1) * GROUP_WIDTH
        z = mm(base + lo, base + hi) + bg_ref[:, lo:hi]
        g_ref[:, lo:hi] = _sigmoid(z).astype(BF16)


def _in_proj(x2, w_perm, b_gate, bsz, seqlen, tm):
    t = x2.shape[0]
    n_in = w_perm.shape[1]
    tiles = seqlen // tm
    row = lambda w: pl.BlockSpec((tm, w), lambda i: (i, 0))
    w3 = 3 * GROUP_WIDTH
    dils = [d for _, d in DILATION_PATTERNS]
    n_stage = (sum(d > 1 for d in dils) * w3 + SSM_WIDTH) // LANES
    res_spec = lambda d, w:pl.BlockSpec((None, d, tm // d, w), lambda i: (i // tiles, 0, i % tiles, 0))
    return pl.pallas_call(
        _in_proj_kernel,
        grid=(t // tm,),
        in_specs=[row(D_MODEL), _const_spec((D_MODEL, n_in)), _const_spec((1, 2 * D_MODEL))],
        out_specs=[res_spec(d, w3) for d in dils] + [res_spec(SSM_CHUNK, SSM_WIDTH), row(2 * D_MODEL)],
        out_shape=[jax.ShapeDtypeStruct((bsz, d, seqlen // d, w3), BF16) for d in dils]
        + [jax.ShapeDtypeStruct((bsz, SSM_CHUNK, seqlen // SSM_CHUNK, SSM_WIDTH), BF16),
           jax.ShapeDtypeStruct((t, 2 * D_MODEL), BF16)],
        scratch_shapes=[pltpu.VMEM((n_stage, max(tm // d * _row_pitch(d) for d in dils + [SSM_CHUNK]), LANES), F32)],
        compiler_params=_params(1),
        name="in_proj",
    )(x2, w_perm, b_gate)


def _attn_kernel(bucket_ref, relb_ref, cur_ref, prev_ref, out_ref, lse_ref, bias_scr, k_scr, vt_scr, *, nq, nres):
    blk = ATTN_BLOCK
    first = (pl.program_id(0) == 0) & (pl.program_id(1) == 0) & (pl.program_id(2) == 0)

    @pl.when(first)
    def _build_bias():
        bucket = bucket_ref[...]
        for h in range(HEADS_PER_GROUP):
            acc = jnp.full(bucket.shape, NEG_INF, F32)
            for bkt in range(N_BUCKETS):
                acc = jnp.where(bucket == bkt, relb_ref[bkt, h], acc)
            bias_scr[h // 2, :, (h % 2) * blk:(h % 2 + 1) * blk] = acc

    row = lax.broadcasted_iota(jnp.int32, (2 * blk, blk), 0)
    row_head = row // HEAD_DIM
    keep_first = (row >= blk) | (pl.program_id(2) > 0)
    for r in range(nres):
        k_scr[r, 0:blk, :] = prev_ref[r, :, GROUP_WIDTH:2 * GROUP_WIDTH]
        k_scr[r, blk:, :] = cur_ref[r, :, GROUP_WIDTH:2 * GROUP_WIDTH]
        vt_scr[r, :, 0:blk] = prev_ref[r, :, 2 * GROUP_WIDTH:3 * GROUP_WIDTH].T
        for j in range(nq):
            vt_scr[r, :, (j + 1) * blk:(j + 2) * blk] = cur_ref[r, j * blk:(j + 1) * blk,
                                                                 2 * GROUP_WIDTH:3 * GROUP_WIDTH].T

    npair = HEADS_PER_GROUP // 2
    units = [(r, j, hp) for r in range(nres) for j in range(nq) for hp in range(npair)]
    qts = {}

    def scores(r, j, hp):
        if (r, j) not in qts:
            qts[(r, j)] = cur_ref[r, j * blk:(j + 1) * blk, 0:GROUP_WIDTH].T
        qt = qts[(r, j)]
        k2 = k_scr[r, j * blk:(j + 2) * blk, :]
        qh = jnp.concatenate([jnp.where(row_head == 2 * hp + i, qt, jnp.zeros_like(qt)) for i in range(2)], axis=1)
        return jnp.dot(k2, qh, preferred_element_type=F32)

    def softmax(r, j, hp, s2):
        ps, ms, ls = [], [], []
        for i in range(2):
            lanes = slice(i * blk, (i + 1) * blk)
            s = s2[:, lanes] + bias_scr[hp, :, lanes]
            if j == 0:
                s = jnp.where(keep_first, s, NEG_INF)
            m = jnp.max(s, axis=0, keepdims=True)
            p = jnp.exp(s - m)
            ls.append(jnp.sum(p, axis=0, keepdims=True))
            ps.append(p.astype(BF16))
            ms.append(m)
        return jnp.concatenate(ps, axis=1), jnp.concatenate(ms, axis=1), jnp.concatenate(ls, axis=1)

    def values(r, j, hp, p, m, l):
        vt = vt_scr[r, 2 * hp * HEAD_DIM:2 * (hp + 1) * HEAD_DIM, j * blk:(j + 2) * blk]
        ot = jnp.dot(vt, p, preferred_element_type=F32) * (1.0 / l)
        lse = jnp.broadcast_to(m + jnp.log(l), (HEAD_DIM, 2 * blk))
        return [(ot[i * HEAD_DIM:(i + 1) * HEAD_DIM, i * blk:(i + 1) * blk], lse[:, i * blk:(i + 1) * blk])
                for i in range(2)]

    raw, soft, done = {}, {}, []
    lag_soft, lag_val = ATTN_SKEW
    for step in range(len(units) + lag_val):
        if step < len(units):
            raw[step] = scores(*units[step])
        if lag_soft <= step < len(units) + lag_soft:
            soft[step - lag_soft] = softmax(*units[step - lag_soft], raw.pop(step - lag_soft))
        if step >= lag_val:
            i = step - lag_val
            r, j, hp = units[i]
            done += values(r, j, hp, *soft.pop(i))
            if hp == npair - 1:
                rows = slice(j * blk, (j + 1) * blk)
                out_ref[r, rows, :] = jnp.concatenate([d[0] for d in done], axis=0).T.astype(BF16)
                lse_ref[r, rows, :] = jnp.concatenate([d[1] for d in done], axis=0).T
                done = []


def _t5_bucket(dist):
    max_exact = N_BUCKETS // 2
    n_log = N_BUCKETS - max_exact
    thresholds = [math.ceil(max_exact * (MAX_DISTANCE / max_exact) ** (k / n_log)) for k in range(1, n_log)]
    large = max_exact + sum((dist >= t).astype(jnp.int32) for t in thresholds)
    return jnp.where(dist < max_exact, dist, large)


def _bucket_table(window, dilation):
    blk = ATTN_BLOCK
    span = window // dilation
    rel = jnp.arange(blk, dtype=jnp.int32)[None, :] + blk - jnp.arange(2 * blk, dtype=jnp.int32)[:, None]
    valid = (rel >= 0) & (rel <= span)
    return jnp.where(valid, _t5_bucket(jnp.maximum(rel, 0) * dilation), -1)


def _attention_group(qkv, rel_bias_g, window, dilation, mt):
    assert window // dilation == ATTN_BLOCK
    bsz, _, n, w3 = qkv.shape
    rows = mt
    mt = min(rows, n)
    nres = rows // mt
    nq = mt // ATTN_BLOCK
    cur = pl.BlockSpec((None, nres, mt, w3), lambda b, r, i: (b, r, i, 0))
    prev = pl.BlockSpec((None, nres, ATTN_BLOCK, w3), lambda b, r, i: (b, r, jnp.maximum(i * nq - 1, 0), 0))
    ospec = pl.BlockSpec((None, nres, mt, GROUP_WIDTH), lambda b, r, i: (b, r, i, 0))
    return pl.pallas_call(
        functools.partial(_attn_kernel, nq=nq, nres=nres),
        grid=(bsz, dilation // nres, n // mt),
        in_specs=[_const_spec((2 * ATTN_BLOCK, ATTN_BLOCK)),
                  pl.BlockSpec(memory_space=pltpu.SMEM), cur, prev],
        out_specs=[ospec, ospec],
        out_shape=[jax.ShapeDtypeStruct((bsz, dilation, n, GROUP_WIDTH), BF16),
                   jax.ShapeDtypeStruct((bsz, dilation, n, GROUP_WIDTH), F32)],
        scratch_shapes=[pltpu.VMEM((HEADS_PER_GROUP // 2, 2 * ATTN_BLOCK, 2 * ATTN_BLOCK), F32),
                        pltpu.VMEM((nres, mt + ATTN_BLOCK, GROUP_WIDTH), BF16),
                        pltpu.VMEM((nres, GROUP_WIDTH, mt + ATTN_BLOCK), BF16)],
        compiler_params=_params(3),
        name=f"attn_d{dilation}",
    )(_bucket_table(window, dilation), rel_bias_g, qkv, qkv)


def _ssm_prep_kernel(logdt_ref, lr_ref, li_ref, bt_re_ref, bt_im_ref, c_re_ref, c_im_ref, dl_ref,
                     toep_ref, bst_ref, cst_ref, a_ref):
    for g in range(SSM_OCT):
        _ssm_prep_group(g, logdt_ref[pl.program_id(0) * SSM_OCT + g], lr_ref, li_ref, bt_re_ref, bt_im_ref,
                        c_re_ref, c_im_ref, dl_ref, toep_ref, bst_ref, cst_ref, a_ref)


def _ssm_prep_group(g, log_dt, lr_ref, li_ref, bt_re_ref, bt_im_ref, c_re_ref, c_im_ref, dl_ref,
                    toep_ref, bst_ref, cst_ref, a_ref):
    cs = SSM_CHUNK
    dt = jnp.exp(jnp.full((1, SSM_STATE), log_dt, F32))
    lr, li = lr_ref[g], li_ref[g]
    mag = jnp.exp(lr * dt)
    ab_re, ab_im = mag * jnp.cos(li * dt), mag * jnp.sin(li * dt)
    den = lr * lr + li * li
    nr = ab_re - 1.0
    k_re = (nr * lr + ab_im * li) / den
    k_im = (ab_im * lr - nr * li) / den
    bt_re, bt_im = bt_re_ref[g], bt_im_ref[g]
    bb_re = k_re * bt_re - k_im * bt_im
    bb_im = k_re * bt_im + k_im * bt_re
    j = lax.broadcasted_iota(jnp.int32, (cs + SUBLANES, SSM_STATE), 0).astype(F32)
    pmag = jnp.exp(lr * dt * j)
    ang = li * dt * j
    p_re, p_im = pmag * jnp.cos(ang), pmag * jnp.sin(ang)
    c_re, c_im = c_re_ref[g], c_im_ref[g]
    cp_re = [c_re * p_re[i:i + 1] - c_im * p_im[i:i + 1] for i in range(cs + 1)]
    cp_im = [c_re * p_im[i:i + 1] + c_im * p_re[i:i + 1] for i in range(cs + 1)]
    nt = (((1,), (1,)), ((), ()))
    hi = lax.Precision.HIGHEST
    kcat = (lax.dot_general(bb_re, jnp.concatenate(cp_re[:cs], axis=0), nt, precision=hi,
                            preferred_element_type=F32)
            - lax.dot_general(bb_im, jnp.concatenate(cp_im[:cs], axis=0), nt, precision=hi,
                              preferred_element_type=F32))
    lane = lax.broadcasted_iota(jnp.int32, kcat.shape, 1)
    row = lax.broadcasted_iota(jnp.int32, kcat.shape, 0)
    dl = dl_ref[g]
    for s in range(cs):
        off = s * SSM_GROUP
        t_s = kcat if s == 0 else jnp.where(lane >= off, pltpu.roll(kcat, off, 1), 0.0)
        toep_ref[s, g] = (t_s + jnp.where(lane == off + row, dl, 0.0)).astype(BF16)
        pe_re, pe_im = p_re[cs - 1 - s:cs - s], p_im[cs - 1 - s:cs - s]
        st_re = pe_re * bb_re - pe_im * bb_im
        st_im = pe_re * bb_im + pe_im * bb_re
        bst_ref[s, g] = jnp.concatenate([st_re, st_im], axis=-1).astype(BF16)
    ro = jnp.concatenate([jnp.concatenate(cp_re[1:], axis=0),
                          -jnp.concatenate(cp_im[1:], axis=0)], axis=-1)
    ro_t = ro.T
    cst_ref[0, g] = ro_t[:SSM_STATE].astype(BF16)
    cst_ref[1, g] = ro_t[SSM_STATE:].astype(BF16)
    a_ref[g] = jnp.concatenate([p_re[cs:cs + 1], p_im[cs:cs + 1]], axis=0)


def _ssm_weights(lambda_re, lambda_im, log_dt, b_re, b_im, c_re, c_im, d_skip):
    f32 = F32
    cs, ng, no = SSM_CHUNK, SSM_GROUPS, SSM_GROUPS // SSM_OCT
    grp = lambda *shape: pl.BlockSpec((SSM_OCT,) + shape, lambda o: (o,) + (0,) * len(shape))
    in_oct = lambda *shape: pl.BlockSpec((None, shape[0], SSM_OCT) + shape[1:],
                                         lambda o: (o,) + (0,) * (len(shape) + 1))
    toep_c, bst_c, cst_c, a32 = pl.pallas_call(
        _ssm_prep_kernel,
        grid=(no,),
        in_specs=[pl.BlockSpec(memory_space=pltpu.SMEM), grp(1, SSM_STATE), grp(1, SSM_STATE),
                  grp(SSM_GROUP, SSM_STATE), grp(SSM_GROUP, SSM_STATE),
                  grp(SSM_GROUP, SSM_STATE), grp(SSM_GROUP, SSM_STATE), grp(1, cs * SSM_GROUP)],
        out_specs=[in_oct(cs, SSM_GROUP, cs * SSM_GROUP), in_oct(cs, SSM_GROUP, 2 * SSM_STATE),
                   in_oct(2, SSM_STATE, cs * SSM_GROUP), grp(2, SSM_STATE)],
        out_shape=[jax.ShapeDtypeStruct((no, cs, SSM_OCT, SSM_GROUP, cs * SSM_GROUP), BF16),
                   jax.ShapeDtypeStruct((no, cs, SSM_OCT, SSM_GROUP, 2 * SSM_STATE), BF16),
                   jax.ShapeDtypeStruct((no, 2, SSM_OCT, SSM_STATE, cs * SSM_GROUP), BF16),
                   jax.ShapeDtypeStruct((ng, 2, SSM_STATE), f32)],
        compiler_params=_params(1),
        name="ssm_prep",
    )(log_dt.astype(f32), lambda_re.astype(f32)[:, None, :], lambda_im.astype(f32)[:, None, :],
      b_re.astype(f32).transpose(0, 2, 1), b_im.astype(f32).transpose(0, 2, 1),
      c_re.astype(f32), c_im.astype(f32), jnp.tile(d_skip.astype(f32), (1, cs))[:, None, :])
    avec = a32.reshape(no, SSM_OCT, 2, SSM_STATE).transpose(0, 2, 1, 3).reshape(no, 2, SSM_OCT * SSM_STATE)
    return (toep_c.reshape(no, cs * LANES, cs * SSM_GROUP),
            bst_c.reshape(no, cs * LANES, 2 * SSM_STATE),
            cst_c.reshape(no, 2 * SSM_OCT * SSM_STATE, cs * SSM_GROUP),
            avec)


def _expand_ssm_weights(toep_ref, bstc_ref, cstc_ref, wt_scr, bst_scr, cst_scr):
    tw = 2 * LANES
    sh_g, sh_n, sh_o = (v.bit_length() - 1 for v in (SSM_GROUP, SSM_STATE, SSM_OCT))
    r = lax.broadcasted_iota(jnp.int32, (tw, tw), 0)
    c = lax.broadcasted_iota(jnp.int32, (tw, tw), 1)
    col_grp = (c >> sh_g) & (SSM_OCT - 1)
    row_grp = (r >> sh_g) & (SSM_OCT - 1)
    for tp in range(SSM_CHUNK // 2):
        src_col = (2 * tp + (c >> (sh_g + sh_o))) * SSM_GROUP + (c & (SSM_GROUP - 1))
        e = jnp.where(r == src_col, 1.0, 0.0).astype(BF16)
        base = tp * (tp + 1) // 2
        for sp in range(tp + 1):
            x = jnp.dot(toep_ref[sp * tw:(sp + 1) * tw, :], e, preferred_element_type=F32)
            wt_scr[base + sp] = jnp.where(row_grp == col_grp, x, 0.0).astype(BF16)
        for j in range(2 * SSM_OCT * SSM_STATE // tw):
            x = jnp.dot(cstc_ref[j * tw:(j + 1) * tw, :], e, preferred_element_type=F32)
            row_grp_n = ((j * tw + r) >> sh_n) & (SSM_OCT - 1)
            cst_scr[j * tw:(j + 1) * tw, tp * tw:(tp + 1) * tw] = jnp.where(row_grp_n == col_grp, x, 0.0).astype(BF16)
    ns = 2 * SSM_OCT * SSM_STATE
    rb = lax.broadcasted_iota(jnp.int32, (2 * SSM_STATE, ns), 0)
    cb = lax.broadcasted_iota(jnp.int32, (2 * SSM_STATE, ns), 1)
    src_col_b = (cb >> (sh_n + sh_o)) * SSM_STATE + (cb & (SSM_STATE - 1))
    eb = jnp.where(rb == src_col_b, 1.0, 0.0).astype(BF16)
    rr = lax.broadcasted_iota(jnp.int32, (tw, ns), 0)
    cc = lax.broadcasted_iota(jnp.int32, (tw, ns), 1)
    keep = ((rr >> sh_g) & (SSM_OCT - 1)) == ((cc >> sh_n) & (SSM_OCT - 1))
    for j in range(SSM_CHUNK * LANES // tw):
        x = jnp.dot(bstc_ref[j * tw:(j + 1) * tw, :], eb, preferred_element_type=F32)
        bst_scr[j * tw:(j + 1) * tw, :] = jnp.where(keep, x, 0.0).astype(BF16)


def _ssm_kernel(u_ref, toep_ref, bstc_ref, cstc_ref, a_ref, y_ref, wt_ref, bst_ref, cst_ref, zz_scr, hp_scr,
                *, n_chunks, nb):
    @pl.when(pl.program_id(1) == 0)
    def _new_octet():
        _expand_ssm_weights(toep_ref, bstc_ref, cstc_ref, wt_ref, bst_ref, cst_ref)

    rows = nb * n_chunks
    nk = SSM_OCT * SSM_STATE // LANES
    u_t = [u_ref[:, s].reshape(rows, LANES) for s in range(SSM_CHUNK)]
    z = jnp.dot(jnp.concatenate(u_t, axis=-1), bst_ref[...], preferred_element_type=F32)
    pitch = _row_pitch(n_chunks)
    for k in range(2 * nk):
        for b in range(nb):
            zz_scr[k, b * pitch:b * pitch + n_chunks, :] = z[b * n_chunks:(b + 1) * n_chunks, k * LANES:(k + 1) * LANES]
    a_re = [jnp.broadcast_to(a_ref[0:1, k * LANES:(k + 1) * LANES], (nb, LANES)) for k in range(nk)]
    a_im = [jnp.broadcast_to(a_ref[1:2, k * LANES:(k + 1) * LANES], (nb, LANES)) for k in range(nk)]

    def step(c, carry):
        h_re, h_im = carry
        rows_c = pl.ds(c, nb, stride=pitch)
        new_re, new_im = [], []
        for k in range(nk):
            hp_scr[k, rows_c, :] = h_re[k]
            hp_scr[nk + k, rows_c, :] = h_im[k]
            new_re.append(a_re[k] * h_re[k] - a_im[k] * h_im[k] + zz_scr[k, rows_c, :])
            new_im.append(a_re[k] * h_im[k] + a_im[k] * h_re[k] + zz_scr[nk + k, rows_c, :])
        return tuple(new_re), tuple(new_im)

    zero = tuple(jnp.zeros((nb, LANES), F32) for _ in range(nk))
    lax.fori_loop(0, n_chunks, step, (zero, zero), unroll=4)
    hp = jnp.concatenate(
        [jnp.concatenate([hp_scr[k, b * pitch:b * pitch + n_chunks, :] for b in range(nb)], axis=0)
         for k in range(2 * nk)], axis=-1).astype(BF16)
    tw = 2 * LANES
    for tp in range(SSM_CHUNK // 2):
        base = tp * (tp + 1) // 2
        y = jnp.dot(hp, cst_ref[:, tp * tw:(tp + 1) * tw], preferred_element_type=F32)
        y = y + jnp.dot(jnp.concatenate(u_t[:2 * (tp + 1)], axis=-1),
                        wt_ref[base:base + tp + 1].reshape((tp + 1) * tw, tw), preferred_element_type=F32)
        y = jax.nn.gelu(y).astype(BF16)
        y_ref[:, 2 * tp] = y[:, :LANES].reshape(nb, n_chunks, LANES)
        y_ref[:, 2 * tp + 1] = y[:, LANES:].reshape(nb, n_chunks, LANES)


def _ssm(u16, weights, nb):
    toep_c, bst_c, cst_c, avec = weights
    bsz, cs, nc, width = u16.shape
    oct_spec = lambda a: pl.BlockSpec((None,) + a.shape[1:], lambda o, b: (o,) + (0,) * (a.ndim - 1))
    io_spec = pl.BlockSpec((nb, cs, nc, LANES), lambda o, b: (b, 0, 0, o))
    n_state = 2 * SSM_OCT * SSM_STATE
    n_pairs = (cs // 2) * (cs // 2 + 1) // 2
    return pl.pallas_call(
        functools.partial(_ssm_kernel, n_chunks=nc, nb=nb),
        grid=(width // LANES, bsz // nb),
        in_specs=[io_spec, oct_spec(toep_c), oct_spec(bst_c), oct_spec(cst_c), oct_spec(avec)],
        out_specs=io_spec,
        out_shape=jax.ShapeDtypeStruct(u16.shape, BF16),
        scratch_shapes=[pltpu.VMEM((n_pairs, 2 * LANES, 2 * LANES), BF16),
                        pltpu.VMEM((cs * LANES, n_state), BF16),
                        pltpu.VMEM((n_state, cs * LANES), BF16),
                        pltpu.VMEM((n_state // LANES, nb * _row_pitch(nc), LANES), F32),
                        pltpu.VMEM((n_state // LANES, nb * _row_pitch(nc), LANES), F32)],
        compiler_params=_params(2),
        name="ssm",
    )(u16, toep_c, bst_c, cst_c, avec)


def _layer_norm(v, g, b):
    mu = jnp.mean(v, axis=-1, keepdims=True)
    vc = v - mu
    var = jnp.mean(vc * vc, axis=-1, keepdims=True)
    return vc * lax.rsqrt(var + LN_EPS) * g + b


def _project_and_norm(o_ref, resid, lhs, w_ref, g_ref, b_ref):
    slab = resid.shape[0] // NORM_SLABS
    vs = [ALPHA * resid[i * slab:(i + 1) * slab]
          + jnp.dot(lhs[i * slab:(i + 1) * slab], w_ref[...], preferred_element_type=F32)
          for i in range(NORM_SLABS)]
    for i, v in enumerate(vs):
        o_ref[i * slab:(i + 1) * slab, :] = _layer_norm(v, g_ref[...], b_ref[...])


def _merge_kernel(o0_ref, o1_ref, o2_ref, l0_ref, l1_ref, l2_ref, ys_ref, g_ref, x_ref,
                  wglu_ref, wsp_ref, wap_ref, wout_ref, lng_ref, lnb_ref, h_ref, o_scr, l_scr, y_scr):
    tm = x_ref.shape[0]

    def token_order(ref, scr, d):
        if d == 1:
            return ref[0].astype(F32)
        nk = ref.shape[-1] // LANES
        pitch = _row_pitch(d)
        for r in range(d):
            v = ref[r].astype(F32)
            for k in range(nk):
                scr[k, pl.ds(r, tm // d, stride=pitch), :] = v[:, k * LANES:(k + 1) * LANES]
        if pitch == d:
            return jnp.concatenate([scr[k, 0:tm, :] for k in range(nk)], axis=-1)
        return jnp.concatenate(
            [jnp.concatenate([scr[k, g * pitch:g * pitch + d, :] for g in range(tm // d)], axis=0)
             for k in range(nk)], axis=-1)

    ls, outs = [], []
    for gi, (o_ref, l_ref) in enumerate(((o0_ref, l0_ref), (o1_ref, l1_ref), (o2_ref, l2_ref))):
        d = DILATION_PATTERNS[gi][1]
        slot = sum(dd > 1 for _, dd in DILATION_PATTERNS[:gi])
        outs.append(token_order(o_ref, o_scr.at[slot], d))
        ls.append(token_order(l_ref, l_scr.at[slot], d))
    mx = jnp.maximum(jnp.maximum(ls[0], ls[1]), ls[2])
    es = [jnp.exp(l - mx) for l in ls]
    num = es[0] * outs[0] + es[1] * outs[1] + es[2] * outs[2]
    y_attn = (num / (es[0] + es[1] + es[2])).astype(BF16)
    ys = token_order(ys_ref, y_scr, SSM_CHUNK).astype(BF16)
    cw = GROUP_WIDTH
    y_ssm = []
    for c in range(SSM_WIDTH // cw):
        a = jnp.dot(ys, wglu_ref[:, c * cw:(c + 1) * cw], preferred_element_type=F32)
        b = jnp.dot(ys, wglu_ref[:, SSM_WIDTH + c * cw:SSM_WIDTH + (c + 1) * cw], preferred_element_type=F32)
        y_ssm.append((a * _sigmoid(b)).astype(BF16))
    y_ssm = jnp.concatenate(y_ssm, axis=-1)
    gated = []
    for c in range(D_MODEL // cw):
        pa = jnp.dot(y_ssm, wsp_ref[:, c * cw:(c + 1) * cw], preferred_element_type=F32)
        pb = jnp.dot(y_attn, wap_ref[:, c * cw:(c + 1) * cw], preferred_element_type=F32)
        g_ssm = g_ref[:, c * cw:(c + 1) * cw].astype(F32)
        g_attn = g_ref[:, D_MODEL + c * cw:D_MODEL + (c + 1) * cw].astype(F32)
        gated.append((g_ssm * pa + g_attn * pb).astype(BF16))
    gated = jnp.concatenate(gated, axis=-1)
    _project_and_norm(h_ref, x_ref, gated, wout_ref, lng_ref, lnb_ref)


def _merge(outs, lses, ys, gates, x2, w_glu, w_sp, w_ap, w_out, ln_g, ln_b, seqlen, tm):
    t = x2.shape[0]
    tiles = seqlen // tm
    stage_rows = max(tm // d * _row_pitch(d) for d in [d for _, d in DILATION_PATTERNS] + [SSM_CHUNK])
    n_dilated = sum(d > 1 for _, d in DILATION_PATTERNS)
    row = lambda w: pl.BlockSpec((tm, w), lambda i: (i, 0))
    res_spec = lambda d, w: pl.BlockSpec((None, d, tm // d, w), lambda i: (i // tiles, 0, i % tiles, 0))
    res_specs = [res_spec(d, GROUP_WIDTH) for _, d in DILATION_PATTERNS]
    return pl.pallas_call(
        _merge_kernel,
        grid=(t // tm,),
        in_specs=res_specs + res_specs + [res_spec(SSM_CHUNK, SSM_WIDTH), row(2 * D_MODEL), row(D_MODEL),
                  _const_spec(w_glu.shape), _const_spec(w_sp.shape), _const_spec(w_ap.shape),
                  _const_spec(w_out.shape), _const_spec((1, D_MODEL)), _const_spec((1, D_MODEL))],
        out_specs=row(D_MODEL),
        out_shape=jax.ShapeDtypeStruct((t, D_MODEL), F32),
        scratch_shapes=[pltpu.VMEM((n_dilated, GROUP_WIDTH // LANES, stage_rows, LANES), F32),
                        pltpu.VMEM((n_dilated, GROUP_WIDTH // LANES, stage_rows, LANES), F32),
                        pltpu.VMEM((SSM_WIDTH // LANES, stage_rows, LANES), F32)],
        compiler_params=_params(1),
        name="merge_ln1",
    )(*outs, *lses, ys, gates, x2, w_glu, w_sp, w_ap, w_out, ln_g, ln_b)


FF_CHUNK = 1024
NORM_SLABS = 4


def _ffn_kernel(h_ref, wup_ref, wdn_ref, lng_ref, lnb_ref, o_ref):
    h = h_ref[...]
    hb = h.astype(BF16)
    acts = []
    for c in range(D_FF // FF_CHUNK):
        lo, hi = c * FF_CHUNK, (c + 1) * FF_CHUNK
        up = jnp.dot(hb, wup_ref[:, lo:hi], preferred_element_type=F32)
        acts.append(jnp.square(jnp.maximum(up, 0.0)).astype(BF16))
    act = jnp.concatenate(acts, axis=-1)
    _project_and_norm(o_ref, h, act, wdn_ref, lng_ref, lnb_ref)


def _ffn(h1, w_up, w_down, ln_g, ln_b, tm):
    t = h1.shape[0]
    row = pl.BlockSpec((tm, D_MODEL), lambda i: (i, 0))
    return pl.pallas_call(
        _ffn_kernel,
        grid=(t // tm,),
        in_specs=[row, _const_spec(w_up.shape), _const_spec(w_down.shape),
                  _const_spec((1, D_MODEL)), _const_spec((1, D_MODEL))],
        out_specs=row,
        out_shape=jax.ShapeDtypeStruct((t, D_MODEL), F32),
        compiler_params=_params(1),
        name="ffn_ln2",
    )(h1, w_up, w_down, ln_g, ln_b)


def _permute_w_in(w):
    aw = ATTN_WIDTH
    cols = []
    for gi in range(N_GROUPS):
        lo, hi = gi * GROUP_WIDTH, (gi + 1) * GROUP_WIDTH
        cols += [w[:, lo:hi] * (HEAD_DIM ** -0.5), w[:, aw + lo:aw + hi], w[:, 2 * aw + lo:2 * aw + hi]]
    cols.append(w[:, 3 * aw:])
    return jnp.concatenate(cols, axis=1).astype(BF16)


def _layer(h2, bsz, seqlen, l, w_in, b_gate, lambda_re, lambda_im, log_dt, ssm_b_re, ssm_b_im,
           ssm_c_re, ssm_c_im, ssm_d, w_glu, w_ssm_proj, rel_bias, w_attn_proj, w_out,
           ln1_g, ln1_b, w_up, w_down, ln2_g, ln2_b, tm=1024, attn_mt=2048, ffn_tm=1024):
    qkv0, qkv1, qkv2, u, gates = _in_proj(h2, _permute_w_in(w_in[l]), b_gate[l][None, :], bsz, seqlen, tm)
    outs, lses = [], []
    for gi, ((window, dilation), qkv) in enumerate(zip(DILATION_PATTERNS, (qkv0, qkv1, qkv2))):
        rb = rel_bias[:, gi * HEADS_PER_GROUP:(gi + 1) * HEADS_PER_GROUP].astype(F32)
        o, s = _attention_group(qkv, rb, window, dilation, attn_mt)
        outs.append(o)
        lses.append(s)
    ssm_w = _ssm_weights(lambda_re[l], lambda_im[l], log_dt[l], ssm_b_re[l], ssm_b_im[l],
                         ssm_c_re[l], ssm_c_im[l], ssm_d[l])
    ys = _ssm(u, ssm_w, nb=4)
    h1 = _merge(outs, lses, ys, gates, h2, w_glu[l].astype(BF16), w_ssm_proj[l].astype(BF16),
                w_attn_proj[l].astype(BF16), w_out[l].astype(BF16),
                ln1_g[l][None, :], ln1_b[l][None, :], seqlen, tm)
    return _ffn(h1, w_up[l].astype(BF16), w_down[l].astype(BF16), ln2_g[l][None, :], ln2_b[l][None, :], ffn_tm)


def kernel(x, w_in, b_gate, lambda_re, lambda_im, log_dt, ssm_b_re, ssm_b_im, ssm_c_re, ssm_c_im,
           ssm_d, w_glu, w_ssm_proj, rel_bias, w_attn_proj, w_out, ln1_g, ln1_b, w_up, w_down,
           ln2_g, ln2_b):
    bsz, seqlen, d = x.shape
    h = x.reshape(bsz * seqlen, d)
    for l in range(w_in.shape[0]):
        h = _layer(h, bsz, seqlen, l, w_in, b_gate, lambda_re, lambda_im, log_dt, ssm_b_re, ssm_b_im,
                   ssm_c_re, ssm_c_im, ssm_d, w_glu, w_ssm_proj, rel_bias, w_attn_proj, w_out,
                   ln1_g, ln1_b, w_up, w_down, ln2_g, ln2_b)
    return h.reshape(bsz, seqlen, d)
```

```python
import functools
import math

import jax
import jax.numpy as jnp
from jax import lax
from jax.experimental import pallas as pl
from jax.experimental.pallas import tpu as pltpu

F32 = jnp.float32
BF16 = jnp.bfloat16

D_MODEL = 1024
HEAD_DIM = 64
HEADS_PER_GROUP = 4
GROUP_WIDTH = HEADS_PER_GROUP * HEAD_DIM
DILATION_PATTERNS = ((128, 1), (512, 4), (2048, 16))
N_GROUPS = len(DILATION_PATTERNS)
ATTN_WIDTH = N_GROUPS * GROUP_WIDTH
N_BUCKETS = 32
MAX_DISTANCE = 2048
SSM_WIDTH = 512
SSM_GROUP = 16
SSM_GROUPS = 32
SSM_STATE = 64
D_FF = 4 * D_MODEL
DEPTH = 1
ALPHA = (2.0 * DEPTH) ** 0.25
LN_EPS = 1e-5
NEG_INF = -1e30

ATTN_BLOCK = 128
ATTN_SKEW = (4, 7)
SSM_CHUNK = 16
SSM_OCT = 8
LANES = 128
SUBLANES = 8
VMEM_LIMIT_BYTES = 56 * 1024 * 1024


def _params(n_axes):
    return pltpu.CompilerParams(dimension_semantics=("arbitrary",) * n_axes,
                                vmem_limit_bytes=VMEM_LIMIT_BYTES)


def _sigmoid(z):
    return 0.5 * jnp.tanh(0.5 * z) + 0.5


def _row_pitch(stride):
    return stride + SUBLANES // 2 if stride % (2 * SUBLANES) == 0 else stride


def _const_spec(shape):
    nd = len(shape)
    return pl.BlockSpec(shape, lambda *_: (0,) * nd, pipeline_mode=pl.Buffered(1))


def _in_proj_kernel(x_ref, w_ref, bg_ref, qkv0_ref, qkv1_ref, qkv2_ref, u_ref, g_ref, scr):
    xb = x_ref[...].astype(BF16)
    tm = xb.shape[0]

    def mm(lo, hi):
        return jnp.dot(xb, w_ref[:, lo:hi], preferred_element_type=F32)

    def emit(ref, res, d, col0):
        width = res.shape[1]
        if d == 1:
            ref[0, :, col0:col0 + width] = res.astype(BF16)
            return
        slot0 = next_slot[0]
        next_slot[0] += width // LANES
        pitch = _row_pitch(d)
        for k in range(width // LANES):
            tile = res[:, k * LANES:(k + 1) * LANES]
            if pitch == d:
                scr[slot0 + k, 0:tm, :] = tile
            else:
                for g in range(tm // d):
                    scr[slot0 + k, g * pitch:g * pitch + d, :] = tile[g * d:(g + 1) * d]
        for r in range(d):
            for k in range(width // LANES):
                col = col0 + k * LANES
                ref[r, :, col:col + LANES] = scr[slot0 + k, pl.ds(r, tm // d, stride=pitch), :].astype(BF16)

    next_slot = [0]

    gw3 = 3 * GROUP_WIDTH
    for gi, ref in enumerate((qkv0_ref, qkv1_ref, qkv2_ref)):
        for c in range(3):
            lo = gi * gw3 + c * GROUP_WIDTH
            emit(ref, mm(lo, lo + GROUP_WIDTH), DILATION_PATTERNS[gi][1], c * GROUP_WIDTH)
    base = N_GROUPS * gw3
    emit(u_ref, mm(base, base + SSM_WIDTH), SSM_CHUNK, 0)
    base += SSM_WIDTH
    for c in range(2 * D_MODEL // GROUP_WIDTH):
        lo, hi = c * GROUP_WIDTH, (c + 1) * GROUP_WIDTH
        z = mm(base + lo, base + hi) + bg_ref[:, lo:hi]
        g_ref[:, lo:hi] = _sigmoid(z).astype(BF16)


def _in_proj(x2, w_perm, b_gate, bsz, seqlen, tm):
    t = x2.shape[0]
    n_in = w_perm.shape[1]
    tiles = seqlen // tm
    row = lambda w: pl.BlockSpec((tm, w), lambda i: (i, 0))
    w3 = 3 * GROUP_WIDTH
    dils = [d for _, d in DILATION_PATTERNS]
    n_stage = (sum(d > 1 for d in dils) * w3 + SSM_WIDTH) // LANES
    res_spec = lambda d, w:pl.BlockSpec((None, d, tm // d, w), lambda i: (i // tiles, 0, i % tiles, 0))
    return pl.pallas_call(
        _in_proj_kernel,
        grid=(t // tm,),
        in_specs=[row(D_MODEL), _const_spec((D_MODEL, n_in)), _const_spec((1, 2 * D_MODEL))],
        out_specs=[res_spec(d, w3) for d in dils] + [res_spec(SSM_CHUNK, SSM_WIDTH), row(2 * D_MODEL)],
        out_shape=[jax.ShapeDtypeStruct((bsz, d, seqlen // d, w3), BF16) for d in dils]
        + [jax.ShapeDtypeStruct((bsz, SSM_CHUNK, seqlen // SSM_CHUNK, SSM_WIDTH), BF16),
           jax.ShapeDtypeStruct((t, 2 * D_MODEL), BF16)],
        scratch_shapes=[pltpu.VMEM((n_stage, max(tm // d * _row_pitch(d) for d in dils + [SSM_CHUNK]), LANES), F32)],
        compiler_params=_params(1),
        name="in_proj",
    )(x2, w_perm, b_gate)


def _attn_kernel(bucket_ref, relb_ref, cur_ref, prev_ref, out_ref, lse_ref, bias_scr, k_scr, vt_scr, *, nq, nres):
    blk = ATTN_BLOCK
    first = (pl.program_id(0) == 0) & (pl.program_id(1) == 0) & (pl.program_id(2) == 0)

    @pl.when(first)
    def _build_bias():
        bucket = bucket_ref[...]
        for h in range(HEADS_PER_GROUP):
            acc = jnp.full(bucket.shape, NEG_INF, F32)
            for bkt in range(N_BUCKETS):
                acc = jnp.where(bucket == bkt, relb_ref[bkt, h], acc)
            bias_scr[h // 2, :, (h % 2) * blk:(h % 2 + 1) * blk] = acc

    row = lax.broadcasted_iota(jnp.int32, (2 * blk, blk), 0)
    row_head = row // HEAD_DIM
    keep_first = (row >= blk) | (pl.program_id(2) > 0)
    for r in range(nres):
        k_scr[r, 0:blk, :] = prev_ref[r, :, GROUP_WIDTH:2 * GROUP_WIDTH]
        k_scr[r, blk:, :] = cur_ref[r, :, GROUP_WIDTH:2 * GROUP_WIDTH]
        vt_scr[r, :, 0:blk] = prev_ref[r, :, 2 * GROUP_WIDTH:3 * GROUP_WIDTH].T
        for j in range(nq):
            vt_scr[r, :, (j + 1) * blk:(j + 2) * blk] = cur_ref[r, j * blk:(j + 1) * blk,
                                                                 2 * GROUP_WIDTH:3 * GROUP_WIDTH].T

    npair = HEADS_PER_GROUP // 2
    units = [(r, j, hp) for r in range(nres) for j in range(nq) for hp in range(npair)]
    qts = {}

    def scores(r, j, hp):
        if (r, j) not in qts:
            qts[(r, j)] = cur_ref[r, j * blk:(j + 1) * blk, 0:GROUP_WIDTH].T
        qt = qts[(r, j)]
        k2 = k_scr[r, j * blk:(j + 2) * blk, :]
        qh = jnp.concatenate([jnp.where(row_head == 2 * hp + i, qt, jnp.zeros_like(qt)) for i in range(2)], axis=1)
        return jnp.dot(k2, qh, preferred_element_type=F32)

    def softmax(r, j, hp, s2):
        ps, ms, ls = [], [], []
        for i in range(2):
            lanes = slice(i * blk, (i + 1) * blk)
            s = s2[:, lanes] + bias_scr[hp, :, lanes]
            if j == 0:
                s = jnp.where(keep_first, s, NEG_INF)
            m = jnp.max(s, axis=0, keepdims=True)
            p = jnp.exp(s - m)
            ls.append(jnp.sum(p, axis=0, keepdims=True))
            ps.append(p.astype(BF16))
            ms.append(m)
        return jnp.concatenate(ps, axis=1), jnp.concatenate(ms, axis=1), jnp.concatenate(ls, axis=1)

    def values(r, j, hp, p, m, l):
        vt = vt_scr[r, 2 * hp * HEAD_DIM:2 * (hp + 1) * HEAD_DIM, j * blk:(j + 2) * blk]
        ot = jnp.dot(vt, p, preferred_element_type=F32) * (1.0 / l)
        lse = jnp.broadcast_to(m + jnp.log(l), (HEAD_DIM, 2 * blk))
        return [(ot[i * HEAD_DIM:(i + 1) * HEAD_DIM, i * blk:(i + 1) * blk], lse[:, i * blk:(i + 1) * blk])
                for i in range(2)]

    raw, soft, done = {}, {}, []
    lag_soft, lag_val = ATTN_SKEW
    for step in range(len(units) + lag_val):
        if step < len(units):
            raw[step] = scores(*units[step])
        if lag_soft <= step < len(units) + lag_soft:
            soft[step - lag_soft] = softmax(*units[step - lag_soft], raw.pop(step - lag_soft))
        if step >= lag_val:
            i = step - lag_val
            r, j, hp = units[i]
            done += values(r, j, hp, *soft.pop(i))
            if hp == npair - 1:
                rows = slice(j * blk, (j + 1) * blk)
                out_ref[r, rows, :] = jnp.concatenate([d[0] for d in done], axis=0).T.astype(BF16)
                lse_ref[r, rows, :] = jnp.concatenate([d[1] for d in done], axis=0).T
                done = []


def _t5_bucket(dist):
    max_exact = N_BUCKETS // 2
    n_log = N_BUCKETS - max_exact
    thresholds = [math.ceil(max_exact * (MAX_DISTANCE / max_exact) ** (k / n_log)) for k in range(1, n_log)]
    large = max_exact + sum((dist >= t).astype(jnp.int32) for t in thresholds)
    return jnp.where(dist < max_exact, dist, large)


def _bucket_table(window, dilation):
    blk = ATTN_BLOCK
    span = window // dilation
    rel = jnp.arange(blk, dtype=jnp.int32)[None, :] + blk - jnp.arange(2 * blk, dtype=jnp.int32)[:, None]
    valid = (rel >= 0) & (rel <= span)
    return jnp.where(valid, _t5_bucket(jnp.maximum(rel, 0) * dilation), -1)


def _attention_group(qkv, rel_bias_g, window, dilation, mt):
    assert window // dilation == ATTN_BLOCK
    bsz, _, n, w3 = qkv.shape
    rows = mt
    mt = min(rows, n)
    nres = rows // mt
    nq = mt // ATTN_BLOCK
    cur = pl.BlockSpec((None, nres, mt, w3), lambda b, r, i: (b, r, i, 0))
    prev = pl.BlockSpec((None, nres, ATTN_BLOCK, w3), lambda b, r, i: (b, r, jnp.maximum(i * nq - 1, 0), 0))
    ospec = pl.BlockSpec((None, nres, mt, GROUP_WIDTH), lambda b, r, i: (b, r, i, 0))
    return pl.pallas_call(
        functools.partial(_attn_kernel, nq=nq, nres=nres),
        grid=(bsz, dilation // nres, n // mt),
        in_specs=[_const_spec((2 * ATTN_BLOCK, ATTN_BLOCK)),
                  pl.BlockSpec(memory_space=pltpu.SMEM), cur, prev],
        out_specs=[ospec, ospec],
        out_shape=[jax.ShapeDtypeStruct((bsz, dilation, n, GROUP_WIDTH), BF16),
                   jax.ShapeDtypeStruct((bsz, dilation, n, GROUP_WIDTH), F32)],
        scratch_shapes=[pltpu.VMEM((HEADS_PER_GROUP // 2, 2 * ATTN_BLOCK, 2 * ATTN_BLOCK), F32),
                        pltpu.VMEM((nres, mt + ATTN_BLOCK, GROUP_WIDTH), BF16),
                        pltpu.VMEM((nres, GROUP_WIDTH, mt + ATTN_BLOCK), BF16)],
        compiler_params=_params(3),
        name=f"attn_d{dilation}",
    )(_bucket_table(window, dilation), rel_bias_g, qkv, qkv)


def _ssm_prep_kernel(logdt_ref, lr_ref, li_ref, bt_re_ref, bt_im_ref, c_re_ref, c_im_ref, dl_ref,
                     toep_ref, bst_ref, cst_ref, a_ref):
    for g in range(SSM_OCT):
        _ssm_prep_group(g, logdt_ref[pl.program_id(0) * SSM_OCT + g], lr_ref, li_ref, bt_re_ref, bt_im_ref,
                        c_re_ref, c_im_ref, dl_ref, toep_ref, bst_ref, cst_ref, a_ref)


def _ssm_prep_group(g, log_dt, lr_ref, li_ref, bt_re_ref, bt_im_ref, c_re_ref, c_im_ref, dl_ref,
                    toep_ref, bst_ref, cst_ref, a_ref):
    cs = SSM_CHUNK
    dt = jnp.exp(jnp.full((1, SSM_STATE), log_dt, F32))
    lr, li = lr_ref[g], li_ref[g]
    mag = jnp.exp(lr * dt)
    ab_re, ab_im = mag * jnp.cos(li * dt), mag * jnp.sin(li * dt)
    den = lr * lr + li * li
    nr = ab_re - 1.0
    k_re = (nr * lr + ab_im * li) / den
    k_im = (ab_im * lr - nr * li) / den
    bt_re, bt_im = bt_re_ref[g], bt_im_ref[g]
    bb_re = k_re * bt_re - k_im * bt_im
    bb_im = k_re * bt_im + k_im * bt_re
    j = lax.broadcasted_iota(jnp.int32, (cs + SUBLANES, SSM_STATE), 0).astype(F32)
    pmag = jnp.exp(lr * dt * j)
    ang = li * dt * j
    p_re, p_im = pmag * jnp.cos(ang), pmag * jnp.sin(ang)
    c_re, c_im = c_re_ref[g], c_im_ref[g]
    cp_re = [c_re * p_re[i:i + 1] - c_im * p_im[i:i + 1] for i in range(cs + 1)]
    cp_im = [c_re * p_im[i:i + 1] + c_im * p_re[i:i + 1] for i in range(cs + 1)]
    nt = (((1,), (1,)), ((), ()))
    hi = lax.Precision.HIGHEST
    kcat = (lax.dot_general(bb_re, jnp.concatenate(cp_re[:cs], axis=0), nt, precision=hi,
                            preferred_element_type=F32)
            - lax.dot_general(bb_im, jnp.concatenate(cp_im[:cs], axis=0), nt, precision=hi,
                              preferred_element_type=F32))
    lane = lax.broadcasted_iota(jnp.int32, kcat.shape, 1)
    row = lax.broadcasted_iota(jnp.int32, kcat.shape, 0)
    dl = dl_ref[g]
    for s in range(cs):
        off = s * SSM_GROUP
        t_s = kcat if s == 0 else jnp.where(lane >= off, pltpu.roll(kcat, off, 1), 0.0)
        toep_ref[s, g] = (t_s + jnp.where(lane == off + row, dl, 0.0)).astype(BF16)
        pe_re, pe_im = p_re[cs - 1 - s:cs - s], p_im[cs - 1 - s:cs - s]
        st_re = pe_re * bb_re - pe_im * bb_im
        st_im = pe_re * bb_im + pe_im * bb_re
        bst_ref[s, g] = jnp.concatenate([st_re, st_im], axis=-1).astype(BF16)
    ro = jnp.concatenate([jnp.concatenate(cp_re[1:], axis=0),
                          -jnp.concatenate(cp_im[1:], axis=0)], axis=-1)
    ro_t = ro.T
    cst_ref[0, g] = ro_t[:SSM_STATE].astype(BF16)
    cst_ref[1, g] = ro_t[SSM_STATE:].astype(BF16)
    a_ref[g] = jnp.concatenate([p_re[cs:cs + 1], p_im[cs:cs + 1]], axis=0)


def _ssm_weights(lambda_re, lambda_im, log_dt, b_re, b_im, c_re, c_im, d_skip):
    f32 = F32
    cs, ng, no = SSM_CHUNK, SSM_GROUPS, SSM_GROUPS // SSM_OCT
    grp = lambda *shape: pl.BlockSpec((SSM_OCT,) + shape, lambda o: (o,) + (0,) * len(shape))
    in_oct = lambda *shape: pl.BlockSpec((None, shape[0], SSM_OCT) + shape[1:],
                                         lambda o: (o,) + (0,) * (len(shape) + 1))
    toep_c, bst_c, cst_c, a32 = pl.pallas_call(
        _ssm_prep_kernel,
        grid=(no,),
        in_specs=[pl.BlockSpec(memory_space=pltpu.SMEM), grp(1, SSM_STATE), grp(1, SSM_STATE),
                  grp(SSM_GROUP, SSM_STATE), grp(SSM_GROUP, SSM_STATE),
                  grp(SSM_GROUP, SSM_STATE), grp(SSM_GROUP, SSM_STATE), grp(1, cs * SSM_GROUP)],
        out_specs=[in_oct(cs, SSM_GROUP, cs * SSM_GROUP), in_oct(cs, SSM_GROUP, 2 * SSM_STATE),
                   in_oct(2, SSM_STATE, cs * SSM_GROUP), grp(2, SSM_STATE)],
        out_shape=[jax.ShapeDtypeStruct((no, cs, SSM_OCT, SSM_GROUP, cs * SSM_GROUP), BF16),
                   jax.ShapeDtypeStruct((no, cs, SSM_OCT, SSM_GROUP, 2 * SSM_STATE), BF16),
                   jax.ShapeDtypeStruct((no, 2, SSM_OCT, SSM_STATE, cs * SSM_GROUP), BF16),
                   jax.ShapeDtypeStruct((ng, 2, SSM_STATE), f32)],
        compiler_params=_params(1),
        name="ssm_prep",
    )(log_dt.astype(f32), lambda_re.astype(f32)[:, None, :], lambda_im.astype(f32)[:, None, :],
      b_re.astype(f32).transpose(0, 2, 1), b_im.astype(f32).transpose(0, 2, 1),
      c_re.astype(f32), c_im.astype(f32), jnp.tile(d_skip.astype(f32), (1, cs))[:, None, :])
    avec = a32.reshape(no, SSM_OCT, 2, SSM_STATE).transpose(0, 2, 1, 3).reshape(no, 2, SSM_OCT * SSM_STATE)
    return (toep_c.reshape(no, cs * LANES, cs * SSM_GROUP),
            bst_c.reshape(no, cs * LANES, 2 * SSM_STATE),
            cst_c.reshape(no, 2 * SSM_OCT * SSM_STATE, cs * SSM_GROUP),
            avec)


def _expand_ssm_weights(toep_ref, bstc_ref, cstc_ref, wt_scr, bst_scr, cst_scr):
    tw = 2 * LANES
    sh_g, sh_n, sh_o = (v.bit_length() - 1 for v in (SSM_GROUP, SSM_STATE, SSM_OCT))
    r = lax.broadcasted_iota(jnp.int32, (tw, tw), 0)
    c = lax.broadcasted_iota(jnp.int32, (tw, tw), 1)
    col_grp = (c >> sh_g) & (SSM_OCT - 1)
    row_grp = (r >> sh_g) & (SSM_OCT - 1)
    for tp in range(SSM_CHUNK // 2):
        src_col = (2 * tp + (c >> (sh_g + sh_o))) * SSM_GROUP + (c & (SSM_GROUP - 1))
        e = jnp.where(r == src_col, 1.0, 0.0).astype(BF16)
        base = tp * (tp + 1) // 2
        for sp in range(tp + 1):
            x = jnp.dot(toep_ref[sp * tw:(sp + 1) * tw, :], e, preferred_element_type=F32)
            wt_scr[base + sp] = jnp.where(row_grp == col_grp, x, 0.0).astype(BF16)
        for j in range(2 * SSM_OCT * SSM_STATE // tw):
            x = jnp.dot(cstc_ref[j * tw:(j + 1) * tw, :], e, preferred_element_type=F32)
            row_grp_n = ((j * tw + r) >> sh_n) & (SSM_OCT - 1)
            cst_scr[j * tw:(j + 1) * tw, tp * tw:(tp + 1) * tw] = jnp.where(row_grp_n == col_grp, x, 0.0).astype(BF16)
    ns = 2 * SSM_OCT * SSM_STATE
    rb = lax.broadcasted_iota(jnp.int32, (2 * SSM_STATE, ns), 0)
    cb = lax.broadcasted_iota(jnp.int32, (2 * SSM_STATE, ns), 1)
    src_col_b = (cb >> (sh_n + sh_o)) * SSM_STATE + (cb & (SSM_STATE - 1))
    eb = jnp.where(rb == src_col_b, 1.0, 0.0).astype(BF16)
    rr = lax.broadcasted_iota(jnp.int32, (tw, ns), 0)
    cc = lax.broadcasted_iota(jnp.int32, (tw, ns), 1)
    keep = ((rr >> sh_g) & (SSM_OCT - 1)) == ((cc >> sh_n) & (SSM_OCT - 1))
    for j in range(SSM_CHUNK * LANES // tw):
        x = jnp.dot(bstc_ref[j * tw:(j + 1) * tw, :], eb, preferred_element_type=F32)
        bst_scr[j * tw:(j + 1) * tw, :] = jnp.where(keep, x, 0.0).astype(BF16)


def _ssm_kernel(u_ref, toep_ref, bstc_ref, cstc_ref, a_ref, y_ref, wt_ref, bst_ref, cst_ref, zz_scr, hp_scr,
                *, n_chunks, nb):
    @pl.when(pl.program_id(1) == 0)
    def _new_octet():
        _expand_ssm_weights(toep_ref, bstc_ref, cstc_ref, wt_ref, bst_ref, cst_ref)

    rows = nb * n_chunks
    nk = SSM_OCT * SSM_STATE // LANES
    u_t = [u_ref[:, s].reshape(rows, LANES) for s in range(SSM_CHUNK)]
    z = jnp.dot(jnp.concatenate(u_t, axis=-1), bst_ref[...], preferred_element_type=F32)
    pitch = _row_pitch(n_chunks)
    for k in range(2 * nk):
        for b in range(nb):
            zz_scr[k, b * pitch:b * pitch + n_chunks, :] = z[b * n_chunks:(b + 1) * n_chunks, k * LANES:(k + 1) * LANES]
    a_re = [jnp.broadcast_to(a_ref[0:1, k * LANES:(k + 1) * LANES], (nb, LANES)) for k in range(nk)]
    a_im = [jnp.broadcast_to(a_ref[1:2, k * LANES:(k + 1) * LANES], (nb, LANES)) for k in range(nk)]

    def step(c, carry):
        h_re, h_im = carry
        rows_c = pl.ds(c, nb, stride=pitch)
        new_re, new_im = [], []
        for k in range(nk):
            hp_scr[k, rows_c, :] = h_re[k]
            hp_scr[nk + k, rows_c, :] = h_im[k]
            new_re.append(a_re[k] * h_re[k] - a_im[k] * h_im[k] + zz_scr[k, rows_c, :])
            new_im.append(a_re[k] * h_im[k] + a_im[k] * h_re[k] + zz_scr[nk + k, rows_c, :])
        return tuple(new_re), tuple(new_im)

    zero = tuple(jnp.zeros((nb, LANES), F32) for _ in range(nk))
    lax.fori_loop(0, n_chunks, step, (zero, zero), unroll=4)
    hp = jnp.concatenate(
        [jnp.concatenate([hp_scr[k, b * pitch:b * pitch + n_chunks, :] for b in range(nb)], axis=0)
         for k in range(2 * nk)], axis=-1).astype(BF16)
    tw = 2 * LANES
    for tp in range(SSM_CHUNK // 2):
        base = tp * (tp + 1) // 2
        y = jnp.dot(hp, cst_ref[:, tp * tw:(tp + 1) * tw], preferred_element_type=F32)
        y = y + jnp.dot(jnp.concatenate(u_t[:2 * (tp + 1)], axis=-1),
                        wt_ref[base:base + tp + 1].reshape((tp + 1) * tw, tw), preferred_element_type=F32)
        y = jax.nn.gelu(y).astype(BF16)
        y_ref[:, 2 * tp] = y[:, :LANES].reshape(nb, n_chunks, LANES)
        y_ref[:, 2 * tp + 1] = y[:, LANES:].reshape(nb, n_chunks, LANES)


def _ssm(u16, weights, nb):
    toep_c, bst_c, cst_c, avec = weights
    bsz, cs, nc, width = u16.shape
    oct_spec = lambda a: pl.BlockSpec((None,) + a.shape[1:], lambda o, b: (o,) + (0,) * (a.ndim - 1))
    io_spec = pl.BlockSpec((nb, cs, nc, LANES), lambda o, b: (b, 0, 0, o))
    n_state = 2 * SSM_OCT * SSM_STATE
    n_pairs = (cs // 2) * (cs // 2 + 1) // 2
    return pl.pallas_call(
        functools.partial(_ssm_kernel, n_chunks=nc, nb=nb),
        grid=(width // LANES, bsz // nb),
        in_specs=[io_spec, oct_spec(toep_c), oct_spec(bst_c), oct_spec(cst_c), oct_spec(avec)],
        out_specs=io_spec,
        out_shape=jax.ShapeDtypeStruct(u16.shape, BF16),
        scratch_shapes=[pltpu.VMEM((n_pairs, 2 * LANES, 2 * LANES), BF16),
                        pltpu.VMEM((cs * LANES, n_state), BF16),
                        pltpu.VMEM((n_state, cs * LANES), BF16),
                        pltpu.VMEM((n_state // LANES, nb * _row_pitch(nc), LANES), F32),
                        pltpu.VMEM((n_state // LANES, nb * _row_pitch(nc), LANES), F32)],
        compiler_params=_params(2),
        name="ssm",
    )(u16, toep_c, bst_c, cst_c, avec)


def _layer_norm(v, g, b):
    mu = jnp.mean(v, axis=-1, keepdims=True)
    vc = v - mu
    var = jnp.mean(vc * vc, axis=-1, keepdims=True)
    return vc * lax.rsqrt(var + LN_EPS) * g + b


def _project_and_norm(o_ref, resid, lhs, w_ref, g_ref, b_ref):
    slab = resid.shape[0] // NORM_SLABS
    vs = [ALPHA * resid[i * slab:(i + 1) * slab]
          + jnp.dot(lhs[i * slab:(i + 1) * slab], w_ref[...], preferred_element_type=F32)
          for i in range(NORM_SLABS)]
    for i, v in enumerate(vs):
        o_ref[i * slab:(i + 1) * slab, :] = _layer_norm(v, g_ref[...], b_ref[...])


def _merge_kernel(o0_ref, o1_ref, o2_ref, l0_ref, l1_ref, l2_ref, ys_ref, g_ref, x_ref,
                  wglu_ref, wsp_ref, wap_ref, wout_ref, lng_ref, lnb_ref, h_ref, o_scr, l_scr, y_scr):
    tm = x_ref.shape[0]

    def token_order(ref, scr, d):
        if d == 1:
            return ref[0].astype(F32)
        nk = ref.shape[-1] // LANES
        pitch = _row_pitch(d)
        for r in range(d):
            v = ref[r].astype(F32)
            for k in range(nk):
                scr[k, pl.ds(r, tm // d, stride=pitch), :] = v[:, k * LANES:(k + 1) * LANES]
        if pitch == d:
            return jnp.concatenate([scr[k, 0:tm, :] for k in range(nk)], axis=-1)
        return jnp.concatenate(
            [jnp.concatenate([scr[k, g * pitch:g * pitch + d, :] for g in range(tm // d)], axis=0)
             for k in range(nk)], axis=-1)

    ls, outs = [], []
    for gi, (o_ref, l_ref) in enumerate(((o0_ref, l0_ref), (o1_ref, l1_ref), (o2_ref, l2_ref))):
        d = DILATION_PATTERNS[gi][1]
        slot = sum(dd > 1 for _, dd in DILATION_PATTERNS[:gi])
        outs.append(token_order(o_ref, o_scr.at[slot], d))
        ls.append(token_order(l_ref, l_scr.at[slot], d))
    mx = jnp.maximum(jnp.maximum(ls[0], ls[1]), ls[2])
    es = [jnp.exp(l - mx) for l in ls]
    num = es[0] * outs[0] + es[1] * outs[1] + es[2] * outs[2]
    y_attn = (num / (es[0] + es[1] + es[2])).astype(BF16)
    ys = token_order(ys_ref, y_scr, SSM_CHUNK).astype(BF16)
    cw = GROUP_WIDTH
    y_ssm = []
    for c in range(SSM_WIDTH // cw):
        a = jnp.dot(ys, wglu_ref[:, c * cw:(c + 1) * cw], preferred_element_type=F32)
        b = jnp.dot(ys, wglu_ref[:, SSM_WIDTH + c * cw:SSM_WIDTH + (c + 1) * cw], preferred_element_type=F32)
        y_ssm.append((a * _sigmoid(b)).astype(BF16))
    y_ssm = jnp.concatenate(y_ssm, axis=-1)
    gated = []
    for c in range(D_MODEL // cw):
        pa = jnp.dot(y_ssm, wsp_ref[:, c * cw:(c + 1) * cw], preferred_element_type=F32)
        pb = jnp.dot(y_attn, wap_ref[:, c * cw:(c + 1) * cw], preferred_element_type=F32)
        g_ssm = g_ref[:, c * cw:(c + 1) * cw].astype(F32)
        g_attn = g_ref[:, D_MODEL + c * cw:D_MODEL + (c + 1) * cw].astype(F32)
        gated.append((g_ssm * pa + g_attn * pb).astype(BF16))
    gated = jnp.concatenate(gated, axis=-1)
    _project_and_norm(h_ref, x_ref, gated, wout_ref, lng_ref, lnb_ref)


def _merge(outs, lses, ys, gates, x2, w_glu, w_sp, w_ap, w_out, ln_g, ln_b, seqlen, tm):
    t = x2.shape[0]
    tiles = seqlen // tm
    stage_rows = max(tm // d * _row_pitch(d) for d in [d for _, d in DILATION_PATTERNS] + [SSM_CHUNK])
    n_dilated = sum(d > 1 for _, d in DILATION_PATTERNS)
    row = lambda w: pl.BlockSpec((tm, w), lambda i: (i, 0))
    res_spec = lambda d, w: pl.BlockSpec((None, d, tm // d, w), lambda i: (i // tiles, 0, i % tiles, 0))
    res_specs = [res_spec(d, GROUP_WIDTH) for _, d in DILATION_PATTERNS]
    return pl.pallas_call(
        _merge_kernel,
        grid=(t // tm,),
        in_specs=res_specs + res_specs + [res_spec(SSM_CHUNK, SSM_WIDTH), row(2 * D_MODEL), row(D_MODEL),
                  _const_spec(w_glu.shape), _const_spec(w_sp.shape), _const_spec(w_ap.shape),
                  _const_spec(w_out.shape), _const_spec((1, D_MODEL)), _const_spec((1, D_MODEL))],
        out_specs=row(D_MODEL),
        out_shape=jax.ShapeDtypeStruct((t, D_MODEL), F32),
        scratch_shapes=[pltpu.VMEM((n_dilated, GROUP_WIDTH // LANES, stage_rows, LANES), F32),
                        pltpu.VMEM((n_dilated, GROUP_WIDTH // LANES, stage_rows, LANES), F32),
                        pltpu.VMEM((SSM_WIDTH // LANES, stage_rows, LANES), F32)],
        compiler_params=_params(1),
        name="merge_ln1",
    )(*outs, *lses, ys, gates, x2, w_glu, w_sp, w_ap, w_out, ln_g, ln_b)


FF_CHUNK = 1024
NORM_SLABS = 4


def _ffn_kernel(h_ref, wup_ref, wdn_ref, lng_ref, lnb_ref, o_ref):
    h = h_ref[...]
    hb = h.astype(BF16)
    acts = []
    for c in range(D_FF // FF_CHUNK):
        lo, hi = c * FF_CHUNK, (c + 1) * FF_CHUNK
        up = jnp.dot(hb, wup_ref[:, lo:hi], preferred_element_type=F32)
        acts.append(jnp.square(jnp.maximum(up, 0.0)).astype(BF16))
    act = jnp.concatenate(acts, axis=-1)
    _project_and_norm(o_ref, h, act, wdn_ref, lng_ref, lnb_ref)


def _ffn(h1, w_up, w_down, ln_g, ln_b, tm):
    t = h1.shape[0]
    row = pl.BlockSpec((tm, D_MODEL), lambda i: (i, 0))
    return pl.pallas_call(
        _ffn_kernel,
        grid=(t // tm,),
        in_specs=[row, _const_spec(w_up.shape), _const_spec(w_down.shape),
                  _const_spec((1, D_MODEL)), _const_spec((1, D_MODEL))],
        out_specs=row,
        out_shape=jax.ShapeDtypeStruct((t, D_MODEL), F32),
        compiler_params=_params(1),
        name="ffn_ln2",
    )(h1, w_up, w_down, ln_g, ln_b)


def _permute_w_in(w):
    aw = ATTN_WIDTH
    cols = []
    for gi in range(N_GROUPS):
        lo, hi = gi * GROUP_WIDTH, (gi + 1) * GROUP_WIDTH
        cols += [w[:, lo:hi] * (HEAD_DIM ** -0.5), w[:, aw + lo:aw + hi], w[:, 2 * aw + lo:2 * aw + hi]]
    cols.append(w[:, 3 * aw:])
    return jnp.concatenate(cols, axis=1).astype(BF16)


def _layer(h2, bsz, seqlen, l, w_in, b_gate, lambda_re, lambda_im, log_dt, ssm_b_re, ssm_b_im,
           ssm_c_re, ssm_c_im, ssm_d, w_glu, w_ssm_proj, rel_bias, w_attn_proj, w_out,
           ln1_g, ln1_b, w_up, w_down, ln2_g, ln2_b, tm=1024, attn_mt=2048, ffn_tm=1024):
    qkv0, qkv1, qkv2, u, gates = _in_proj(h2, _permute_w_in(w_in[l]), b_gate[l][None, :], bsz, seqlen, tm)
    outs, lses = [], []
    for gi, ((window, dilation), qkv) in enumerate(zip(DILATION_PATTERNS, (qkv0, qkv1, qkv2))):
        rb = rel_bias[:, gi * HEADS_PER_GROUP:(gi + 1) * HEADS_PER_GROUP].astype(F32)
        o, s = _attention_group(qkv, rb, window, dilation, attn_mt)
        outs.append(o)
        lses.append(s)
    ssm_w = _ssm_weights(lambda_re[l], lambda_im[l], log_dt[l], ssm_b_re[l], ssm_b_im[l],
                         ssm_c_re[l], ssm_c_im[l], ssm_d[l])
    ys = _ssm(u, ssm_w, nb=4)
    h1 = _merge(outs, lses, ys, gates, h2, w_glu[l].astype(BF16), w_ssm_proj[l].astype(BF16),
                w_attn_proj[l].astype(BF16), w_out[l].astype(BF16),
                ln1_g[l][None, :], ln1_b[l][None, :], seqlen, tm)
    return _ffn(h1, w_up[l].astype(BF16), w_down[l].astype(BF16), ln2_g[l][None, :], ln2_b[l][None, :], ffn_tm)


def kernel(x, w_in, b_gate, lambda_re, lambda_im, log_dt, ssm_b_re, ssm_b_im, ssm_c_re, ssm_c_im,
           ssm_d, w_glu, w_ssm_proj, rel_bias, w_attn_proj, w_out, ln1_g, ln1_b, w_up, w_down,
           ln2_g, ln2_b):
    bsz, seqlen, d = x.shape
    h = x.reshape(bsz * seqlen, d)
    for l in range(w_in.shape[0]):
        h = _layer(h, bsz, seqlen, l, w_in, b_gate, lambda_re, lambda_im, log_dt, ssm_b_re, ssm_b_im,
                   ssm_c_re, ssm_c_im, ssm_d, w_glu, w_ssm_proj, rel_bias, w_attn_proj, w_out,
                   ln1_g, ln1_b, w_up, w_down, ln2_g, ln2_b)
    return h.reshape(bsz, seqlen, d)
```

```python
import functools
import math

import jax
import jax.numpy as jnp
from jax import lax
from jax.experimental import pallas as pl
from jax.experimental.pallas import tpu as pltpu

F32 = jnp.float32
BF16 = jnp.bfloat16

D_MODEL = 1024
HEAD_DIM = 64
HEADS_PER_GROUP = 4
GROUP_WIDTH = HEADS_PER_GROUP * HEAD_DIM
DILATION_PATTERNS = ((128, 1), (512, 4), (2048, 16))
N_GROUPS = len(DILATION_PATTERNS)
ATTN_WIDTH = N_GROUPS * GROUP_WIDTH
N_BUCKETS = 32
MAX_DISTANCE = 2048
SSM_WIDTH = 512
SSM_GROUP = 16
SSM_GROUPS = 32
SSM_STATE = 64
D_FF = 4 * D_MODEL
DEPTH = 1
ALPHA = (2.0 * DEPTH) ** 0.25
LN_EPS = 1e-5
NEG_INF = -1e30

ATTN_BLOCK = 128
ATTN_SKEW = (4, 7)
SSM_CHUNK = 16
SSM_OCT = 8
LANES = 128
SUBLANES = 8
VMEM_LIMIT_BYTES = 56 * 1024 * 1024


def _params(n_axes):
    return pltpu.CompilerParams(dimension_semantics=("arbitrary",) * n_axes,
                                vmem_limit_bytes=VMEM_LIMIT_BYTES)


def _sigmoid(z):
    return 0.5 * jnp.tanh(0.5 * z) + 0.5


def _row_pitch(stride):
    return stride + SUBLANES // 2 if stride % (2 * SUBLANES) == 0 else stride


def _const_spec(shape):
    nd = len(shape)
    return pl.BlockSpec(shape, lambda *_: (0,) * nd, pipeline_mode=pl.Buffered(1))


def _in_proj_kernel(x_ref, w_ref, bg_ref, qkv0_ref, qkv1_ref, qkv2_ref, u_ref, g_ref, scr):
    xb = x_ref[...].astype(BF16)
    tm = xb.shape[0]

    def mm(lo, hi):
        return jnp.dot(xb, w_ref[:, lo:hi], preferred_element_type=F32)

    def emit(ref, res, d, col0):
        width = res.shape[1]
        if d == 1:
            ref[0, :, col0:col0 + width] = res.astype(BF16)
            return
        slot0 = next_slot[0]
        next_slot[0] += width // LANES
        pitch = _row_pitch(d)
        for k in range(width // LANES):
            tile = res[:, k * LANES:(k + 1) * LANES]
            if pitch == d:
                scr[slot0 + k, 0:tm, :] = tile
            else:
                for g in range(tm // d):
                    scr[slot0 + k, g * pitch:g * pitch + d, :] = tile[g * d:(g + 1) * d]
        for r in range(d):
            for k in range(width // LANES):
                col = col0 + k * LANES
                ref[r, :, col:col + LANES] = scr[slot0 + k, pl.ds(r, tm // d, stride=pitch), :].astype(BF16)

    next_slot = [0]

    for gi, ref in enumerate((qkv0_ref, qkv1_ref, qkv2_ref)):
        for c in range(3):
            lo = c * ATTN_WIDTH + gi * GROUP_WIDTH
            emit(ref, mm(lo, lo + GROUP_WIDTH), DILATION_PATTERNS[gi][1], c * GROUP_WIDTH)
    base = 3 * ATTN_WIDTH
    emit(u_ref, mm(base, base + SSM_WIDTH), SSM_CHUNK, 0)
    base += SSM_WIDTH
    for c in range(2 * D_MODEL // GROUP_WIDTH):
        lo, hi = c * GROUP_WIDTH, (c + 1) * GROUP_WIDTH
        z = mm(base + lo, base + hi) + bg_ref[:, lo:hi]
        g_ref[:, lo:hi] = _sigmoid(z).astype(BF16)


def _in_proj(x2, w_perm, b_gate, bsz, seqlen, tm):
    t = x2.shape[0]
    n_in = w_perm.shape[1]
    tiles = seqlen // tm
    row = lambda w: pl.BlockSpec((tm, w), lambda i: (i, 0))
    w3 = 3 * GROUP_WIDTH
    dils = [d for _, d in DILATION_PATTERNS]
    n_stage = (sum(d > 1 for d in dils) * w3 + SSM_WIDTH) // LANES
    res_spec = lambda d, w:pl.BlockSpec((None, d, tm // d, w), lambda i: (i // tiles, 0, i % tiles, 0))
    return pl.pallas_call(
        _in_proj_kernel,
        grid=(t // tm,),
        in_specs=[row(D_MODEL), _const_spec((D_MODEL, n_in)), _const_spec((1, 2 * D_MODEL))],
        out_specs=[res_spec(d, w3) for d in dils] + [res_spec(SSM_CHUNK, SSM_WIDTH), row(2 * D_MODEL)],
        out_shape=[jax.ShapeDtypeStruct((bsz, d, seqlen // d, w3), BF16) for d in dils]
        + [jax.ShapeDtypeStruct((bsz, SSM_CHUNK, seqlen // SSM_CHUNK, SSM_WIDTH), BF16),
           jax.ShapeDtypeStruct((t, 2 * D_MODEL), BF16)],
        scratch_shapes=[pltpu.VMEM((n_stage, max(tm // d * _row_pitch(d) for d in dils + [SSM_CHUNK]), LANES), F32)],
        compiler_params=_params(1),
        name="in_proj",
    )(x2, w_perm, b_gate)


def _attn_kernel(bucket_ref, relb_ref, cur_ref, prev_ref, out_ref, lse_ref, bias_scr, k_scr, vt_scr, *, nq, nres):
    blk = ATTN_BLOCK
    first = (pl.program_id(0) == 0) & (pl.program_id(1) == 0) & (pl.program_id(2) == 0)

    @pl.when(first)
    def _build_bias():
        bucket = bucket_ref[...]
        for h in range(HEADS_PER_GROUP):
            acc = jnp.full(bucket.shape, NEG_INF, F32)
            for bkt in range(N_BUCKETS):
                acc = jnp.where(bucket == bkt, relb_ref[bkt, h], acc)
            bias_scr[h // 2, :, (h % 2) * blk:(h % 2 + 1) * blk] = acc

    row = lax.broadcasted_iota(jnp.int32, (2 * blk, blk), 0)
    row_head = row // HEAD_DIM
    keep_first = (row >= blk) | (pl.program_id(2) > 0)
    for r in range(nres):
        k_scr[r, 0:blk, :] = prev_ref[r, :, GROUP_WIDTH:2 * GROUP_WIDTH]
        k_scr[r, blk:, :] = cur_ref[r, :, GROUP_WIDTH:2 * GROUP_WIDTH]
        vt_scr[r, :, 0:blk] = prev_ref[r, :, 2 * GROUP_WIDTH:3 * GROUP_WIDTH].T
        for j in range(nq):
            vt_scr[r, :, (j + 1) * blk:(j + 2) * blk] = cur_ref[r, j * blk:(j + 1) * blk,
                                                                 2 * GROUP_WIDTH:3 * GROUP_WIDTH].T

    npair = HEADS_PER_GROUP // 2
    units = [(r, j, hp) for r in range(nres) for j in range(nq) for hp in range(npair)]
    qts = {}

    def scores(r, j, hp):
        if (r, j) not in qts:
            qts[(r, j)] = cur_ref[r, j * blk:(j + 1) * blk, 0:GROUP_WIDTH].T
        qt = qts[(r, j)]
        k2 = k_scr[r, j * blk:(j + 2) * blk, :]
        qh = jnp.concatenate([jnp.where(row_head == 2 * hp + i, qt, jnp.zeros_like(qt)) for i in range(2)], axis=1)
        return jnp.dot(k2, qh, preferred_element_type=F32)

    def softmax(r, j, hp, s2):
        ps, ms, ls = [], [], []
        for i in range(2):
            lanes = slice(i * blk, (i + 1) * blk)
            s = s2[:, lanes] + bias_scr[hp, :, lanes]
            if j == 0:
                s = jnp.where(keep_first, s, NEG_INF)
            m = jnp.max(s, axis=0, keepdims=True)
            p = jnp.exp(s - m)
            ls.append(jnp.sum(p, axis=0, keepdims=True))
            ps.append(p.astype(BF16))
            ms.append(m)
        return jnp.concatenate(ps, axis=1), jnp.concatenate(ms, axis=1), jnp.concatenate(ls, axis=1)

    def values(r, j, hp, p, m, l):
        vt = vt_scr[r, 2 * hp * HEAD_DIM:2 * (hp + 1) * HEAD_DIM, j * blk:(j + 2) * blk]
        ot = jnp.dot(vt, p, preferred_element_type=F32) * (1.0 / l)
        lse = jnp.broadcast_to(m + jnp.log(l), (HEAD_DIM, 2 * blk))
        return [(ot[i * HEAD_DIM:(i + 1) * HEAD_DIM, i * blk:(i + 1) * blk], lse[:, i * blk:(i + 1) * blk])
                for i in range(2)]

    raw, soft, done = {}, {}, []
    lag_soft, lag_val = ATTN_SKEW
    for step in range(len(units) + lag_val):
        if step < len(units):
            raw[step] = scores(*units[step])
        if lag_soft <= step < len(units) + lag_soft:
            soft[step - lag_soft] = softmax(*units[step - lag_soft], raw.pop(step - lag_soft))
        if step >= lag_val:
            i = step - lag_val
            r, j, hp = units[i]
            done += values(r, j, hp, *soft.pop(i))
            if hp == npair - 1:
                rows = slice(j * blk, (j + 1) * blk)
                out_ref[r, rows, :] = jnp.concatenate([d[0] for d in done], axis=0).T.astype(BF16)
                lse_ref[r, rows, :] = jnp.concatenate([d[1] for d in done], axis=0).T
                done = []


def _t5_bucket(dist):
    max_exact = N_BUCKETS // 2
    n_log = N_BUCKETS - max_exact
    thresholds = [math.ceil(max_exact * (MAX_DISTANCE / max_exact) ** (k / n_log)) for k in range(1, n_log)]
    large = max_exact + sum((dist >= t).astype(jnp.int32) for t in thresholds)
    return jnp.where(dist < max_exact, dist, large)


def _bucket_table(window, dilation):
    blk = ATTN_BLOCK
    span = window // dilation
    rel = jnp.arange(blk, dtype=jnp.int32)[None, :] + blk - jnp.arange(2 * blk, dtype=jnp.int32)[:, None]
    valid = (rel >= 0) & (rel <= span)
    return jnp.where(valid, _t5_bucket(jnp.maximum(rel, 0) * dilation), -1)


def _attention_group(qkv, rel_bias_g, window, dilation, mt):
    assert window // dilation == ATTN_BLOCK
    bsz, _, n, w3 = qkv.shape
    rows = mt
    mt = min(rows, n)
    nres = rows // mt
    nq = mt // ATTN_BLOCK
    cur = pl.BlockSpec((None, nres, mt, w3), lambda b, r, i: (b, r, i, 0))
    prev = pl.BlockSpec((None, nres, ATTN_BLOCK, w3), lambda b, r, i: (b, r, jnp.maximum(i * nq - 1, 0), 0))
    ospec = pl.BlockSpec((None, nres, mt, GROUP_WIDTH), lambda b, r, i: (b, r, i, 0))
    return pl.pallas_call(
        functools.partial(_attn_kernel, nq=nq, nres=nres),
        grid=(bsz, dilation // nres, n // mt),
        in_specs=[_const_spec((2 * ATTN_BLOCK, ATTN_BLOCK)),
                  pl.BlockSpec(memory_space=pltpu.SMEM), cur, prev],
        out_specs=[ospec, ospec],
        out_shape=[jax.ShapeDtypeStruct((bsz, dilation, n, GROUP_WIDTH), BF16),
                   jax.ShapeDtypeStruct((bsz, dilation, n, GROUP_WIDTH), F32)],
        scratch_shapes=[pltpu.VMEM((HEADS_PER_GROUP // 2, 2 * ATTN_BLOCK, 2 * ATTN_BLOCK), F32),
                        pltpu.VMEM((nres, mt + ATTN_BLOCK, GROUP_WIDTH), BF16),
                        pltpu.VMEM((nres, GROUP_WIDTH, mt + ATTN_BLOCK), BF16)],
        compiler_params=_params(3),
        name=f"attn_d{dilation}",
    )(_bucket_table(window, dilation), rel_bias_g, qkv, qkv)


def _ssm_prep_kernel(logdt_ref, lr_ref, li_ref, bt_re_ref, bt_im_ref, c_re_ref, c_im_ref, dl_ref,
                     toep_ref, bst_ref, cst_ref, a_ref):
    for g in range(SSM_OCT):
        _ssm_prep_group(g, logdt_ref[pl.program_id(0) * SSM_OCT + g], lr_ref, li_ref, bt_re_ref, bt_im_ref,
                        c_re_ref, c_im_ref, dl_ref, toep_ref, bst_ref, cst_ref, a_ref)


def _ssm_prep_group(g, log_dt, lr_ref, li_ref, bt_re_ref, bt_im_ref, c_re_ref, c_im_ref, dl_ref,
                    toep_ref, bst_ref, cst_ref, a_ref):
    cs = SSM_CHUNK
    dt = jnp.exp(jnp.full((1, SSM_STATE), log_dt, F32))
    lr, li = lr_ref[g], li_ref[g]
    mag = jnp.exp(lr * dt)
    ab_re, ab_im = mag * jnp.cos(li * dt), mag * jnp.sin(li * dt)
    den = lr * lr + li * li
    nr = ab_re - 1.0
    k_re = (nr * lr + ab_im * li) / den
    k_im = (ab_im * lr - nr * li) / den
    bt_re, bt_im = bt_re_ref[g], bt_im_ref[g]
    bb_re = k_re * bt_re - k_im * bt_im
    bb_im = k_re * bt_im + k_im * bt_re
    j = lax.broadcasted_iota(jnp.int32, (cs + SUBLANES, SSM_STATE), 0).astype(F32)
    pmag = jnp.exp(lr * dt * j)
    ang = li * dt * j
    p_re, p_im = pmag * jnp.cos(ang), pmag * jnp.sin(ang)
    c_re, c_im = c_re_ref[g], c_im_ref[g]
    cp_re = [c_re * p_re[i:i + 1] - c_im * p_im[i:i + 1] for i in range(cs + 1)]
    cp_im = [c_re * p_im[i:i + 1] + c_im * p_re[i:i + 1] for i in range(cs + 1)]
    nt = (((1,), (1,)), ((), ()))
    hi = lax.Precision.HIGHEST
    kcat = (lax.dot_general(bb_re, jnp.concatenate(cp_re[:cs], axis=0), nt, precision=hi,
                            preferred_element_type=F32)
            - lax.dot_general(bb_im, jnp.concatenate(cp_im[:cs], axis=0), nt, precision=hi,
                              preferred_element_type=F32))
    lane = lax.broadcasted_iota(jnp.int32, kcat.shape, 1)
    row = lax.broadcasted_iota(jnp.int32, kcat.shape, 0)
    dl = dl_ref[g]
    for s in range(cs):
        off = s * SSM_GROUP
        t_s = kcat if s == 0 else jnp.where(lane >= off, pltpu.roll(kcat, off, 1), 0.0)
        toep_ref[s, g] = (t_s + jnp.where(lane == off + row, dl, 0.0)).astype(BF16)
        pe_re, pe_im = p_re[cs - 1 - s:cs - s], p_im[cs - 1 - s:cs - s]
        st_re = pe_re * bb_re - pe_im * bb_im
        st_im = pe_re * bb_im + pe_im * bb_re
        bst_ref[s, g] = jnp.concatenate([st_re, st_im], axis=-1).astype(BF16)
    ro = jnp.concatenate([jnp.concatenate(cp_re[1:], axis=0),
                          -jnp.concatenate(cp_im[1:], axis=0)], axis=-1)
    ro_t = ro.T
    cst_ref[0, g] = ro_t[:SSM_STATE].astype(BF16)
    cst_ref[1, g] = ro_t[SSM_STATE:].astype(BF16)
    a_ref[g] = jnp.concatenate([p_re[cs:cs + 1], p_im[cs:cs + 1]], axis=0)


def _ssm_weights(lambda_re, lambda_im, log_dt, b_re, b_im, c_re, c_im, d_skip):
    f32 = F32
    cs, ng, no = SSM_CHUNK, SSM_GROUPS, SSM_GROUPS // SSM_OCT
    grp = lambda *shape: pl.BlockSpec((SSM_OCT,) + shape, lambda o: (o,) + (0,) * len(shape))
    in_oct = lambda *shape: pl.BlockSpec((None, shape[0], SSM_OCT) + shape[1:],
                                         lambda o: (o,) + (0,) * (len(shape) + 1))
    toep_c, bst_c, cst_c, a32 = pl.pallas_call(
        _ssm_prep_kernel,
        grid=(no,),
        in_specs=[pl.BlockSpec(memory_space=pltpu.SMEM), grp(1, SSM_STATE), grp(1, SSM_STATE),
                  grp(SSM_GROUP, SSM_STATE), grp(SSM_GROUP, SSM_STATE),
                  grp(SSM_GROUP, SSM_STATE), grp(SSM_GROUP, SSM_STATE), grp(1, cs * SSM_GROUP)],
        out_specs=[in_oct(cs, SSM_GROUP, cs * SSM_GROUP), in_oct(cs, SSM_GROUP, 2 * SSM_STATE),
                   in_oct(2, SSM_STATE, cs * SSM_GROUP), grp(2, SSM_STATE)],
        out_shape=[jax.ShapeDtypeStruct((no, cs, SSM_OCT, SSM_GROUP, cs * SSM_GROUP), BF16),
                   jax.ShapeDtypeStruct((no, cs, SSM_OCT, SSM_GROUP, 2 * SSM_STATE), BF16),
                   jax.ShapeDtypeStruct((no, 2, SSM_OCT, SSM_STATE, cs * SSM_GROUP), BF16),
                   jax.ShapeDtypeStruct((ng, 2, SSM_STATE), f32)],
        compiler_params=_params(1),
        name="ssm_prep",
    )(log_dt.astype(f32), lambda_re.astype(f32)[:, None, :], lambda_im.astype(f32)[:, None, :],
      b_re.astype(f32).transpose(0, 2, 1), b_im.astype(f32).transpose(0, 2, 1),
      c_re.astype(f32), c_im.astype(f32), jnp.tile(d_skip.astype(f32), (1, cs))[:, None, :])
    avec = a32.reshape(no, SSM_OCT, 2, SSM_STATE).transpose(0, 2, 1, 3).reshape(no, 2, SSM_OCT * SSM_STATE)
    return (toep_c.reshape(no, cs * LANES, cs * SSM_GROUP),
            bst_c.reshape(no, cs * LANES, 2 * SSM_STATE),
            cst_c.reshape(no, 2 * SSM_OCT * SSM_STATE, cs * SSM_GROUP),
            avec)


def _expand_ssm_weights(toep_ref, bstc_ref, cstc_ref, wt_scr, bst_scr, cst_scr):
    tw = 2 * LANES
    sh_g, sh_n, sh_o = (v.bit_length() - 1 for v in (SSM_GROUP, SSM_STATE, SSM_OCT))
    r = lax.broadcasted_iota(jnp.int32, (tw, tw), 0)
    c = lax.broadcasted_iota(jnp.int32, (tw, tw), 1)
    col_grp = (c >> sh_g) & (SSM_OCT - 1)
    row_grp = (r >> sh_g) & (SSM_OCT - 1)
    for tp in range(SSM_CHUNK // 2):
        src_col = (2 * tp + (c >> (sh_g + sh_o))) * SSM_GROUP + (c & (SSM_GROUP - 1))
        e = jnp.where(r == src_col, 1.0, 0.0).astype(BF16)
        base = tp * (tp + 1) // 2
        for sp in range(tp + 1):
            x = jnp.dot(toep_ref[sp * tw:(sp + 1) * tw, :], e, preferred_element_type=F32)
            wt_scr[base + sp] = jnp.where(row_grp == col_grp, x, 0.0).astype(BF16)
        for j in range(2 * SSM_OCT * SSM_STATE // tw):
            x = jnp.dot(cstc_ref[j * tw:(j + 1) * tw, :], e, preferred_element_type=F32)
            row_grp_n = ((j * tw + r) >> sh_n) & (SSM_OCT - 1)
            cst_scr[j * tw:(j + 1) * tw, tp * tw:(tp + 1) * tw] = jnp.where(row_grp_n == col_grp, x, 0.0).astype(BF16)
    ns = 2 * SSM_OCT * SSM_STATE
    rb = lax.broadcasted_iota(jnp.int32, (2 * SSM_STATE, ns), 0)
    cb = lax.broadcasted_iota(jnp.int32, (2 * SSM_STATE, ns), 1)
    src_col_b = (cb >> (sh_n + sh_o)) * SSM_STATE + (cb & (SSM_STATE - 1))
    eb = jnp.where(rb == src_col_b, 1.0, 0.0).astype(BF16)
    rr = lax.broadcasted_iota(jnp.int32, (tw, ns), 0)
    cc = lax.broadcasted_iota(jnp.int32, (tw, ns), 1)
    keep = ((rr >> sh_g) & (SSM_OCT - 1)) == ((cc >> sh_n) & (SSM_OCT - 1))
    for j in range(SSM_CHUNK * LANES // tw):
        x = jnp.dot(bstc_ref[j * tw:(j + 1) * tw, :], eb, preferred_element_type=F32)
        bst_scr[j * tw:(j + 1) * tw, :] = jnp.where(keep, x, 0.0).astype(BF16)


def _ssm_kernel(u_ref, toep_ref, bstc_ref, cstc_ref, a_ref, y_ref, wt_ref, bst_ref, cst_ref, zz_scr, hp_scr,
                *, n_chunks, nb):
    @pl.when(pl.program_id(1) == 0)
    def _new_octet():
        _expand_ssm_weights(toep_ref, bstc_ref, cstc_ref, wt_ref, bst_ref, cst_ref)

    rows = nb * n_chunks
    nk = SSM_OCT * SSM_STATE // LANES
    u_t = [u_ref[:, s].reshape(rows, LANES) for s in range(SSM_CHUNK)]
    z = jnp.dot(jnp.concatenate(u_t, axis=-1), bst_ref[...], preferred_element_type=F32)
    pitch = _row_pitch(n_chunks)
    for k in range(2 * nk):
        for b in range(nb):
            zz_scr[k, b * pitch:b * pitch + n_chunks, :] = z[b * n_chunks:(b + 1) * n_chunks, k * LANES:(k + 1) * LANES]
    a_re = [jnp.broadcast_to(a_ref[0:1, k * LANES:(k + 1) * LANES], (nb, LANES)) for k in range(nk)]
    a_im = [jnp.broadcast_to(a_ref[1:2, k * LANES:(k + 1) * LANES], (nb, LANES)) for k in range(nk)]

    def step(c, carry):
        h_re, h_im = carry
        rows_c = pl.ds(c, nb, stride=pitch)
        new_re, new_im = [], []
        for k in range(nk):
            hp_scr[k, rows_c, :] = h_re[k]
            hp_scr[nk + k, rows_c, :] = h_im[k]
            new_re.append(a_re[k] * h_re[k] - a_im[k] * h_im[k] + zz_scr[k, rows_c, :])
            new_im.append(a_re[k] * h_im[k] + a_im[k] * h_re[k] + zz_scr[nk + k, rows_c, :])
        return tuple(new_re), tuple(new_im)

    zero = tuple(jnp.zeros((nb, LANES), F32) for _ in range(nk))
    lax.fori_loop(0, n_chunks, step, (zero, zero), unroll=4)
    hp = jnp.concatenate(
        [jnp.concatenate([hp_scr[k, b * pitch:b * pitch + n_chunks, :] for b in range(nb)], axis=0)
         for k in range(2 * nk)], axis=-1).astype(BF16)
    tw = 2 * LANES
    for tp in range(SSM_CHUNK // 2):
        base = tp * (tp + 1) // 2
        y = jnp.dot(hp, cst_ref[:, tp * tw:(tp + 1) * tw], preferred_element_type=F32)
        y = y + jnp.dot(jnp.concatenate(u_t[:2 * (tp + 1)], axis=-1),
                        wt_ref[base:base + tp + 1].reshape((tp + 1) * tw, tw), preferred_element_type=F32)
        y = jax.nn.gelu(y).astype(BF16)
        y_ref[:, 2 * tp] = y[:, :LANES].reshape(nb, n_chunks, LANES)
        y_ref[:, 2 * tp + 1] = y[:, LANES:].reshape(nb, n_chunks, LANES)


def _ssm(u16, weights, nb):
    toep_c, bst_c, cst_c, avec = weights
    bsz, cs, nc, width = u16.shape
    oct_spec = lambda a: pl.BlockSpec((None,) + a.shape[1:], lambda o, b: (o,) + (0,) * (a.ndim - 1))
    io_spec = pl.BlockSpec((nb, cs, nc, LANES), lambda o, b: (b, 0, 0, o))
    n_state = 2 * SSM_OCT * SSM_STATE
    n_pairs = (cs // 2) * (cs // 2 + 1) // 2
    return pl.pallas_call(
        functools.partial(_ssm_kernel, n_chunks=nc, nb=nb),
        grid=(width // LANES, bsz // nb),
        in_specs=[io_spec, oct_spec(toep_c), oct_spec(bst_c), oct_spec(cst_c), oct_spec(avec)],
        out_specs=io_spec,
        out_shape=jax.ShapeDtypeStruct(u16.shape, BF16),
        scratch_shapes=[pltpu.VMEM((n_pairs, 2 * LANES, 2 * LANES), BF16),
                        pltpu.VMEM((cs * LANES, n_state), BF16),
                        pltpu.VMEM((n_state, cs * LANES), BF16),
                        pltpu.VMEM((n_state // LANES, nb * _row_pitch(nc), LANES), F32),
                        pltpu.VMEM((n_state // LANES, nb * _row_pitch(nc), LANES), F32)],
        compiler_params=_params(2),
        name="ssm",
    )(u16, toep_c, bst_c, cst_c, avec)


def _layer_norm(v, g, b):
    mu = jnp.mean(v, axis=-1, keepdims=True)
    vc = v - mu
    var = jnp.mean(vc * vc, axis=-1, keepdims=True)
    return vc * lax.rsqrt(var + LN_EPS) * g + b


def _project_and_norm(o_ref, resid, lhs, w_ref, g_ref, b_ref):
    slab = resid.shape[0] // NORM_SLABS
    vs = [ALPHA * resid[i * slab:(i + 1) * slab]
          + jnp.dot(lhs[i * slab:(i + 1) * slab], w_ref[...], preferred_element_type=F32)
          for i in range(NORM_SLABS)]
    for i, v in enumerate(vs):
        o_ref[i * slab:(i + 1) * slab, :] = _layer_norm(v, g_ref[...], b_ref[...])


def _merge_kernel(o0_ref, o1_ref, o2_ref, l0_ref, l1_ref, l2_ref, ys_ref, g_ref, x_ref,
                  wglu_ref, wsp_ref, wap_ref, wout_ref, lng_ref, lnb_ref, h_ref, o_scr, l_scr, y_scr):
    tm = x_ref.shape[0]

    def token_order(ref, scr, d):
        if d == 1:
            return ref[0].astype(F32)
        nk = ref.shape[-1] // LANES
        pitch = _row_pitch(d)
        for r in range(d):
            v = ref[r].astype(F32)
            for k in range(nk):
                scr[k, pl.ds(r, tm // d, stride=pitch), :] = v[:, k * LANES:(k + 1) * LANES]
        if pitch == d:
            return jnp.concatenate([scr[k, 0:tm, :] for k in range(nk)], axis=-1)
        return jnp.concatenate(
            [jnp.concatenate([scr[k, g * pitch:g * pitch + d, :] for g in range(tm // d)], axis=0)
             for k in range(nk)], axis=-1)

    ls, outs = [], []
    for gi, (o_ref, l_ref) in enumerate(((o0_ref, l0_ref), (o1_ref, l1_ref), (o2_ref, l2_ref))):
        d = DILATION_PATTERNS[gi][1]
        slot = sum(dd > 1 for _, dd in DILATION_PATTERNS[:gi])
        outs.append(token_order(o_ref, o_scr.at[slot], d))
        ls.append(token_order(l_ref, l_scr.at[slot], d))
    mx = jnp.maximum(jnp.maximum(ls[0], ls[1]), ls[2])
    es = [jnp.exp(l - mx) for l in ls]
    num = es[0] * outs[0] + es[1] * outs[1] + es[2] * outs[2]
    y_attn = (num / (es[0] + es[1] + es[2])).astype(BF16)
    ys = token_order(ys_ref, y_scr, SSM_CHUNK).astype(BF16)
    cw = GROUP_WIDTH
    y_ssm = []
    for c in range(SSM_WIDTH // cw):
        a = jnp.dot(ys, wglu_ref[:, c * cw:(c + 1) * cw], preferred_element_type=F32)
        b = jnp.dot(ys, wglu_ref[:, SSM_WIDTH + c * cw:SSM_WIDTH + (c + 1) * cw], preferred_element_type=F32)
        y_ssm.append((a * _sigmoid(b)).astype(BF16))
    y_ssm = jnp.concatenate(y_ssm, axis=-1)
    gated = []
    for c in range(D_MODEL // cw):
        pa = jnp.dot(y_ssm, wsp_ref[:, c * cw:(c + 1) * cw], preferred_element_type=F32)
        pb = jnp.dot(y_attn, wap_ref[:, c * cw:(c + 1) * cw], preferred_element_type=F32)
        g_ssm = g_ref[:, c * cw:(c + 1) * cw].astype(F32)
        g_attn = g_ref[:, D_MODEL + c * cw:D_MODEL + (c + 1) * cw].astype(F32)
        gated.append((g_ssm * pa + g_attn * pb).astype(BF16))
    gated = jnp.concatenate(gated, axis=-1)
    _project_and_norm(h_ref, x_ref, gated, wout_ref, lng_ref, lnb_ref)


def _merge(outs, lses, ys, gates, x2, w_glu, w_sp, w_ap, w_out, ln_g, ln_b, seqlen, tm):
    t = x2.shape[0]
    tiles = seqlen // tm
    stage_rows = max(tm // d * _row_pitch(d) for d in [d for _, d in DILATION_PATTERNS] + [SSM_CHUNK])
    n_dilated = sum(d > 1 for _, d in DILATION_PATTERNS)
    row = lambda w: pl.BlockSpec((tm, w), lambda i: (i, 0))
    res_spec = lambda d, w: pl.BlockSpec((None, d, tm // d, w), lambda i: (i // tiles, 0, i % tiles, 0))
    res_specs = [res_spec(d, GROUP_WIDTH) for _, d in DILATION_PATTERNS]
    return pl.pallas_call(
        _merge_kernel,
        grid=(t // tm,),
        in_specs=res_specs + res_specs + [res_spec(SSM_CHUNK, SSM_WIDTH), row(2 * D_MODEL), row(D_MODEL),
                  _const_spec(w_glu.shape), _const_spec(w_sp.shape), _const_spec(w_ap.shape),
                  _const_spec(w_out.shape), _const_spec((1, D_MODEL)), _const_spec((1, D_MODEL))],
        out_specs=row(D_MODEL),
        out_shape=jax.ShapeDtypeStruct((t, D_MODEL), F32),
        scratch_shapes=[pltpu.VMEM((n_dilated, GROUP_WIDTH // LANES, stage_rows, LANES), F32),
                        pltpu.VMEM((n_dilated, GROUP_WIDTH // LANES, stage_rows, LANES), F32),
                        pltpu.VMEM((SSM_WIDTH // LANES, stage_rows, LANES), F32)],
        compiler_params=_params(1),
        name="merge_ln1",
    )(*outs, *lses, ys, gates, x2, w_glu, w_sp, w_ap, w_out, ln_g, ln_b)


FF_CHUNK = 1024
NORM_SLABS = 4


def _ffn_kernel(h_ref, wup_ref, wdn_ref, lng_ref, lnb_ref, o_ref):
    h = h_ref[...]
    hb = h.astype(BF16)
    acts = []
    for c in range(D_FF // FF_CHUNK):
        lo, hi = c * FF_CHUNK, (c + 1) * FF_CHUNK
        up = jnp.dot(hb, wup_ref[:, lo:hi], preferred_element_type=F32)
        acts.append(jnp.square(jnp.maximum(up, 0.0)).astype(BF16))
    act = jnp.concatenate(acts, axis=-1)
    _project_and_norm(o_ref, h, act, wdn_ref, lng_ref, lnb_ref)


def _ffn(h1, w_up, w_down, ln_g, ln_b, tm):
    t = h1.shape[0]
    row = pl.BlockSpec((tm, D_MODEL), lambda i: (i, 0))
    return pl.pallas_call(
        _ffn_kernel,
        grid=(t // tm,),
        in_specs=[row, _const_spec(w_up.shape), _const_spec(w_down.shape),
                  _const_spec((1, D_MODEL)), _const_spec((1, D_MODEL))],
        out_specs=row,
        out_shape=jax.ShapeDtypeStruct((t, D_MODEL), F32),
        compiler_params=_params(1),
        name="ffn_ln2",
    )(h1, w_up, w_down, ln_g, ln_b)


def _prepare_w_in(w):
    col = jnp.arange(w.shape[1])
    return (w * jnp.where(col < ATTN_WIDTH, HEAD_DIM ** -0.5, 1.0)[None, :]).astype(BF16)


def _layer(h2, bsz, seqlen, l, w_in, b_gate, lambda_re, lambda_im, log_dt, ssm_b_re, ssm_b_im,
           ssm_c_re, ssm_c_im, ssm_d, w_glu, w_ssm_proj, rel_bias, w_attn_proj, w_out,
           ln1_g, ln1_b, w_up, w_down, ln2_g, ln2_b, tm=1024, attn_mt=2048, ffn_tm=1024):
    qkv0, qkv1, qkv2, u, gates = _in_proj(h2, _prepare_w_in(w_in[l]), b_gate[l][None, :], bsz, seqlen, tm)
    outs, lses = [], []
    for gi, ((window, dilation), qkv) in enumerate(zip(DILATION_PATTERNS, (qkv0, qkv1, qkv2))):
        rb = rel_bias[:, gi * HEADS_PER_GROUP:(gi + 1) * HEADS_PER_GROUP].astype(F32)
        o, s = _attention_group(qkv, rb, window, dilation, attn_mt)
        outs.append(o)
        lses.append(s)
    ssm_w = _ssm_weights(lambda_re[l], lambda_im[l], log_dt[l], ssm_b_re[l], ssm_b_im[l],
                         ssm_c_re[l], ssm_c_im[l], ssm_d[l])
    ys = _ssm(u, ssm_w, nb=4)
    h1 = _merge(outs, lses, ys, gates, h2, w_glu[l].astype(BF16), w_ssm_proj[l].astype(BF16),
                w_attn_proj[l].astype(BF16), w_out[l].astype(BF16),
                ln1_g[l][None, :], ln1_b[l][None, :], seqlen, tm)
    return _ffn(h1, w_up[l].astype(BF16), w_down[l].astype(BF16), ln2_g[l][None, :], ln2_b[l][None, :], ffn_tm)


def kernel(x, w_in, b_gate, lambda_re, lambda_im, log_dt, ssm_b_re, ssm_b_im, ssm_c_re, ssm_c_im,
           ssm_d, w_glu, w_ssm_proj, rel_bias, w_attn_proj, w_out, ln1_g, ln1_b, w_up, w_down,
           ln2_g, ln2_b):
    bsz, seqlen, d = x.shape
    h = x.reshape(bsz * seqlen, d)
    for l in range(w_in.shape[0]):
        h = _layer(h, bsz, seqlen, l, w_in, b_gate, lambda_re, lambda_im, log_dt, ssm_b_re, ssm_b_im,
                   ssm_c_re, ssm_c_im, ssm_d, w_glu, w_ssm_proj, rel_bias, w_attn_proj, w_out,
                   ln1_g, ln1_b, w_up, w_down, ln2_g, ln2_b)
    return h.reshape(bsz, seqlen, d)
```

```python
import functools
import math

import jax
import jax.numpy as jnp
from jax import lax
from jax.experimental import pallas as pl
from jax.experimental.pallas import tpu as pltpu

F32 = jnp.float32
BF16 = jnp.bfloat16

D_MODEL = 1024
HEAD_DIM = 64
HEADS_PER_GROUP = 4
GROUP_WIDTH = HEADS_PER_GROUP * HEAD_DIM
DILATION_PATTERNS = ((128, 1), (512, 4), (2048, 16))
N_GROUPS = len(DILATION_PATTERNS)
ATTN_WIDTH = N_GROUPS * GROUP_WIDTH
N_BUCKETS = 32
MAX_DISTANCE = 2048
SSM_WIDTH = 512
SSM_GROUP = 16
SSM_GROUPS = 32
SSM_STATE = 64
D_FF = 4 * D_MODEL
DEPTH = 1
ALPHA = (2.0 * DEPTH) ** 0.25
LN_EPS = 1e-5
NEG_INF = -1e30

ATTN_BLOCK = 128
ATTN_SKEW = (4, 7)
SSM_CHUNK = 16
SSM_OCT = 8
LANES = 128
SUBLANES = 8
VMEM_LIMIT_BYTES = 56 * 1024 * 1024


def _params(n_axes):
    return pltpu.CompilerParams(dimension_semantics=("arbitrary",) * n_axes,
                                vmem_limit_bytes=VMEM_LIMIT_BYTES)


def _sigmoid(z):
    return 0.5 * jnp.tanh(0.5 * z) + 0.5


def _row_pitch(stride):
    return stride + SUBLANES // 2 if stride % (2 * SUBLANES) == 0 else stride


def _const_spec(shape):
    nd = len(shape)
    return pl.BlockSpec(shape, lambda *_: (0,) * nd, pipeline_mode=pl.Buffered(1))


def _in_proj_kernel(x_ref, w_ref, bg_ref, qkv0_ref, qkv1_ref, qkv2_ref, u_ref, g_ref, scr, xb_scr):
    tm = x_ref.shape[0]
    xb_scr[...] = x_ref[...].astype(BF16)

    def mm(lo, hi):
        return jnp.dot(xb_scr[...], w_ref[:, lo:hi], preferred_element_type=F32)

    def emit(ref, res, d, col0):
        width = res.shape[1]
        if d == 1:
            ref[0, :, col0:col0 + width] = res.astype(BF16)
            return
        slot0 = next_slot[0]
        next_slot[0] += width // LANES
        pitch = _row_pitch(d)
        for k in range(width // LANES):
            tile = res[:, k * LANES:(k + 1) * LANES]
            if pitch == d:
                scr[slot0 + k, 0:tm, :] = tile
            else:
                for g in range(tm // d):
                    scr[slot0 + k, g * pitch:g * pitch + d, :] = tile[g * d:(g + 1) * d]
        for r in range(d):
            for k in range(width // LANES):
                col = col0 + k * LANES
                ref[r, :, col:col + LANES] = scr[slot0 + k, pl.ds(r, tm // d, stride=pitch), :].astype(BF16)

    next_slot = [0]

    gw3 = 3 * GROUP_WIDTH
    for gi, ref in enumerate((qkv0_ref, qkv1_ref, qkv2_ref)):
        for c in range(3):
            lo = gi * gw3 + c * GROUP_WIDTH
            emit(ref, mm(lo, lo + GROUP_WIDTH), DILATION_PATTERNS[gi][1], c * GROUP_WIDTH)
    base = N_GROUPS * gw3
    emit(u_ref, mm(base, base + SSM_WIDTH), SSM_CHUNK, 0)
    base += SSM_WIDTH
    for c in range(2 * D_MODEL // GROUP_WIDTH):
        lo, hi = c * GROUP_WIDTH, (c + 1) * GROUP_WIDTH
        z = mm(base + lo, base + hi) + bg_ref[:, lo:hi]
        g_ref[:, lo:hi] = _sigmoid(z).astype(BF16)


def _in_proj(x2, w_perm, b_gate, bsz, seqlen, tm):
    t = x2.shape[0]
    n_in = w_perm.shape[1]
    tiles = seqlen // tm
    row = lambda w: pl.BlockSpec((tm, w), lambda i: (i, 0))
    w3 = 3 * GROUP_WIDTH
    dils = [d for _, d in DILATION_PATTERNS]
    n_stage = (sum(d > 1 for d in dils) * w3 + SSM_WIDTH) // LANES
    res_spec = lambda d, w:pl.BlockSpec((None, d, tm // d, w), lambda i: (i // tiles, 0, i % tiles, 0))
    return pl.pallas_call(
        _in_proj_kernel,
        grid=(t // tm,),
        in_specs=[row(D_MODEL), _const_spec((D_MODEL, n_in)), _const_spec((1, 2 * D_MODEL))],
        out_specs=[res_spec(d, w3) for d in dils] + [res_spec(SSM_CHUNK, SSM_WIDTH), row(2 * D_MODEL)],
        out_shape=[jax.ShapeDtypeStruct((bsz, d, seqlen // d, w3), BF16) for d in dils]
        + [jax.ShapeDtypeStruct((bsz, SSM_CHUNK, seqlen // SSM_CHUNK, SSM_WIDTH), BF16),
           jax.ShapeDtypeStruct((t, 2 * D_MODEL), BF16)],
        scratch_shapes=[pltpu.VMEM((n_stage, max(tm // d * _row_pitch(d) for d in dils + [SSM_CHUNK]), LANES), F32),
                        pltpu.VMEM((tm, D_MODEL), BF16)],
        compiler_params=_params(1),
        name="in_proj",
    )(x2, w_perm, b_gate)


def _attn_kernel(bucket_ref, relb_ref, cur_ref, prev_ref, out_ref, lse_ref, bias_scr, k_scr, vt_scr, *, nq, nres):
    blk = ATTN_BLOCK
    first = (pl.program_id(0) == 0) & (pl.program_id(1) == 0) & (pl.program_id(2) == 0)

    @pl.when(first)
    def _build_bias():
        bucket = bucket_ref[...]
        for h in range(HEADS_PER_GROUP):
            acc = jnp.full(bucket.shape, NEG_INF, F32)
            for bkt in range(N_BUCKETS):
                acc = jnp.where(bucket == bkt, relb_ref[bkt, h], acc)
            bias_scr[h // 2, :, (h % 2) * blk:(h % 2 + 1) * blk] = acc

    row = lax.broadcasted_iota(jnp.int32, (2 * blk, blk), 0)
    row_head = row // HEAD_DIM
    keep_first = (row >= blk) | (pl.program_id(2) > 0)
    for r in range(nres):
        k_scr[r, 0:blk, :] = prev_ref[r, :, GROUP_WIDTH:2 * GROUP_WIDTH]
        k_scr[r, blk:, :] = cur_ref[r, :, GROUP_WIDTH:2 * GROUP_WIDTH]
        vt_scr[r, :, 0:blk] = prev_ref[r, :, 2 * GROUP_WIDTH:3 * GROUP_WIDTH].T
        for j in range(nq):
            vt_scr[r, :, (j + 1) * blk:(j + 2) * blk] = cur_ref[r, j * blk:(j + 1) * blk,
                                                                 2 * GROUP_WIDTH:3 * GROUP_WIDTH].T

    npair = HEADS_PER_GROUP // 2
    units = [(r, j, hp) for r in range(nres) for j in range(nq) for hp in range(npair)]
    qts = {}

    def scores(r, j, hp):
        if (r, j) not in qts:
            qts[(r, j)] = cur_ref[r, j * blk:(j + 1) * blk, 0:GROUP_WIDTH].T
        qt = qts[(r, j)]
        k2 = k_scr[r, j * blk:(j + 2) * blk, :]
        qh = jnp.concatenate([jnp.where(row_head == 2 * hp + i, qt, jnp.zeros_like(qt)) for i in range(2)], axis=1)
        return jnp.dot(k2, qh, preferred_element_type=F32)

    def softmax(r, j, hp, s2):
        ps, ms, ls = [], [], []
        for i in range(2):
            lanes = slice(i * blk, (i + 1) * blk)
            s = s2[:, lanes] + bias_scr[hp, :, lanes]
            if j == 0:
                s = jnp.where(keep_first, s, NEG_INF)
            m = jnp.max(s, axis=0, keepdims=True)
            p = jnp.exp(s - m)
            ls.append(jnp.sum(p, axis=0, keepdims=True))
            ps.append(p.astype(BF16))
            ms.append(m)
        return jnp.concatenate(ps, axis=1), jnp.concatenate(ms, axis=1), jnp.concatenate(ls, axis=1)

    def values(r, j, hp, p, m, l):
        vt = vt_scr[r, 2 * hp * HEAD_DIM:2 * (hp + 1) * HEAD_DIM, j * blk:(j + 2) * blk]
        ot = jnp.dot(vt, p, preferred_element_type=F32) * (1.0 / l)
        lse = jnp.broadcast_to(m + jnp.log(l), (HEAD_DIM, 2 * blk))
        return [(ot[i * HEAD_DIM:(i + 1) * HEAD_DIM, i * blk:(i + 1) * blk], lse[:, i * blk:(i + 1) * blk])
                for i in range(2)]

    raw, soft, done = {}, {}, []
    lag_soft, lag_val = ATTN_SKEW
    for step in range(len(units) + lag_val):
        if step < len(units):
            raw[step] = scores(*units[step])
        if lag_soft <= step < len(units) + lag_soft:
            soft[step - lag_soft] = softmax(*units[step - lag_soft], raw.pop(step - lag_soft))
        if step >= lag_val:
            i = step - lag_val
            r, j, hp = units[i]
            done += values(r, j, hp, *soft.pop(i))
            if hp == npair - 1:
                rows = slice(j * blk, (j + 1) * blk)
                out_ref[r, rows, :] = jnp.concatenate([d[0] for d in done], axis=0).T.astype(BF16)
                lse_ref[r, rows, :] = jnp.concatenate([d[1] for d in done], axis=0).T
                done = []


def _t5_bucket(dist):
    max_exact = N_BUCKETS // 2
    n_log = N_BUCKETS - max_exact
    thresholds = [math.ceil(max_exact * (MAX_DISTANCE / max_exact) ** (k / n_log)) for k in range(1, n_log)]
    large = max_exact + sum((dist >= t).astype(jnp.int32) for t in thresholds)
    return jnp.where(dist < max_exact, dist, large)


def _bucket_table(window, dilation):
    blk = ATTN_BLOCK
    span = window // dilation
    rel = jnp.arange(blk, dtype=jnp.int32)[None, :] + blk - jnp.arange(2 * blk, dtype=jnp.int32)[:, None]
    valid = (rel >= 0) & (rel <= span)
    return jnp.where(valid, _t5_bucket(jnp.maximum(rel, 0) * dilation), -1)


def _attention_group(qkv, rel_bias_g, window, dilation, mt):
    assert window // dilation == ATTN_BLOCK
    bsz, _, n, w3 = qkv.shape
    rows = mt
    mt = min(rows, n)
    nres = rows // mt
    nq = mt // ATTN_BLOCK
    cur = pl.BlockSpec((None, nres, mt, w3), lambda b, r, i: (b, r, i, 0))
    prev = pl.BlockSpec((None, nres, ATTN_BLOCK, w3), lambda b, r, i: (b, r, jnp.maximum(i * nq - 1, 0), 0))
    ospec = pl.BlockSpec((None, nres, mt, GROUP_WIDTH), lambda b, r, i: (b, r, i, 0))
    return pl.pallas_call(
        functools.partial(_attn_kernel, nq=nq, nres=nres),
        grid=(bsz, dilation // nres, n // mt),
        in_specs=[_const_spec((2 * ATTN_BLOCK, ATTN_BLOCK)),
                  pl.BlockSpec(memory_space=pltpu.SMEM), cur, prev],
        out_specs=[ospec, ospec],
        out_shape=[jax.ShapeDtypeStruct((bsz, dilation, n, GROUP_WIDTH), BF16),
                   jax.ShapeDtypeStruct((bsz, dilation, n, GROUP_WIDTH), F32)],
        scratch_shapes=[pltpu.VMEM((HEADS_PER_GROUP // 2, 2 * ATTN_BLOCK, 2 * ATTN_BLOCK), F32),
                        pltpu.VMEM((nres, mt + ATTN_BLOCK, GROUP_WIDTH), BF16),
                        pltpu.VMEM((nres, GROUP_WIDTH, mt + ATTN_BLOCK), BF16)],
        compiler_params=_params(3),
        name=f"attn_d{dilation}",
    )(_bucket_table(window, dilation), rel_bias_g, qkv, qkv)


def _ssm_prep_kernel(logdt_ref, lr_ref, li_ref, bt_re_ref, bt_im_ref, c_re_ref, c_im_ref, dl_ref,
                     toep_ref, bst_ref, cst_ref, a_ref):
    for g in range(SSM_OCT):
        _ssm_prep_group(g, logdt_ref[pl.program_id(0) * SSM_OCT + g], lr_ref, li_ref, bt_re_ref, bt_im_ref,
                        c_re_ref, c_im_ref, dl_ref, toep_ref, bst_ref, cst_ref, a_ref)


def _ssm_prep_group(g, log_dt, lr_ref, li_ref, bt_re_ref, bt_im_ref, c_re_ref, c_im_ref, dl_ref,
                    toep_ref, bst_ref, cst_ref, a_ref):
    cs = SSM_CHUNK
    dt = jnp.exp(jnp.full((1, SSM_STATE), log_dt, F32))
    lr, li = lr_ref[g], li_ref[g]
    mag = jnp.exp(lr * dt)
    ab_re, ab_im = mag * jnp.cos(li * dt), mag * jnp.sin(li * dt)
    den = lr * lr + li * li
    nr = ab_re - 1.0
    k_re = (nr * lr + ab_im * li) / den
    k_im = (ab_im * lr - nr * li) / den
    bt_re, bt_im = bt_re_ref[g], bt_im_ref[g]
    bb_re = k_re * bt_re - k_im * bt_im
    bb_im = k_re * bt_im + k_im * bt_re
    j = lax.broadcasted_iota(jnp.int32, (cs + SUBLANES, SSM_STATE), 0).astype(F32)
    pmag = jnp.exp(lr * dt * j)
    ang = li * dt * j
    p_re, p_im = pmag * jnp.cos(ang), pmag * jnp.sin(ang)
    c_re, c_im = c_re_ref[g], c_im_ref[g]
    cp_re = [c_re * p_re[i:i + 1] - c_im * p_im[i:i + 1] for i in range(cs + 1)]
    cp_im = [c_re * p_im[i:i + 1] + c_im * p_re[i:i + 1] for i in range(cs + 1)]
    nt = (((1,), (1,)), ((), ()))
    hi = lax.Precision.HIGHEST
    kcat = (lax.dot_general(bb_re, jnp.concatenate(cp_re[:cs], axis=0), nt, precision=hi,
                            preferred_element_type=F32)
            - lax.dot_general(bb_im, jnp.concatenate(cp_im[:cs], axis=0), nt, precision=hi,
                              preferred_element_type=F32))
    lane = lax.broadcasted_iota(jnp.int32, kcat.shape, 1)
    row = lax.broadcasted_iota(jnp.int32, kcat.shape, 0)
    dl = dl_ref[g]
    for s in range(cs):
        off = s * SSM_GROUP
        t_s = kcat if s == 0 else jnp.where(lane >= off, pltpu.roll(kcat, off, 1), 0.0)
        toep_ref[s, g] = (t_s + jnp.where(lane == off + row, dl, 0.0)).astype(BF16)
        pe_re, pe_im = p_re[cs - 1 - s:cs - s], p_im[cs - 1 - s:cs - s]
        st_re = pe_re * bb_re - pe_im * bb_im
        st_im = pe_re * bb_im + pe_im * bb_re
        bst_ref[s, g] = jnp.concatenate([st_re, st_im], axis=-1).astype(BF16)
    ro = jnp.concatenate([jnp.concatenate(cp_re[1:], axis=0),
                          -jnp.concatenate(cp_im[1:], axis=0)], axis=-1)
    ro_t = ro.T
    cst_ref[0, g] = ro_t[:SSM_STATE].astype(BF16)
    cst_ref[1, g] = ro_t[SSM_STATE:].astype(BF16)
    a_ref[g] = jnp.concatenate([p_re[cs:cs + 1], p_im[cs:cs + 1]], axis=0)


def _ssm_weights(lambda_re, lambda_im, log_dt, b_re, b_im, c_re, c_im, d_skip):
    f32 = F32
    cs, ng, no = SSM_CHUNK, SSM_GROUPS, SSM_GROUPS // SSM_OCT
    grp = lambda *shape: pl.BlockSpec((SSM_OCT,) + shape, lambda o: (o,) + (0,) * len(shape))
    in_oct = lambda *shape: pl.BlockSpec((None, shape[0], SSM_OCT) + shape[1:],
                                         lambda o: (o,) + (0,) * (len(shape) + 1))
    toep_c, bst_c, cst_c, a32 = pl.pallas_call(
        _ssm_prep_kernel,
        grid=(no,),
        in_specs=[pl.BlockSpec(memory_space=pltpu.SMEM), grp(1, SSM_STATE), grp(1, SSM_STATE),
                  grp(SSM_GROUP, SSM_STATE), grp(SSM_GROUP, SSM_STATE),
                  grp(SSM_GROUP, SSM_STATE), grp(SSM_GROUP, SSM_STATE), grp(1, cs * SSM_GROUP)],
        out_specs=[in_oct(cs, SSM_GROUP, cs * SSM_GROUP), in_oct(cs, SSM_GROUP, 2 * SSM_STATE),
                   in_oct(2, SSM_STATE, cs * SSM_GROUP), grp(2, SSM_STATE)],
        out_shape=[jax.ShapeDtypeStruct((no, cs, SSM_OCT, SSM_GROUP, cs * SSM_GROUP), BF16),
                   jax.ShapeDtypeStruct((no, cs, SSM_OCT, SSM_GROUP, 2 * SSM_STATE), BF16),
                   jax.ShapeDtypeStruct((no, 2, SSM_OCT, SSM_STATE, cs * SSM_GROUP), BF16),
                   jax.ShapeDtypeStruct((ng, 2, SSM_STATE), f32)],
        compiler_params=_params(1),
        name="ssm_prep",
    )(log_dt.astype(f32), lambda_re.astype(f32)[:, None, :], lambda_im.astype(f32)[:, None, :],
      b_re.astype(f32).transpose(0, 2, 1), b_im.astype(f32).transpose(0, 2, 1),
      c_re.astype(f32), c_im.astype(f32), jnp.tile(d_skip.astype(f32), (1, cs))[:, None, :])
    avec = a32.reshape(no, SSM_OCT, 2, SSM_STATE).transpose(0, 2, 1, 3).reshape(no, 2, SSM_OCT * SSM_STATE)
    return (toep_c.reshape(no, cs * LANES, cs * SSM_GROUP),
            bst_c.reshape(no, cs * LANES, 2 * SSM_STATE),
            cst_c.reshape(no, 2 * SSM_OCT * SSM_STATE, cs * SSM_GROUP),
            avec)


def _expand_ssm_weights(toep_ref, bstc_ref, cstc_ref, wt_scr, bst_scr, cst_scr):
    tw = 2 * LANES
    sh_g, sh_n, sh_o = (v.bit_length() - 1 for v in (SSM_GROUP, SSM_STATE, SSM_OCT))
    r = lax.broadcasted_iota(jnp.int32, (tw, tw), 0)
    c = lax.broadcasted_iota(jnp.int32, (tw, tw), 1)
    col_grp = (c >> sh_g) & (SSM_OCT - 1)
    row_grp = (r >> sh_g) & (SSM_OCT - 1)
    for tp in range(SSM_CHUNK // 2):
        src_col = (2 * tp + (c >> (sh_g + sh_o))) * SSM_GROUP + (c & (SSM_GROUP - 1))
        e = jnp.where(r == src_col, 1.0, 0.0).astype(BF16)
        base = tp * (tp + 1) // 2
        for sp in range(tp + 1):
            x = jnp.dot(toep_ref[sp * tw:(sp + 1) * tw, :], e, preferred_element_type=F32)
            wt_scr[base + sp] = jnp.where(row_grp == col_grp, x, 0.0).astype(BF16)
        for j in range(2 * SSM_OCT * SSM_STATE // tw):
            x = jnp.dot(cstc_ref[j * tw:(j + 1) * tw, :], e, preferred_element_type=F32)
            row_grp_n = ((j * tw + r) >> sh_n) & (SSM_OCT - 1)
            cst_scr[j * tw:(j + 1) * tw, tp * tw:(tp + 1) * tw] = jnp.where(row_grp_n == col_grp, x, 0.0).astype(BF16)
    ns = 2 * SSM_OCT * SSM_STATE
    rb = lax.broadcasted_iota(jnp.int32, (2 * SSM_STATE, ns), 0)
    cb = lax.broadcasted_iota(jnp.int32, (2 * SSM_STATE, ns), 1)
    src_col_b = (cb >> (sh_n + sh_o)) * SSM_STATE + (cb & (SSM_STATE - 1))
    eb = jnp.where(rb == src_col_b, 1.0, 0.0).astype(BF16)
    rr = lax.broadcasted_iota(jnp.int32, (tw, ns), 0)
    cc = lax.broadcasted_iota(jnp.int32, (tw, ns), 1)
    keep = ((rr >> sh_g) & (SSM_OCT - 1)) == ((cc >> sh_n) & (SSM_OCT - 1))
    for j in range(SSM_CHUNK * LANES // tw):
        x = jnp.dot(bstc_ref[j * tw:(j + 1) * tw, :], eb, preferred_element_type=F32)
        bst_scr[j * tw:(j + 1) * tw, :] = jnp.where(keep, x, 0.0).astype(BF16)


def _ssm_kernel(u_ref, toep_ref, bstc_ref, cstc_ref, a_ref, y_ref, wt_ref, bst_ref, cst_ref, zz_scr, hp_scr,
                *, n_chunks, nb):
    @pl.when(pl.program_id(1) == 0)
    def _new_octet():
        _expand_ssm_weights(toep_ref, bstc_ref, cstc_ref, wt_ref, bst_ref, cst_ref)

    rows = nb * n_chunks
    nk = SSM_OCT * SSM_STATE // LANES
    u_t = [u_ref[:, s].reshape(rows, LANES) for s in range(SSM_CHUNK)]
    z = jnp.dot(jnp.concatenate(u_t, axis=-1), bst_ref[...], preferred_element_type=F32)
    pitch = _row_pitch(n_chunks)
    for k in range(2 * nk):
        for b in range(nb):
            zz_scr[k, b * pitch:b * pitch + n_chunks, :] = z[b * n_chunks:(b + 1) * n_chunks, k * LANES:(k + 1) * LANES]
    a_re = [jnp.broadcast_to(a_ref[0:1, k * LANES:(k + 1) * LANES], (nb, LANES)) for k in range(nk)]
    a_im = [jnp.broadcast_to(a_ref[1:2, k * LANES:(k + 1) * LANES], (nb, LANES)) for k in range(nk)]

    def step(c, carry):
        h_re, h_im = carry
        rows_c = pl.ds(c, nb, stride=pitch)
        new_re, new_im = [], []
        for k in range(nk):
            hp_scr[k, rows_c, :] = h_re[k]
            hp_scr[nk + k, rows_c, :] = h_im[k]
            new_re.append(a_re[k] * h_re[k] - a_im[k] * h_im[k] + zz_scr[k, rows_c, :])
            new_im.append(a_re[k] * h_im[k] + a_im[k] * h_re[k] + zz_scr[nk + k, rows_c, :])
        return tuple(new_re), tuple(new_im)

    zero = tuple(jnp.zeros((nb, LANES), F32) for _ in range(nk))
    lax.fori_loop(0, n_chunks, step, (zero, zero), unroll=4)
    hp = jnp.concatenate(
        [jnp.concatenate([hp_scr[k, b * pitch:b * pitch + n_chunks, :] for b in range(nb)], axis=0)
         for k in range(2 * nk)], axis=-1).astype(BF16)
    tw = 2 * LANES
    for tp in range(SSM_CHUNK // 2):
        base = tp * (tp + 1) // 2
        y = jnp.dot(hp, cst_ref[:, tp * tw:(tp + 1) * tw], preferred_element_type=F32)
        y = y + jnp.dot(jnp.concatenate(u_t[:2 * (tp + 1)], axis=-1),
                        wt_ref[base:base + tp + 1].reshape((tp + 1) * tw, tw), preferred_element_type=F32)
        y = jax.nn.gelu(y).astype(BF16)
        y_ref[:, 2 * tp] = y[:, :LANES].reshape(nb, n_chunks, LANES)
        y_ref[:, 2 * tp + 1] = y[:, LANES:].reshape(nb, n_chunks, LANES)


def _ssm(u16, weights, nb):
    toep_c, bst_c, cst_c, avec = weights
    bsz, cs, nc, width = u16.shape
    oct_spec = lambda a: pl.BlockSpec((None,) + a.shape[1:], lambda o, b: (o,) + (0,) * (a.ndim - 1))
    io_spec = pl.BlockSpec((nb, cs, nc, LANES), lambda o, b: (b, 0, 0, o))
    n_state = 2 * SSM_OCT * SSM_STATE
    n_pairs = (cs // 2) * (cs // 2 + 1) // 2
    return pl.pallas_call(
        functools.partial(_ssm_kernel, n_chunks=nc, nb=nb),
        grid=(width // LANES, bsz // nb),
        in_specs=[io_spec, oct_spec(toep_c), oct_spec(bst_c), oct_spec(cst_c), oct_spec(avec)],
        out_specs=io_spec,
        out_shape=jax.ShapeDtypeStruct(u16.shape, BF16),
        scratch_shapes=[pltpu.VMEM((n_pairs, 2 * LANES, 2 * LANES), BF16),
                        pltpu.VMEM((cs * LANES, n_state), BF16),
                        pltpu.VMEM((n_state, cs * LANES), BF16),
                        pltpu.VMEM((n_state // LANES, nb * _row_pitch(nc), LANES), F32),
                        pltpu.VMEM((n_state // LANES, nb * _row_pitch(nc), LANES), F32)],
        compiler_params=_params(2),
        name="ssm",
    )(u16, toep_c, bst_c, cst_c, avec)


def _layer_norm(v, g, b):
    mu = jnp.mean(v, axis=-1, keepdims=True)
    vc = v - mu
    var = jnp.mean(vc * vc, axis=-1, keepdims=True)
    return vc * lax.rsqrt(var + LN_EPS) * g + b


def _project_and_norm(o_ref, resid, lhs, w_ref, g_ref, b_ref):
    slab = resid.shape[0] // NORM_SLABS
    vs = [ALPHA * resid[i * slab:(i + 1) * slab]
          + jnp.dot(lhs[i * slab:(i + 1) * slab], w_ref[...], preferred_element_type=F32)
          for i in range(NORM_SLABS)]
    for i, v in enumerate(vs):
        o_ref[i * slab:(i + 1) * slab, :] = _layer_norm(v, g_ref[...], b_ref[...])


def _merge_kernel(o0_ref, o1_ref, o2_ref, l0_ref, l1_ref, l2_ref, ys_ref, g_ref, x_ref,
                  wglu_ref, wsp_ref, wap_ref, wout_ref, lng_ref, lnb_ref, h_ref, o_scr, l_scr, y_scr):
    tm = x_ref.shape[0]

    def token_order(ref, scr, d):
        if d == 1:
            return ref[0].astype(F32)
        nk = ref.shape[-1] // LANES
        pitch = _row_pitch(d)
        for r in range(d):
            v = ref[r].astype(F32)
            for k in range(nk):
                scr[k, pl.ds(r, tm // d, stride=pitch), :] = v[:, k * LANES:(k + 1) * LANES]
        if pitch == d:
            return jnp.concatenate([scr[k, 0:tm, :] for k in range(nk)], axis=-1)
        return jnp.concatenate(
            [jnp.concatenate([scr[k, g * pitch:g * pitch + d, :] for g in range(tm // d)], axis=0)
             for k in range(nk)], axis=-1)

    ls, outs = [], []
    for gi, (o_ref, l_ref) in enumerate(((o0_ref, l0_ref), (o1_ref, l1_ref), (o2_ref, l2_ref))):
        d = DILATION_PATTERNS[gi][1]
        slot = sum(dd > 1 for _, dd in DILATION_PATTERNS[:gi])
        outs.append(token_order(o_ref, o_scr.at[slot], d))
        ls.append(token_order(l_ref, l_scr.at[slot], d))
    mx = jnp.maximum(jnp.maximum(ls[0], ls[1]), ls[2])
    es = [jnp.exp(l - mx) for l in ls]
    num = es[0] * outs[0] + es[1] * outs[1] + es[2] * outs[2]
    y_attn = (num / (es[0] + es[1] + es[2])).astype(BF16)
    ys = token_order(ys_ref, y_scr, SSM_CHUNK).astype(BF16)
    cw = GROUP_WIDTH
    y_ssm = []
    for c in range(SSM_WIDTH // cw):
        a = jnp.dot(ys, wglu_ref[:, c * cw:(c + 1) * cw], preferred_element_type=F32)
        b = jnp.dot(ys, wglu_ref[:, SSM_WIDTH + c * cw:SSM_WIDTH + (c + 1) * cw], preferred_element_type=F32)
        y_ssm.append((a * _sigmoid(b)).astype(BF16))
    y_ssm = jnp.concatenate(y_ssm, axis=-1)
    gated = []
    for c in range(D_MODEL // cw):
        pa = jnp.dot(y_ssm, wsp_ref[:, c * cw:(c + 1) * cw], preferred_element_type=F32)
        pb = jnp.dot(y_attn, wap_ref[:, c * cw:(c + 1) * cw], preferred_element_type=F32)
        g_ssm = g_ref[:, c * cw:(c + 1) * cw].astype(F32)
        g_attn = g_ref[:, D_MODEL + c * cw:D_MODEL + (c + 1) * cw].astype(F32)
        gated.append((g_ssm * pa + g_attn * pb).astype(BF16))
    gated = jnp.concatenate(gated, axis=-1)
    _project_and_norm(h_ref, x_ref, gated, wout_ref, lng_ref, lnb_ref)


def _merge(outs, lses, ys, gates, x2, w_glu, w_sp, w_ap, w_out, ln_g, ln_b, seqlen, tm):
    t = x2.shape[0]
    tiles = seqlen // tm
    stage_rows = max(tm // d * _row_pitch(d) for d in [d for _, d in DILATION_PATTERNS] + [SSM_CHUNK])
    n_dilated = sum(d > 1 for _, d in DILATION_PATTERNS)
    row = lambda w: pl.BlockSpec((tm, w), lambda i: (i, 0))
    res_spec = lambda d, w: pl.BlockSpec((None, d, tm // d, w), lambda i: (i // tiles, 0, i % tiles, 0))
    res_specs = [res_spec(d, GROUP_WIDTH) for _, d in DILATION_PATTERNS]
    return pl.pallas_call(
        _merge_kernel,
        grid=(t // tm,),
        in_specs=res_specs + res_specs + [res_spec(SSM_CHUNK, SSM_WIDTH), row(2 * D_MODEL), row(D_MODEL),
                  _const_spec(w_glu.shape), _const_spec(w_sp.shape), _const_spec(w_ap.shape),
                  _const_spec(w_out.shape), _const_spec((1, D_MODEL)), _const_spec((1, D_MODEL))],
        out_specs=row(D_MODEL),
        out_shape=jax.ShapeDtypeStruct((t, D_MODEL), F32),
        scratch_shapes=[pltpu.VMEM((n_dilated, GROUP_WIDTH // LANES, stage_rows, LANES), F32),
                        pltpu.VMEM((n_dilated, GROUP_WIDTH // LANES, stage_rows, LANES), F32),
                        pltpu.VMEM((SSM_WIDTH // LANES, stage_rows, LANES), F32)],
        compiler_params=_params(1),
        name="merge_ln1",
    )(*outs, *lses, ys, gates, x2, w_glu, w_sp, w_ap, w_out, ln_g, ln_b)


FF_CHUNK = 1024
NORM_SLABS = 4


def _ffn_kernel(h_ref, wup_ref, wdn_ref, lng_ref, lnb_ref, o_ref, hb_scr, act_scr):
    hb_scr[...] = h_ref[...].astype(BF16)
    for c in range(D_FF // FF_CHUNK):
        lo, hi = c * FF_CHUNK, (c + 1) * FF_CHUNK
        up = jnp.dot(hb_scr[...], wup_ref[:, lo:hi], preferred_element_type=F32)
        act_scr[:, lo:hi] = jnp.square(jnp.maximum(up, 0.0)).astype(BF16)
    _project_and_norm(o_ref, h_ref, act_scr, wdn_ref, lng_ref, lnb_ref)


def _ffn(h1, w_up, w_down, ln_g, ln_b, tm):
    t = h1.shape[0]
    row = pl.BlockSpec((tm, D_MODEL), lambda i: (i, 0))
    return pl.pallas_call(
        _ffn_kernel,
        grid=(t // tm,),
        in_specs=[row, _const_spec(w_up.shape), _const_spec(w_down.shape),
                  _const_spec((1, D_MODEL)), _const_spec((1, D_MODEL))],
        out_specs=row,
        out_shape=jax.ShapeDtypeStruct((t, D_MODEL), F32),
        scratch_shapes=[pltpu.VMEM((tm, D_MODEL), BF16), pltpu.VMEM((tm, D_FF), BF16)],
        compiler_params=_params(1),
        name="ffn_ln2",
    )(h1, w_up, w_down, ln_g, ln_b)


def _permute_w_in(w):
    aw = ATTN_WIDTH
    cols = []
    for gi in range(N_GROUPS):
        lo, hi = gi * GROUP_WIDTH, (gi + 1) * GROUP_WIDTH
        cols += [w[:, lo:hi] * (HEAD_DIM ** -0.5), w[:, aw + lo:aw + hi], w[:, 2 * aw + lo:2 * aw + hi]]
    cols.append(w[:, 3 * aw:])
    return jnp.concatenate(cols, axis=1).astype(BF16)


def _layer(h2, bsz, seqlen, l, w_in, b_gate, lambda_re, lambda_im, log_dt, ssm_b_re, ssm_b_im,
           ssm_c_re, ssm_c_im, ssm_d, w_glu, w_ssm_proj, rel_bias, w_attn_proj, w_out,
           ln1_g, ln1_b, w_up, w_down, ln2_g, ln2_b, tm=1024, attn_mt=2048, ffn_tm=1024):
    qkv0, qkv1, qkv2, u, gates = _in_proj(h2, _permute_w_in(w_in[l]), b_gate[l][None, :], bsz, seqlen, tm)
    outs, lses = [], []
    for gi, ((window, dilation), qkv) in enumerate(zip(DILATION_PATTERNS, (qkv0, qkv1, qkv2))):
        rb = rel_bias[:, gi * HEADS_PER_GROUP:(gi + 1) * HEADS_PER_GROUP].astype(F32)
        o, s = _attention_group(qkv, rb, window, dilation, attn_mt)
        outs.append(o)
        lses.append(s)
    ssm_w = _ssm_weights(lambda_re[l], lambda_im[l], log_dt[l], ssm_b_re[l], ssm_b_im[l],
                         ssm_c_re[l], ssm_c_im[l], ssm_d[l])
    ys = _ssm(u, ssm_w, nb=4)
    h1 = _merge(outs, lses, ys, gates, h2, w_glu[l].astype(BF16), w_ssm_proj[l].astype(BF16),
                w_attn_proj[l].astype(BF16), w_out[l].astype(BF16),
                ln1_g[l][None, :], ln1_b[l][None, :], seqlen, tm)
    return _ffn(h1, w_up[l].astype(BF16), w_down[l].astype(BF16), ln2_g[l][None, :], ln2_b[l][None, :], ffn_tm)


def kernel(x, w_in, b_gate, lambda_re, lambda_im, log_dt, ssm_b_re, ssm_b_im, ssm_c_re, ssm_c_im,
           ssm_d, w_glu, w_ssm_proj, rel_bias, w_attn_proj, w_out, ln1_g, ln1_b, w_up, w_down,
           ln2_g, ln2_b):
    bsz, seqlen, d = x.shape
    h = x.reshape(bsz * seqlen, d)
    for l in range(w_in.shape[0]):
        h = _layer(h, bsz, seqlen, l, w_in, b_gate, lambda_re, lambda_im, log_dt, ssm_b_re, ssm_b_im,
                   ssm_c_re, ssm_c_im, ssm_d, w_glu, w_ssm_proj, rel_bias, w_attn_proj, w_out,
                   ln1_g, ln1_b, w_up, w_down, ln2_g, ln2_b)
    return h.reshape(bsz, seqlen, d)
```

```python
import functools
import math

import jax
import jax.numpy as jnp
from jax import lax
from jax.experimental import pallas as pl
from jax.experimental.pallas import tpu as pltpu

F32 = jnp.float32
BF16 = jnp.bfloat16

D_MODEL = 1024
HEAD_DIM = 64
HEADS_PER_GROUP = 4
GROUP_WIDTH = HEADS_PER_GROUP * HEAD_DIM
DILATION_PATTERNS = ((128, 1), (512, 4), (2048, 16))
N_GROUPS = len(DILATION_PATTERNS)
ATTN_WIDTH = N_GROUPS * GROUP_WIDTH
N_BUCKETS = 32
MAX_DISTANCE = 2048
SSM_WIDTH = 512
SSM_GROUP = 16
SSM_GROUPS = 32
SSM_STATE = 64
D_FF = 4 * D_MODEL
DEPTH = 1
ALPHA = (2.0 * DEPTH) ** 0.25
LN_EPS = 1e-5
NEG_INF = -1e30

ATTN_BLOCK = 128
ATTN_SKEW = (4, 7)
SSM_CHUNK = 16
SSM_OCT = 8
LANES = 128
SUBLANES = 8
VMEM_LIMIT_BYTES = 56 * 1024 * 1024


def _params(n_axes):
    return pltpu.CompilerParams(dimension_semantics=("arbitrary",) * n_axes,
                                vmem_limit_bytes=VMEM_LIMIT_BYTES)


def _sigmoid(z):
    return 0.5 * jnp.tanh(0.5 * z) + 0.5


def _row_pitch(stride):
    return stride + SUBLANES // 2 if stride % (2 * SUBLANES) == 0 else stride


def _const_spec(shape):
    nd = len(shape)
    return pl.BlockSpec(shape, lambda *_: (0,) * nd, pipeline_mode=pl.Buffered(1))


def _in_proj_kernel(x_ref, w_ref, bg_ref, qkv0_ref, qkv1_ref, qkv2_ref, u_ref, g_ref, scr, xb_scr):
    tm = x_ref.shape[0]
    xb_scr[...] = x_ref[...].astype(BF16)

    def mm(lo, hi):
        return jnp.dot(xb_scr[...], w_ref[:, lo:hi], preferred_element_type=F32)

    def emit(ref, res, d, col0):
        width = res.shape[1]
        if d == 1:
            ref[0, :, col0:col0 + width] = res.astype(BF16)
            return
        slot0 = next_slot[0]
        next_slot[0] += width // LANES
        pitch = _row_pitch(d)
        for k in range(width // LANES):
            tile = res[:, k * LANES:(k + 1) * LANES]
            if pitch == d:
                scr[slot0 + k, 0:tm, :] = tile
            else:
                for g in range(tm // d):
                    scr[slot0 + k, g * pitch:g * pitch + d, :] = tile[g * d:(g + 1) * d]
        for r in range(d):
            for k in range(width // LANES):
                col = col0 + k * LANES
                ref[r, :, col:col + LANES] = scr[slot0 + k, pl.ds(r, tm // d, stride=pitch), :].astype(BF16)

    next_slot = [0]

    gw3 = 3 * GROUP_WIDTH
    for gi, ref in enumerate((qkv0_ref, qkv1_ref, qkv2_ref)):
        for c in range(3):
            lo = gi * gw3 + c * GROUP_WIDTH
            emit(ref, mm(lo, lo + GROUP_WIDTH), DILATION_PATTERNS[gi][1], c * GROUP_WIDTH)
    base = N_GROUPS * gw3
    emit(u_ref, mm(base, base + SSM_WIDTH), SSM_CHUNK, 0)
    base += SSM_WIDTH
    for c in range(2 * D_MODEL // GROUP_WIDTH):
        lo, hi = c * GROUP_WIDTH, (c + 1) * GROUP_WIDTH
        z = mm(base + lo, base + hi) + bg_ref[:, lo:hi]
        g_ref[:, lo:hi] = _sigmoid(z).astype(BF16)


def _in_proj(x2, w_perm, b_gate, bsz, seqlen, tm):
    t = x2.shape[0]
    n_in = w_perm.shape[1]
    tiles = seqlen // tm
    row = lambda w: pl.BlockSpec((tm, w), lambda i: (i, 0))
    w3 = 3 * GROUP_WIDTH
    dils = [d for _, d in DILATION_PATTERNS]
    n_stage = (sum(d > 1 for d in dils) * w3 + SSM_WIDTH) // LANES
    res_spec = lambda d, w:pl.BlockSpec((None, d, tm // d, w), lambda i: (i // tiles, 0, i % tiles, 0))
    return pl.pallas_call(
        _in_proj_kernel,
        grid=(t // tm,),
        in_specs=[row(D_MODEL), _const_spec((D_MODEL, n_in)), _const_spec((1, 2 * D_MODEL))],
        out_specs=[res_spec(d, w3) for d in dils] + [res_spec(SSM_CHUNK, SSM_WIDTH), row(2 * D_MODEL)],
        out_shape=[jax.ShapeDtypeStruct((bsz, d, seqlen // d, w3), BF16) for d in dils]
        + [jax.ShapeDtypeStruct((bsz, SSM_CHUNK, seqlen // SSM_CHUNK, SSM_WIDTH), BF16),
           jax.ShapeDtypeStruct((t, 2 * D_MODEL), BF16)],
        scratch_shapes=[pltpu.VMEM((n_stage, max(tm // d * _row_pitch(d) for d in dils + [SSM_CHUNK]), LANES), F32),
                        pltpu.VMEM((tm, D_MODEL), BF16)],
        compiler_params=_params(1),
        name="in_proj",
    )(x2, w_perm, b_gate)


def _attn_kernel(bucket_ref, relb_ref, cur_ref, prev_ref, out_ref, lse_ref, bias_scr, k_scr, vt_scr, *, nq, nres):
    blk = ATTN_BLOCK
    first = (pl.program_id(0) == 0) & (pl.program_id(1) == 0) & (pl.program_id(2) == 0)

    @pl.when(first)
    def _build_bias():
        bucket = bucket_ref[...]
        for h in range(HEADS_PER_GROUP):
            acc = jnp.full(bucket.shape, NEG_INF, F32)
            for bkt in range(N_BUCKETS):
                acc = jnp.where(bucket == bkt, relb_ref[bkt, h], acc)
            bias_scr[h // 2, :, (h % 2) * blk:(h % 2 + 1) * blk] = acc

    row = lax.broadcasted_iota(jnp.int32, (2 * blk, blk), 0)
    row_head = row // HEAD_DIM
    keep_first = (row >= blk) | (pl.program_id(2) > 0)
    for r in range(nres):
        k_scr[r, 0:blk, :] = prev_ref[r, :, GROUP_WIDTH:2 * GROUP_WIDTH]
        k_scr[r, blk:, :] = cur_ref[r, :, GROUP_WIDTH:2 * GROUP_WIDTH]
        vt_scr[r, :, 0:blk] = prev_ref[r, :, 2 * GROUP_WIDTH:3 * GROUP_WIDTH].T
        for j in range(nq):
            vt_scr[r, :, (j + 1) * blk:(j + 2) * blk] = cur_ref[r, j * blk:(j + 1) * blk,
                                                                 2 * GROUP_WIDTH:3 * GROUP_WIDTH].T

    npair = HEADS_PER_GROUP // 2
    units = [(r, j, hp) for r in range(nres) for j in range(nq) for hp in range(npair)]
    qts = {}

    def scores(r, j, hp):
        if (r, j) not in qts:
            qts[(r, j)] = cur_ref[r, j * blk:(j + 1) * blk, 0:GROUP_WIDTH].T
        qt = qts[(r, j)]
        k2 = k_scr[r, j * blk:(j + 2) * blk, :]
        qh = jnp.concatenate([jnp.where(row_head == 2 * hp + i, qt, jnp.zeros_like(qt)) for i in range(2)], axis=1)
        return jnp.dot(k2, qh, preferred_element_type=F32)

    def softmax(r, j, hp, s2):
        ps, ms, ls = [], [], []
        for i in range(2):
            lanes = slice(i * blk, (i + 1) * blk)
            s = s2[:, lanes] + bias_scr[hp, :, lanes]
            if j == 0:
                s = jnp.where(keep_first, s, NEG_INF)
            m = jnp.max(s, axis=0, keepdims=True)
            p = jnp.exp(s - m)
            ls.append(jnp.sum(p, axis=0, keepdims=True))
            ps.append(p.astype(BF16))
            ms.append(m)
        return jnp.concatenate(ps, axis=1), jnp.concatenate(ms, axis=1), jnp.concatenate(ls, axis=1)

    def values(r, j, hp, p, m, l):
        vt = vt_scr[r, 2 * hp * HEAD_DIM:2 * (hp + 1) * HEAD_DIM, j * blk:(j + 2) * blk]
        ot = jnp.dot(vt, p, preferred_element_type=F32) * (1.0 / l)
        lse = jnp.broadcast_to(m + jnp.log(l), (HEAD_DIM, 2 * blk))
        return [(ot[i * HEAD_DIM:(i + 1) * HEAD_DIM, i * blk:(i + 1) * blk], lse[:, i * blk:(i + 1) * blk])
                for i in range(2)]

    raw, soft, done = {}, {}, []
    lag_soft, lag_val = ATTN_SKEW
    for step in range(len(units) + lag_val):
        if step < len(units):
            raw[step] = scores(*units[step])
        if lag_soft <= step < len(units) + lag_soft:
            soft[step - lag_soft] = softmax(*units[step - lag_soft], raw.pop(step - lag_soft))
        if step >= lag_val:
            i = step - lag_val
            r, j, hp = units[i]
            done += values(r, j, hp, *soft.pop(i))
            if hp == npair - 1:
                rows = slice(j * blk, (j + 1) * blk)
                out_ref[r, rows, :] = jnp.concatenate([d[0] for d in done], axis=0).T.astype(BF16)
                lse_ref[r, rows, :] = jnp.concatenate([d[1] for d in done], axis=0).T
                done = []


def _t5_bucket(dist):
    max_exact = N_BUCKETS // 2
    n_log = N_BUCKETS - max_exact
    thresholds = [math.ceil(max_exact * (MAX_DISTANCE / max_exact) ** (k / n_log)) for k in range(1, n_log)]
    large = max_exact + sum((dist >= t).astype(jnp.int32) for t in thresholds)
    return jnp.where(dist < max_exact, dist, large)


def _bucket_table(window, dilation):
    blk = ATTN_BLOCK
    span = window // dilation
    rel = jnp.arange(blk, dtype=jnp.int32)[None, :] + blk - jnp.arange(2 * blk, dtype=jnp.int32)[:, None]
    valid = (rel >= 0) & (rel <= span)
    return jnp.where(valid, _t5_bucket(jnp.maximum(rel, 0) * dilation), -1)


def _attention_group(qkv, rel_bias_g, window, dilation, mt):
    assert window // dilation == ATTN_BLOCK
    bsz, _, n, w3 = qkv.shape
    rows = mt
    mt = min(rows, n)
    nres = rows // mt
    nq = mt // ATTN_BLOCK
    cur = pl.BlockSpec((None, nres, mt, w3), lambda b, r, i: (b, r, i, 0))
    prev = pl.BlockSpec((None, nres, ATTN_BLOCK, w3), lambda b, r, i: (b, r, jnp.maximum(i * nq - 1, 0), 0))
    ospec = pl.BlockSpec((None, nres, mt, GROUP_WIDTH), lambda b, r, i: (b, r, i, 0))
    return pl.pallas_call(
        functools.partial(_attn_kernel, nq=nq, nres=nres),
        grid=(bsz, dilation // nres, n // mt),
        in_specs=[_const_spec((2 * ATTN_BLOCK, ATTN_BLOCK)),
                  pl.BlockSpec(memory_space=pltpu.SMEM), cur, prev],
        out_specs=[ospec, ospec],
        out_shape=[jax.ShapeDtypeStruct((bsz, dilation, n, GROUP_WIDTH), BF16),
                   jax.ShapeDtypeStruct((bsz, dilation, n, GROUP_WIDTH), F32)],
        scratch_shapes=[pltpu.VMEM((HEADS_PER_GROUP // 2, 2 * ATTN_BLOCK, 2 * ATTN_BLOCK), F32),
                        pltpu.VMEM((nres, mt + ATTN_BLOCK, GROUP_WIDTH), BF16),
                        pltpu.VMEM((nres, GROUP_WIDTH, mt + ATTN_BLOCK), BF16)],
        compiler_params=_params(3),
        name=f"attn_d{dilation}",
    )(_bucket_table(window, dilation), rel_bias_g, qkv, qkv)


def _ssm_prep_kernel(logdt_ref, lr_ref, li_ref, bt_re_ref, bt_im_ref, c_re_ref, c_im_ref, dl_ref,
                     toep_ref, bst_ref, cst_ref, a_ref):
    for g in range(SSM_OCT):
        _ssm_prep_group(g, logdt_ref[pl.program_id(0) * SSM_OCT + g], lr_ref, li_ref, bt_re_ref, bt_im_ref,
                        c_re_ref, c_im_ref, dl_ref, toep_ref, bst_ref, cst_ref, a_ref)


def _ssm_prep_group(g, log_dt, lr_ref, li_ref, bt_re_ref, bt_im_ref, c_re_ref, c_im_ref, dl_ref,
                    toep_ref, bst_ref, cst_ref, a_ref):
    cs = SSM_CHUNK
    dt = jnp.exp(jnp.full((1, SSM_STATE), log_dt, F32))
    lr, li = lr_ref[g], li_ref[g]
    mag = jnp.exp(lr * dt)
    ab_re, ab_im = mag * jnp.cos(li * dt), mag * jnp.sin(li * dt)
    den = lr * lr + li * li
    nr = ab_re - 1.0
    k_re = (nr * lr + ab_im * li) / den
    k_im = (ab_im * lr - nr * li) / den
    bt_re, bt_im = bt_re_ref[g], bt_im_ref[g]
    bb_re = k_re * bt_re - k_im * bt_im
    bb_im = k_re * bt_im + k_im * bt_re
    j = lax.broadcasted_iota(jnp.int32, (cs + SUBLANES, SSM_STATE), 0).astype(F32)
    pmag = jnp.exp(lr * dt * j)
    ang = li * dt * j
    p_re, p_im = pmag * jnp.cos(ang), pmag * jnp.sin(ang)
    c_re, c_im = c_re_ref[g], c_im_ref[g]
    cp_re = [c_re * p_re[i:i + 1] - c_im * p_im[i:i + 1] for i in range(cs + 1)]
    cp_im = [c_re * p_im[i:i + 1] + c_im * p_re[i:i + 1] for i in range(cs + 1)]
    nt = (((1,), (1,)), ((), ()))
    hi = lax.Precision.HIGHEST
    kcat = (lax.dot_general(bb_re, jnp.concatenate(cp_re[:cs], axis=0), nt, precision=hi,
                            preferred_element_type=F32)
            - lax.dot_general(bb_im, jnp.concatenate(cp_im[:cs], axis=0), nt, precision=hi,
                              preferred_element_type=F32))
    lane = lax.broadcasted_iota(jnp.int32, kcat.shape, 1)
    row = lax.broadcasted_iota(jnp.int32, kcat.shape, 0)
    dl = dl_ref[g]
    for s in range(cs):
        off = s * SSM_GROUP
        t_s = kcat if s == 0 else jnp.where(lane >= off, pltpu.roll(kcat, off, 1), 0.0)
        toep_ref[s, g] = (t_s + jnp.where(lane == off + row, dl, 0.0)).astype(BF16)
        pe_re, pe_im = p_re[cs - 1 - s:cs - s], p_im[cs - 1 - s:cs - s]
        st_re = pe_re * bb_re - pe_im * bb_im
        st_im = pe_re * bb_im + pe_im * bb_re
        bst_ref[s, g] = jnp.concatenate([st_re, st_im], axis=-1).astype(BF16)
    ro = jnp.concatenate([jnp.concatenate(cp_re[1:], axis=0),
                          -jnp.concatenate(cp_im[1:], axis=0)], axis=-1)
    ro_t = ro.T
    cst_ref[0, g] = ro_t[:SSM_STATE].astype(BF16)
    cst_ref[1, g] = ro_t[SSM_STATE:].astype(BF16)
    a_ref[g] = jnp.concatenate([p_re[cs:cs + 1], p_im[cs:cs + 1]], axis=0)


def _ssm_weights(lambda_re, lambda_im, log_dt, b_re, b_im, c_re, c_im, d_skip):
    f32 = F32
    cs, ng, no = SSM_CHUNK, SSM_GROUPS, SSM_GROUPS // SSM_OCT
    grp = lambda *shape: pl.BlockSpec((SSM_OCT,) + shape, lambda o: (o,) + (0,) * len(shape))
    in_oct = lambda *shape: pl.BlockSpec((None, shape[0], SSM_OCT) + shape[1:],
                                         lambda o: (o,) + (0,) * (len(shape) + 1))
    toep_c, bst_c, cst_c, a32 = pl.pallas_call(
        _ssm_prep_kernel,
        grid=(no,),
        in_specs=[pl.BlockSpec(memory_space=pltpu.SMEM), grp(1, SSM_STATE), grp(1, SSM_STATE),
                  grp(SSM_GROUP, SSM_STATE), grp(SSM_GROUP, SSM_STATE),
                  grp(SSM_GROUP, SSM_STATE), grp(SSM_GROUP, SSM_STATE), grp(1, cs * SSM_GROUP)],
        out_specs=[in_oct(cs, SSM_GROUP, cs * SSM_GROUP), in_oct(cs, SSM_GROUP, 2 * SSM_STATE),
                   in_oct(2, SSM_STATE, cs * SSM_GROUP), grp(2, SSM_STATE)],
        out_shape=[jax.ShapeDtypeStruct((no, cs, SSM_OCT, SSM_GROUP, cs * SSM_GROUP), BF16),
                   jax.ShapeDtypeStruct((no, cs, SSM_OCT, SSM_GROUP, 2 * SSM_STATE), BF16),
                   jax.ShapeDtypeStruct((no, 2, SSM_OCT, SSM_STATE, cs * SSM_GROUP), BF16),
                   jax.ShapeDtypeStruct((ng, 2, SSM_STATE), f32)],
        compiler_params=_params(1),
        name="ssm_prep",
    )(log_dt.astype(f32), lambda_re.astype(f32)[:, None, :], lambda_im.astype(f32)[:, None, :],
      b_re.astype(f32).transpose(0, 2, 1), b_im.astype(f32).transpose(0, 2, 1),
      c_re.astype(f32), c_im.astype(f32), jnp.tile(d_skip.astype(f32), (1, cs))[:, None, :])
    avec = a32.reshape(no, SSM_OCT, 2, SSM_STATE).transpose(0, 2, 1, 3).reshape(no, 2, SSM_OCT * SSM_STATE)
    return (toep_c.reshape(no, cs * LANES, cs * SSM_GROUP),
            bst_c.reshape(no, cs * LANES, 2 * SSM_STATE),
            cst_c.reshape(no, 2 * SSM_OCT * SSM_STATE, cs * SSM_GROUP),
            avec)


def _expand_ssm_weights(toep_ref, bstc_ref, cstc_ref, wt_scr, bst_scr, cst_scr):
    tw = 2 * LANES
    sh_g, sh_n, sh_o = (v.bit_length() - 1 for v in (SSM_GROUP, SSM_STATE, SSM_OCT))
    r = lax.broadcasted_iota(jnp.int32, (tw, tw), 0)
    c = lax.broadcasted_iota(jnp.int32, (tw, tw), 1)
    col_grp = (c >> sh_g) & (SSM_OCT - 1)
    row_grp = (r >> sh_g) & (SSM_OCT - 1)
    for tp in range(SSM_CHUNK // 2):
        src_col = (2 * tp + (c >> (sh_g + sh_o))) * SSM_GROUP + (c & (SSM_GROUP - 1))
        e = jnp.where(r == src_col, 1.0, 0.0).astype(BF16)
        base = tp * (tp + 1) // 2
        for sp in range(tp + 1):
            x = jnp.dot(toep_ref[sp * tw:(sp + 1) * tw, :], e, preferred_element_type=F32)
            wt_scr[base + sp] = jnp.where(row_grp == col_grp, x, 0.0).astype(BF16)
        for j in range(2 * SSM_OCT * SSM_STATE // tw):
            x = jnp.dot(cstc_ref[j * tw:(j + 1) * tw, :], e, preferred_element_type=F32)
            row_grp_n = ((j * tw + r) >> sh_n) & (SSM_OCT - 1)
            cst_scr[j * tw:(j + 1) * tw, tp * tw:(tp + 1) * tw] = jnp.where(row_grp_n == col_grp, x, 0.0).astype(BF16)
    ns = 2 * SSM_OCT * SSM_STATE
    rb = lax.broadcasted_iota(jnp.int32, (2 * SSM_STATE, ns), 0)
    cb = lax.broadcasted_iota(jnp.int32, (2 * SSM_STATE, ns), 1)
    src_col_b = (cb >> (sh_n + sh_o)) * SSM_STATE + (cb & (SSM_STATE - 1))
    eb = jnp.where(rb == src_col_b, 1.0, 0.0).astype(BF16)
    rr = lax.broadcasted_iota(jnp.int32, (tw, ns), 0)
    cc = lax.broadcasted_iota(jnp.int32, (tw, ns), 1)
    keep = ((rr >> sh_g) & (SSM_OCT - 1)) == ((cc >> sh_n) & (SSM_OCT - 1))
    for j in range(SSM_CHUNK * LANES // tw):
        x = jnp.dot(bstc_ref[j * tw:(j + 1) * tw, :], eb, preferred_element_type=F32)
        bst_scr[j * tw:(j + 1) * tw, :] = jnp.where(keep, x, 0.0).astype(BF16)


def _ssm_kernel(u_ref, toep_ref, bstc_ref, cstc_ref, a_ref, y_ref, wt_ref, bst_ref, cst_ref, zz_scr, hp_scr,
                *, n_chunks, nb):
    @pl.when(pl.program_id(1) == 0)
    def _new_octet():
        _expand_ssm_weights(toep_ref, bstc_ref, cstc_ref, wt_ref, bst_ref, cst_ref)

    rows = nb * n_chunks
    nk = SSM_OCT * SSM_STATE // LANES
    u_t = [u_ref[:, s].reshape(rows, LANES) for s in range(SSM_CHUNK)]
    z = jnp.dot(jnp.concatenate(u_t, axis=-1), bst_ref[...], preferred_element_type=F32)
    pitch = _row_pitch(n_chunks)
    for k in range(2 * nk):
        for b in range(nb):
            zz_scr[k, b * pitch:b * pitch + n_chunks, :] = z[b * n_chunks:(b + 1) * n_chunks, k * LANES:(k + 1) * LANES]
    a_re = [jnp.broadcast_to(a_ref[0:1, k * LANES:(k + 1) * LANES], (nb, LANES)) for k in range(nk)]
    a_im = [jnp.broadcast_to(a_ref[1:2, k * LANES:(k + 1) * LANES], (nb, LANES)) for k in range(nk)]

    def step(c, carry):
        h_re, h_im = carry
        rows_c = pl.ds(c, nb, stride=pitch)
        new_re, new_im = [], []
        for k in range(nk):
            hp_scr[k, rows_c, :] = h_re[k]
            hp_scr[nk + k, rows_c, :] = h_im[k]
            new_re.append(a_re[k] * h_re[k] - a_im[k] * h_im[k] + zz_scr[k, rows_c, :])
            new_im.append(a_re[k] * h_im[k] + a_im[k] * h_re[k] + zz_scr[nk + k, rows_c, :])
        return tuple(new_re), tuple(new_im)

    zero = tuple(jnp.zeros((nb, LANES), F32) for _ in range(nk))
    lax.fori_loop(0, n_chunks, step, (zero, zero), unroll=4)
    hp = jnp.concatenate(
        [jnp.concatenate([hp_scr[k, b * pitch:b * pitch + n_chunks, :] for b in range(nb)], axis=0)
         for k in range(2 * nk)], axis=-1).astype(BF16)
    tw = 2 * LANES
    for tp in range(SSM_CHUNK // 2):
        base = tp * (tp + 1) // 2
        y = jnp.dot(hp, cst_ref[:, tp * tw:(tp + 1) * tw], preferred_element_type=F32)
        y = y + jnp.dot(jnp.concatenate(u_t[:2 * (tp + 1)], axis=-1),
                        wt_ref[base:base + tp + 1].reshape((tp + 1) * tw, tw), preferred_element_type=F32)
        y = jax.nn.gelu(y).astype(BF16)
        y_ref[:, 2 * tp] = y[:, :LANES].reshape(nb, n_chunks, LANES)
        y_ref[:, 2 * tp + 1] = y[:, LANES:].reshape(nb, n_chunks, LANES)


def _ssm(u16, weights, nb):
    toep_c, bst_c, cst_c, avec = weights
    bsz, cs, nc, width = u16.shape
    oct_spec = lambda a: pl.BlockSpec((None,) + a.shape[1:], lambda o, b: (o,) + (0,) * (a.ndim - 1))
    io_spec = pl.BlockSpec((nb, cs, nc, LANES), lambda o, b: (b, 0, 0, o))
    n_state = 2 * SSM_OCT * SSM_STATE
    n_pairs = (cs // 2) * (cs // 2 + 1) // 2
    return pl.pallas_call(
        functools.partial(_ssm_kernel, n_chunks=nc, nb=nb),
        grid=(width // LANES, bsz // nb),
        in_specs=[io_spec, oct_spec(toep_c), oct_spec(bst_c), oct_spec(cst_c), oct_spec(avec)],
        out_specs=io_spec,
        out_shape=jax.ShapeDtypeStruct(u16.shape, BF16),
        scratch_shapes=[pltpu.VMEM((n_pairs, 2 * LANES, 2 * LANES), BF16),
                        pltpu.VMEM((cs * LANES, n_state), BF16),
                        pltpu.VMEM((n_state, cs * LANES), BF16),
                        pltpu.VMEM((n_state // LANES, nb * _row_pitch(nc), LANES), F32),
                        pltpu.VMEM((n_state // LANES, nb * _row_pitch(nc), LANES), F32)],
        compiler_params=_params(2),
        name="ssm",
    )(u16, toep_c, bst_c, cst_c, avec)


def _layer_norm(v, g, b):
    mu = jnp.mean(v, axis=-1, keepdims=True)
    vc = v - mu
    var = jnp.mean(vc * vc, axis=-1, keepdims=True)
    return vc * lax.rsqrt(var + LN_EPS) * g + b


def _project_and_norm(o_ref, resid, lhs, w_ref, g_ref, b_ref):
    slab = resid.shape[0] // NORM_SLABS
    vs = [ALPHA * resid[i * slab:(i + 1) * slab]
          + jnp.dot(lhs[i * slab:(i + 1) * slab], w_ref[...], preferred_element_type=F32)
          for i in range(NORM_SLABS)]
    for i, v in enumerate(vs):
        o_ref[i * slab:(i + 1) * slab, :] = _layer_norm(v, g_ref[...], b_ref[...])


def _merge_kernel(o0_ref, o1_ref, o2_ref, l0_ref, l1_ref, l2_ref, ys_ref, g_ref, x_ref,
                  wglu_ref, wsp_ref, wap_ref, wout_ref, lng_ref, lnb_ref, h_ref, o_scr, l_scr, y_scr):
    tm = x_ref.shape[0]

    def token_order(ref, scr, d):
        if d == 1:
            return ref[0].astype(F32)
        nk = ref.shape[-1] // LANES
        pitch = _row_pitch(d)
        for r in range(d):
            v = ref[r].astype(F32)
            for k in range(nk):
                scr[k, pl.ds(r, tm // d, stride=pitch), :] = v[:, k * LANES:(k + 1) * LANES]
        if pitch == d:
            return jnp.concatenate([scr[k, 0:tm, :] for k in range(nk)], axis=-1)
        return jnp.concatenate(
            [jnp.concatenate([scr[k, g * pitch:g * pitch + d, :] for g in range(tm // d)], axis=0)
             for k in range(nk)], axis=-1)

    ls, outs = [], []
    for gi, (o_ref, l_ref) in enumerate(((o0_ref, l0_ref), (o1_ref, l1_ref), (o2_ref, l2_ref))):
        d = DILATION_PATTERNS[gi][1]
        slot = sum(dd > 1 for _, dd in DILATION_PATTERNS[:gi])
        outs.append(token_order(o_ref, o_scr.at[slot], d))
        ls.append(token_order(l_ref, l_scr.at[slot], d))
    slab = tm // NORM_SLABS
    pb_slabs = []
    for i in range(NORM_SLABS):
        rows = slice(i * slab, (i + 1) * slab)
        lr = [l[rows] for l in ls]
        mx = jnp.maximum(jnp.maximum(lr[0], lr[1]), lr[2])
        es = [jnp.exp(l - mx) for l in lr]
        num = es[0] * outs[0][rows] + es[1] * outs[1][rows] + es[2] * outs[2][rows]
        y_attn = (num / (es[0] + es[1] + es[2])).astype(BF16)
        pb_slabs.append(jnp.dot(y_attn, wap_ref[...], preferred_element_type=F32))
    ys = token_order(ys_ref, y_scr, SSM_CHUNK).astype(BF16)
    cw = GROUP_WIDTH
    y_ssm = []
    for c in range(SSM_WIDTH // cw):
        a = jnp.dot(ys, wglu_ref[:, c * cw:(c + 1) * cw], preferred_element_type=F32)
        b = jnp.dot(ys, wglu_ref[:, SSM_WIDTH + c * cw:SSM_WIDTH + (c + 1) * cw], preferred_element_type=F32)
        y_ssm.append((a * _sigmoid(b)).astype(BF16))
    y_ssm = jnp.concatenate(y_ssm, axis=-1)
    gated = []
    for c in range(D_MODEL // cw):
        pa = jnp.dot(y_ssm, wsp_ref[:, c * cw:(c + 1) * cw], preferred_element_type=F32)
        pb = jnp.concatenate([p[:, c * cw:(c + 1) * cw] for p in pb_slabs], axis=0)
        g_ssm = g_ref[:, c * cw:(c + 1) * cw].astype(F32)
        g_attn = g_ref[:, D_MODEL + c * cw:D_MODEL + (c + 1) * cw].astype(F32)
        gated.append((g_ssm * pa + g_attn * pb).astype(BF16))
    gated = jnp.concatenate(gated, axis=-1)
    _project_and_norm(h_ref, x_ref, gated, wout_ref, lng_ref, lnb_ref)


def _merge(outs, lses, ys, gates, x2, w_glu, w_sp, w_ap, w_out, ln_g, ln_b, seqlen, tm):
    t = x2.shape[0]
    tiles = seqlen // tm
    stage_rows = max(tm // d * _row_pitch(d) for d in [d for _, d in DILATION_PATTERNS] + [SSM_CHUNK])
    n_dilated = sum(d > 1 for _, d in DILATION_PATTERNS)
    row = lambda w: pl.BlockSpec((tm, w), lambda i: (i, 0))
    res_spec = lambda d, w: pl.BlockSpec((None, d, tm // d, w), lambda i: (i // tiles, 0, i % tiles, 0))
    res_specs = [res_spec(d, GROUP_WIDTH) for _, d in DILATION_PATTERNS]
    return pl.pallas_call(
        _merge_kernel,
        grid=(t // tm,),
        in_specs=res_specs + res_specs + [res_spec(SSM_CHUNK, SSM_WIDTH), row(2 * D_MODEL), row(D_MODEL),
                  _const_spec(w_glu.shape), _const_spec(w_sp.shape), _const_spec(w_ap.shape),
                  _const_spec(w_out.shape), _const_spec((1, D_MODEL)), _const_spec((1, D_MODEL))],
        out_specs=row(D_MODEL),
        out_shape=jax.ShapeDtypeStruct((t, D_MODEL), F32),
        scratch_shapes=[pltpu.VMEM((n_dilated, GROUP_WIDTH // LANES, stage_rows, LANES), F32),
                        pltpu.VMEM((n_dilated, GROUP_WIDTH // LANES, stage_rows, LANES), F32),
                        pltpu.VMEM((SSM_WIDTH // LANES, stage_rows, LANES), F32)],
        compiler_params=_params(1),
        name="merge_ln1",
    )(*outs, *lses, ys, gates, x2, w_glu, w_sp, w_ap, w_out, ln_g, ln_b)


FF_CHUNK = 1024
NORM_SLABS = 4


def _ffn_kernel(h_ref, wup_ref, wdn_ref, lng_ref, lnb_ref, o_ref, hb_scr, act_scr):
    hb_scr[...] = h_ref[...].astype(BF16)
    for c in range(D_FF // FF_CHUNK):
        lo, hi = c * FF_CHUNK, (c + 1) * FF_CHUNK
        up = jnp.dot(hb_scr[...], wup_ref[:, lo:hi], preferred_element_type=F32)
        act_scr[:, lo:hi] = jnp.square(jnp.maximum(up, 0.0)).astype(BF16)
    _project_and_norm(o_ref, h_ref, act_scr, wdn_ref, lng_ref, lnb_ref)


def _ffn(h1, w_up, w_down, ln_g, ln_b, tm):
    t = h1.shape[0]
    row = pl.BlockSpec((tm, D_MODEL), lambda i: (i, 0))
    return pl.pallas_call(
        _ffn_kernel,
        grid=(t // tm,),
        in_specs=[row, _const_spec(w_up.shape), _const_spec(w_down.shape),
                  _const_spec((1, D_MODEL)), _const_spec((1, D_MODEL))],
        out_specs=row,
        out_shape=jax.ShapeDtypeStruct((t, D_MODEL), F32),
        scratch_shapes=[pltpu.VMEM((tm, D_MODEL), BF16), pltpu.VMEM((tm, D_FF), BF16)],
        compiler_params=_params(1),
        name="ffn_ln2",
    )(h1, w_up, w_down, ln_g, ln_b)


def _permute_w_in(w):
    aw = ATTN_WIDTH
    cols = []
    for gi in range(N_GROUPS):
        lo, hi = gi * GROUP_WIDTH, (gi + 1) * GROUP_WIDTH
        cols += [w[:, lo:hi] * (HEAD_DIM ** -0.5), w[:, aw + lo:aw + hi], w[:, 2 * aw + lo:2 * aw + hi]]
    cols.append(w[:, 3 * aw:])
    return jnp.concatenate(cols, axis=1).astype(BF16)


def _layer(h2, bsz, seqlen, l, w_in, b_gate, lambda_re, lambda_im, log_dt, ssm_b_re, ssm_b_im,
           ssm_c_re, ssm_c_im, ssm_d, w_glu, w_ssm_proj, rel_bias, w_attn_proj, w_out,
           ln1_g, ln1_b, w_up, w_down, ln2_g, ln2_b, tm=1024, attn_mt=2048, ffn_tm=1024):
    qkv0, qkv1, qkv2, u, gates = _in_proj(h2, _permute_w_in(w_in[l]), b_gate[l][None, :], bsz, seqlen, tm)
    outs, lses = [], []
    for gi, ((window, dilation), qkv) in enumerate(zip(DILATION_PATTERNS, (qkv0, qkv1, qkv2))):
        rb = rel_bias[:, gi * HEADS_PER_GROUP:(gi + 1) * HEADS_PER_GROUP].astype(F32)
        o, s = _attention_group(qkv, rb, window, dilation, attn_mt)
        outs.append(o)
        lses.append(s)
    ssm_w = _ssm_weights(lambda_re[l], lambda_im[l], log_dt[l], ssm_b_re[l], ssm_b_im[l],
                         ssm_c_re[l], ssm_c_im[l], ssm_d[l])
    ys = _ssm(u, ssm_w, nb=4)
    h1 = _merge(outs, lses, ys, gates, h2, w_glu[l].astype(BF16), w_ssm_proj[l].astype(BF16),
                w_attn_proj[l].astype(BF16), w_out[l].astype(BF16),
                ln1_g[l][None, :], ln1_b[l][None, :], seqlen, tm)
    return _ffn(h1, w_up[l].astype(BF16), w_down[l].astype(BF16), ln2_g[l][None, :], ln2_b[l][None, :], ffn_tm)


def kernel(x, w_in, b_gate, lambda_re, lambda_im, log_dt, ssm_b_re, ssm_b_im, ssm_c_re, ssm_c_im,
           ssm_d, w_glu, w_ssm_proj, rel_bias, w_attn_proj, w_out, ln1_g, ln1_b, w_up, w_down,
           ln2_g, ln2_b):
    bsz, seqlen, d = x.shape
    h = x.reshape(bsz * seqlen, d)
    for l in range(w_in.shape[0]):
        h = _layer(h, bsz, seqlen, l, w_in, b_gate, lambda_re, lambda_im, log_dt, ssm_b_re, ssm_b_im,
                   ssm_c_re, ssm_c_im, ssm_d, w_glu, w_ssm_proj, rel_bias, w_attn_proj, w_out,
                   ln1_g, ln1_b, w_up, w_down, ln2_g, ln2_b)
    return h.reshape(bsz, seqlen, d)
```
